```python
import math
import jax, jax.numpy as jnp
from jax import lax
import numpy as np

D_MODEL = 1024
BATCH = 32
SEQ = 256
DEPTH = 2
DEC_BATCH = 4
DEC_SEQ = 2048
PAST_LEN = 256

GRID_W = 64
Q_BLOCK = 128
SSD_CHUNK = 128
ROPE_THETA = 10000.0
EPS = 1e-6

HEAD_DIM = 64
A_Q_HEADS = 6
A_KV_HEADS = 2
A_GROUP = A_Q_HEADS // A_KV_HEADS
A_WIDTH = A_Q_HEADS * HEAD_DIM
B_HEADS = 4
B_HALF = 32
B_WIDTH = B_HEADS * 2 * B_HALF
C_HEADS = 6
C_HEADDIM = 64
C_INNER = C_HEADS * C_HEADDIM
C_GROUPS = 2
C_STATE = 128
C_CONV = 3
C_CONV_CH = C_INNER + 2 * C_GROUPS * C_STATE

MIX_WIDTH = A_WIDTH + B_WIDTH + C_INNER
D_FF = 2816
FFN_CONV = 3

PROJ_SIZES = (A_WIDTH, A_KV_HEADS * HEAD_DIM, A_KV_HEADS * HEAD_DIM,
              B_WIDTH, B_WIDTH, B_WIDTH,
              C_INNER, C_INNER, C_GROUPS * C_STATE, C_GROUPS * C_STATE, 2 * C_HEADS)
PROJ_WIDTH = A_WIDTH + 4 * HEAD_DIM + 3 * B_WIDTH + 2 * C_INNER + 4 * C_STATE + 2 * C_HEADS

kernel_name = "hybrid_diffusion_prefix_context_step"


def rmsnorm(x, w=None):
    xf = x.astype(jnp.float32)
    y = xf * lax.rsqrt(jnp.mean(xf * xf, axis=-1, keepdims=True) + EPS)
    if w is not None:
        y = y * w.astype(jnp.float32)
    return y.astype(x.dtype)


def adanorm(x, shift, scale):
    return rmsnorm(x) * (1 + scale) + shift


def modulation(cvec, w, b):
    m = jax.nn.silu(cvec) @ w + b
    return [t[:, None, :] for t in jnp.split(m, 6, axis=-1)]


def axial_rope_tables(L, d):
    rows = L // GRID_W
    row = jnp.repeat(jnp.arange(rows), GRID_W).astype(jnp.float32)
    col = jnp.tile(jnp.arange(GRID_W), rows).astype(jnp.float32)
    quarter = d // 4
    inv = ROPE_THETA ** (-jnp.arange(quarter, dtype=jnp.float32) / quarter)
    ang_r = row[:, None] * inv[None, :]
    ang_c = col[:, None] * inv[None, :]
    ang = jnp.concatenate([ang_r, ang_r, ang_c, ang_c], axis=-1)
    return jnp.cos(ang), jnp.sin(ang)


def apply_rope(x, cos, sin):
    xf = x.astype(jnp.float32)
    q1, q2, q3, q4 = jnp.split(xf, 4, axis=-1)
    rot = jnp.concatenate([-q2, q1, -q4, q3], axis=-1)
    return (xf * cos + rot * sin).astype(x.dtype)


def dwconv(x, w, b):
    K = w.shape[0]
    y = lax.conv_general_dilated(x, w[:, None, :].astype(x.dtype), window_strides=(1,),
                                 padding=[(K // 2, K // 2)],
                                 dimension_numbers=('NWC', 'WIO', 'NWC'),
                                 feature_group_count=x.shape[-1])
    return y + b.astype(x.dtype)


def sweep_query_blocks(fn, q):
    b, L = q.shape[0], q.shape[1]
    nb = L // Q_BLOCK
    qb = jnp.moveaxis(q.reshape((b, nb, Q_BLOCK) + q.shape[2:]), 1, 0)
    ob = lax.map(fn, qb)
    return jnp.moveaxis(ob, 0, 1).reshape((b, L) + ob.shape[3:])


def gqa_attend(q, k, v):
    scale = q.shape[-1] ** -0.5

    def block(qb):
        s = jnp.einsum('bqhgd,bkhd->bhgqk', qb, k).astype(jnp.float32) * scale
        p = jax.nn.softmax(s, axis=-1).astype(v.dtype)
        return jnp.einsum('bhgqk,bkhd->bqhgd', p, v)

    return sweep_query_blocks(block, q)


def diff_attend(q, k, v, lam):
    scale = q.shape[-1] ** -0.5

    def block(qb):
        s = jnp.einsum('bqhmd,bkhmd->bhmqk', qb, k).astype(jnp.float32) * scale
        p = jax.nn.softmax(s, axis=-1)
        a = (p[:, :, 0] - lam * p[:, :, 1]).astype(v.dtype)
        return jnp.einsum('bhqk,bkhe->bqhe', a, v)

    return sweep_query_blocks(block, q)


def ssd_scan(x, dt, A, Bm, Cm, init):
    f32 = jnp.float32
    b, L, h, p = x.shape
    g, n = Bm.shape[2], Bm.shape[3]
    nc = L // SSD_CHUNK
    rep = h // g
    dtf = dt.astype(f32)
    Bh = jnp.repeat(Bm.astype(f32), rep, axis=2).reshape(b, nc, SSD_CHUNK, h, n)
    Ch = jnp.repeat(Cm.astype(f32), rep, axis=2).reshape(b, nc, SSD_CHUNK, h, n)
    xdt = (x.astype(f32) * dtf[..., None]).reshape(b, nc, SSD_CHUNK, h, p)
    cs = jnp.cumsum((dtf * A).reshape(b, nc, SSD_CHUNK, h), axis=2)
    mask = jnp.tril(jnp.ones((SSD_CHUNK, SSD_CHUNK), dtype=bool))[None, None, :, :, None]
    decay_in = jnp.exp(jnp.where(mask, cs[:, :, :, None, :] - cs[:, :, None, :, :], -jnp.inf))
    scores = jnp.einsum('bclhn,bcshn->bclsh', Ch, Bh) * decay_in
    y_diag = jnp.einsum('bclsh,bcshp->bclhp', scores, xdt)
    decay_to_end = jnp.exp(cs[:, :, -1:, :] - cs)
    chunk_states = jnp.einsum('bclhn,bclh,bclhp->bchpn', Bh, decay_to_end, xdt)
    chunk_decay = jnp.exp(cs[:, :, -1, :])

    def step(state, inp):
        dec, new = inp
        return state * dec[:, :, None, None] + new, state

    final, entering = lax.scan(step, init.astype(f32),
                               (jnp.moveaxis(chunk_decay, 1, 0), jnp.moveaxis(chunk_states, 1, 0)))
    entering = jnp.moveaxis(entering, 0, 1)
    y_off = jnp.einsum('bclhn,bchpn,bclh->bclhp', Ch, entering, jnp.exp(cs))
    return (y_diag + y_off).reshape(b, L, h, p), final


def mixer(h, lw, lam_init, rope=None, ctx=None):
    nb, L, _ = h.shape
    offsets = [int(v) for v in np.cumsum(PROJ_SIZES)[:-1]]
    aq, ak, av, bq, bk, bv, cx, cz, cB, cC, cdt = jnp.split(h @ lw['w_in'], offsets, axis=-1)
    aq = rmsnorm(aq.reshape(nb, L, A_Q_HEADS, HEAD_DIM), lw['a_q_norm'])
    ak = rmsnorm(ak.reshape(nb, L, A_KV_HEADS, HEAD_DIM), lw['a_k_norm'])
    av = av.reshape(nb, L, A_KV_HEADS, HEAD_DIM)
    bq = bq.reshape(nb, L, B_HEADS, 2, B_HALF)
    bk = bk.reshape(nb, L, B_HEADS, 2, B_HALF)
    bv = bv.reshape(nb, L, B_HEADS, 2 * B_HALF)
    lv = lw['b_lambda'].astype(jnp.float32)
    lam = jnp.exp(jnp.sum(lv[0] * lv[1])) - jnp.exp(jnp.sum(lv[2] * lv[3])) + lam_init
    if rope is not None:
        cos_a, sin_a, cos_b, sin_b = rope
        aq = apply_rope(aq, cos_a[:, None, :], sin_a[:, None, :])
        ak = apply_rope(ak, cos_a[:, None, :], sin_a[:, None, :])
        bq = apply_rope(bq, cos_b[:, None, None, :], sin_b[:, None, None, :])
        bk = apply_rope(bk, cos_b[:, None, None, :], sin_b[:, None, None, :])
    if ctx is None:
        keys_a, vals_a, keys_b, vals_b = ak, av, bk, bv
        init = jnp.zeros((nb, 2, C_HEADS, C_HEADDIM, C_STATE), jnp.float32)
    else:
        cak, cav, cbk, cbv, init = ctx
        keys_a = jnp.concatenate([ak, cak.astype(ak.dtype)], axis=1)
        vals_a = jnp.concatenate([av, cav.astype(av.dtype)], axis=1)
        keys_b = jnp.concatenate([bk, cbk.astype(bk.dtype)], axis=1)
        vals_b = jnp.concatenate([bv, cbv.astype(bv.dtype)], axis=1)
    a_out = gqa_attend(aq.reshape(nb, L, A_KV_HEADS, A_GROUP, HEAD_DIM), keys_a, vals_a)
    a_out = a_out.reshape(nb, L, A_WIDTH)
    b_o = diff_attend(bq, keys_b, vals_b, lam)
    b_out = (rmsnorm(b_o, lw['b_subln']) * (1.0 - lam_init)).reshape(nb, L, B_WIDTH)
    xbc = jax.nn.silu(dwconv(jnp.concatenate([cx, cB, cC], axis=-1), lw['ssm_conv_w'], lw['ssm_conv_b']))
    cx, cB, cC = jnp.split(xbc, [C_INNER, C_INNER + C_GROUPS * C_STATE], axis=-1)
    xs = cx.reshape(nb, L, C_HEADS, C_HEADDIM)
    Bm = cB.reshape(nb, L, C_GROUPS, C_STATE)
    Cm = cC.reshape(nb, L, C_GROUPS, C_STATE)
    dt = jax.nn.softplus(cdt.reshape(nb, L, 2, C_HEADS).astype(jnp.float32)
                         + lw['ssm_dt_bias'].astype(jnp.float32))
    A = -jnp.exp(lw['ssm_A_log'].astype(jnp.float32))
    y_f, s_f = ssd_scan(xs, dt[:, :, 0], A[0], Bm, Cm, init[:, 0])
    y_b, s_b = ssd_scan(jnp.flip(xs, 1), jnp.flip(dt[:, :, 1], 1), A[1],
                        jnp.flip(Bm, 1), jnp.flip(Cm, 1), init[:, 1])
    y = y_f + jnp.flip(y_b, 1) + xs.astype(jnp.float32) * lw['ssm_D'].astype(jnp.float32)[:, None]
    y = y.reshape(nb, L, C_INNER).astype(h.dtype)
    c_out = rmsnorm(y * jax.nn.silu(cz), lw['ssm_norm_w'])
    out = jnp.concatenate([a_out, b_out, c_out], axis=-1) @ lw['w_out']
    if ctx is None:
        return out, (ak, av, bk, bv, jnp.stack([s_f, s_b], axis=1))
    return out


def conv_ffn(h, lw):
    u = dwconv(h @ lw['ffn_up'], lw['ffn_conv_w'], lw['ffn_conv_b'])
    g, v = jnp.split(u, 2, axis=-1)
    return (jax.nn.silu(g) * v) @ lw['ffn_down']


def setup_inputs(seed: int = 0) -> dict:
    key = jax.random.key(seed)
    ks = jax.random.split(key, 32)

    def nrm(k, shape, s):
        return jax.random.normal(k, shape, jnp.float32) * s

    x_prompt = nrm(ks[0], (BATCH, SEQ, D_MODEL), 1.0)
    x_sample = nrm(ks[1], (DEC_BATCH, DEC_SEQ, D_MODEL), 1.0)
    cache_a_k = nrm(ks[2], (DEC_BATCH, DEPTH, PAST_LEN, A_KV_HEADS, HEAD_DIM), 1.0)
    cache_a_v = nrm(ks[3], (DEC_BATCH, DEPTH, PAST_LEN, A_KV_HEADS, HEAD_DIM), 1.0)
    cache_b_k = nrm(ks[4], (DEC_BATCH, DEPTH, PAST_LEN, B_HEADS, 2, B_HALF), 1.0)
    cache_b_v = nrm(ks[5], (DEC_BATCH, DEPTH, PAST_LEN, B_HEADS, 2 * B_HALF), 1.0)
    state_ssm = nrm(ks[6], (DEC_BATCH, DEPTH, 2, C_HEADS, C_HEADDIM, C_STATE), 0.1)
    c = nrm(ks[7], (DEC_BATCH, D_MODEL), 1.0)
    c_ctx = nrm(ks[8], (D_MODEL,), 1.0)
    ada_w = nrm(ks[9], (DEPTH, D_MODEL, 6 * D_MODEL), 0.5 * D_MODEL ** -0.5)
    ada_b = nrm(ks[10], (DEPTH, 6 * D_MODEL), 0.01)
    w_in = nrm(ks[11], (DEPTH, D_MODEL, PROJ_WIDTH), D_MODEL ** -0.5)
    a_q_norm = 1.0 + nrm(ks[12], (DEPTH, HEAD_DIM), 0.1)
    a_k_norm = 1.0 + nrm(ks[13], (DEPTH, HEAD_DIM), 0.1)
    b_lambda = nrm(ks[14], (DEPTH, 4, B_HALF), 0.1)
    b_subln = 1.0 + nrm(ks[15], (DEPTH, 2 * B_HALF), 0.1)
    ssm_conv_w = nrm(ks[16], (DEPTH, C_CONV, C_CONV_CH), C_CONV ** -0.5)
    ssm_conv_b = nrm(ks[17], (DEPTH, C_CONV_CH), 0.01)
    ssm_A_log = jnp.log(jax.random.uniform(ks[18], (DEPTH, 2, C_HEADS), jnp.float32, 1.0, 16.0))
    dt0 = jnp.exp(jax.random.uniform(ks[19], (DEPTH, 2, C_HEADS), jnp.float32,
                                     math.log(1e-3), math.log(1e-1)))
    ssm_dt_bias = dt0 + jnp.log(-jnp.expm1(-dt0))
    ssm_D = 1.0 + nrm(ks[20], (DEPTH, C_HEADS), 0.1)
    ssm_norm_w = 1.0 + nrm(ks[21], (DEPTH, C_INNER), 0.1)
    w_out = nrm(ks[22], (DEPTH, MIX_WIDTH, D_MODEL), MIX_WIDTH ** -0.5)
    ffn_up = nrm(ks[23], (DEPTH, D_MODEL, 2 * D_FF), D_MODEL ** -0.5)
    ffn_conv_w = nrm(ks[24], (DEPTH, FFN_CONV, 2 * D_FF), FFN_CONV ** -0.5)
    ffn_conv_b = nrm(ks[25], (DEPTH, 2 * D_FF), 0.01)
    ffn_down = nrm(ks[26], (DEPTH, D_FF, D_MODEL), D_FF ** -0.5)
    final_norm_w = 1.0 + nrm(ks[27], (D_MODEL,), 0.1)
    return {"x_prompt": x_prompt, "x_sample": x_sample,
            "cache_a_k": cache_a_k, "cache_a_v": cache_a_v,
            "cache_b_k": cache_b_k, "cache_b_v": cache_b_v, "state_ssm": state_ssm,
            "c": c, "c_ctx": c_ctx, "ada_w": ada_w, "ada_b": ada_b, "w_in": w_in,
            "a_q_norm": a_q_norm, "a_k_norm": a_k_norm, "b_lambda": b_lambda, "b_subln": b_subln,
            "ssm_conv_w": ssm_conv_w, "ssm_conv_b": ssm_conv_b, "ssm_A_log": ssm_A_log,
            "ssm_dt_bias": ssm_dt_bias, "ssm_D": ssm_D, "ssm_norm_w": ssm_norm_w,
            "w_out": w_out, "ffn_up": ffn_up, "ffn_conv_w": ffn_conv_w, "ffn_conv_b": ffn_conv_b,
            "ffn_down": ffn_down, "final_norm_w": final_norm_w}


def reference(x_prompt, x_sample, cache_a_k, cache_a_v, cache_b_k, cache_b_v, state_ssm, c,
              c_ctx, ada_w, ada_b, w_in, a_q_norm, a_k_norm, b_lambda, b_subln,
              ssm_conv_w, ssm_conv_b, ssm_A_log, ssm_dt_bias, ssm_D, ssm_norm_w,
              w_out, ffn_up, ffn_conv_w, ffn_conv_b, ffn_down, final_norm_w):
    L_lat = x_sample.shape[1]
    cos_a, sin_a = axial_rope_tables(L_lat, HEAD_DIM)
    cos_b, sin_b = axial_rope_tables(L_lat, B_HALF)
    rope = (cos_a, sin_a, cos_b, sin_b)
    yp, ys = x_prompt, x_sample
    ak_l, av_l, bk_l, bv_l, st_l = [], [], [], [], []
    for l in range(DEPTH):
        lam_init = 0.8 - 0.6 * math.exp(-0.3 * l)
        lw = {'w_in': w_in[l], 'a_q_norm': a_q_norm[l], 'a_k_norm': a_k_norm[l],
              'b_lambda': b_lambda[l], 'b_subln': b_subln[l],
              'ssm_conv_w': ssm_conv_w[l], 'ssm_conv_b': ssm_conv_b[l], 'ssm_A_log': ssm_A_log[l],
              'ssm_dt_bias': ssm_dt_bias[l], 'ssm_D': ssm_D[l], 'ssm_norm_w': ssm_norm_w[l],
              'w_out': w_out[l], 'ffn_up': ffn_up[l], 'ffn_conv_w': ffn_conv_w[l],
              'ffn_conv_b': ffn_conv_b[l], 'ffn_down': ffn_down[l]}
        sh1, sc1, g1, sh2, sc2, g2 = modulation(c_ctx[None, :], ada_w[l], ada_b[l])
        mix, (ka, va, kb, vb, st) = mixer(adanorm(yp, sh1, sc1), lw, lam_init)
        yp = yp + g1 * mix
        yp = yp + g2 * conv_ffn(adanorm(yp, sh2, sc2), lw)
        ak_l.append(ka)
        av_l.append(va)
        bk_l.append(kb)
        bv_l.append(vb)
        st_l.append(st)
        sh1, sc1, g1, sh2, sc2, g2 = modulation(c, ada_w[l], ada_b[l])
        ctx = (cache_a_k[:, l], cache_a_v[:, l], cache_b_k[:, l], cache_b_v[:, l], state_ssm[:, l])
        ys = ys + g1 * mixer(adanorm(ys, sh1, sc1), lw, lam_init, rope, ctx)
        ys = ys + g2 * conv_ffn(adanorm(ys, sh2, sc2), lw)
    y_prompt = rmsnorm(yp, final_norm_w)
    y_sample = rmsnorm(ys, final_norm_w)
    new_a_k = jnp.stack(ak_l, axis=1)
    new_a_v = jnp.stack(av_l, axis=1)
    new_b_k = jnp.stack(bk_l, axis=1)
    new_b_v = jnp.stack(bv_l, axis=1)
    new_ssm = jnp.stack(st_l, axis=1)
    return (y_prompt, y_sample, new_a_k, new_a_v, new_b_k, new_b_v, new_ssm)
```

```python
import functools
import math

import jax
import jax.numpy as jnp
from jax import lax
from jax.experimental import pallas as pl
from jax.experimental.pallas import tpu as pltpu

F32 = jnp.float32
BF16 = jnp.bfloat16

LANES = 128
VMEM_LIMIT_BYTES = 56 * 1024 * 1024

D_MODEL = 1024
GRID_W = 64
SSD_CHUNK = 128
ROPE_THETA = 10000.0
EPS = 1e-6
HEAD_DIM = 64
A_Q_HEADS = 6
A_KV_HEADS = 2
A_GROUP = A_Q_HEADS // A_KV_HEADS
A_WIDTH = A_Q_HEADS * HEAD_DIM
B_HEADS = 4
B_HALF = 32
B_WIDTH = B_HEADS * 2 * B_HALF
C_HEADS = 6
C_HEADDIM = 64
C_INNER = C_HEADS * C_HEADDIM
C_GROUPS = 2
C_STATE = 128
C_CONV_CH = C_INNER + 2 * C_GROUPS * C_STATE
D_FF = 2816
PROJ_WIDTH = A_WIDTH + 4 * HEAD_DIM + 3 * B_WIDTH + 2 * C_INNER + 4 * C_STATE + 2 * C_HEADS
PROJ_PAD = ((PROJ_WIDTH + 2 * LANES - 1) // (2 * LANES)) * (2 * LANES)

OFF_AQ = 0
OFF_AK = OFF_AQ + A_WIDTH
OFF_AV = OFF_AK + A_KV_HEADS * HEAD_DIM
OFF_BQ = OFF_AV + A_KV_HEADS * HEAD_DIM
OFF_BK = OFF_BQ + B_WIDTH
OFF_BV = OFF_BK + B_WIDTH
OFF_CX = OFF_BV + B_WIDTH
OFF_CZ = OFF_CX + C_INNER
OFF_CB = OFF_CZ + C_INNER
OFF_CC = OFF_CB + C_GROUPS * C_STATE
OFF_DT = OFF_CC + C_GROUPS * C_STATE

MOD_BLOCK = 1024
PROJ_TM = 512
FFN_TM = 512
FFN_FC = 256
ATTN_TQ = 256
ATTN_KC = 512
ROW_BLOCK = 256
NEG_BIG = -1e30


def _cparams(n_grid):
    return pltpu.CompilerParams(
        dimension_semantics=("parallel",) * n_grid,
        vmem_limit_bytes=VMEM_LIMIT_BYTES,
    )


def _sigmoid(x):
    return 1.0 / (1.0 + jnp.exp(-x))


def _left_mask():
    lane = lax.broadcasted_iota(jnp.int32, (1, LANES), 1)
    return lane < (LANES // 2)


def _pair_rmsnorm(y, w):
    left = _left_mask()
    sq = y * y
    s_l = jnp.sum(jnp.where(left, sq, 0.0), axis=-1, keepdims=True)
    s_r = jnp.sum(jnp.where(left, 0.0, sq), axis=-1, keepdims=True)
    ms = jnp.where(left, s_l, s_r) * (1.0 / HEAD_DIM)
    return y * lax.rsqrt(ms + EPS) * w


def _rope(y, cos, s_up, s_dn, quarter):
    return (y * cos
            + pltpu.roll(y, LANES - quarter, 1) * s_up
            + pltpu.roll(y, quarter, 1) * s_dn)


def _modulation_kernel(c_ref, w_ref, b_ref, o_ref):
    c = c_ref[...]
    s = c * _sigmoid(c)
    o_ref[0] = jnp.dot(s.astype(BF16), w_ref[0].astype(BF16),
                       preferred_element_type=F32) + b_ref[0]


def _modulation(cvecs, ada_w, ada_b):
    depth, d, width = ada_w.shape
    rows = cvecs.shape[0]
    return pl.pallas_call(
        _modulation_kernel,
        out_shape=jax.ShapeDtypeStruct((depth, rows, width), F32),
        grid=(depth, width // MOD_BLOCK),
        in_specs=[
            pl.BlockSpec((rows, d), lambda l, j: (0, 0)),
            pl.BlockSpec((1, d, MOD_BLOCK), lambda l, j: (l, 0, j)),
            pl.BlockSpec((1, 1, MOD_BLOCK), lambda l, j: (l, 0, j)),
        ],
        out_specs=pl.BlockSpec((1, rows, MOD_BLOCK), lambda l, j: (l, 0, j)),
        compiler_params=_cparams(2),
        name="modulation",
    )(cvecs, ada_w, ada_b.reshape(depth, 1, width))


def _proj_kernel(*refs, rope, kv_f32):
    x_ref, mod_ref, w_ref, qn_ref, kn_ref = refs[:5]
    pos = 5
    if rope:
        ca_ref, sau_ref, sad_ref, cb_ref, sbu_ref, sbd_ref = refs[pos:pos + 6]
        pos += 6
    (qa_o, k3_o, v3_o, qb_o, kb_o, vb_o, xbc_o, z_o, dt_o) = refs[pos:pos + 9]
    pos += 9
    if kv_f32:
        ka32_o, va32_o, kb32_o, vb32_o = refs[pos:pos + 4]
        pos += 4
    p_scr = refs[pos]

    x = x_ref[0]
    ms = jnp.mean(x * x, axis=-1, keepdims=True)
    shift = mod_ref[0, 0:1, :]
    scale = mod_ref[0, 1:2, :]
    h = (x * lax.rsqrt(ms + EPS)) * (1.0 + scale) + shift
    p_scr[...] = jnp.dot(h.astype(BF16), w_ref[...], preferred_element_type=F32)

    left = _left_mask()

    def rope_a(y):
        if not rope:
            return y
        return _rope(y, ca_ref[...], sau_ref[...], sad_ref[...], HEAD_DIM // 4)

    def rope_b(y):
        if not rope:
            return y
        return _rope(y, cb_ref[...], sbu_ref[...], sbd_ref[...], B_HALF // 4)

    a_scale = HEAD_DIM ** -0.5
    for j in range(A_WIDTH // LANES):
        y = p_scr[:, OFF_AQ + j * LANES:OFF_AQ + (j + 1) * LANES]
        y = rope_a(_pair_rmsnorm(y, qn_ref[...]))
        qa_o[0, :, j * LANES:(j + 1) * LANES] = (y * a_scale).astype(BF16)

    k = rope_a(_pair_rmsnorm(p_scr[:, OFF_AK:OFF_AK + LANES], kn_ref[...]))
    v = p_scr[:, OFF_AV:OFF_AV + LANES]
    if kv_f32:
        ka32_o[0] = k
        va32_o[0] = v
    for src, dst in ((k, k3_o), (v, v3_o)):
        swapped = pltpu.roll(src, LANES // 2, 1)
        dst[0, :, 0:LANES] = jnp.where(left, src, swapped).astype(BF16)
        dst[0, :, LANES:2 * LANES] = src.astype(BF16)
        dst[0, :, 2 * LANES:3 * LANES] = jnp.where(left, swapped, src).astype(BF16)

    b_scale = B_HALF ** -0.5
    for j in range(B_WIDTH // LANES):
        sl = slice(j * LANES, (j + 1) * LANES)
        q = rope_b(p_scr[:, OFF_BQ + j * LANES:OFF_BQ + (j + 1) * LANES])
        qb_o[0, :, sl] = (q * b_scale).astype(BF16)
        kb = rope_b(p_scr[:, OFF_BK + j * LANES:OFF_BK + (j + 1) * LANES])
        vb = p_scr[:, OFF_BV + j * LANES:OFF_BV + (j + 1) * LANES]
        kb_o[0, :, sl] = kb.astype(BF16)
        vb_o[0, :, sl] = vb.astype(BF16)
        if kv_f32:
            kb32_o[0, :, sl] = kb
            vb32_o[0, :, sl] = vb

    xbc_o[0, :, 0:C_INNER] = p_scr[:, OFF_CX:OFF_CX + C_INNER]
    xbc_o[0, :, C_INNER:C_CONV_CH] = p_scr[:, OFF_CB:OFF_CB + 2 * C_GROUPS * C_STATE]
    z_o[0] = p_scr[:, OFF_CZ:OFF_CZ + C_INNER]
    dt_o[0] = p_scr[:, OFF_DT:OFF_DT + LANES]


def _proj(x, mod, w_in_p, qn, kn, rope_tabs, *, per_batch_mod, kv_f32):
    nb, L, d = x.shape
    tm = min(PROJ_TM, L)
    rope = rope_tabs is not None
    mod_map = (lambda b, i: (b, 0, 0)) if per_batch_mod else (lambda b, i: (0, 0, 0))
    row_spec = lambda w: pl.BlockSpec((1, tm, w), lambda b, i: (b, i, 0))
    in_specs = [
        row_spec(d),
        pl.BlockSpec((1, 6, d), mod_map),
        pl.BlockSpec((d, PROJ_PAD), lambda b, i: (0, 0)),
        pl.BlockSpec((1, LANES), lambda b, i: (0, 0)),
        pl.BlockSpec((1, LANES), lambda b, i: (0, 0)),
    ]
    args = [x, mod, w_in_p, qn, kn]
    if rope:
        in_specs += [pl.BlockSpec((tm, LANES), lambda b, i: (i, 0))] * 6
        args += list(rope_tabs)
    widths = [(A_WIDTH, BF16), (A_WIDTH, BF16), (A_WIDTH, BF16), (B_WIDTH, BF16), (B_WIDTH, BF16),
              (B_WIDTH, BF16), (C_CONV_CH, F32), (C_INNER, F32), (LANES, F32)]
    if kv_f32:
        widths += [(LANES, F32), (LANES, F32), (B_WIDTH, F32), (B_WIDTH, F32)]
    out_shape = [jax.ShapeDtypeStruct((nb, L, w), dt) for w, dt in widths]
    out_specs = [row_spec(w) for w, _ in widths]
    return pl.pallas_call(
        functools.partial(_proj_kernel, rope=rope, kv_f32=kv_f32),
        out_shape=out_shape,
        grid=(nb, L // tm),
        in_specs=in_specs,
        out_specs=out_specs,
        scratch_shapes=[pltpu.VMEM((tm, PROJ_PAD), F32)],
        compiler_params=_cparams(2),
        name="proj_rope" if rope else "proj",
    )(*args)


def _softmax_pv(qm, chunks, s_ref):
    off = 0
    m_part = None
    for k_fn, _, w in chunks:
        s = lax.dot_general(qm, k_fn(), (((1,), (1,)), ((), ())), preferred_element_type=F32)
        s_ref[:, off:off + w] = s
        for t in range(w // LANES):
            piece = s[:, t * LANES:(t + 1) * LANES]
            m_part = piece if m_part is None else jnp.maximum(m_part, piece)
        off += w
    m = jnp.max(m_part, axis=-1, keepdims=True)
    off = 0
    l_part = None
    acc = None
    for _, v_fn, w in chunks:
        e = jnp.exp(s_ref[:, off:off + w] - m)
        for t in range(w // LANES):
            piece = e[:, t * LANES:(t + 1) * LANES]
            l_part = piece if l_part is None else l_part + piece
        pv = jnp.dot(e.astype(BF16), v_fn(), preferred_element_type=F32)
        acc = pv if acc is None else acc + pv
        off += w
    return acc, jnp.sum(l_part, axis=-1, keepdims=True)


def _attn_kernel(*refs, has_cache, lam_init, seq_len):
    qa_ref, k3_ref, v3_ref, qb_ref, kb_ref, vb_ref, lamv_ref, subln_ref = refs[:8]
    pos = 8
    if has_cache:
        cka_ref, cva_ref, ckb_ref, cvb_ref = refs[pos:pos + 4]
        pos += 4
    ao_ref, bo_ref, s_scr = refs[pos:pos + 3]

    left = _left_mask()
    lane = lax.broadcasted_iota(jnp.int32, (1, LANES), 1)
    kc = min(ATTN_KC, seq_len)
    n_kc = seq_len // kc
    call_idx = [0]

    def next_scratch():
        call_idx[0] += 1
        return s_scr.at[call_idx[0] % 2]

    def new_chunks(k_ref, v_ref, j):
        sl = slice(j * LANES, (j + 1) * LANES)
        out = []
        for c in range(n_kc):
            rows = slice(c * kc, (c + 1) * kc)
            out.append((functools.partial(lambda r, s: k_ref[0, r, s], rows, sl),
                        functools.partial(lambda r, s: v_ref[0, r, s], rows, sl), kc))
        return out

    if has_cache:
        ck = cka_ref[0, 0]
        cv = cva_ref[0, 0]
        ck_sw = pltpu.roll(ck, LANES // 2, 1)
        cv_sw = pltpu.roll(cv, LANES // 2, 1)
        ck3 = [jnp.where(left, ck, ck_sw), ck, jnp.where(left, ck_sw, ck)]
        cv3 = [jnp.where(left, cv, cv_sw), cv, jnp.where(left, cv_sw, cv)]
        past = ck.shape[0]
    for j in range(A_WIDTH // LANES):
        q = qa_ref[0, :, j * LANES:(j + 1) * LANES]
        chunks = new_chunks(k3_ref, v3_ref, j)
        if has_cache:
            ckj = ck3[j].astype(BF16)
            cvj = cv3[j].astype(BF16)
            chunks.append((lambda a=ckj: a, lambda a=cvj: a, past))
        halves = []
        for e in range(2):
            qm = jnp.where(left if e == 0 else jnp.logical_not(left), q, jnp.zeros_like(q))
            acc, l = _softmax_pv(qm, chunks, next_scratch())
            halves.append(acc * (1.0 / l))
        ao_ref[0, :, j * LANES:(j + 1) * LANES] = jnp.where(left, halves[0], halves[1]).astype(BF16)

    lv = lamv_ref[...]
    lam = (jnp.exp(jnp.sum(lv[0:1] * lv[1:2], axis=-1, keepdims=True))
           - jnp.exp(jnp.sum(lv[2:3] * lv[3:4], axis=-1, keepdims=True)) + lam_init)
    for j in range(B_WIDTH // LANES):
        sl = slice(j * LANES, (j + 1) * LANES)
        q = qb_ref[0, :, sl]
        chunks = new_chunks(kb_ref, vb_ref, j)
        if has_cache:
            ckj = ckb_ref[0, 0, :, sl].astype(BF16)
            cvj = cvb_ref[0, 0, :, sl].astype(BF16)
            chunks.append((lambda a=ckj: a, lambda a=cvj: a, ckj.shape[0]))
        halves = []
        for e in range(2):
            maps = []
            for mi in range(2):
                sel = (lane // B_HALF) == (2 * e + mi)
                qm = jnp.where(sel, q, jnp.zeros_like(q))
                acc, l = _softmax_pv(qm, chunks, next_scratch())
                maps.append(acc * (1.0 / l))
            halves.append(maps[0] - lam * maps[1])
        o = jnp.where(left, halves[0], halves[1])
        o = _pair_rmsnorm(o, subln_ref[...]) * (1.0 - lam_init)
        bo_ref[0, :, sl] = o.astype(BF16)


def _attn(qa, k3, v3, qb, kb, vb, lamv, subln, caches, layer, *, lam_init):
    nb, L, _ = qa.shape
    tq = min(ATTN_TQ, L)
    has_cache = caches is not None
    q_spec = lambda w: pl.BlockSpec((1, tq, w), lambda b, i: (b, i, 0))
    kv_spec = lambda w: pl.BlockSpec((1, L, w), lambda b, i: (b, 0, 0))
    in_specs = [q_spec(A_WIDTH), kv_spec(A_WIDTH), kv_spec(A_WIDTH),
                q_spec(B_WIDTH), kv_spec(B_WIDTH), kv_spec(B_WIDTH),
                pl.BlockSpec((4, B_HALF), lambda b, i: (0, 0)),
                pl.BlockSpec((1, LANES), lambda b, i: (0, 0))]
    args = [qa, k3, v3, qb, kb, vb, lamv, subln]
    lk = L
    if has_cache:
        past = caches[0].shape[2]
        lk += past
        for c in caches:
            in_specs.append(pl.BlockSpec((1, 1, past, c.shape[-1]), lambda b, i: (b, layer, 0, 0)))
            args.append(c)
    return pl.pallas_call(
        functools.partial(_attn_kernel, has_cache=has_cache, lam_init=lam_init, seq_len=L),
        out_shape=[jax.ShapeDtypeStruct((nb, L, A_WIDTH), BF16),
                   jax.ShapeDtypeStruct((nb, L, B_WIDTH), BF16)],
        grid=(nb, L // tq),
        in_specs=in_specs,
        out_specs=[q_spec(A_WIDTH), q_spec(B_WIDTH)],
        scratch_shapes=[pltpu.VMEM((2, tq, lk), F32)],
        compiler_params=_cparams(2),
        name="attn_cache" if has_cache else "attn",
    )(*args)


def _prefix_sums(tri, a):
    hi = a.astype(BF16)
    lo = (a - hi.astype(F32)).astype(BF16)
    return (jnp.dot(tri, hi, preferred_element_type=F32)
            + jnp.dot(tri, lo, preferred_element_type=F32))


def _pair_cols(x, c0):
    return jnp.where(_left_mask(), x[:, c0:c0 + 1], x[:, c0 + 1:c0 + 2])


def _ssd_kernel(*refs, has_init, emit_state, seq_len):
    xbc_ref, z_ref, dt_ref, cw_ref, cb_ref, dtb_ref, alog_ref, dexp_ref, nw_ref = refs[:9]
    pos = 9
    if has_init:
        init_ref = refs[pos]
        pos += 1
    co_ref = refs[pos]
    pos += 1
    if emit_state:
        st_ref = refs[pos]
        pos += 1
    xc_scr, y_scr, st_scr = refs[pos:pos + 3]

    L = seq_len
    rb = min(ROW_BLOCK, L)
    q = SSD_CHUNK
    nc = L // q
    left = _left_mask()

    w0 = cw_ref[0:1, :]
    w1 = cw_ref[1:2, :]
    w2 = cw_ref[2:3, :]
    row = lax.broadcasted_iota(jnp.int32, (rb, 1), 0)
    for r0 in range(0, L, rb):
        cur = xbc_ref[0, r0:r0 + rb, :]
        prev = pltpu.roll(cur, 1, 0)
        nxt = pltpu.roll(cur, rb - 1, 0)
        edge_prev = xbc_ref[0, r0 - 1:r0, :] if r0 > 0 else jnp.zeros((1, C_CONV_CH), F32)
        edge_next = xbc_ref[0, r0 + rb:r0 + rb + 1, :] if r0 + rb < L else jnp.zeros((1, C_CONV_CH), F32)
        prev = jnp.where(row == 0, edge_prev, prev)
        nxt = jnp.where(row == rb - 1, edge_next, nxt)
        y = prev * w0 + cur * w1 + nxt * w2 + cb_ref[...]
        xc = y * _sigmoid(y)
        xc_scr[r0:r0 + rb, :] = xc
        y_scr[r0:r0 + rb, :] = xc[:, 0:C_INNER] * dexp_ref[...]

    for d in range(2):
        if has_init:
            st_scr[d] = init_ref[0, 0, d].reshape(C_INNER, C_STATE).T
        else:
            st_scr[d] = jnp.zeros((C_STATE, C_INNER), F32)

    a_neg = -jnp.exp(alog_ref[...])
    ti = lax.broadcasted_iota(jnp.int32, (q, q), 0)
    si = lax.broadcasted_iota(jnp.int32, (q, q), 1)
    tri_incl = (si <= ti).astype(BF16)

    def process(d, r0):
        rows = pl.ds(r0, q)
        lo = d * C_HEADS
        causal = (si <= ti) if d == 0 else (si >= ti)
        xs = xc_scr[rows, 0:C_INNER]
        bm = [xc_scr[rows, C_INNER + g * C_STATE:C_INNER + (g + 1) * C_STATE] for g in range(C_GROUPS)]
        cm = [xc_scr[rows, C_INNER + (C_GROUPS + g) * C_STATE:C_INNER + (C_GROUPS + g + 1) * C_STATE]
              for g in range(C_GROUPS)]
        bm_b = [b.astype(BF16) for b in bm]
        cm_b = [c.astype(BF16) for c in cm]
        bmt_b = [b.T.astype(BF16) for b in bm]
        gmat = [lax.dot_general(cm_b[g], bm_b[g], (((1,), (1,)), ((), ())), preferred_element_type=F32)
                for g in range(C_GROUPS)]

        dtr = dt_ref[0, rows, :] + dtb_ref[...]
        dt = jnp.maximum(dtr, 0.0) + jnp.log1p(jnp.exp(-jnp.abs(dtr)))
        a = dt * a_neg
        cs = _prefix_sums(tri_incl, a)
        tot = cs[q - 1:q, :]
        ev = cs if d == 0 else (tot - cs + a)
        e_in = jnp.exp(ev)
        e_out = jnp.exp(tot - ev)
        dec = jnp.exp(tot)
        ev_t = ev.T

        st = st_scr.at[d]
        for j in range(C_INNER // LANES):
            sl = slice(j * LANES, (j + 1) * LANES)
            h0 = 2 * j
            g0, g1 = h0 // (C_HEADS // C_GROUPS), (h0 + 1) // (C_HEADS // C_GROUPS)
            xdt = xs[:, sl] * _pair_cols(dt, lo + h0)
            xdt_b = xdt.astype(BF16)
            st_blk = st[:, sl]
            st_b = st_blk.astype(BF16)
            yd = []
            for e in range(2):
                hh = lo + h0 + e
                g = g0 if e == 0 else g1
                diff = ev[:, hh:hh + 1] - ev_t[hh:hh + 1, :]
                lmat = jnp.exp(jnp.where(causal, diff, NEG_BIG))
                yd.append(jnp.dot((gmat[g] * lmat).astype(BF16), xdt_b, preferred_element_type=F32))
            yo0 = jnp.dot(cm_b[g0], st_b, preferred_element_type=F32)
            yo1 = yo0 if g1 == g0 else jnp.dot(cm_b[g1], st_b, preferred_element_type=F32)
            y_blk = (jnp.where(left, yd[0], yd[1])
                     + jnp.where(left, yo0, yo1) * _pair_cols(e_in, lo + h0))
            y_scr[rows, sl] = y_scr[rows, sl] + y_blk
            xw = (xdt * _pair_cols(e_out, lo + h0)).astype(BF16)
            up0 = jnp.dot(bmt_b[g0], xw, preferred_element_type=F32)
            up1 = up0 if g1 == g0 else jnp.dot(bmt_b[g1], xw, preferred_element_type=F32)
            st[:, sl] = st_blk * _pair_cols(dec, lo + h0) + jnp.where(left, up0, up1)

    if nc <= 2:
        for c in range(nc):
            process(0, c * q)
            process(1, (nc - 1 - c) * q)
    else:
        def body(c, carry):
            process(0, pl.multiple_of(c * q, q))
            process(1, pl.multiple_of((nc - 1 - c) * q, q))
            return carry
        lax.fori_loop(0, nc, body, 0)

    if emit_state:
        for d in range(2):
            st_ref[0, d] = st_scr[d].T.reshape(C_HEADS, C_HEADDIM, C_STATE)

    for r0 in range(0, L, rb):
        zz = z_ref[0, r0:r0 + rb, :]
        yg = y_scr[r0:r0 + rb, :] * (zz * _sigmoid(zz))
        ms = jnp.mean(yg * yg, axis=-1, keepdims=True)
        co_ref[0, r0:r0 + rb, :] = (yg * lax.rsqrt(ms + EPS) * nw_ref[...]).astype(BF16)


def _ssd(xbc, z, dt, conv_w, conv_b, dtb, alog, dexp, norm_w, state, layer, *, emit_state):
    nb, L, _ = xbc.shape
    has_init = state is not None
    seq_spec = lambda w: pl.BlockSpec((1, L, w), lambda b: (b, 0, 0))
    par_spec = lambda r, w: pl.BlockSpec((r, w), lambda b: (0, 0))
    in_specs = [seq_spec(C_CONV_CH), seq_spec(C_INNER), seq_spec(LANES),
                par_spec(3, C_CONV_CH), par_spec(1, C_CONV_CH), par_spec(1, LANES), par_spec(1, LANES),
                par_spec(1, C_INNER), par_spec(1, C_INNER)]
    args = [xbc, z, dt, conv_w, conv_b, dtb, alog, dexp, norm_w]
    if has_init:
        in_specs.append(pl.BlockSpec((1, 1, 2, C_HEADS, C_HEADDIM, C_STATE),
                                     lambda b: (b, layer, 0, 0, 0, 0)))
        args.append(state)
    out_shape = [jax.ShapeDtypeStruct((nb, L, C_INNER), BF16)]
    out_specs = [seq_spec(C_INNER)]
    if emit_state:
        out_shape.append(jax.ShapeDtypeStruct((nb, 2, C_HEADS, C_HEADDIM, C_STATE), F32))
        out_specs.append(pl.BlockSpec((1, 2, C_HEADS, C_HEADDIM, C_STATE), lambda b: (b, 0, 0, 0, 0)))
    return pl.pallas_call(
        functools.partial(_ssd_kernel, has_init=has_init, emit_state=emit_state, seq_len=L),
        out_shape=out_shape,
        grid=(nb,),
        in_specs=in_specs,
        out_specs=out_specs,
        scratch_shapes=[pltpu.VMEM((L, C_CONV_CH), F32), pltpu.VMEM((L, C_INNER), F32),
                        pltpu.VMEM((2, C_STATE, C_INNER), F32)],
        compiler_params=_cparams(1),
        name="ssd_init" if has_init else "ssd",
    )(*args)


def _outp_kernel(a_ref, b_ref, c_ref, x_ref, mod_ref, w_ref, x1_o, h2_o):
    mix = jnp.concatenate([a_ref[0], b_ref[0], c_ref[0]], axis=-1)
    o = jnp.dot(mix, w_ref[...], preferred_element_type=F32)
    gate1 = mod_ref[0, 2:3, :]
    shift2 = mod_ref[0, 3:4, :]
    scale2 = mod_ref[0, 4:5, :]
    x1 = x_ref[0] + gate1 * o
    ms = jnp.mean(x1 * x1, axis=-1, keepdims=True)
    x1_o[0] = x1
    h2_o[0] = ((x1 * lax.rsqrt(ms + EPS)) * (1.0 + scale2) + shift2).astype(BF16)


def _outp(a_out, b_out, c_out, x, mod, w_out_b, *, per_batch_mod):
    nb, L, d = x.shape
    tm = min(PROJ_TM, L)
    mod_map = (lambda b, i: (b, 0, 0)) if per_batch_mod else (lambda b, i: (0, 0, 0))
    row_spec = lambda w: pl.BlockSpec((1, tm, w), lambda b, i: (b, i, 0))
    return pl.pallas_call(
        _outp_kernel,
        out_shape=[jax.ShapeDtypeStruct((nb, L, d), F32), jax.ShapeDtypeStruct((nb, L, d), BF16)],
        grid=(nb, L // tm),
        in_specs=[row_spec(A_WIDTH), row_spec(B_WIDTH), row_spec(C_INNER), row_spec(d),
                  pl.BlockSpec((1, 6, d), mod_map),
                  pl.BlockSpec((d, d), lambda b, i: (0, 0))],
        out_specs=[row_spec(d), row_spec(d)],
        compiler_params=_cparams(2),
        name="outp",
    )(a_out, b_out, c_out, x, mod, w_out_b)


HALO = 16


def _ffn_kernel(h_ref, hp_ref, hn_ref, x1_ref, mod_ref, wu_ref, cw_ref, cb_ref, wd_ref, fw_ref,
                o_ref, acc_scr, *, seq_len, final_norm, n_tiles):
    tm = h_ref.shape[1]
    i = pl.program_id(1)
    h = h_ref[0]
    h_prev = hp_ref[0]
    h_next = hn_ref[0]
    row = lax.broadcasted_iota(jnp.int32, (tm, 1), 0)
    pos = (row + i * tm) & (seq_len - 1)
    first = pos == 0
    last = pos == seq_len - 1
    n_chunks = wu_ref.shape[0]
    acc_scr[...] = jnp.zeros_like(acc_scr)

    def body(c, carry):
        wu = wu_ref[c]
        u = jnp.dot(h, wu, preferred_element_type=F32)
        u_prev = jnp.dot(h_prev, wu, preferred_element_type=F32)[HALO - 1:HALO, :]
        u_next = jnp.dot(h_next, wu, preferred_element_type=F32)[0:1, :]
        prev = jnp.where(row == 0, u_prev, pltpu.roll(u, 1, 0))
        nxt = jnp.where(row == tm - 1, u_next, pltpu.roll(u, tm - 1, 0))
        prev = jnp.where(first, 0.0, prev)
        nxt = jnp.where(last, 0.0, nxt)
        cw = cw_ref[c]
        y = prev * cw[0:1, :] + u * cw[1:2, :] + nxt * cw[2:3, :] + cb_ref[c]
        g = y[:, 0:FFN_FC]
        v = y[:, FFN_FC:2 * FFN_FC]
        act = (g * _sigmoid(g) * v).astype(BF16)
        acc_scr[...] += jnp.dot(act, wd_ref[c], preferred_element_type=F32)
        return carry

    lax.fori_loop(0, n_chunks, body, 0)
    gate2 = mod_ref[0, 5:6, :]
    out = x1_ref[0] + gate2 * acc_scr[...]
    if final_norm:
        ms = jnp.mean(out * out, axis=-1, keepdims=True)
        out = out * lax.rsqrt(ms + EPS) * fw_ref[...]
    o_ref[0] = out


def _ffn(h2, x1, mod, wu_c, cw_c, cb_c, wd_c, final_w, *, per_batch_mod, seq_len, final_norm):
    nb, L, d = x1.shape
    tm = min(FFN_TM, L)
    n_tiles = L // tm
    assert seq_len & (seq_len - 1) == 0 and (L % seq_len == 0) and (seq_len % tm == 0 or tm % seq_len == 0)
    n_chunks = wu_c.shape[0]
    hb = tm // HALO
    n_hblk = L // HALO
    mod_map = (lambda b, i: (b, 0, 0)) if per_batch_mod else (lambda b, i: (0, 0, 0))
    row_spec = lambda w: pl.BlockSpec((1, tm, w), lambda b, i: (b, i, 0))
    const3 = lambda s: pl.BlockSpec(s, lambda b, i: (0, 0, 0))
    return pl.pallas_call(
        functools.partial(_ffn_kernel, seq_len=seq_len, final_norm=final_norm, n_tiles=n_tiles),
        out_shape=jax.ShapeDtypeStruct((nb, L, d), F32),
        grid=(nb, n_tiles),
        in_specs=[row_spec(d),
                  pl.BlockSpec((1, HALO, d), lambda b, i: (b, jnp.maximum(i * hb - 1, 0), 0)),
                  pl.BlockSpec((1, HALO, d), lambda b, i: (b, jnp.minimum((i + 1) * hb, n_hblk - 1), 0)),
                  row_spec(d),
                  pl.BlockSpec((1, 6, d), mod_map),
                  const3((n_chunks, d, 2 * FFN_FC)),
                  const3((n_chunks, 3, 2 * FFN_FC)),
                  const3((n_chunks, 1, 2 * FFN_FC)),
                  const3((n_chunks, FFN_FC, d)),
                  pl.BlockSpec((1, d), lambda b, i: (0, 0))],
        out_specs=row_spec(d),
        scratch_shapes=[pltpu.VMEM((tm, d), F32)],
        compiler_params=_cparams(2),
        name="ffn_final" if final_norm else "ffn",
    )(h2, h2, h2, x1, mod, wu_c, cw_c, cb_c, wd_c, final_w)


def _rope_tables(L, d):
    rows = L // GRID_W
    row = jnp.repeat(jnp.arange(rows), GRID_W).astype(F32)
    col = jnp.tile(jnp.arange(GRID_W), rows).astype(F32)
    quarter = d // 4
    inv = ROPE_THETA ** (-jnp.arange(quarter, dtype=F32) / quarter)
    ang_r = row[:, None] * inv[None, :]
    ang_c = col[:, None] * inv[None, :]
    ang = jnp.concatenate([ang_r, ang_r, ang_c, ang_c], axis=-1)
    cos, sin = jnp.cos(ang), jnp.sin(ang)
    even = ((jnp.arange(d) // quarter) % 2 == 0)[None, :]
    s_up = jnp.where(even, -sin, 0.0)
    s_dn = jnp.where(even, 0.0, sin)
    reps = LANES // d
    return tuple(jnp.tile(t, (1, reps)) for t in (cos, s_up, s_dn))


def _pad_lanes(v, width=LANES):
    v = v.reshape(1, -1).astype(F32)
    return jnp.pad(v, ((0, 0), (0, width - v.shape[1])))


def _layer_params(l, w_in, a_q_norm, a_k_norm, b_lambda, b_subln, ssm_conv_w, ssm_conv_b, ssm_A_log,
                  ssm_dt_bias, ssm_D, ssm_norm_w, w_out, ffn_up, ffn_conv_w, ffn_conv_b, ffn_down):
    n_chunks = D_FF // FFN_FC
    d = w_in.shape[1]

    def gv_chunks(t):
        lead = t.shape[:-1]
        g = t[..., :D_FF].reshape(lead + (n_chunks, FFN_FC))
        v = t[..., D_FF:].reshape(lead + (n_chunks, FFN_FC))
        gv = jnp.concatenate([g, v], axis=-1)
        return jnp.moveaxis(gv, -2, 0)

    return dict(
        w_in=jnp.pad(w_in[l], ((0, 0), (0, PROJ_PAD - PROJ_WIDTH))).astype(BF16),
        qn=jnp.tile(a_q_norm[l], 2).reshape(1, LANES),
        kn=jnp.tile(a_k_norm[l], 2).reshape(1, LANES),
        lamv=b_lambda[l],
        subln=jnp.tile(b_subln[l], 2).reshape(1, LANES),
        conv_w=ssm_conv_w[l],
        conv_b=ssm_conv_b[l].reshape(1, C_CONV_CH),
        dtb=_pad_lanes(ssm_dt_bias[l]),
        alog=_pad_lanes(ssm_A_log[l]),
        dexp=jnp.repeat(ssm_D[l], C_HEADDIM).reshape(1, C_INNER),
        norm_w=ssm_norm_w[l].reshape(1, C_INNER),
        w_out=w_out[l].astype(BF16),
        wu=gv_chunks(ffn_up[l]).astype(BF16),
        cw=gv_chunks(ffn_conv_w[l]),
        cb=gv_chunks(ffn_conv_b[l].reshape(1, 2 * D_FF)),
        wd=ffn_down[l].reshape(n_chunks, FFN_FC, d).astype(BF16),
    )


def _block(x, mod, p, rope_tabs, caches, state, layer, *, seq_shape, per_batch_mod, is_ctx, lam_init,
           final_w, final_norm):
    n_seq, seq_len = seq_shape
    outs = _proj(x, mod, p["w_in"], p["qn"], p["kn"], rope_tabs,
                 per_batch_mod=per_batch_mod, kv_f32=is_ctx)
    seq = lambda t: t.reshape(n_seq, seq_len, t.shape[-1])
    qa, k3, v3, qb, kb, vb, xbc, z, dt = [seq(t) for t in outs[:9]]
    a_out, b_out = _attn(qa, k3, v3, qb, kb, vb, p["lamv"], p["subln"], caches, layer, lam_init=lam_init)
    ssd_out = _ssd(xbc, z, dt, p["conv_w"], p["conv_b"], p["dtb"], p["alog"], p["dexp"], p["norm_w"],
                   state, layer, emit_state=is_ctx)
    c_out = ssd_out[0]
    flat = lambda t: t.reshape(x.shape[0], x.shape[1], t.shape[-1])
    x1, h2 = _outp(flat(a_out), flat(b_out), flat(c_out), x, mod, p["w_out"], per_batch_mod=per_batch_mod)
    x2 = _ffn(h2, x1, mod, p["wu"], p["cw"], p["cb"], p["wd"], final_w,
              per_batch_mod=per_batch_mod, seq_len=seq_len, final_norm=final_norm)
    extras = None
    if is_ctx:
        extras = tuple(seq(t) for t in outs[9:13]) + (ssd_out[1],)
    return x2, extras


def kernel(x_prompt, x_sample, cache_a_k, cache_a_v, cache_b_k, cache_b_v, state_ssm, c, c_ctx, ada_w, ada_b, w_in, a_q_norm, a_k_norm, b_lambda, b_subln, ssm_conv_w, ssm_conv_b, ssm_A_log, ssm_dt_bias, ssm_D, ssm_norm_w, w_out, ffn_up, ffn_conv_w, ffn_conv_b, ffn_down, final_norm_w):
    depth = w_in.shape[0]
    nbp, lp, d = x_prompt.shape
    nbs, ls, _ = x_sample.shape
    past = cache_a_k.shape[2]

    mod_rows = 8
    assert 1 + nbs <= mod_rows
    cvecs = jnp.concatenate([c_ctx[None, :], c, jnp.zeros((mod_rows - 1 - nbs, d), F32)], axis=0)
    mod = _modulation(cvecs, ada_w, ada_b)

    rope_tabs = _rope_tables(ls, HEAD_DIM) + _rope_tables(ls, B_HALF)
    caches = (cache_a_k.reshape(nbs, depth, past, A_KV_HEADS * HEAD_DIM),
              cache_a_v.reshape(nbs, depth, past, A_KV_HEADS * HEAD_DIM),
              cache_b_k.reshape(nbs, depth, past, B_WIDTH),
              cache_b_v.reshape(nbs, depth, past, B_WIDTH))
    final_w = final_norm_w.reshape(1, d)

    yp = x_prompt.reshape(1, nbp * lp, d)
    ys = x_sample
    ctx_extras = []
    for l in range(depth):
        lam_init = 0.8 - 0.6 * math.exp(-0.3 * l)
        p = _layer_params(l, w_in, a_q_norm, a_k_norm, b_lambda, b_subln, ssm_conv_w, ssm_conv_b,
                          ssm_A_log, ssm_dt_bias, ssm_D, ssm_norm_w, w_out, ffn_up, ffn_conv_w,
                          ffn_conv_b, ffn_down)
        last = l == depth - 1
        mod_ctx = mod[l, 0].reshape(1, 6, d)
        mod_lat = mod[l, 1:1 + nbs].reshape(nbs, 6, d)
        yp, extras = _block(yp, mod_ctx, p, None, None, None, l, seq_shape=(nbp, lp),
                            per_batch_mod=False, is_ctx=True, lam_init=lam_init,
                            final_w=final_w, final_norm=last)
        ctx_extras.append(extras)
        ys, _ = _block(ys, mod_lat, p, rope_tabs, caches, state_ssm, l, seq_shape=(nbs, ls),
                       per_batch_mod=True, is_ctx=False, lam_init=lam_init,
                       final_w=final_w, final_norm=last)

    y_prompt = yp.reshape(nbp, lp, d)
    stack = lambda k: jnp.stack([e[k] for e in ctx_extras], axis=1)
    new_a_k = stack(0).reshape(nbp, depth, lp, A_KV_HEADS, HEAD_DIM)
    new_a_v = stack(1).reshape(nbp, depth, lp, A_KV_HEADS, HEAD_DIM)
    new_b_k = stack(2).reshape(nbp, depth, lp, B_HEADS, 2, B_HALF)
    new_b_v = stack(3).reshape(nbp, depth, lp, B_HEADS, 2 * B_HALF)
    new_ssm = stack(4)
    return (y_prompt, ys, new_a_k, new_a_v, new_b_k, new_b_v, new_ssm)
```

```python
import functools
import math

import jax
import jax.numpy as jnp
from jax import lax
from jax.experimental import pallas as pl
from jax.experimental.pallas import tpu as pltpu

F32 = jnp.float32
BF16 = jnp.bfloat16

LANES = 128
VMEM_LIMIT_BYTES = 56 * 1024 * 1024

D_MODEL = 1024
GRID_W = 64
SSD_CHUNK = 128
ROPE_THETA = 10000.0
EPS = 1e-6
HEAD_DIM = 64
A_Q_HEADS = 6
A_KV_HEADS = 2
A_GROUP = A_Q_HEADS // A_KV_HEADS
A_WIDTH = A_Q_HEADS * HEAD_DIM
B_HEADS = 4
B_HALF = 32
B_WIDTH = B_HEADS * 2 * B_HALF
C_HEADS = 6
C_HEADDIM = 64
C_INNER = C_HEADS * C_HEADDIM
C_GROUPS = 2
C_STATE = 128
C_CONV_CH = C_INNER + 2 * C_GROUPS * C_STATE
D_FF = 2816
PROJ_WIDTH = A_WIDTH + 4 * HEAD_DIM + 3 * B_WIDTH + 2 * C_INNER + 4 * C_STATE + 2 * C_HEADS
PROJ_PAD = ((PROJ_WIDTH + 2 * LANES - 1) // (2 * LANES)) * (2 * LANES)

OFF_AQ = 0
OFF_AK = OFF_AQ + A_WIDTH
OFF_AV = OFF_AK + A_KV_HEADS * HEAD_DIM
OFF_BQ = OFF_AV + A_KV_HEADS * HEAD_DIM
OFF_BK = OFF_BQ + B_WIDTH
OFF_BV = OFF_BK + B_WIDTH
OFF_CX = OFF_BV + B_WIDTH
OFF_CZ = OFF_CX + C_INNER
OFF_CB = OFF_CZ + C_INNER
OFF_CC = OFF_CB + C_GROUPS * C_STATE
OFF_DT = OFF_CC + C_GROUPS * C_STATE

MOD_BLOCK = 1024
PROJ_TM = 512
FFN_TM = 512
FFN_FC = 256
FFN_RB = 256
ATTN_TQ = 256
ATTN_KC = 512
ROW_BLOCK = 256
NEG_BIG = -1e30


def _cparams(n_grid):
    return pltpu.CompilerParams(
        dimension_semantics=("parallel",) * n_grid,
        vmem_limit_bytes=VMEM_LIMIT_BYTES,
    )


def _sigmoid(x):
    return 1.0 / (1.0 + jnp.exp(-x))


def _left_mask():
    lane = lax.broadcasted_iota(jnp.int32, (1, LANES), 1)
    return lane < (LANES // 2)


def _pair_rmsnorm(y, w):
    left = _left_mask()
    sq = y * y
    s_l = jnp.sum(jnp.where(left, sq, 0.0), axis=-1, keepdims=True)
    s_r = jnp.sum(jnp.where(left, 0.0, sq), axis=-1, keepdims=True)
    ms = jnp.where(left, s_l, s_r) * (1.0 / HEAD_DIM)
    return y * lax.rsqrt(ms + EPS) * w


def _rope(y, cos, s_up, s_dn, quarter):
    return (y * cos
            + pltpu.roll(y, LANES - quarter, 1) * s_up
            + pltpu.roll(y, quarter, 1) * s_dn)


def _modulation_kernel(c_ref, w_ref, b_ref, o_ref):
    c = c_ref[...]
    s = c * _sigmoid(c)
    o_ref[0] = jnp.dot(s.astype(BF16), w_ref[0].astype(BF16),
                       preferred_element_type=F32) + b_ref[0]


def _modulation(cvecs, ada_w, ada_b):
    depth, d, width = ada_w.shape
    rows = cvecs.shape[0]
    return pl.pallas_call(
        _modulation_kernel,
        out_shape=jax.ShapeDtypeStruct((depth, rows, width), F32),
        grid=(depth, width // MOD_BLOCK),
        in_specs=[
            pl.BlockSpec((rows, d), lambda l, j: (0, 0)),
            pl.BlockSpec((1, d, MOD_BLOCK), lambda l, j: (l, 0, j)),
            pl.BlockSpec((1, 1, MOD_BLOCK), lambda l, j: (l, 0, j)),
        ],
        out_specs=pl.BlockSpec((1, rows, MOD_BLOCK), lambda l, j: (l, 0, j)),
        compiler_params=_cparams(2),
        name="modulation",
    )(cvecs, ada_w, ada_b.reshape(depth, 1, width))


CAST_ROWS = 256


def _cast_kernel(x_ref, o_ref):
    o_ref[...] = x_ref[...].astype(o_ref.dtype)


def _cast_bf16(x):
    depth, rows, cols = x.shape
    spec = pl.BlockSpec((1, CAST_ROWS, cols), lambda l, r: (l, r, 0))
    return pl.pallas_call(
        _cast_kernel, out_shape=jax.ShapeDtypeStruct(x.shape, BF16),
        grid=(depth, rows // CAST_ROWS), in_specs=[spec], out_specs=spec,
        compiler_params=_cparams(2), name="cast_bf16",
    )(x)


def _pad_cast_kernel(x_ref, o_ref):
    rows, w = x_ref.shape[1], x_ref.shape[2]
    aligned = (w // LANES) * LANES
    o_ref[0, :, 0:aligned] = x_ref[0, :, 0:aligned].astype(BF16)
    o_ref[0, :, aligned:] = jnp.zeros((rows, o_ref.shape[2] - aligned), BF16)
    o_ref[0, :, aligned:w] = x_ref[0, :, aligned:w].astype(BF16)


def _pad_cast_bf16(x, width):
    depth, rows, cols = x.shape
    return pl.pallas_call(
        _pad_cast_kernel, out_shape=jax.ShapeDtypeStruct((depth, rows, width), BF16),
        grid=(depth, rows // CAST_ROWS),
        in_specs=[pl.BlockSpec((1, CAST_ROWS, cols), lambda l, r: (l, r, 0))],
        out_specs=pl.BlockSpec((1, CAST_ROWS, width), lambda l, r: (l, r, 0)),
        compiler_params=_cparams(2), name="pad_cast_bf16",
    )(x)


def _gv_cast_kernel(g_ref, v_ref, o_ref):
    o_ref[0, 0, :, 0:FFN_FC] = g_ref[0].astype(BF16)
    o_ref[0, 0, :, FFN_FC:2 * FFN_FC] = v_ref[0].astype(BF16)


def _gv_cast_bf16(up):
    depth, d, _ = up.shape
    n_chunks = D_FF // FFN_FC
    return pl.pallas_call(
        _gv_cast_kernel, out_shape=jax.ShapeDtypeStruct((depth, n_chunks, d, 2 * FFN_FC), BF16),
        grid=(depth, n_chunks),
        in_specs=[pl.BlockSpec((1, d, FFN_FC), lambda l, c: (l, 0, c)),
                  pl.BlockSpec((1, d, FFN_FC), lambda l, c: (l, 0, n_chunks + c))],
        out_specs=pl.BlockSpec((1, 1, d, 2 * FFN_FC), lambda l, c: (l, c, 0, 0)),
        compiler_params=_cparams(2), name="gv_cast_bf16",
    )(up, up)


def _proj_kernel(*refs, rope, kv_f32):
    x_ref, mod_ref, w_ref, qn_ref, kn_ref = refs[:5]
    pos = 5
    if rope:
        ca_ref, sau_ref, sad_ref, cb_ref, sbu_ref, sbd_ref = refs[pos:pos + 6]
        pos += 6
    (qa_o, k3_o, v3_o, qb_o, kb_o, vb_o, xbc_o, z_o, dt_o) = refs[pos:pos + 9]
    pos += 9
    if kv_f32:
        ka32_o, va32_o, kb32_o, vb32_o = refs[pos:pos + 4]
        pos += 4
    p_scr = refs[pos]

    x = x_ref[0]
    ms = jnp.mean(x * x, axis=-1, keepdims=True)
    shift = mod_ref[0, 0:1, :]
    scale = mod_ref[0, 1:2, :]
    h = (x * lax.rsqrt(ms + EPS)) * (1.0 + scale) + shift
    p_scr[...] = jnp.dot(h.astype(BF16), w_ref[...], preferred_element_type=F32)

    left = _left_mask()

    def rope_a(y):
        if not rope:
            return y
        return _rope(y, ca_ref[...], sau_ref[...], sad_ref[...], HEAD_DIM // 4)

    def rope_b(y):
        if not rope:
            return y
        return _rope(y, cb_ref[...], sbu_ref[...], sbd_ref[...], B_HALF // 4)

    a_scale = HEAD_DIM ** -0.5
    for j in range(A_WIDTH // LANES):
        y = p_scr[:, OFF_AQ + j * LANES:OFF_AQ + (j + 1) * LANES]
        y = rope_a(_pair_rmsnorm(y, qn_ref[...]))
        qa_o[0, :, j * LANES:(j + 1) * LANES] = (y * a_scale).astype(BF16)

    k = rope_a(_pair_rmsnorm(p_scr[:, OFF_AK:OFF_AK + LANES], kn_ref[...]))
    v = p_scr[:, OFF_AV:OFF_AV + LANES]
    if kv_f32:
        ka32_o[0] = k
        va32_o[0] = v
    for src, dst in ((k, k3_o), (v, v3_o)):
        swapped = pltpu.roll(src, LANES // 2, 1)
        dst[0, :, 0:LANES] = jnp.where(left, src, swapped).astype(BF16)
        dst[0, :, LANES:2 * LANES] = src.astype(BF16)
        dst[0, :, 2 * LANES:3 * LANES] = jnp.where(left, swapped, src).astype(BF16)

    b_scale = B_HALF ** -0.5
    for j in range(B_WIDTH // LANES):
        sl = slice(j * LANES, (j + 1) * LANES)
        q = rope_b(p_scr[:, OFF_BQ + j * LANES:OFF_BQ + (j + 1) * LANES])
        qb_o[0, :, sl] = (q * b_scale).astype(BF16)
        kb = rope_b(p_scr[:, OFF_BK + j * LANES:OFF_BK + (j + 1) * LANES])
        vb = p_scr[:, OFF_BV + j * LANES:OFF_BV + (j + 1) * LANES]
        kb_o[0, :, sl] = kb.astype(BF16)
        vb_o[0, :, sl] = vb.astype(BF16)
        if kv_f32:
            kb32_o[0, :, sl] = kb
            vb32_o[0, :, sl] = vb

    xbc_o[0, :, 0:C_INNER] = p_scr[:, OFF_CX:OFF_CX + C_INNER]
    xbc_o[0, :, C_INNER:C_CONV_CH] = p_scr[:, OFF_CB:OFF_CB + 2 * C_GROUPS * C_STATE]
    z_o[0] = p_scr[:, OFF_CZ:OFF_CZ + C_INNER]
    dt_o[0] = p_scr[:, OFF_DT:OFF_DT + LANES]


def _proj(x, mod, w_in_p, layer, qn, kn, rope_tabs, *, per_batch_mod, kv_f32):
    nb, L, d = x.shape
    tm = min(PROJ_TM, L)
    rope = rope_tabs is not None
    mod_map = (lambda b, i: (b, 0, 0)) if per_batch_mod else (lambda b, i: (0, 0, 0))
    row_spec = lambda w: pl.BlockSpec((1, tm, w), lambda b, i: (b, i, 0))
    in_specs = [
        row_spec(d),
        pl.BlockSpec((1, 6, d), mod_map),
        pl.BlockSpec((None, d, PROJ_PAD), lambda b, i: (layer, 0, 0)),
        pl.BlockSpec((1, LANES), lambda b, i: (0, 0)),
        pl.BlockSpec((1, LANES), lambda b, i: (0, 0)),
    ]
    args = [x, mod, w_in_p, qn, kn]
    if rope:
        in_specs += [pl.BlockSpec((tm, LANES), lambda b, i: (i, 0))] * 6
        args += list(rope_tabs)
    widths = [(A_WIDTH, BF16), (A_WIDTH, BF16), (A_WIDTH, BF16), (B_WIDTH, BF16), (B_WIDTH, BF16),
              (B_WIDTH, BF16), (C_CONV_CH, F32), (C_INNER, F32), (LANES, F32)]
    if kv_f32:
        widths += [(LANES, F32), (LANES, F32), (B_WIDTH, F32), (B_WIDTH, F32)]
    out_shape = [jax.ShapeDtypeStruct((nb, L, w), dt) for w, dt in widths]
    out_specs = [row_spec(w) for w, _ in widths]
    return pl.pallas_call(
        functools.partial(_proj_kernel, rope=rope, kv_f32=kv_f32),
        out_shape=out_shape,
        grid=(nb, L // tm),
        in_specs=in_specs,
        out_specs=out_specs,
        scratch_shapes=[pltpu.VMEM((tm, PROJ_PAD), F32)],
        compiler_params=_cparams(2),
        name="proj_rope" if rope else "proj",
    )(*args)


def _softmax_pv(qm, chunks, s_ref):
    off = 0
    m_part = None
    for k_fn, _, w in chunks:
        s = lax.dot_general(qm, k_fn(), (((1,), (1,)), ((), ())), preferred_element_type=F32)
        s_ref[:, off:off + w] = s
        for t in range(w // LANES):
            piece = s[:, t * LANES:(t + 1) * LANES]
            m_part = piece if m_part is None else jnp.maximum(m_part, piece)
        off += w
    m = jnp.max(m_part, axis=-1, keepdims=True)
    off = 0
    l_part = None
    acc = None
    for _, v_fn, w in chunks:
        e = jnp.exp(s_ref[:, off:off + w] - m)
        for t in range(w // LANES):
            piece = e[:, t * LANES:(t + 1) * LANES]
            l_part = piece if l_part is None else l_part + piece
        pv = jnp.dot(e.astype(BF16), v_fn(), preferred_element_type=F32)
        acc = pv if acc is None else acc + pv
        off += w
    return acc, jnp.sum(l_part, axis=-1, keepdims=True)


def _attn_kernel(*refs, has_cache, lam_init, seq_len):
    qa_ref, k3_ref, v3_ref, qb_ref, kb_ref, vb_ref, lamv_ref, subln_ref = refs[:8]
    pos = 8
    if has_cache:
        cka_ref, cva_ref, ckb_ref, cvb_ref = refs[pos:pos + 4]
        pos += 4
    ao_ref, bo_ref, s_scr = refs[pos:pos + 3]

    left = _left_mask()
    lane = lax.broadcasted_iota(jnp.int32, (1, LANES), 1)
    kc = min(ATTN_KC, seq_len)
    n_kc = seq_len // kc
    call_idx = [0]

    def next_scratch():
        call_idx[0] += 1
        return s_scr.at[call_idx[0] % 2]

    def new_chunks(k_ref, v_ref, j):
        sl = slice(j * LANES, (j + 1) * LANES)
        out = []
        for c in range(n_kc):
            rows = slice(c * kc, (c + 1) * kc)
            out.append((functools.partial(lambda r, s: k_ref[0, r, s], rows, sl),
                        functools.partial(lambda r, s: v_ref[0, r, s], rows, sl), kc))
        return out

    if has_cache:
        ck = cka_ref[0, 0]
        cv = cva_ref[0, 0]
        ck_sw = pltpu.roll(ck, LANES // 2, 1)
        cv_sw = pltpu.roll(cv, LANES // 2, 1)
        ck3 = [jnp.where(left, ck, ck_sw), ck, jnp.where(left, ck_sw, ck)]
        cv3 = [jnp.where(left, cv, cv_sw), cv, jnp.where(left, cv_sw, cv)]
        past = ck.shape[0]
    for j in range(A_WIDTH // LANES):
        q = qa_ref[0, :, j * LANES:(j + 1) * LANES]
        chunks = new_chunks(k3_ref, v3_ref, j)
        if has_cache:
            ckj = ck3[j].astype(BF16)
            cvj = cv3[j].astype(BF16)
            chunks.append((lambda a=ckj: a, lambda a=cvj: a, past))
        halves = []
        for e in range(2):
            qm = jnp.where(left if e == 0 else jnp.logical_not(left), q, jnp.zeros_like(q))
            acc, l = _softmax_pv(qm, chunks, next_scratch())
            halves.append(acc * (1.0 / l))
        ao_ref[0, :, j * LANES:(j + 1) * LANES] = jnp.where(left, halves[0], halves[1]).astype(BF16)

    lv = lamv_ref[...]
    lam = (jnp.exp(jnp.sum(lv[0:1] * lv[1:2], axis=-1, keepdims=True))
           - jnp.exp(jnp.sum(lv[2:3] * lv[3:4], axis=-1, keepdims=True)) + lam_init)
    for j in range(B_WIDTH // LANES):
        sl = slice(j * LANES, (j + 1) * LANES)
        q = qb_ref[0, :, sl]
        chunks = new_chunks(kb_ref, vb_ref, j)
        if has_cache:
            ckj = ckb_ref[0, 0, :, sl].astype(BF16)
            cvj = cvb_ref[0, 0, :, sl].astype(BF16)
            chunks.append((lambda a=ckj: a, lambda a=cvj: a, ckj.shape[0]))
        halves = []
        for e in range(2):
            maps = []
            for mi in range(2):
                sel = (lane // B_HALF) == (2 * e + mi)
                qm = jnp.where(sel, q, jnp.zeros_like(q))
                acc, l = _softmax_pv(qm, chunks, next_scratch())
                maps.append(acc * (1.0 / l))
            halves.append(maps[0] - lam * maps[1])
        o = jnp.where(left, halves[0], halves[1])
        o = _pair_rmsnorm(o, subln_ref[...]) * (1.0 - lam_init)
        bo_ref[0, :, sl] = o.astype(BF16)


def _attn(qa, k3, v3, qb, kb, vb, lamv, subln, caches, layer, *, lam_init):
    nb, L, _ = qa.shape
    tq = min(ATTN_TQ, L)
    has_cache = caches is not None
    q_spec = lambda w: pl.BlockSpec((1, tq, w), lambda b, i: (b, i, 0))
    kv_spec = lambda w: pl.BlockSpec((1, L, w), lambda b, i: (b, 0, 0))
    in_specs = [q_spec(A_WIDTH), kv_spec(A_WIDTH), kv_spec(A_WIDTH),
                q_spec(B_WIDTH), kv_spec(B_WIDTH), kv_spec(B_WIDTH),
                pl.BlockSpec((4, B_HALF), lambda b, i: (0, 0)),
                pl.BlockSpec((1, LANES), lambda b, i: (0, 0))]
    args = [qa, k3, v3, qb, kb, vb, lamv, subln]
    lk = L
    if has_cache:
        past = caches[0].shape[2]
        lk += past
        for c in caches:
            in_specs.append(pl.BlockSpec((1, 1, past, c.shape[-1]), lambda b, i: (b, layer, 0, 0)))
            args.append(c)
    return pl.pallas_call(
        functools.partial(_attn_kernel, has_cache=has_cache, lam_init=lam_init, seq_len=L),
        out_shape=[jax.ShapeDtypeStruct((nb, L, A_WIDTH), BF16),
                   jax.ShapeDtypeStruct((nb, L, B_WIDTH), BF16)],
        grid=(nb, L // tq),
        in_specs=in_specs,
        out_specs=[q_spec(A_WIDTH), q_spec(B_WIDTH)],
        scratch_shapes=[pltpu.VMEM((2, tq, lk), F32)],
        compiler_params=_cparams(2),
        name="attn_cache" if has_cache else "attn",
    )(*args)


def _prefix_sums(tri, a):
    hi = a.astype(BF16)
    lo = (a - hi.astype(F32)).astype(BF16)
    return (jnp.dot(tri, hi, preferred_element_type=F32)
            + jnp.dot(tri, lo, preferred_element_type=F32))


def _pair_cols(x, c0):
    return jnp.where(_left_mask(), x[:, c0:c0 + 1], x[:, c0 + 1:c0 + 2])


def _ssd_kernel(*refs, has_init, emit_state, seq_len):
    xbc_ref, z_ref, dt_ref, cw_ref, cb_ref, dtb_ref, alog_ref, dexp_ref, nw_ref = refs[:9]
    pos = 9
    if has_init:
        init_ref = refs[pos]
        pos += 1
    co_ref = refs[pos]
    pos += 1
    if emit_state:
        st_ref = refs[pos]
        pos += 1
    xc_scr, y_scr, st_scr = refs[pos:pos + 3]

    L = seq_len
    rb = min(ROW_BLOCK, L)
    q = SSD_CHUNK
    nc = L // q
    left = _left_mask()

    w0 = cw_ref[0:1, :]
    w1 = cw_ref[1:2, :]
    w2 = cw_ref[2:3, :]
    row = lax.broadcasted_iota(jnp.int32, (rb, 1), 0)
    for r0 in range(0, L, rb):
        cur = xbc_ref[0, r0:r0 + rb, :]
        prev = pltpu.roll(cur, 1, 0)
        nxt = pltpu.roll(cur, rb - 1, 0)
        edge_prev = xbc_ref[0, r0 - 1:r0, :] if r0 > 0 else jnp.zeros((1, C_CONV_CH), F32)
        edge_next = xbc_ref[0, r0 + rb:r0 + rb + 1, :] if r0 + rb < L else jnp.zeros((1, C_CONV_CH), F32)
        prev = jnp.where(row == 0, edge_prev, prev)
        nxt = jnp.where(row == rb - 1, edge_next, nxt)
        y = prev * w0 + cur * w1 + nxt * w2 + cb_ref[...]
        xc = y * _sigmoid(y)
        xc_scr[r0:r0 + rb, :] = xc
        y_scr[r0:r0 + rb, :] = xc[:, 0:C_INNER] * dexp_ref[...]

    for d in range(2):
        if has_init:
            st_scr[d] = init_ref[0, 0, d].reshape(C_INNER, C_STATE).T
        else:
            st_scr[d] = jnp.zeros((C_STATE, C_INNER), F32)

    a_neg = -jnp.exp(alog_ref[...])
    ti = lax.broadcasted_iota(jnp.int32, (q, q), 0)
    si = lax.broadcasted_iota(jnp.int32, (q, q), 1)
    tri_incl = (si <= ti).astype(BF16)

    def process(d, r0):
        rows = pl.ds(r0, q)
        lo = d * C_HEADS
        causal = (si <= ti) if d == 0 else (si >= ti)
        xs = xc_scr[rows, 0:C_INNER]
        bm = [xc_scr[rows, C_INNER + g * C_STATE:C_INNER + (g + 1) * C_STATE] for g in range(C_GROUPS)]
        cm = [xc_scr[rows, C_INNER + (C_GROUPS + g) * C_STATE:C_INNER + (C_GROUPS + g + 1) * C_STATE]
              for g in range(C_GROUPS)]
        bm_b = [b.astype(BF16) for b in bm]
        cm_b = [c.astype(BF16) for c in cm]
        bmt_b = [b.T.astype(BF16) for b in bm]
        gmat = [lax.dot_general(cm_b[g], bm_b[g], (((1,), (1,)), ((), ())), preferred_element_type=F32)
                for g in range(C_GROUPS)]

        dtr = dt_ref[0, rows, :] + dtb_ref[...]
        dt = jnp.maximum(dtr, 0.0) + jnp.log1p(jnp.exp(-jnp.abs(dtr)))
        a = dt * a_neg
        cs = _prefix_sums(tri_incl, a)
        tot = cs[q - 1:q, :]
        ev = cs if d == 0 else (tot - cs + a)
        e_in = jnp.exp(ev)
        e_out = jnp.exp(tot - ev)
        dec = jnp.exp(tot)
        ev_t = ev.T

        st = st_scr.at[d]
        for j in range(C_INNER // LANES):
            sl = slice(j * LANES, (j + 1) * LANES)
            h0 = 2 * j
            g0, g1 = h0 // (C_HEADS // C_GROUPS), (h0 + 1) // (C_HEADS // C_GROUPS)
            xdt = xs[:, sl] * _pair_cols(dt, lo + h0)
            xdt_b = xdt.astype(BF16)
            st_blk = st[:, sl]
            st_b = st_blk.astype(BF16)
            yd = []
            for e in range(2):
                hh = lo + h0 + e
                g = g0 if e == 0 else g1
                diff = ev[:, hh:hh + 1] - ev_t[hh:hh + 1, :]
                lmat = jnp.exp(jnp.where(causal, diff, NEG_BIG))
                yd.append(jnp.dot((gmat[g] * lmat).astype(BF16), xdt_b, preferred_element_type=F32))
            yo0 = jnp.dot(cm_b[g0], st_b, preferred_element_type=F32)
            yo1 = yo0 if g1 == g0 else jnp.dot(cm_b[g1], st_b, preferred_element_type=F32)
            y_blk = (jnp.where(left, yd[0], yd[1])
                     + jnp.where(left, yo0, yo1) * _pair_cols(e_in, lo + h0))
            y_scr[rows, sl] = y_scr[rows, sl] + y_blk
            xw = (xdt * _pair_cols(e_out, lo + h0)).astype(BF16)
            up0 = jnp.dot(bmt_b[g0], xw, preferred_element_type=F32)
            up1 = up0 if g1 == g0 else jnp.dot(bmt_b[g1], xw, preferred_element_type=F32)
            st[:, sl] = st_blk * _pair_cols(dec, lo + h0) + jnp.where(left, up0, up1)

    if nc <= 2:
        for c in range(nc):
            process(0, c * q)
            process(1, (nc - 1 - c) * q)
    else:
        def body(c, carry):
            process(0, pl.multiple_of(c * q, q))
            process(1, pl.multiple_of((nc - 1 - c) * q, q))
            return carry
        lax.fori_loop(0, nc, body, 0)

    if emit_state:
        for d in range(2):
            st_ref[0, d] = st_scr[d].T.reshape(C_HEADS, C_HEADDIM, C_STATE)

    for r0 in range(0, L, rb):
        zz = z_ref[0, r0:r0 + rb, :]
        yg = y_scr[r0:r0 + rb, :] * (zz * _sigmoid(zz))
        ms = jnp.mean(yg * yg, axis=-1, keepdims=True)
        co_ref[0, r0:r0 + rb, :] = (yg * lax.rsqrt(ms + EPS) * nw_ref[...]).astype(BF16)


def _ssd(xbc, z, dt, conv_w, conv_b, dtb, alog, dexp, norm_w, state, layer, *, emit_state):
    nb, L, _ = xbc.shape
    has_init = state is not None
    seq_spec = lambda w: pl.BlockSpec((1, L, w), lambda b: (b, 0, 0))
    par_spec = lambda r, w: pl.BlockSpec((r, w), lambda b: (0, 0))
    in_specs = [seq_spec(C_CONV_CH), seq_spec(C_INNER), seq_spec(LANES),
                par_spec(3, C_CONV_CH), par_spec(1, C_CONV_CH), par_spec(1, LANES), par_spec(1, LANES),
                par_spec(1, C_INNER), par_spec(1, C_INNER)]
    args = [xbc, z, dt, conv_w, conv_b, dtb, alog, dexp, norm_w]
    if has_init:
        in_specs.append(pl.BlockSpec((1, 1, 2, C_HEADS, C_HEADDIM, C_STATE),
                                     lambda b: (b, layer, 0, 0, 0, 0)))
        args.append(state)
    out_shape = [jax.ShapeDtypeStruct((nb, L, C_INNER), BF16)]
    out_specs = [seq_spec(C_INNER)]
    if emit_state:
        out_shape.append(jax.ShapeDtypeStruct((nb, 2, C_HEADS, C_HEADDIM, C_STATE), F32))
        out_specs.append(pl.BlockSpec((1, 2, C_HEADS, C_HEADDIM, C_STATE), lambda b: (b, 0, 0, 0, 0)))
    return pl.pallas_call(
        functools.partial(_ssd_kernel, has_init=has_init, emit_state=emit_state, seq_len=L),
        out_shape=out_shape,
        grid=(nb,),
        in_specs=in_specs,
        out_specs=out_specs,
        scratch_shapes=[pltpu.VMEM((L, C_CONV_CH), F32), pltpu.VMEM((L, C_INNER), F32),
                        pltpu.VMEM((2, C_STATE, C_INNER), F32)],
        compiler_params=_cparams(1),
        name="ssd_init" if has_init else "ssd",
    )(*args)


def _outp_kernel(a_ref, b_ref, c_ref, x_ref, mod_ref, w_ref, x1_o, h2_o):
    mix = jnp.concatenate([a_ref[0], b_ref[0], c_ref[0]], axis=-1)
    o = jnp.dot(mix, w_ref[...], preferred_element_type=F32)
    gate1 = mod_ref[0, 2:3, :]
    shift2 = mod_ref[0, 3:4, :]
    scale2 = mod_ref[0, 4:5, :]
    x1 = x_ref[0] + gate1 * o
    ms = jnp.mean(x1 * x1, axis=-1, keepdims=True)
    x1_o[0] = x1
    h2_o[0] = ((x1 * lax.rsqrt(ms + EPS)) * (1.0 + scale2) + shift2).astype(BF16)


def _outp(a_out, b_out, c_out, x, mod, w_out_b, layer, *, per_batch_mod):
    nb, L, d = x.shape
    tm = min(PROJ_TM, L)
    mod_map = (lambda b, i: (b, 0, 0)) if per_batch_mod else (lambda b, i: (0, 0, 0))
    row_spec = lambda w: pl.BlockSpec((1, tm, w), lambda b, i: (b, i, 0))
    return pl.pallas_call(
        _outp_kernel,
        out_shape=[jax.ShapeDtypeStruct((nb, L, d), F32), jax.ShapeDtypeStruct((nb, L, d), BF16)],
        grid=(nb, L // tm),
        in_specs=[row_spec(A_WIDTH), row_spec(B_WIDTH), row_spec(C_INNER), row_spec(d),
                  pl.BlockSpec((1, 6, d), mod_map),
                  pl.BlockSpec((None, d, d), lambda b, i: (layer, 0, 0))],
        out_specs=[row_spec(d), row_spec(d)],
        compiler_params=_cparams(2),
        name="outp",
    )(a_out, b_out, c_out, x, mod, w_out_b)


HALO = 16


def _ffn_kernel(h_ref, hp_ref, hn_ref, x1_ref, mod_ref, wu_ref, cw_ref, cb_ref, wd_ref, fw_ref,
                o_ref, hext_scr, u_scr, acc_scr, *, seq_len, final_norm):
    tm = h_ref.shape[1]
    i = pl.program_id(1)
    n_chunks = wu_ref.shape[0]
    rb = min(FFN_RB, tm)

    has_prev = ((i * tm) & (seq_len - 1)) != 0
    has_next = ((i * tm + tm) & (seq_len - 1)) != 0
    hext_scr[0:HALO, :] = jnp.where(has_prev, hp_ref[0], jnp.zeros_like(hp_ref[0]))
    hext_scr[HALO:HALO + tm, :] = h_ref[0]
    hext_scr[HALO + tm:HALO + tm + HALO, :] = jnp.where(has_next, hn_ref[0], jnp.zeros_like(hn_ref[0]))
    acc_scr[...] = jnp.zeros_like(acc_scr)
    row8 = lax.broadcasted_iota(jnp.int32, (8, 1), 0)

    n_uslab = 2 * FFN_FC // LANES
    n_oslab = acc_scr.shape[0]
    half = rb // 2

    def up(c, slot):
        u = jnp.dot(hext_scr[...], wu_ref[c], preferred_element_type=F32)
        for s in range(n_uslab):
            u_scr[slot, s] = u[:, s * LANES:(s + 1) * LANES]

    def conv_down(c, slot):
        cw = cw_ref[c]
        bias = cb_ref[c]
        for r0 in range(0, tm, rb):
            base = HALO + r0
            ys = []
            for s in range(n_uslab):
                ls = slice(s * LANES, (s + 1) * LANES)
                even = u_scr[slot, s, pl.ds(base, half, stride=2), :]
                odd = u_scr[slot, s, pl.ds(base + 1, half, stride=2), :]
                odd_before = u_scr[slot, s, pl.ds(base - 1, half, stride=2), :]
                even_after = u_scr[slot, s, pl.ds(base + 2, half, stride=2), :]
                if r0 > 0 and r0 % seq_len == 0:
                    odd_before = jnp.concatenate(
                        [jnp.where(row8 == 0, 0.0, odd_before[0:8]), odd_before[8:]], axis=0)
                if r0 + rb < tm and (r0 + rb) % seq_len == 0:
                    even_after = jnp.concatenate(
                        [even_after[:half - 8], jnp.where(row8 == 7, 0.0, even_after[half - 8:])], axis=0)
                w0, w1, w2, b = cw[0:1, ls], cw[1:2, ls], cw[2:3, ls], bias[:, ls]
                y_even = odd_before * w0 + even * w1 + odd * w2 + b
                y_odd = even * w0 + odd * w1 + even_after * w2 + b
                ys.append(jnp.concatenate([y_even, y_odd], axis=0))
            n_g = n_uslab // 2
            act = jnp.concatenate([ys[s] * _sigmoid(ys[s]) * ys[n_g + s] for s in range(n_g)], axis=1)
            dn = jnp.dot(act.astype(BF16), wd_ref[c], preferred_element_type=F32)
            for s in range(n_oslab):
                ls = slice(s * LANES, (s + 1) * LANES)
                acc_scr[s, pl.ds(r0, half, stride=2), :] += dn[0:half, ls]
                acc_scr[s, pl.ds(r0 + 1, half, stride=2), :] += dn[half:rb, ls]

    up(0, 0)

    def pair(k, carry):
        c = 2 * k
        up(c + 1, 1)
        conv_down(c, 0)
        up(c + 2, 0)
        conv_down(c + 1, 1)
        return carry

    lax.fori_loop(0, (n_chunks - 1) // 2, pair, 0)
    if n_chunks % 2 == 1:
        conv_down(n_chunks - 1, 0)
    else:
        up(n_chunks - 1, 1)
        conv_down(n_chunks - 2, 0)
        conv_down(n_chunks - 1, 1)

    gate2 = mod_ref[0, 5:6, :]
    ffn_out = jnp.concatenate([acc_scr[s] for s in range(n_oslab)], axis=1)
    out = x1_ref[0] + gate2 * ffn_out
    if final_norm:
        ms = jnp.mean(out * out, axis=-1, keepdims=True)
        out = out * lax.rsqrt(ms + EPS) * fw_ref[...]
    o_ref[0] = out


def _ffn(h2, x1, mod, wu_c, cw_c, cb_c, wd_c, layer, final_w, *, per_batch_mod, seq_len, final_norm):
    nb, L, d = x1.shape
    tm = min(FFN_TM, L)
    n_tiles = L // tm
    rb = min(FFN_RB, tm)
    assert seq_len & (seq_len - 1) == 0 and L % seq_len == 0
    assert seq_len % tm == 0 or (tm % seq_len == 0 and seq_len % rb == 0)
    n_chunks = wu_c.shape[1]
    hb = tm // HALO
    n_hblk = L // HALO
    mod_map = (lambda b, i: (b, 0, 0)) if per_batch_mod else (lambda b, i: (0, 0, 0))
    row_spec = lambda w: pl.BlockSpec((1, tm, w), lambda b, i: (b, i, 0))
    const3 = lambda s: pl.BlockSpec(s, lambda b, i: (0, 0, 0))
    return pl.pallas_call(
        functools.partial(_ffn_kernel, seq_len=seq_len, final_norm=final_norm),
        out_shape=jax.ShapeDtypeStruct((nb, L, d), F32),
        grid=(nb, n_tiles),
        in_specs=[row_spec(d),
                  pl.BlockSpec((1, HALO, d), lambda b, i: (b, jnp.maximum(i * hb - 1, 0), 0)),
                  pl.BlockSpec((1, HALO, d), lambda b, i: (b, jnp.minimum((i + 1) * hb, n_hblk - 1), 0)),
                  row_spec(d),
                  pl.BlockSpec((1, 6, d), mod_map),
                  pl.BlockSpec((None, n_chunks, d, 2 * FFN_FC), lambda b, i: (layer, 0, 0, 0)),
                  const3((n_chunks, 3, 2 * FFN_FC)),
                  const3((n_chunks, 1, 2 * FFN_FC)),
                  pl.BlockSpec((None, n_chunks, FFN_FC, d), lambda b, i: (layer, 0, 0, 0)),
                  pl.BlockSpec((1, d), lambda b, i: (0, 0))],
        out_specs=row_spec(d),
        scratch_shapes=[pltpu.VMEM((tm + 2 * HALO, d), BF16),
                        pltpu.VMEM((2, 2 * FFN_FC // LANES, tm + 2 * HALO, LANES), F32),
                        pltpu.VMEM((d // LANES, tm, LANES), F32)],
        compiler_params=_cparams(2),
        name="ffn_final" if final_norm else "ffn",
    )(h2, h2, h2, x1, mod, wu_c, cw_c, cb_c, wd_c, final_w)


def _rope_tables(L, d):
    rows = L // GRID_W
    row = jnp.repeat(jnp.arange(rows), GRID_W).astype(F32)
    col = jnp.tile(jnp.arange(GRID_W), rows).astype(F32)
    quarter = d // 4
    inv = ROPE_THETA ** (-jnp.arange(quarter, dtype=F32) / quarter)
    ang_r = row[:, None] * inv[None, :]
    ang_c = col[:, None] * inv[None, :]
    ang = jnp.concatenate([ang_r, ang_r, ang_c, ang_c], axis=-1)
    cos, sin = jnp.cos(ang), jnp.sin(ang)
    even = ((jnp.arange(d) // quarter) % 2 == 0)[None, :]
    s_up = jnp.where(even, -sin, 0.0)
    s_dn = jnp.where(even, 0.0, sin)
    reps = LANES // d
    return tuple(jnp.tile(t, (1, reps)) for t in (cos, s_up, s_dn))


def _pad_lanes(v, width=LANES):
    v = v.reshape(1, -1).astype(F32)
    return jnp.pad(v, ((0, 0), (0, width - v.shape[1])))


def _layer_params(l, a_q_norm, a_k_norm, b_lambda, b_subln, ssm_conv_w, ssm_conv_b, ssm_A_log,
                  ssm_dt_bias, ssm_D, ssm_norm_w, ffn_conv_w, ffn_conv_b):
    n_chunks = D_FF // FFN_FC

    def gv_chunks(t):
        lead = t.shape[:-1]
        g = t[..., :D_FF].reshape(lead + (n_chunks, FFN_FC))
        v = t[..., D_FF:].reshape(lead + (n_chunks, FFN_FC))
        gv = jnp.concatenate([g, v], axis=-1)
        return jnp.moveaxis(gv, -2, 0)

    return dict(
        qn=jnp.tile(a_q_norm[l], 2).reshape(1, LANES),
        kn=jnp.tile(a_k_norm[l], 2).reshape(1, LANES),
        lamv=b_lambda[l],
        subln=jnp.tile(b_subln[l], 2).reshape(1, LANES),
        conv_w=ssm_conv_w[l],
        conv_b=ssm_conv_b[l].reshape(1, C_CONV_CH),
        dtb=_pad_lanes(ssm_dt_bias[l]),
        alog=_pad_lanes(ssm_A_log[l]),
        dexp=jnp.repeat(ssm_D[l], C_HEADDIM).reshape(1, C_INNER),
        norm_w=ssm_norm_w[l].reshape(1, C_INNER),
        cw=gv_chunks(ffn_conv_w[l]),
        cb=gv_chunks(ffn_conv_b[l].reshape(1, 2 * D_FF)),
    )


def _block(x, mod, p, wts, rope_tabs, caches, state, layer, *, seq_shape, per_batch_mod, is_ctx, lam_init,
           final_w, final_norm):
    n_seq, seq_len = seq_shape
    outs = _proj(x, mod, wts["w_in"], layer, p["qn"], p["kn"], rope_tabs,
                 per_batch_mod=per_batch_mod, kv_f32=is_ctx)
    seq = lambda t: t.reshape(n_seq, seq_len, t.shape[-1])
    qa, k3, v3, qb, kb, vb, xbc, z, dt = [seq(t) for t in outs[:9]]
    a_out, b_out = _attn(qa, k3, v3, qb, kb, vb, p["lamv"], p["subln"], caches, layer, lam_init=lam_init)
    ssd_out = _ssd(xbc, z, dt, p["conv_w"], p["conv_b"], p["dtb"], p["alog"], p["dexp"], p["norm_w"],
                   state, layer, emit_state=is_ctx)
    c_out = ssd_out[0]
    flat = lambda t: t.reshape(x.shape[0], x.shape[1], t.shape[-1])
    x1, h2 = _outp(flat(a_out), flat(b_out), flat(c_out), x, mod, wts["w_out"], layer,
                   per_batch_mod=per_batch_mod)
    x2 = _ffn(h2, x1, mod, wts["wu"], p["cw"], p["cb"], wts["wd"], layer, final_w,
              per_batch_mod=per_batch_mod, seq_len=seq_len, final_norm=final_norm)
    extras = None
    if is_ctx:
        extras = tuple(seq(t) for t in outs[9:13]) + (ssd_out[1],)
    return x2, extras


def kernel(x_prompt, x_sample, cache_a_k, cache_a_v, cache_b_k, cache_b_v, state_ssm, c, c_ctx, ada_w, ada_b, w_in, a_q_norm, a_k_norm, b_lambda, b_subln, ssm_conv_w, ssm_conv_b, ssm_A_log, ssm_dt_bias, ssm_D, ssm_norm_w, w_out, ffn_up, ffn_conv_w, ffn_conv_b, ffn_down, final_norm_w):
    depth = w_in.shape[0]
    nbp, lp, d = x_prompt.shape
    nbs, ls, _ = x_sample.shape
    past = cache_a_k.shape[2]

    mod_rows = 8
    assert 1 + nbs <= mod_rows
    cvecs = jnp.concatenate([c_ctx[None, :], c, jnp.zeros((mod_rows - 1 - nbs, d), F32)], axis=0)
    mod = _modulation(cvecs, ada_w, ada_b)

    rope_tabs = _rope_tables(ls, HEAD_DIM) + _rope_tables(ls, B_HALF)
    caches = (cache_a_k.reshape(nbs, depth, past, A_KV_HEADS * HEAD_DIM),
              cache_a_v.reshape(nbs, depth, past, A_KV_HEADS * HEAD_DIM),
              cache_b_k.reshape(nbs, depth, past, B_WIDTH),
              cache_b_v.reshape(nbs, depth, past, B_WIDTH))
    final_w = final_norm_w.reshape(1, d)
    n_chunks = D_FF // FFN_FC
    wts = dict(
        w_in=_pad_cast_bf16(w_in, PROJ_PAD),
        w_out=_cast_bf16(w_out),
        wu=_gv_cast_bf16(ffn_up),
        wd=_cast_bf16(ffn_down).reshape(depth, n_chunks, FFN_FC, d),
    )

    yp = x_prompt.reshape(1, nbp * lp, d)
    ys = x_sample
    ctx_extras = []
    for l in range(depth):
        lam_init = 0.8 - 0.6 * math.exp(-0.3 * l)
        p = _layer_params(l, a_q_norm, a_k_norm, b_lambda, b_subln, ssm_conv_w, ssm_conv_b,
                          ssm_A_log, ssm_dt_bias, ssm_D, ssm_norm_w, ffn_conv_w, ffn_conv_b)
        last = l == depth - 1
        mod_ctx = mod[l, 0].reshape(1, 6, d)
        mod_lat = mod[l, 1:1 + nbs].reshape(nbs, 6, d)
        yp, extras = _block(yp, mod_ctx, p, wts, None, None, None, l, seq_shape=(nbp, lp),
                            per_batch_mod=False, is_ctx=True, lam_init=lam_init,
                            final_w=final_w, final_norm=last)
        ctx_extras.append(extras)
        ys, _ = _block(ys, mod_lat, p, wts, rope_tabs, caches, state_ssm, l, seq_shape=(nbs, ls),
                       per_batch_mod=True, is_ctx=False, lam_init=lam_init,
                       final_w=final_w, final_norm=last)

    y_prompt = yp.reshape(nbp, lp, d)
    stack = lambda k: jnp.stack([e[k] for e in ctx_extras], axis=1)
    new_a_k = stack(0).reshape(nbp, depth, lp, A_KV_HEADS, HEAD_DIM)
    new_a_v = stack(1).reshape(nbp, depth, lp, A_KV_HEADS, HEAD_DIM)
    new_b_k = stack(2).reshape(nbp, depth, lp, B_HEADS, 2, B_HALF)
    new_b_v = stack(3).reshape(nbp, depth, lp, B_HEADS, 2 * B_HALF)
    new_ssm = stack(4)
    return (y_prompt, ys, new_a_k, new_a_v, new_b_k, new_b_v, new_ssm)
```

```python
import functools
import math

import jax
import jax.numpy as jnp
from jax import lax
from jax.experimental import pallas as pl
from jax.experimental.pallas import tpu as pltpu

F32 = jnp.float32
BF16 = jnp.bfloat16

LANES = 128
VMEM_LIMIT_BYTES = 56 * 1024 * 1024

D_MODEL = 1024
GRID_W = 64
SSD_CHUNK = 128
ROPE_THETA = 10000.0
EPS = 1e-6
HEAD_DIM = 64
A_Q_HEADS = 6
A_KV_HEADS = 2
A_GROUP = A_Q_HEADS // A_KV_HEADS
A_WIDTH = A_Q_HEADS * HEAD_DIM
B_HEADS = 4
B_HALF = 32
B_WIDTH = B_HEADS * 2 * B_HALF
C_HEADS = 6
C_HEADDIM = 64
C_INNER = C_HEADS * C_HEADDIM
C_GROUPS = 2
C_STATE = 128
C_CONV_CH = C_INNER + 2 * C_GROUPS * C_STATE
D_FF = 2816
PROJ_WIDTH = A_WIDTH + 4 * HEAD_DIM + 3 * B_WIDTH + 2 * C_INNER + 4 * C_STATE + 2 * C_HEADS
PROJ_PAD = ((PROJ_WIDTH + 2 * LANES - 1) // (2 * LANES)) * (2 * LANES)

OFF_AQ = 0
OFF_AK = OFF_AQ + A_WIDTH
OFF_AV = OFF_AK + A_KV_HEADS * HEAD_DIM
OFF_BQ = OFF_AV + A_KV_HEADS * HEAD_DIM
OFF_BK = OFF_BQ + B_WIDTH
OFF_BV = OFF_BK + B_WIDTH
OFF_CX = OFF_BV + B_WIDTH
OFF_CZ = OFF_CX + C_INNER
OFF_CB = OFF_CZ + C_INNER
OFF_CC = OFF_CB + C_GROUPS * C_STATE
OFF_DT = OFF_CC + C_GROUPS * C_STATE

MOD_BLOCK = 1024
PROJ_TM = 512
FFN_TM = 512
FFN_FC = 256
FFN_RB = 256
ATTN_TQ = 512
ATTN_KC = 512
ROW_BLOCK = 256
NEG_BIG = -1e30
LOG2E = 1.4426950408889634


def _cparams(n_grid):
    return pltpu.CompilerParams(
        dimension_semantics=("parallel",) * n_grid,
        vmem_limit_bytes=VMEM_LIMIT_BYTES,
    )


def _sigmoid(x):
    return 1.0 / (1.0 + jnp.exp(-x))


def _left_mask():
    lane = lax.broadcasted_iota(jnp.int32, (1, LANES), 1)
    return lane < (LANES // 2)


def _pair_rmsnorm(y, w):
    left = _left_mask()
    sq = y * y
    s_l = jnp.sum(jnp.where(left, sq, 0.0), axis=-1, keepdims=True)
    s_r = jnp.sum(jnp.where(left, 0.0, sq), axis=-1, keepdims=True)
    ms = jnp.where(left, s_l, s_r) * (1.0 / HEAD_DIM)
    return y * lax.rsqrt(ms + EPS) * w


def _rope(y, cos, s_up, s_dn, quarter):
    return (y * cos
            + pltpu.roll(y, LANES - quarter, 1) * s_up
            + pltpu.roll(y, quarter, 1) * s_dn)


def _modulation_kernel(c_ref, w_ref, b_ref, o_ref):
    c = c_ref[...]
    s = c * _sigmoid(c)
    o_ref[0] = jnp.dot(s.astype(BF16), w_ref[0].astype(BF16),
                       preferred_element_type=F32) + b_ref[0]


def _modulation(cvecs, ada_w, ada_b):
    depth, d, width = ada_w.shape
    rows = cvecs.shape[0]
    return pl.pallas_call(
        _modulation_kernel,
        out_shape=jax.ShapeDtypeStruct((depth, rows, width), F32),
        grid=(depth, width // MOD_BLOCK),
        in_specs=[
            pl.BlockSpec((rows, d), lambda l, j: (0, 0)),
            pl.BlockSpec((1, d, MOD_BLOCK), lambda l, j: (l, 0, j)),
            pl.BlockSpec((1, 1, MOD_BLOCK), lambda l, j: (l, 0, j)),
        ],
        out_specs=pl.BlockSpec((1, rows, MOD_BLOCK), lambda l, j: (l, 0, j)),
        compiler_params=_cparams(2),
        name="modulation",
    )(cvecs, ada_w, ada_b.reshape(depth, 1, width))


CAST_ROWS = 256


def _cast_kernel(x_ref, o_ref):
    o_ref[...] = x_ref[...].astype(o_ref.dtype)


def _cast_bf16(x):
    depth, rows, cols = x.shape
    spec = pl.BlockSpec((1, CAST_ROWS, cols), lambda l, r: (l, r, 0))
    return pl.pallas_call(
        _cast_kernel, out_shape=jax.ShapeDtypeStruct(x.shape, BF16),
        grid=(depth, rows // CAST_ROWS), in_specs=[spec], out_specs=spec,
        compiler_params=_cparams(2), name="cast_bf16",
    )(x)


def _pad_cast_kernel(x_ref, o_ref):
    rows, w = x_ref.shape[1], x_ref.shape[2]
    aligned = (w // LANES) * LANES
    o_ref[0, :, 0:aligned] = x_ref[0, :, 0:aligned].astype(BF16)
    o_ref[0, :, aligned:] = jnp.zeros((rows, o_ref.shape[2] - aligned), BF16)
    o_ref[0, :, aligned:w] = x_ref[0, :, aligned:w].astype(BF16)


def _pad_cast_bf16(x, width):
    depth, rows, cols = x.shape
    return pl.pallas_call(
        _pad_cast_kernel, out_shape=jax.ShapeDtypeStruct((depth, rows, width), BF16),
        grid=(depth, rows // CAST_ROWS),
        in_specs=[pl.BlockSpec((1, CAST_ROWS, cols), lambda l, r: (l, r, 0))],
        out_specs=pl.BlockSpec((1, CAST_ROWS, width), lambda l, r: (l, r, 0)),
        compiler_params=_cparams(2), name="pad_cast_bf16",
    )(x)


def _gv_cast_kernel(g_ref, v_ref, o_ref):
    o_ref[0, 0, :, 0:FFN_FC] = g_ref[0].astype(BF16)
    o_ref[0, 0, :, FFN_FC:2 * FFN_FC] = v_ref[0].astype(BF16)


def _gv_cast_bf16(up):
    depth, d, _ = up.shape
    n_chunks = D_FF // FFN_FC
    return pl.pallas_call(
        _gv_cast_kernel, out_shape=jax.ShapeDtypeStruct((depth, n_chunks, d, 2 * FFN_FC), BF16),
        grid=(depth, n_chunks),
        in_specs=[pl.BlockSpec((1, d, FFN_FC), lambda l, c: (l, 0, c)),
                  pl.BlockSpec((1, d, FFN_FC), lambda l, c: (l, 0, n_chunks + c))],
        out_specs=pl.BlockSpec((1, 1, d, 2 * FFN_FC), lambda l, c: (l, c, 0, 0)),
        compiler_params=_cparams(2), name="gv_cast_bf16",
    )(up, up)


def _proj_kernel(*refs, rope, kv_f32):
    x_ref, mod_ref, w_ref, qn_ref, kn_ref = refs[:5]
    pos = 5
    if rope:
        ca_ref, sau_ref, sad_ref, cb_ref, sbu_ref, sbd_ref = refs[pos:pos + 6]
        pos += 6
    (qa_o, k3_o, vta_o, qb_o, kb_o, vtb_o, xbc_o, z_o, dt_o) = refs[pos:pos + 9]
    pos += 9
    if kv_f32:
        ka32_o, va32_o, kb32_o, vb32_o = refs[pos:pos + 4]
        pos += 4
    p_scr = refs[pos]

    x = x_ref[0]
    ms = jnp.mean(x * x, axis=-1, keepdims=True)
    shift = mod_ref[0, 0:1, :]
    scale = mod_ref[0, 1:2, :]
    h = (x * lax.rsqrt(ms + EPS)) * (1.0 + scale) + shift
    p_scr[...] = jnp.dot(h.astype(BF16), w_ref[...], preferred_element_type=F32)

    left = _left_mask()

    def rope_a(y):
        if not rope:
            return y
        return _rope(y, ca_ref[...], sau_ref[...], sad_ref[...], HEAD_DIM // 4)

    def rope_b(y):
        if not rope:
            return y
        return _rope(y, cb_ref[...], sbu_ref[...], sbd_ref[...], B_HALF // 4)

    n_sub, sub_len = vta_o.shape[0], vta_o.shape[3]
    ones_half = jnp.ones((HEAD_DIM, sub_len), BF16)

    def store_vt(dst, h, vt_half, e):
        for s in range(n_sub):
            dst[s, h, e * HEAD_DIM:(e + 1) * HEAD_DIM, :] = vt_half[:, s * sub_len:(s + 1) * sub_len]
            dst[s, h, (1 - e) * HEAD_DIM:(2 - e) * HEAD_DIM, :] = ones_half

    a_scale = HEAD_DIM ** -0.5 * LOG2E
    for j in range(A_WIDTH // LANES):
        y = p_scr[:, OFF_AQ + j * LANES:OFF_AQ + (j + 1) * LANES]
        y = rope_a(_pair_rmsnorm(y, qn_ref[...]))
        qa_o[0, :, j * LANES:(j + 1) * LANES] = (y * a_scale).astype(BF16)

    k = rope_a(_pair_rmsnorm(p_scr[:, OFF_AK:OFF_AK + LANES], kn_ref[...]))
    v = p_scr[:, OFF_AV:OFF_AV + LANES]
    if kv_f32:
        ka32_o[0] = k
        va32_o[0] = v
    swapped = pltpu.roll(k, LANES // 2, 1)
    k3_o[0, :, 0:LANES] = jnp.where(left, k, swapped).astype(BF16)
    k3_o[0, :, LANES:2 * LANES] = k.astype(BF16)
    k3_o[0, :, 2 * LANES:3 * LANES] = jnp.where(left, swapped, k).astype(BF16)
    vt = v.T.astype(BF16)
    for h in range(A_Q_HEADS):
        g = h // A_GROUP
        store_vt(vta_o, h, vt[g * HEAD_DIM:(g + 1) * HEAD_DIM], h % 2)

    b_scale = B_HALF ** -0.5 * LOG2E
    for j in range(B_WIDTH // LANES):
        sl = slice(j * LANES, (j + 1) * LANES)
        q = rope_b(p_scr[:, OFF_BQ + j * LANES:OFF_BQ + (j + 1) * LANES])
        qb_o[0, :, sl] = (q * b_scale).astype(BF16)
        kb = rope_b(p_scr[:, OFF_BK + j * LANES:OFF_BK + (j + 1) * LANES])
        vb = p_scr[:, OFF_BV + j * LANES:OFF_BV + (j + 1) * LANES]
        kb_o[0, :, sl] = kb.astype(BF16)
        vbt = vb.T.astype(BF16)
        for e in range(2):
            store_vt(vtb_o, 2 * j + e, vbt[e * HEAD_DIM:(e + 1) * HEAD_DIM], e)
        if kv_f32:
            kb32_o[0, :, sl] = kb
            vb32_o[0, :, sl] = vb

    xbc_o[0, :, 0:C_INNER] = p_scr[:, OFF_CX:OFF_CX + C_INNER]
    xbc_o[0, :, C_INNER:C_CONV_CH] = p_scr[:, OFF_CB:OFF_CB + 2 * C_GROUPS * C_STATE]
    z_o[0] = p_scr[:, OFF_CZ:OFF_CZ + C_INNER]
    dt_o[0] = p_scr[:, OFF_DT:OFF_DT + LANES]


def _proj(x, mod, w_in_p, layer, qn, kn, rope_tabs, *, per_batch_mod, kv_f32, seq_len):
    nb, L, d = x.shape
    tm = min(PROJ_TM, L)
    rope = rope_tabs is not None
    mod_map = (lambda b, i: (b, 0, 0)) if per_batch_mod else (lambda b, i: (0, 0, 0))
    row_spec = lambda w: pl.BlockSpec((1, tm, w), lambda b, i: (b, i, 0))
    in_specs = [
        row_spec(d),
        pl.BlockSpec((1, 6, d), mod_map),
        pl.BlockSpec((None, d, PROJ_PAD), lambda b, i: (layer, 0, 0)),
        pl.BlockSpec((1, LANES), lambda b, i: (0, 0)),
        pl.BlockSpec((1, LANES), lambda b, i: (0, 0)),
    ]
    args = [x, mod, w_in_p, qn, kn]
    if rope:
        in_specs += [pl.BlockSpec((tm, LANES), lambda b, i: (i, 0))] * 6
        args += list(rope_tabs)
    widths = [(A_WIDTH, BF16), (A_WIDTH, BF16), (-A_Q_HEADS, BF16), (B_WIDTH, BF16), (B_WIDTH, BF16),
              (-B_HEADS, BF16), (C_CONV_CH, F32), (C_INNER, F32), (LANES, F32)]
    if kv_f32:
        widths += [(LANES, F32), (LANES, F32), (B_WIDTH, F32), (B_WIDTH, F32)]
    out_shape, out_specs = [], []
    n_seq = nb * L // seq_len
    for w, dt in widths:
        if w > 0:
            out_shape.append(jax.ShapeDtypeStruct((nb, L, w), dt))
            out_specs.append(row_spec(w))
        elif seq_len >= tm:
            assert L == seq_len
            out_shape.append(jax.ShapeDtypeStruct((n_seq, -w, LANES, seq_len), dt))
            out_specs.append(pl.BlockSpec((1, -w, LANES, tm), lambda b, i: (b, 0, 0, i)))
        else:
            assert nb == 1 and tm % seq_len == 0
            out_shape.append(jax.ShapeDtypeStruct((n_seq, -w, LANES, seq_len), dt))
            out_specs.append(pl.BlockSpec((tm // seq_len, -w, LANES, seq_len), lambda b, i: (i, 0, 0, 0)))
    return pl.pallas_call(
        functools.partial(_proj_kernel, rope=rope, kv_f32=kv_f32),
        out_shape=out_shape,
        grid=(nb, L // tm),
        in_specs=in_specs,
        out_specs=out_specs,
        scratch_shapes=[pltpu.VMEM((tm, PROJ_PAD), F32)],
        compiler_params=_cparams(2),
        name="proj_rope" if rope else "proj",
    )(*args)


class _ScoreMap:
    def __init__(self, tag, qm, chunks, s_ref):
        self.tag, self.qm, self.chunks, self.s_ref = tag, qm, chunks, s_ref
        self.offsets = [sum(w for _, _, w in chunks[:c]) for c in range(len(chunks))]
        self.m_part = self.m = self.acc = None

    def pass1(self, c):
        k_fn, _, w = self.chunks[c]
        tq = self.qm.shape[0]
        s = lax.dot_general(k_fn(), self.qm, (((1,), (1,)), ((), ())), preferred_element_type=F32)
        self.s_ref[self.offsets[c]:self.offsets[c] + w, :] = s
        part = jnp.max(s.reshape(w // 8, 8, tq), axis=0)
        self.m_part = part if self.m_part is None else jnp.maximum(self.m_part, part)

    def finish_max(self):
        self.m = jnp.max(self.m_part, axis=0, keepdims=True)

    def pass2(self, c):
        _, vt_fn, w = self.chunks[c]
        e = jnp.exp2(self.s_ref[self.offsets[c]:self.offsets[c] + w, :] - self.m).astype(BF16)
        pv = jnp.dot(vt_fn(), e, preferred_element_type=F32)
        self.acc = pv if self.acc is None else self.acc + pv


def _run_score_maps(maps, on_done):
    first = maps[0]
    for c in range(len(first.chunks)):
        first.pass1(c)
    first.finish_max()
    for i, mp in enumerate(maps):
        nxt = maps[i + 1] if i + 1 < len(maps) else None
        n_next = len(nxt.chunks) if nxt is not None else 0
        for c in range(max(len(mp.chunks), n_next)):
            if c < len(mp.chunks):
                mp.pass2(c)
            if c < n_next:
                nxt.pass1(c)
        if nxt is not None:
            nxt.finish_max()
        on_done(mp)


def _normalised_half(acc, e):
    l = acc[(1 - e) * HEAD_DIM:(1 - e) * HEAD_DIM + 1, :]
    return acc[e * HEAD_DIM:(e + 1) * HEAD_DIM, :] * (1.0 / l)


def _attn_kernel(*refs, has_cache, lam_init, seq_len):
    qa_ref, k3_ref, vta_ref, qb_ref, kb_ref, vtb_ref, lamv_ref, subln_ref = refs[:8]
    pos = 8
    if has_cache:
        cka_ref, cva_ref, ckb_ref, cvb_ref = refs[pos:pos + 4]
        pos += 4
    ao_ref, bo_ref, s_scr = refs[pos:pos + 3]

    left = _left_mask()
    lane = lax.broadcasted_iota(jnp.int32, (1, LANES), 1)
    kc = min(ATTN_KC, seq_len)
    n_kc = seq_len // kc
    maps = []

    def add_map(tag, qm, chunks):
        maps.append(_ScoreMap(tag, qm, chunks, s_scr.at[len(maps) % 2]))

    def new_chunks(k_ref, vt_ref, j, h):
        sl = slice(j * LANES, (j + 1) * LANES)
        out = []
        for c in range(n_kc):
            rows = slice(c * kc, (c + 1) * kc)
            out.append((functools.partial(lambda r, s: k_ref[0, r, s], rows, sl),
                        functools.partial(lambda r, hh: vt_ref[0, hh, :, r], rows, h), kc))
        return out

    def value_block(vt_half, e):
        ones = jnp.ones_like(vt_half)
        return jnp.concatenate([vt_half, ones] if e == 0 else [ones, vt_half], axis=0).astype(BF16)

    if has_cache:
        ck = cka_ref[0, 0]
        ck_sw = pltpu.roll(ck, LANES // 2, 1)
        ck3 = [jnp.where(left, ck, ck_sw), ck, jnp.where(left, ck_sw, ck)]
        cvt = cva_ref[0, 0].T
        past = ck.shape[0]
    for j in range(A_WIDTH // LANES):
        q = qa_ref[0, :, j * LANES:(j + 1) * LANES]
        for e in range(2):
            h = 2 * j + e
            g = h // A_GROUP
            chunks = new_chunks(k3_ref, vta_ref, j, h)
            if has_cache:
                ckj = ck3[j].astype(BF16)
                cvj = value_block(cvt[g * HEAD_DIM:(g + 1) * HEAD_DIM], e)
                chunks.append((lambda a=ckj: a, lambda a=cvj: a, past))
            qm = jnp.where(left if e == 0 else jnp.logical_not(left), q, jnp.zeros_like(q))
            add_map(("a", j, e, 0), qm, chunks)

    lv = lamv_ref[...]
    lam = (jnp.exp(jnp.sum(lv[0:1] * lv[1:2], axis=-1, keepdims=True))
           - jnp.exp(jnp.sum(lv[2:3] * lv[3:4], axis=-1, keepdims=True)) + lam_init)
    for j in range(B_WIDTH // LANES):
        sl = slice(j * LANES, (j + 1) * LANES)
        q = qb_ref[0, :, sl]
        if has_cache:
            ckj = ckb_ref[0, 0, :, sl].astype(BF16)
            cvt_j = cvb_ref[0, 0, :, sl].T
        for e in range(2):
            h = 2 * j + e
            chunks = new_chunks(kb_ref, vtb_ref, j, h)
            if has_cache:
                cvj = value_block(cvt_j[e * HEAD_DIM:(e + 1) * HEAD_DIM], e)
                chunks.append((lambda a=ckj: a, lambda a=cvj: a, ckj.shape[0]))
            for mi in range(2):
                sel = (lane // B_HALF) == (2 * e + mi)
                add_map(("b", j, e, mi), jnp.where(sel, q, jnp.zeros_like(q)), chunks)

    done = {}

    def on_done(mp):
        kind, j, e, mi = mp.tag
        done[mp.tag] = _normalised_half(mp.acc, e)
        if kind == "a" and e == 1:
            o_t = jnp.concatenate([done[("a", j, 0, 0)], done[("a", j, 1, 0)]], axis=0)
            ao_ref[0, :, j * LANES:(j + 1) * LANES] = o_t.T.astype(BF16)
        if kind == "b" and e == 1 and mi == 1:
            halves = []
            for ee in range(2):
                o = done[("b", j, ee, 0)] - lam * done[("b", j, ee, 1)]
                ms = jnp.mean(o * o, axis=0, keepdims=True)
                halves.append(o * lax.rsqrt(ms + EPS))
            o_t = jnp.concatenate(halves, axis=0)
            bo_ref[0, :, j * LANES:(j + 1) * LANES] = (
                o_t.T * subln_ref[...] * (1.0 - lam_init)).astype(BF16)

    _run_score_maps(maps, on_done)


def _attn(qa, k3, vta, qb, kb, vtb, lamv, subln, caches, layer, *, lam_init):
    nb, L, _ = qa.shape
    tq = min(ATTN_TQ, L)
    has_cache = caches is not None
    q_spec = lambda w: pl.BlockSpec((1, tq, w), lambda b, i: (b, i, 0))
    kv_spec = lambda w: pl.BlockSpec((1, L, w), lambda b, i: (b, 0, 0))
    vt_spec = lambda n: pl.BlockSpec((1, n, LANES, L), lambda b, i: (b, 0, 0, 0))
    in_specs = [q_spec(A_WIDTH), kv_spec(A_WIDTH), vt_spec(A_Q_HEADS),
                q_spec(B_WIDTH), kv_spec(B_WIDTH), vt_spec(B_HEADS),
                pl.BlockSpec((4, B_HALF), lambda b, i: (0, 0)),
                pl.BlockSpec((1, LANES), lambda b, i: (0, 0))]
    args = [qa, k3, vta, qb, kb, vtb, lamv, subln]
    lk = L
    if has_cache:
        past = caches[0].shape[2]
        lk += past
        for c in caches:
            in_specs.append(pl.BlockSpec((1, 1, past, c.shape[-1]), lambda b, i: (b, layer, 0, 0)))
            args.append(c)
    return pl.pallas_call(
        functools.partial(_attn_kernel, has_cache=has_cache, lam_init=lam_init, seq_len=L),
        out_shape=[jax.ShapeDtypeStruct((nb, L, A_WIDTH), BF16),
                   jax.ShapeDtypeStruct((nb, L, B_WIDTH), BF16)],
        grid=(nb, L // tq),
        in_specs=in_specs,
        out_specs=[q_spec(A_WIDTH), q_spec(B_WIDTH)],
        scratch_shapes=[pltpu.VMEM((2, lk, tq), F32)],
        compiler_params=_cparams(2),
        name="attn_cache" if has_cache else "attn",
    )(*args)


def _prefix_sums(tri, a):
    hi = a.astype(BF16)
    lo = (a - hi.astype(F32)).astype(BF16)
    return (jnp.dot(tri, hi, preferred_element_type=F32)
            + jnp.dot(tri, lo, preferred_element_type=F32))


def _pair_cols(x, c0):
    return jnp.where(_left_mask(), x[:, c0:c0 + 1], x[:, c0 + 1:c0 + 2])


def _ssd_kernel(*refs, has_init, emit_state, seq_len):
    xbc_ref, z_ref, dt_ref, cw_ref, cb_ref, dtb_ref, alog_ref, dexp_ref, nw_ref = refs[:9]
    pos = 9
    if has_init:
        init_ref = refs[pos]
        pos += 1
    co_ref = refs[pos]
    pos += 1
    if emit_state:
        st_ref = refs[pos]
        pos += 1
    xc_scr, y_scr, st_scr = refs[pos:pos + 3]

    L = seq_len
    rb = min(ROW_BLOCK, L)
    q = SSD_CHUNK
    nc = L // q
    left = _left_mask()

    w0 = cw_ref[0:1, :]
    w1 = cw_ref[1:2, :]
    w2 = cw_ref[2:3, :]
    row = lax.broadcasted_iota(jnp.int32, (rb, 1), 0)
    for r0 in range(0, L, rb):
        cur = xbc_ref[0, r0:r0 + rb, :]
        prev = pltpu.roll(cur, 1, 0)
        nxt = pltpu.roll(cur, rb - 1, 0)
        edge_prev = xbc_ref[0, r0 - 1:r0, :] if r0 > 0 else jnp.zeros((1, C_CONV_CH), F32)
        edge_next = xbc_ref[0, r0 + rb:r0 + rb + 1, :] if r0 + rb < L else jnp.zeros((1, C_CONV_CH), F32)
        prev = jnp.where(row == 0, edge_prev, prev)
        nxt = jnp.where(row == rb - 1, edge_next, nxt)
        y = prev * w0 + cur * w1 + nxt * w2 + cb_ref[...]
        xc = y * _sigmoid(y)
        xc_scr[r0:r0 + rb, :] = xc
        y_scr[r0:r0 + rb, :] = xc[:, 0:C_INNER] * dexp_ref[...]

    for d in range(2):
        if has_init:
            st_scr[d] = init_ref[0, 0, d].reshape(C_INNER, C_STATE).T
        else:
            st_scr[d] = jnp.zeros((C_STATE, C_INNER), F32)

    a_neg = -jnp.exp(alog_ref[...])
    ti = lax.broadcasted_iota(jnp.int32, (q, q), 0)
    si = lax.broadcasted_iota(jnp.int32, (q, q), 1)
    tri_incl = (si <= ti).astype(BF16)

    def process(d, r0):
        rows = pl.ds(r0, q)
        lo = d * C_HEADS
        causal = (si <= ti) if d == 0 else (si >= ti)
        xs = xc_scr[rows, 0:C_INNER]
        bm = [xc_scr[rows, C_INNER + g * C_STATE:C_INNER + (g + 1) * C_STATE] for g in range(C_GROUPS)]
        cm = [xc_scr[rows, C_INNER + (C_GROUPS + g) * C_STATE:C_INNER + (C_GROUPS + g + 1) * C_STATE]
              for g in range(C_GROUPS)]
        bm_b = [b.astype(BF16) for b in bm]
        cm_b = [c.astype(BF16) for c in cm]
        bmt_b = [b.T.astype(BF16) for b in bm]
        gmat = [lax.dot_general(cm_b[g], bm_b[g], (((1,), (1,)), ((), ())), preferred_element_type=F32)
                for g in range(C_GROUPS)]

        dtr = dt_ref[0, rows, :] + dtb_ref[...]
        dt = jnp.maximum(dtr, 0.0) + jnp.log1p(jnp.exp(-jnp.abs(dtr)))
        a = dt * a_neg
        cs = _prefix_sums(tri_incl, a)
        tot = cs[q - 1:q, :]
        ev = cs if d == 0 else (tot - cs + a)
        e_in = jnp.exp(ev)
        e_out = jnp.exp(tot - ev)
        dec = jnp.exp(tot)
        ev_t = ev.T

        st = st_scr.at[d]
        for j in range(C_INNER // LANES):
            sl = slice(j * LANES, (j + 1) * LANES)
            h0 = 2 * j
            g0, g1 = h0 // (C_HEADS // C_GROUPS), (h0 + 1) // (C_HEADS // C_GROUPS)
            xdt = xs[:, sl] * _pair_cols(dt, lo + h0)
            xdt_b = xdt.astype(BF16)
            st_blk = st[:, sl]
            st_b = st_blk.astype(BF16)
            yd = []
            for e in range(2):
                hh = lo + h0 + e
                g = g0 if e == 0 else g1
                diff = ev[:, hh:hh + 1] - ev_t[hh:hh + 1, :]
                lmat = jnp.exp(jnp.where(causal, diff, NEG_BIG))
                yd.append(jnp.dot((gmat[g] * lmat).astype(BF16), xdt_b, preferred_element_type=F32))
            yo0 = jnp.dot(cm_b[g0], st_b, preferred_element_type=F32)
            yo1 = yo0 if g1 == g0 else jnp.dot(cm_b[g1], st_b, preferred_element_type=F32)
            y_blk = (jnp.where(left, yd[0], yd[1])
                     + jnp.where(left, yo0, yo1) * _pair_cols(e_in, lo + h0))
            y_scr[rows, sl] = y_scr[rows, sl] + y_blk
            xw = (xdt * _pair_cols(e_out, lo + h0)).astype(BF16)
            up0 = jnp.dot(bmt_b[g0], xw, preferred_element_type=F32)
            up1 = up0 if g1 == g0 else jnp.dot(bmt_b[g1], xw, preferred_element_type=F32)
            st[:, sl] = st_blk * _pair_cols(dec, lo + h0) + jnp.where(left, up0, up1)

    if nc <= 2:
        for c in range(nc):
            process(0, c * q)
            process(1, (nc - 1 - c) * q)
    else:
        def body(c, carry):
            process(0, pl.multiple_of(c * q, q))
            process(1, pl.multiple_of((nc - 1 - c) * q, q))
            return carry
        lax.fori_loop(0, nc, body, 0)

    if emit_state:
        for d in range(2):
            st_ref[0, d] = st_scr[d].T.reshape(C_HEADS, C_HEADDIM, C_STATE)

    for r0 in range(0, L, rb):
        zz = z_ref[0, r0:r0 + rb, :]
        yg = y_scr[r0:r0 + rb, :] * (zz * _sigmoid(zz))
        ms = jnp.mean(yg * yg, axis=-1, keepdims=True)
        co_ref[0, r0:r0 + rb, :] = (yg * lax.rsqrt(ms + EPS) * nw_ref[...]).astype(BF16)


def _ssd(xbc, z, dt, conv_w, conv_b, dtb, alog, dexp, norm_w, state, layer, *, emit_state):
    nb, L, _ = xbc.shape
    has_init = state is not None
    seq_spec = lambda w: pl.BlockSpec((1, L, w), lambda b: (b, 0, 0))
    par_spec = lambda r, w: pl.BlockSpec((r, w), lambda b: (0, 0))
    in_specs = [seq_spec(C_CONV_CH), seq_spec(C_INNER), seq_spec(LANES),
                par_spec(3, C_CONV_CH), par_spec(1, C_CONV_CH), par_spec(1, LANES), par_spec(1, LANES),
                par_spec(1, C_INNER), par_spec(1, C_INNER)]
    args = [xbc, z, dt, conv_w, conv_b, dtb, alog, dexp, norm_w]
    if has_init:
        in_specs.append(pl.BlockSpec((1, 1, 2, C_HEADS, C_HEADDIM, C_STATE),
                                     lambda b: (b, layer, 0, 0, 0, 0)))
        args.append(state)
    out_shape = [jax.ShapeDtypeStruct((nb, L, C_INNER), BF16)]
    out_specs = [seq_spec(C_INNER)]
    if emit_state:
        out_shape.append(jax.ShapeDtypeStruct((nb, 2, C_HEADS, C_HEADDIM, C_STATE), F32))
        out_specs.append(pl.BlockSpec((1, 2, C_HEADS, C_HEADDIM, C_STATE), lambda b: (b, 0, 0, 0, 0)))
    return pl.pallas_call(
        functools.partial(_ssd_kernel, has_init=has_init, emit_state=emit_state, seq_len=L),
        out_shape=out_shape,
        grid=(nb,),
        in_specs=in_specs,
        out_specs=out_specs,
        scratch_shapes=[pltpu.VMEM((L, C_CONV_CH), F32), pltpu.VMEM((L, C_INNER), F32),
                        pltpu.VMEM((2, C_STATE, C_INNER), F32)],
        compiler_params=_cparams(1),
        name="ssd_init" if has_init else "ssd",
    )(*args)


def _outp_kernel(a_ref, b_ref, c_ref, x_ref, mod_ref, w_ref, x1_o, h2_o):
    mix = jnp.concatenate([a_ref[0], b_ref[0], c_ref[0]], axis=-1)
    o = jnp.dot(mix, w_ref[...], preferred_element_type=F32)
    gate1 = mod_ref[0, 2:3, :]
    shift2 = mod_ref[0, 3:4, :]
    scale2 = mod_ref[0, 4:5, :]
    x1 = x_ref[0] + gate1 * o
    ms = jnp.mean(x1 * x1, axis=-1, keepdims=True)
    x1_o[0] = x1
    h2_o[0] = ((x1 * lax.rsqrt(ms + EPS)) * (1.0 + scale2) + shift2).astype(BF16)


def _outp(a_out, b_out, c_out, x, mod, w_out_b, layer, *, per_batch_mod):
    nb, L, d = x.shape
    tm = min(PROJ_TM, L)
    mod_map = (lambda b, i: (b, 0, 0)) if per_batch_mod else (lambda b, i: (0, 0, 0))
    row_spec = lambda w: pl.BlockSpec((1, tm, w), lambda b, i: (b, i, 0))
    return pl.pallas_call(
        _outp_kernel,
        out_shape=[jax.ShapeDtypeStruct((nb, L, d), F32), jax.ShapeDtypeStruct((nb, L, d), BF16)],
        grid=(nb, L // tm),
        in_specs=[row_spec(A_WIDTH), row_spec(B_WIDTH), row_spec(C_INNER), row_spec(d),
                  pl.BlockSpec((1, 6, d), mod_map),
                  pl.BlockSpec((None, d, d), lambda b, i: (layer, 0, 0))],
        out_specs=[row_spec(d), row_spec(d)],
        compiler_params=_cparams(2),
        name="outp",
    )(a_out, b_out, c_out, x, mod, w_out_b)


HALO = 16


def _ffn_kernel(h_ref, hp_ref, hn_ref, x1_ref, mod_ref, wu_ref, cw_ref, cb_ref, wd_ref, fw_ref,
                o_ref, hext_scr, u_scr, acc_scr, *, seq_len, final_norm):
    tm = h_ref.shape[1]
    i = pl.program_id(1)
    n_chunks = wu_ref.shape[0]
    rb = min(FFN_RB, tm)

    has_prev = ((i * tm) & (seq_len - 1)) != 0
    has_next = ((i * tm + tm) & (seq_len - 1)) != 0
    hext_scr[0:HALO, :] = jnp.where(has_prev, hp_ref[0], jnp.zeros_like(hp_ref[0]))
    hext_scr[HALO:HALO + tm, :] = h_ref[0]
    hext_scr[HALO + tm:HALO + tm + HALO, :] = jnp.where(has_next, hn_ref[0], jnp.zeros_like(hn_ref[0]))
    acc_scr[...] = jnp.zeros_like(acc_scr)
    row8 = lax.broadcasted_iota(jnp.int32, (8, 1), 0)

    n_uslab = 2 * FFN_FC // LANES
    n_oslab = acc_scr.shape[0]
    half = rb // 2

    def up(c, slot):
        u = jnp.dot(hext_scr[...], wu_ref[c], preferred_element_type=F32)
        for s in range(n_uslab):
            u_scr[slot, s] = u[:, s * LANES:(s + 1) * LANES]

    def conv_down(c, slot):
        cw = cw_ref[c]
        bias = cb_ref[c]
        for r0 in range(0, tm, rb):
            base = HALO + r0
            ys = []
            for s in range(n_uslab):
                ls = slice(s * LANES, (s + 1) * LANES)
                even = u_scr[slot, s, pl.ds(base, half, stride=2), :]
                odd = u_scr[slot, s, pl.ds(base + 1, half, stride=2), :]
                odd_before = u_scr[slot, s, pl.ds(base - 1, half, stride=2), :]
                even_after = u_scr[slot, s, pl.ds(base + 2, half, stride=2), :]
                if r0 > 0 and r0 % seq_len == 0:
                    odd_before = jnp.concatenate(
                        [jnp.where(row8 == 0, 0.0, odd_before[0:8]), odd_before[8:]], axis=0)
                if r0 + rb < tm and (r0 + rb) % seq_len == 0:
                    even_after = jnp.concatenate(
                        [even_after[:half - 8], jnp.where(row8 == 7, 0.0, even_after[half - 8:])], axis=0)
                w0, w1, w2, b = cw[0:1, ls], cw[1:2, ls], cw[2:3, ls], bias[:, ls]
                y_even = odd_before * w0 + even * w1 + odd * w2 + b
                y_odd = even * w0 + odd * w1 + even_after * w2 + b
                ys.append(jnp.concatenate([y_even, y_odd], axis=0))
            n_g = n_uslab // 2
            act = jnp.concatenate([ys[s] * _sigmoid(ys[s]) * ys[n_g + s] for s in range(n_g)], axis=1)
            dn = jnp.dot(act.astype(BF16), wd_ref[c], preferred_element_type=F32)
            for s in range(n_oslab):
                ls = slice(s * LANES, (s + 1) * LANES)
                acc_scr[s, pl.ds(r0, half, stride=2), :] += dn[0:half, ls]
                acc_scr[s, pl.ds(r0 + 1, half, stride=2), :] += dn[half:rb, ls]

    up(0, 0)

    def pair(k, carry):
        c = 2 * k
        up(c + 1, 1)
        conv_down(c, 0)
        up(c + 2, 0)
        conv_down(c + 1, 1)
        return carry

    lax.fori_loop(0, (n_chunks - 1) // 2, pair, 0)
    if n_chunks % 2 == 1:
        conv_down(n_chunks - 1, 0)
    else:
        up(n_chunks - 1, 1)
        conv_down(n_chunks - 2, 0)
        conv_down(n_chunks - 1, 1)

    gate2 = mod_ref[0, 5:6, :]
    ffn_out = jnp.concatenate([acc_scr[s] for s in range(n_oslab)], axis=1)
    out = x1_ref[0] + gate2 * ffn_out
    if final_norm:
        ms = jnp.mean(out * out, axis=-1, keepdims=True)
        out = out * lax.rsqrt(ms + EPS) * fw_ref[...]
    o_ref[0] = out


def _ffn(h2, x1, mod, wu_c, cw_c, cb_c, wd_c, layer, final_w, *, per_batch_mod, seq_len, final_norm):
    nb, L, d = x1.shape
    tm = min(FFN_TM, L)
    n_tiles = L // tm
    rb = min(FFN_RB, tm)
    assert seq_len & (seq_len - 1) == 0 and L % seq_len == 0
    assert seq_len % tm == 0 or (tm % seq_len == 0 and seq_len % rb == 0)
    n_chunks = wu_c.shape[1]
    hb = tm // HALO
    n_hblk = L // HALO
    mod_map = (lambda b, i: (b, 0, 0)) if per_batch_mod else (lambda b, i: (0, 0, 0))
    row_spec = lambda w: pl.BlockSpec((1, tm, w), lambda b, i: (b, i, 0))
    const3 = lambda s: pl.BlockSpec(s, lambda b, i: (0, 0, 0))
    return pl.pallas_call(
        functools.partial(_ffn_kernel, seq_len=seq_len, final_norm=final_norm),
        out_shape=jax.ShapeDtypeStruct((nb, L, d), F32),
        grid=(nb, n_tiles),
        in_specs=[row_spec(d),
                  pl.BlockSpec((1, HALO, d), lambda b, i: (b, jnp.maximum(i * hb - 1, 0), 0)),
                  pl.BlockSpec((1, HALO, d), lambda b, i: (b, jnp.minimum((i + 1) * hb, n_hblk - 1), 0)),
                  row_spec(d),
                  pl.BlockSpec((1, 6, d), mod_map),
                  pl.BlockSpec((None, n_chunks, d, 2 * FFN_FC), lambda b, i: (layer, 0, 0, 0)),
                  const3((n_chunks, 3, 2 * FFN_FC)),
                  const3((n_chunks, 1, 2 * FFN_FC)),
                  pl.BlockSpec((None, n_chunks, FFN_FC, d), lambda b, i: (layer, 0, 0, 0)),
                  pl.BlockSpec((1, d), lambda b, i: (0, 0))],
        out_specs=row_spec(d),
        scratch_shapes=[pltpu.VMEM((tm + 2 * HALO, d), BF16),
                        pltpu.VMEM((2, 2 * FFN_FC // LANES, tm + 2 * HALO, LANES), F32),
                        pltpu.VMEM((d // LANES, tm, LANES), F32)],
        compiler_params=_cparams(2),
        name="ffn_final" if final_norm else "ffn",
    )(h2, h2, h2, x1, mod, wu_c, cw_c, cb_c, wd_c, final_w)


def _rope_tables(L, d):
    rows = L // GRID_W
    row = jnp.repeat(jnp.arange(rows), GRID_W).astype(F32)
    col = jnp.tile(jnp.arange(GRID_W), rows).astype(F32)
    quarter = d // 4
    inv = ROPE_THETA ** (-jnp.arange(quarter, dtype=F32) / quarter)
    ang_r = row[:, None] * inv[None, :]
    ang_c = col[:, None] * inv[None, :]
    ang = jnp.concatenate([ang_r, ang_r, ang_c, ang_c], axis=-1)
    cos, sin = jnp.cos(ang), jnp.sin(ang)
    even = ((jnp.arange(d) // quarter) % 2 == 0)[None, :]
    s_up = jnp.where(even, -sin, 0.0)
    s_dn = jnp.where(even, 0.0, sin)
    reps = LANES // d
    return tuple(jnp.tile(t, (1, reps)) for t in (cos, s_up, s_dn))


def _pad_lanes(v, width=LANES):
    v = v.reshape(1, -1).astype(F32)
    return jnp.pad(v, ((0, 0), (0, width - v.shape[1])))


def _layer_params(l, a_q_norm, a_k_norm, b_lambda, b_subln, ssm_conv_w, ssm_conv_b, ssm_A_log,
                  ssm_dt_bias, ssm_D, ssm_norm_w, ffn_conv_w, ffn_conv_b):
    n_chunks = D_FF // FFN_FC

    def gv_chunks(t):
        lead = t.shape[:-1]
        g = t[..., :D_FF].reshape(lead + (n_chunks, FFN_FC))
        v = t[..., D_FF:].reshape(lead + (n_chunks, FFN_FC))
        gv = jnp.concatenate([g, v], axis=-1)
        return jnp.moveaxis(gv, -2, 0)

    return dict(
        qn=jnp.tile(a_q_norm[l], 2).reshape(1, LANES),
        kn=jnp.tile(a_k_norm[l], 2).reshape(1, LANES),
        lamv=b_lambda[l],
        subln=jnp.tile(b_subln[l], 2).reshape(1, LANES),
        conv_w=ssm_conv_w[l],
        conv_b=ssm_conv_b[l].reshape(1, C_CONV_CH),
        dtb=_pad_lanes(ssm_dt_bias[l]),
        alog=_pad_lanes(ssm_A_log[l]),
        dexp=jnp.repeat(ssm_D[l], C_HEADDIM).reshape(1, C_INNER),
        norm_w=ssm_norm_w[l].reshape(1, C_INNER),
        cw=gv_chunks(ffn_conv_w[l]),
        cb=gv_chunks(ffn_conv_b[l].reshape(1, 2 * D_FF)),
    )


def _block(x, mod, p, wts, rope_tabs, caches, state, layer, *, seq_shape, per_batch_mod, is_ctx, lam_init,
           final_w, final_norm):
    n_seq, seq_len = seq_shape
    outs = _proj(x, mod, wts["w_in"], layer, p["qn"], p["kn"], rope_tabs,
                 per_batch_mod=per_batch_mod, kv_f32=is_ctx, seq_len=seq_len)
    seq = lambda t: t.reshape(n_seq, seq_len, t.shape[-1]) if t.ndim == 3 else t
    qa, k3, vta, qb, kb, vtb, xbc, z, dt = [seq(t) for t in outs[:9]]
    a_out, b_out = _attn(qa, k3, vta, qb, kb, vtb, p["lamv"], p["subln"], caches, layer, lam_init=lam_init)
    ssd_out = _ssd(xbc, z, dt, p["conv_w"], p["conv_b"], p["dtb"], p["alog"], p["dexp"], p["norm_w"],
                   state, layer, emit_state=is_ctx)
    c_out = ssd_out[0]
    flat = lambda t: t.reshape(x.shape[0], x.shape[1], t.shape[-1])
    x1, h2 = _outp(flat(a_out), flat(b_out), flat(c_out), x, mod, wts["w_out"], layer,
                   per_batch_mod=per_batch_mod)
    x2 = _ffn(h2, x1, mod, wts["wu"], p["cw"], p["cb"], wts["wd"], layer, final_w,
              per_batch_mod=per_batch_mod, seq_len=seq_len, final_norm=final_norm)
    extras = None
    if is_ctx:
        extras = tuple(seq(t) for t in outs[9:13]) + (ssd_out[1],)
    return x2, extras


def kernel(x_prompt, x_sample, cache_a_k, cache_a_v, cache_b_k, cache_b_v, state_ssm, c, c_ctx, ada_w, ada_b, w_in, a_q_norm, a_k_norm, b_lambda, b_subln, ssm_conv_w, ssm_conv_b, ssm_A_log, ssm_dt_bias, ssm_D, ssm_norm_w, w_out, ffn_up, ffn_conv_w, ffn_conv_b, ffn_down, final_norm_w):
    depth = w_in.shape[0]
    nbp, lp, d = x_prompt.shape
    nbs, ls, _ = x_sample.shape
    past = cache_a_k.shape[2]

    mod_rows = 8
    assert 1 + nbs <= mod_rows
    cvecs = jnp.concatenate([c_ctx[None, :], c, jnp.zeros((mod_rows - 1 - nbs, d), F32)], axis=0)
    mod = _modulation(cvecs, ada_w, ada_b)

    rope_tabs = _rope_tables(ls, HEAD_DIM) + _rope_tables(ls, B_HALF)
    caches = (cache_a_k.reshape(nbs, depth, past, A_KV_HEADS * HEAD_DIM),
              cache_a_v.reshape(nbs, depth, past, A_KV_HEADS * HEAD_DIM),
              cache_b_k.reshape(nbs, depth, past, B_WIDTH),
              cache_b_v.reshape(nbs, depth, past, B_WIDTH))
    final_w = final_norm_w.reshape(1, d)
    n_chunks = D_FF // FFN_FC
    wts = dict(
        w_in=_pad_cast_bf16(w_in, PROJ_PAD),
        w_out=_cast_bf16(w_out),
        wu=_gv_cast_bf16(ffn_up),
        wd=_cast_bf16(ffn_down).reshape(depth, n_chunks, FFN_FC, d),
    )

    yp = x_prompt.reshape(1, nbp * lp, d)
    ys = x_sample
    ctx_extras = []
    for l in range(depth):
        lam_init = 0.8 - 0.6 * math.exp(-0.3 * l)
        p = _layer_params(l, a_q_norm, a_k_norm, b_lambda, b_subln, ssm_conv_w, ssm_conv_b,
                          ssm_A_log, ssm_dt_bias, ssm_D, ssm_norm_w, ffn_conv_w, ffn_conv_b)
        last = l == depth - 1
        mod_ctx = mod[l, 0].reshape(1, 6, d)
        mod_lat = mod[l, 1:1 + nbs].reshape(nbs, 6, d)
        yp, extras = _block(yp, mod_ctx, p, wts, None, None, None, l, seq_shape=(nbp, lp),
                            per_batch_mod=False, is_ctx=True, lam_init=lam_init,
                            final_w=final_w, final_norm=last)
        ctx_extras.append(extras)
        ys, _ = _block(ys, mod_lat, p, wts, rope_tabs, caches, state_ssm, l, seq_shape=(nbs, ls),
                       per_batch_mod=True, is_ctx=False, lam_init=lam_init,
                       final_w=final_w, final_norm=last)

    y_prompt = yp.reshape(nbp, lp, d)
    stack = lambda k: jnp.stack([e[k] for e in ctx_extras], axis=1)
    new_a_k = stack(0).reshape(nbp, depth, lp, A_KV_HEADS, HEAD_DIM)
    new_a_v = stack(1).reshape(nbp, depth, lp, A_KV_HEADS, HEAD_DIM)
    new_b_k = stack(2).reshape(nbp, depth, lp, B_HEADS, 2, B_HALF)
    new_b_v = stack(3).reshape(nbp, depth, lp, B_HEADS, 2 * B_HALF)
    new_ssm = stack(4)
    return (y_prompt, ys, new_a_k, new_a_v, new_b_k, new_b_v, new_ssm)
```

```python
import functools
import math

import jax
import jax.numpy as jnp
from jax import lax
from jax.experimental import pallas as pl
from jax.experimental.pallas import tpu as pltpu

F32 = jnp.float32
BF16 = jnp.bfloat16

LANES = 128
VMEM_LIMIT_BYTES = 56 * 1024 * 1024

D_MODEL = 1024
GRID_W = 64
SSD_CHUNK = 128
ROPE_THETA = 10000.0
EPS = 1e-6
HEAD_DIM = 64
A_Q_HEADS = 6
A_KV_HEADS = 2
A_GROUP = A_Q_HEADS // A_KV_HEADS
A_WIDTH = A_Q_HEADS * HEAD_DIM
B_HEADS = 4
B_HALF = 32
B_WIDTH = B_HEADS * 2 * B_HALF
C_HEADS = 6
C_HEADDIM = 64
C_INNER = C_HEADS * C_HEADDIM
C_GROUPS = 2
C_STATE = 128
C_CONV_CH = C_INNER + 2 * C_GROUPS * C_STATE
D_FF = 2816
PROJ_WIDTH = A_WIDTH + 4 * HEAD_DIM + 3 * B_WIDTH + 2 * C_INNER + 4 * C_STATE + 2 * C_HEADS
PROJ_PAD = ((PROJ_WIDTH + 2 * LANES - 1) // (2 * LANES)) * (2 * LANES)

OFF_AQ = 0
OFF_AK = OFF_AQ + A_WIDTH
OFF_AV = OFF_AK + A_KV_HEADS * HEAD_DIM
OFF_BQ = OFF_AV + A_KV_HEADS * HEAD_DIM
OFF_BK = OFF_BQ + B_WIDTH
OFF_BV = OFF_BK + B_WIDTH
OFF_CX = OFF_BV + B_WIDTH
OFF_CZ = OFF_CX + C_INNER
OFF_CB = OFF_CZ + C_INNER
OFF_CC = OFF_CB + C_GROUPS * C_STATE
OFF_DT = OFF_CC + C_GROUPS * C_STATE

MOD_BLOCK = 1024
PROJ_TM = 512
FFN_TM = 512
FFN_FC = 256
FFN_RB = 256
ATTN_TQ = 512
ATTN_KC = 512
ROW_BLOCK = 256
CONV_ROWS = 128
NEG_BIG = -1e30
LOG2E = 1.4426950408889634


def _cparams(n_grid):
    return pltpu.CompilerParams(
        dimension_semantics=("parallel",) * n_grid,
        vmem_limit_bytes=VMEM_LIMIT_BYTES,
    )


def _sigmoid(x):
    return 1.0 / (1.0 + jnp.exp(-x))


def _left_mask():
    lane = lax.broadcasted_iota(jnp.int32, (1, LANES), 1)
    return lane < (LANES // 2)


def _pair_rmsnorm(y, w):
    left = _left_mask()
    sq = y * y
    s_l = jnp.sum(jnp.where(left, sq, 0.0), axis=-1, keepdims=True)
    s_r = jnp.sum(jnp.where(left, 0.0, sq), axis=-1, keepdims=True)
    ms = jnp.where(left, s_l, s_r) * (1.0 / HEAD_DIM)
    return y * lax.rsqrt(ms + EPS) * w


def _rope(y, cos, s_up, s_dn, quarter):
    return (y * cos
            + pltpu.roll(y, LANES - quarter, 1) * s_up
            + pltpu.roll(y, quarter, 1) * s_dn)


def _modulation_kernel(c_ref, w_ref, b_ref, o_ref):
    c = c_ref[...]
    s = c * _sigmoid(c)
    o_ref[0] = jnp.dot(s.astype(BF16), w_ref[0].astype(BF16),
                       preferred_element_type=F32) + b_ref[0]


def _modulation(cvecs, ada_w, ada_b):
    depth, d, width = ada_w.shape
    rows = cvecs.shape[0]
    return pl.pallas_call(
        _modulation_kernel,
        out_shape=jax.ShapeDtypeStruct((depth, rows, width), F32),
        grid=(depth, width // MOD_BLOCK),
        in_specs=[
            pl.BlockSpec((rows, d), lambda l, j: (0, 0)),
            pl.BlockSpec((1, d, MOD_BLOCK), lambda l, j: (l, 0, j)),
            pl.BlockSpec((1, 1, MOD_BLOCK), lambda l, j: (l, 0, j)),
        ],
        out_specs=pl.BlockSpec((1, rows, MOD_BLOCK), lambda l, j: (l, 0, j)),
        compiler_params=_cparams(2),
        name="modulation",
    )(cvecs, ada_w, ada_b.reshape(depth, 1, width))


CAST_ROWS = 256


def _cast_kernel(x_ref, o_ref):
    o_ref[...] = x_ref[...].astype(o_ref.dtype)


def _cast_bf16(x):
    depth, rows, cols = x.shape
    spec = pl.BlockSpec((1, CAST_ROWS, cols), lambda l, r: (l, r, 0))
    return pl.pallas_call(
        _cast_kernel, out_shape=jax.ShapeDtypeStruct(x.shape, BF16),
        grid=(depth, rows // CAST_ROWS), in_specs=[spec], out_specs=spec,
        compiler_params=_cparams(2), name="cast_bf16",
    )(x)


def _pad_cast_kernel(x_ref, o_ref):
    rows, w = x_ref.shape[1], x_ref.shape[2]
    aligned = (w // LANES) * LANES
    o_ref[0, :, 0:aligned] = x_ref[0, :, 0:aligned].astype(BF16)
    o_ref[0, :, aligned:] = jnp.zeros((rows, o_ref.shape[2] - aligned), BF16)
    o_ref[0, :, aligned:w] = x_ref[0, :, aligned:w].astype(BF16)


def _pad_cast_bf16(x, width):
    depth, rows, cols = x.shape
    return pl.pallas_call(
        _pad_cast_kernel, out_shape=jax.ShapeDtypeStruct((depth, rows, width), BF16),
        grid=(depth, rows // CAST_ROWS),
        in_specs=[pl.BlockSpec((1, CAST_ROWS, cols), lambda l, r: (l, r, 0))],
        out_specs=pl.BlockSpec((1, CAST_ROWS, width), lambda l, r: (l, r, 0)),
        compiler_params=_cparams(2), name="pad_cast_bf16",
    )(x)


def _gv_cast_kernel(g_ref, v_ref, o_ref):
    o_ref[0, 0, :, 0:FFN_FC] = g_ref[0].astype(BF16)
    o_ref[0, 0, :, FFN_FC:2 * FFN_FC] = v_ref[0].astype(BF16)


def _gv_cast_bf16(up):
    depth, d, _ = up.shape
    n_chunks = D_FF // FFN_FC
    return pl.pallas_call(
        _gv_cast_kernel, out_shape=jax.ShapeDtypeStruct((depth, n_chunks, d, 2 * FFN_FC), BF16),
        grid=(depth, n_chunks),
        in_specs=[pl.BlockSpec((1, d, FFN_FC), lambda l, c: (l, 0, c)),
                  pl.BlockSpec((1, d, FFN_FC), lambda l, c: (l, 0, n_chunks + c))],
        out_specs=pl.BlockSpec((1, 1, d, 2 * FFN_FC), lambda l, c: (l, c, 0, 0)),
        compiler_params=_cparams(2), name="gv_cast_bf16",
    )(up, up)


def _proj_kernel(*refs, rope, kv_f32):
    x_ref, mod_ref, w_ref, qn_ref, kn_ref = refs[:5]
    pos = 5
    if rope:
        ca_ref, sau_ref, sad_ref, cb_ref, sbu_ref, sbd_ref = refs[pos:pos + 6]
        pos += 6
    (qa_o, k3_o, vta_o, qb_o, kb_o, vtb_o, xbc_o, z_o, dt_o) = refs[pos:pos + 9]
    pos += 9
    if kv_f32:
        ka32_o, va32_o, kb32_o, vb32_o = refs[pos:pos + 4]
        pos += 4
    p_scr = refs[pos]

    x = x_ref[0]
    ms = jnp.mean(x * x, axis=-1, keepdims=True)
    shift = mod_ref[0, 0:1, :]
    scale = mod_ref[0, 1:2, :]
    h = (x * lax.rsqrt(ms + EPS)) * (1.0 + scale) + shift
    p_scr[...] = jnp.dot(h.astype(BF16), w_ref[...], preferred_element_type=F32)

    left = _left_mask()

    def rope_a(y):
        if not rope:
            return y
        return _rope(y, ca_ref[...], sau_ref[...], sad_ref[...], HEAD_DIM // 4)

    def rope_b(y):
        if not rope:
            return y
        return _rope(y, cb_ref[...], sbu_ref[...], sbd_ref[...], B_HALF // 4)

    values_t = len(vta_o.shape) == 4
    if values_t:
        n_sub, sub_len = vta_o.shape[0], vta_o.shape[3]
        ones_half = jnp.ones((HEAD_DIM, sub_len), BF16)

    def store_vt(dst, h, vt_half, e):
        for s in range(n_sub):
            dst[s, h, e * HEAD_DIM:(e + 1) * HEAD_DIM, :] = vt_half[:, s * sub_len:(s + 1) * sub_len]
            dst[s, h, (1 - e) * HEAD_DIM:(2 - e) * HEAD_DIM, :] = ones_half

    a_scale = HEAD_DIM ** -0.5 * LOG2E
    for j in range(A_WIDTH // LANES):
        y = p_scr[:, OFF_AQ + j * LANES:OFF_AQ + (j + 1) * LANES]
        y = rope_a(_pair_rmsnorm(y, qn_ref[...]))
        qa_o[0, :, j * LANES:(j + 1) * LANES] = (y * a_scale).astype(BF16)

    k = rope_a(_pair_rmsnorm(p_scr[:, OFF_AK:OFF_AK + LANES], kn_ref[...]))
    v = p_scr[:, OFF_AV:OFF_AV + LANES]
    if kv_f32:
        ka32_o[0] = k
        va32_o[0] = v
    swapped = pltpu.roll(k, LANES // 2, 1)
    k3_o[0, :, 0:LANES] = jnp.where(left, k, swapped).astype(BF16)
    k3_o[0, :, LANES:2 * LANES] = k.astype(BF16)
    k3_o[0, :, 2 * LANES:3 * LANES] = jnp.where(left, swapped, k).astype(BF16)
    if values_t:
        vt = v.T.astype(BF16)
        for h in range(A_Q_HEADS):
            g = h // A_GROUP
            store_vt(vta_o, h, vt[g * HEAD_DIM:(g + 1) * HEAD_DIM], h % 2)
    else:
        v_sw = pltpu.roll(v, LANES // 2, 1)
        vta_o[0, :, 0:LANES] = jnp.where(left, v, v_sw).astype(BF16)
        vta_o[0, :, LANES:2 * LANES] = v.astype(BF16)
        vta_o[0, :, 2 * LANES:3 * LANES] = jnp.where(left, v_sw, v).astype(BF16)

    b_scale = B_HALF ** -0.5 * LOG2E
    for j in range(B_WIDTH // LANES):
        sl = slice(j * LANES, (j + 1) * LANES)
        q = rope_b(p_scr[:, OFF_BQ + j * LANES:OFF_BQ + (j + 1) * LANES])
        qb_o[0, :, sl] = (q * b_scale).astype(BF16)
        kb = rope_b(p_scr[:, OFF_BK + j * LANES:OFF_BK + (j + 1) * LANES])
        vb = p_scr[:, OFF_BV + j * LANES:OFF_BV + (j + 1) * LANES]
        kb_o[0, :, sl] = kb.astype(BF16)
        if values_t:
            vbt = vb.T.astype(BF16)
            for e in range(2):
                store_vt(vtb_o, 2 * j + e, vbt[e * HEAD_DIM:(e + 1) * HEAD_DIM], e)
        else:
            vtb_o[0, :, sl] = vb.astype(BF16)
        if kv_f32:
            kb32_o[0, :, sl] = kb
            vb32_o[0, :, sl] = vb

    xbc_sub, xbc_len = xbc_o.shape[0], xbc_o.shape[2]
    for s in range(C_CONV_CH // LANES):
        col = OFF_CX + s * LANES if s < C_INNER // LANES else OFF_CB + s * LANES - C_INNER
        for u in range(xbc_sub):
            xbc_o[u, s] = p_scr[u * xbc_len:(u + 1) * xbc_len, col:col + LANES]
    z_o[0] = p_scr[:, OFF_CZ:OFF_CZ + C_INNER]
    dt_o[0] = p_scr[:, OFF_DT:OFF_DT + LANES]


def _proj(x, mod, w_in_p, layer, qn, kn, rope_tabs, *, per_batch_mod, kv_f32, seq_len, values_t):
    nb, L, d = x.shape
    tm = min(PROJ_TM, L)
    rope = rope_tabs is not None
    mod_map = (lambda b, i: (b, 0, 0)) if per_batch_mod else (lambda b, i: (0, 0, 0))
    row_spec = lambda w: pl.BlockSpec((1, tm, w), lambda b, i: (b, i, 0))
    in_specs = [
        row_spec(d),
        pl.BlockSpec((1, 6, d), mod_map),
        pl.BlockSpec((None, d, PROJ_PAD), lambda b, i: (layer, 0, 0)),
        pl.BlockSpec((1, LANES), lambda b, i: (0, 0)),
        pl.BlockSpec((1, LANES), lambda b, i: (0, 0)),
    ]
    args = [x, mod, w_in_p, qn, kn]
    if rope:
        in_specs += [pl.BlockSpec((tm, LANES), lambda b, i: (i, 0))] * 6
        args += list(rope_tabs)
    widths = [(A_WIDTH, BF16), (A_WIDTH, BF16), (-A_Q_HEADS if values_t else A_WIDTH, BF16),
              (B_WIDTH, BF16), (B_WIDTH, BF16), (-B_HEADS if values_t else B_WIDTH, BF16),
              ("slabs", F32), (C_INNER, F32), (LANES, F32)]
    if kv_f32:
        widths += [(LANES, F32), (LANES, F32), (B_WIDTH, F32), (B_WIDTH, F32)]
    out_shape, out_specs = [], []
    n_seq = nb * L // seq_len
    in_seq = seq_len >= tm
    assert (L == seq_len) if in_seq else (nb == 1 and tm % seq_len == 0)
    n_slab = C_CONV_CH // LANES
    for w, dt in widths:
        if w == "slabs":
            out_shape.append(jax.ShapeDtypeStruct((n_seq, n_slab, seq_len, LANES), dt))
            out_specs.append(pl.BlockSpec((1, n_slab, tm, LANES), lambda b, i: (b, 0, i, 0)) if in_seq else
                             pl.BlockSpec((tm // seq_len, n_slab, seq_len, LANES), lambda b, i: (i, 0, 0, 0)))
        elif w > 0:
            out_shape.append(jax.ShapeDtypeStruct((nb, L, w), dt))
            out_specs.append(row_spec(w))
        else:
            out_shape.append(jax.ShapeDtypeStruct((n_seq, -w, LANES, seq_len), dt))
            out_specs.append(pl.BlockSpec((1, -w, LANES, tm), lambda b, i: (b, 0, 0, i)) if in_seq else
                             pl.BlockSpec((tm // seq_len, -w, LANES, seq_len), lambda b, i: (i, 0, 0, 0)))
    return pl.pallas_call(
        functools.partial(_proj_kernel, rope=rope, kv_f32=kv_f32),
        out_shape=out_shape,
        grid=(nb, L // tm),
        in_specs=in_specs,
        out_specs=out_specs,
        scratch_shapes=[pltpu.VMEM((tm, PROJ_PAD), F32)],
        compiler_params=_cparams(2),
        name="proj_rope" if rope else "proj",
    )(*args)


class _ScoreMap:
    def __init__(self, tag, qm, chunks, s_ref):
        self.tag, self.qm, self.chunks, self.s_ref = tag, qm, chunks, s_ref
        self.offsets = [sum(w for _, _, w in chunks[:c]) for c in range(len(chunks))]
        self.m_part = self.m = self.acc = None

    def pass1(self, c):
        k_fn, _, w = self.chunks[c]
        tq = self.qm.shape[0]
        s = lax.dot_general(k_fn(), self.qm, (((1,), (1,)), ((), ())), preferred_element_type=F32)
        self.s_ref[self.offsets[c]:self.offsets[c] + w, :] = s
        part = jnp.max(s.reshape(w // 8, 8, tq), axis=0)
        self.m_part = part if self.m_part is None else jnp.maximum(self.m_part, part)

    def finish_max(self):
        self.m = jnp.max(self.m_part, axis=0, keepdims=True)

    def pass2(self, c):
        _, vt_fn, w = self.chunks[c]
        e = jnp.exp2(self.s_ref[self.offsets[c]:self.offsets[c] + w, :] - self.m).astype(BF16)
        pv = jnp.dot(vt_fn(), e, preferred_element_type=F32)
        self.acc = pv if self.acc is None else self.acc + pv


def _run_score_maps(maps, on_done):
    first = maps[0]
    for c in range(len(first.chunks)):
        first.pass1(c)
    first.finish_max()
    for i, mp in enumerate(maps):
        nxt = maps[i + 1] if i + 1 < len(maps) else None
        n_next = len(nxt.chunks) if nxt is not None else 0
        for c in range(max(len(mp.chunks), n_next)):
            if c < len(mp.chunks):
                mp.pass2(c)
            if c < n_next:
                nxt.pass1(c)
        if nxt is not None:
            nxt.finish_max()
        on_done(mp)


def _normalised_half(acc, e):
    l = acc[(1 - e) * HEAD_DIM:(1 - e) * HEAD_DIM + 1, :]
    return acc[e * HEAD_DIM:(e + 1) * HEAD_DIM, :] * (1.0 / l)


def _attn_kernel(*refs, has_cache, lam_init, seq_len):
    qa_ref, k3_ref, vta_ref, qb_ref, kb_ref, vtb_ref, lamv_ref, subln_ref = refs[:8]
    pos = 8
    if has_cache:
        cka_ref, cva_ref, ckb_ref, cvb_ref = refs[pos:pos + 4]
        pos += 4
    ao_ref, bo_ref, s_scr = refs[pos:pos + 3]

    left = _left_mask()
    lane = lax.broadcasted_iota(jnp.int32, (1, LANES), 1)
    kc = min(ATTN_KC, seq_len)
    n_kc = seq_len // kc
    maps = []

    def add_map(tag, qm, chunks):
        maps.append(_ScoreMap(tag, qm, chunks, s_scr.at[len(maps) % 2]))

    def new_chunks(k_ref, vt_ref, j, h):
        sl = slice(j * LANES, (j + 1) * LANES)
        out = []
        for c in range(n_kc):
            rows = slice(c * kc, (c + 1) * kc)
            out.append((functools.partial(lambda r, s: k_ref[0, r, s], rows, sl),
                        functools.partial(lambda r, hh: vt_ref[0, hh, :, r], rows, h), kc))
        return out

    def value_block(vt_half, e):
        ones = jnp.ones_like(vt_half)
        return jnp.concatenate([vt_half, ones] if e == 0 else [ones, vt_half], axis=0).astype(BF16)

    if has_cache:
        ck = cka_ref[0, 0]
        ck_sw = pltpu.roll(ck, LANES // 2, 1)
        ck3 = [jnp.where(left, ck, ck_sw), ck, jnp.where(left, ck_sw, ck)]
        cvt = cva_ref[0, 0].T
        past = ck.shape[0]
    for j in range(A_WIDTH // LANES):
        q = qa_ref[0, :, j * LANES:(j + 1) * LANES]
        for e in range(2):
            h = 2 * j + e
            g = h // A_GROUP
            chunks = new_chunks(k3_ref, vta_ref, j, h)
            if has_cache:
                ckj = ck3[j].astype(BF16)
                cvj = value_block(cvt[g * HEAD_DIM:(g + 1) * HEAD_DIM], e)
                chunks.append((lambda a=ckj: a, lambda a=cvj: a, past))
            qm = jnp.where(left if e == 0 else jnp.logical_not(left), q, jnp.zeros_like(q))
            add_map(("a", j, e, 0), qm, chunks)

    lv = lamv_ref[...]
    lam = (jnp.exp(jnp.sum(lv[0:1] * lv[1:2], axis=-1, keepdims=True))
           - jnp.exp(jnp.sum(lv[2:3] * lv[3:4], axis=-1, keepdims=True)) + lam_init)
    for j in range(B_WIDTH // LANES):
        sl = slice(j * LANES, (j + 1) * LANES)
        q = qb_ref[0, :, sl]
        if has_cache:
            ckj = ckb_ref[0, 0, :, sl].astype(BF16)
            cvt_j = cvb_ref[0, 0, :, sl].T
        for e in range(2):
            h = 2 * j + e
            chunks = new_chunks(kb_ref, vtb_ref, j, h)
            if has_cache:
                cvj = value_block(cvt_j[e * HEAD_DIM:(e + 1) * HEAD_DIM], e)
                chunks.append((lambda a=ckj: a, lambda a=cvj: a, ckj.shape[0]))
            for mi in range(2):
                sel = (lane // B_HALF) == (2 * e + mi)
                add_map(("b", j, e, mi), jnp.where(sel, q, jnp.zeros_like(q)), chunks)

    done = {}

    def on_done(mp):
        kind, j, e, mi = mp.tag
        done[mp.tag] = _normalised_half(mp.acc, e)
        if kind == "a" and e == 1:
            o_t = jnp.concatenate([done[("a", j, 0, 0)], done[("a", j, 1, 0)]], axis=0)
            ao_ref[0, :, j * LANES:(j + 1) * LANES] = o_t.T.astype(BF16)
        if kind == "b" and e == 1 and mi == 1:
            halves = []
            for ee in range(2):
                o = done[("b", j, ee, 0)] - lam * done[("b", j, ee, 1)]
                ms = jnp.mean(o * o, axis=0, keepdims=True)
                halves.append(o * lax.rsqrt(ms + EPS))
            o_t = jnp.concatenate(halves, axis=0)
            bo_ref[0, :, j * LANES:(j + 1) * LANES] = (
                o_t.T * subln_ref[...] * (1.0 - lam_init)).astype(BF16)

    _run_score_maps(maps, on_done)


def _softmax_pv_rows(qm, k, v):
    s = lax.dot_general(qm, k, (((1,), (1,)), ((), ())), preferred_element_type=F32)
    m_part = s[:, 0:LANES]
    for t in range(1, s.shape[1] // LANES):
        m_part = jnp.maximum(m_part, s[:, t * LANES:(t + 1) * LANES])
    e = jnp.exp2(s - jnp.max(m_part, axis=-1, keepdims=True))
    l_part = e[:, 0:LANES]
    for t in range(1, s.shape[1] // LANES):
        l_part = l_part + e[:, t * LANES:(t + 1) * LANES]
    acc = jnp.dot(e.astype(BF16), v, preferred_element_type=F32)
    return acc, jnp.sum(l_part, axis=-1, keepdims=True)


def _attn_rows_kernel(qa_ref, k3_ref, v3_ref, qb_ref, kb_ref, vb_ref, lamv_ref, subln_ref,
                      ao_ref, bo_ref, *, lam_init):
    left = _left_mask()
    lane = lax.broadcasted_iota(jnp.int32, (1, LANES), 1)
    for j in range(A_WIDTH // LANES):
        sl = slice(j * LANES, (j + 1) * LANES)
        q = qa_ref[0, :, sl]
        halves = []
        for e in range(2):
            qm = jnp.where(left if e == 0 else jnp.logical_not(left), q, jnp.zeros_like(q))
            acc, l = _softmax_pv_rows(qm, k3_ref[0, :, sl], v3_ref[0, :, sl])
            halves.append(acc * (1.0 / l))
        ao_ref[0, :, sl] = jnp.where(left, halves[0], halves[1]).astype(BF16)

    lv = lamv_ref[...]
    lam = (jnp.exp(jnp.sum(lv[0:1] * lv[1:2], axis=-1, keepdims=True))
           - jnp.exp(jnp.sum(lv[2:3] * lv[3:4], axis=-1, keepdims=True)) + lam_init)
    for j in range(B_WIDTH // LANES):
        sl = slice(j * LANES, (j + 1) * LANES)
        q = qb_ref[0, :, sl]
        halves = []
        for e in range(2):
            maps = []
            for mi in range(2):
                sel = (lane // B_HALF) == (2 * e + mi)
                acc, l = _softmax_pv_rows(jnp.where(sel, q, jnp.zeros_like(q)), kb_ref[0, :, sl], vb_ref[0, :, sl])
                maps.append(acc * (1.0 / l))
            halves.append(maps[0] - lam * maps[1])
        o = jnp.where(left, halves[0], halves[1])
        bo_ref[0, :, sl] = (_pair_rmsnorm(o, subln_ref[...]) * (1.0 - lam_init)).astype(BF16)


def _attn_rows(qa, k3, v3, qb, kb, vb, lamv, subln, *, lam_init):
    nb, L, _ = qa.shape
    spec = lambda w: pl.BlockSpec((1, L, w), lambda b: (b, 0, 0))
    return pl.pallas_call(
        functools.partial(_attn_rows_kernel, lam_init=lam_init),
        out_shape=[jax.ShapeDtypeStruct((nb, L, A_WIDTH), BF16),
                   jax.ShapeDtypeStruct((nb, L, B_WIDTH), BF16)],
        grid=(nb,),
        in_specs=[spec(A_WIDTH), spec(A_WIDTH), spec(A_WIDTH), spec(B_WIDTH), spec(B_WIDTH), spec(B_WIDTH),
                  pl.BlockSpec((4, B_HALF), lambda b: (0, 0)),
                  pl.BlockSpec((1, LANES), lambda b: (0, 0))],
        out_specs=[spec(A_WIDTH), spec(B_WIDTH)],
        compiler_params=_cparams(1),
        name="attn",
    )(qa, k3, v3, qb, kb, vb, lamv, subln)


def _attn(qa, k3, vta, qb, kb, vtb, lamv, subln, caches, layer, *, lam_init):
    nb, L, _ = qa.shape
    tq = min(ATTN_TQ, L)
    has_cache = caches is not None
    q_spec = lambda w: pl.BlockSpec((1, tq, w), lambda b, i: (b, i, 0))
    kv_spec = lambda w: pl.BlockSpec((1, L, w), lambda b, i: (b, 0, 0))
    vt_spec = lambda n: pl.BlockSpec((1, n, LANES, L), lambda b, i: (b, 0, 0, 0))
    in_specs = [q_spec(A_WIDTH), kv_spec(A_WIDTH), vt_spec(A_Q_HEADS),
                q_spec(B_WIDTH), kv_spec(B_WIDTH), vt_spec(B_HEADS),
                pl.BlockSpec((4, B_HALF), lambda b, i: (0, 0)),
                pl.BlockSpec((1, LANES), lambda b, i: (0, 0))]
    args = [qa, k3, vta, qb, kb, vtb, lamv, subln]
    lk = L
    if has_cache:
        past = caches[0].shape[2]
        lk += past
        for c in caches:
            in_specs.append(pl.BlockSpec((1, 1, past, c.shape[-1]), lambda b, i: (b, layer, 0, 0)))
            args.append(c)
    return pl.pallas_call(
        functools.partial(_attn_kernel, has_cache=has_cache, lam_init=lam_init, seq_len=L),
        out_shape=[jax.ShapeDtypeStruct((nb, L, A_WIDTH), BF16),
                   jax.ShapeDtypeStruct((nb, L, B_WIDTH), BF16)],
        grid=(nb, L // tq),
        in_specs=in_specs,
        out_specs=[q_spec(A_WIDTH), q_spec(B_WIDTH)],
        scratch_shapes=[pltpu.VMEM((2, lk, tq), F32)],
        compiler_params=_cparams(2),
        name="attn_cache" if has_cache else "attn",
    )(*args)


def _prefix_sums(tri, a):
    hi = a.astype(BF16)
    lo = (a - hi.astype(F32)).astype(BF16)
    return (jnp.dot(tri, hi, preferred_element_type=F32)
            + jnp.dot(tri, lo, preferred_element_type=F32))


def _pair_cols(x, c0):
    return jnp.where(_left_mask(), x[:, c0:c0 + 1], x[:, c0 + 1:c0 + 2])


def _ssd_kernel(*refs, has_init, emit_state, seq_len):
    xbc_ref, z_ref, dt_ref, cw_ref, cb_ref, dtb_ref, alog_ref, dexp_ref, nw_ref = refs[:9]
    pos = 9
    if has_init:
        init_ref = refs[pos]
        pos += 1
    co_ref = refs[pos]
    pos += 1
    if emit_state:
        st_ref = refs[pos]
        pos += 1
    xc_scr, y_scr, st_scr = refs[pos:pos + 3]

    L = seq_len
    rb = min(ROW_BLOCK, L)
    q = SSD_CHUNK
    nc = L // q
    left = _left_mask()

    w0 = cw_ref[0:1, :]
    w1 = cw_ref[1:2, :]
    w2 = cw_ref[2:3, :]
    cb = CONV_ROWS
    half = cb // 2
    hrow = lax.broadcasted_iota(jnp.int32, (half, 1), 0)
    for s in range(C_CONV_CH // LANES):
        ls = slice(s * LANES, (s + 1) * LANES)
        w0s, w1s, w2s, bs = w0[:, ls], w1[:, ls], w2[:, ls], cb_ref[:, ls]
        for r0 in range(0, L, cb):
            even = xbc_ref[0, s, pl.ds(r0, half, stride=2), :]
            odd = xbc_ref[0, s, pl.ds(r0 + 1, half, stride=2), :]
            if r0 > 0:
                odd_before = xbc_ref[0, s, pl.ds(r0 - 1, half, stride=2), :]
            else:
                odd_before = jnp.where(hrow == 0, 0.0, pltpu.roll(odd, 1, 0))
            if r0 + cb < L:
                even_after = xbc_ref[0, s, pl.ds(r0 + 2, half, stride=2), :]
            else:
                even_after = jnp.where(hrow == half - 1, 0.0, pltpu.roll(even, half - 1, 0))
            y_even = odd_before * w0s + even * w1s + odd * w2s + bs
            y_odd = even * w0s + odd * w1s + even_after * w2s + bs
            xc_scr[s, pl.ds(r0, half, stride=2), :] = y_even * _sigmoid(y_even)
            xc_scr[s, pl.ds(r0 + 1, half, stride=2), :] = y_odd * _sigmoid(y_odd)
    for j in range(C_INNER // LANES):
        for r0 in range(0, L, rb):
            y_scr[r0:r0 + rb, j * LANES:(j + 1) * LANES] = (
                xc_scr[j, r0:r0 + rb, :] * dexp_ref[:, j * LANES:(j + 1) * LANES])

    for d in range(2):
        if has_init:
            st_scr[d] = init_ref[0, 0, d].reshape(C_INNER, C_STATE).T
        else:
            st_scr[d] = jnp.zeros((C_STATE, C_INNER), F32)

    a_neg = -jnp.exp(alog_ref[...])
    ti = lax.broadcasted_iota(jnp.int32, (q, q), 0)
    si = lax.broadcasted_iota(jnp.int32, (q, q), 1)
    tri_incl = (si <= ti).astype(BF16)
    src_lane = lax.broadcasted_iota(jnp.int32, (LANES, C_INNER), 0)
    dst_head = lax.broadcasted_iota(jnp.int32, (LANES, C_INNER), 1) // C_HEADDIM
    spread = [(src_lane == d * C_HEADS + dst_head).astype(BF16) for d in range(2)]

    def process(d, r0):
        rows = pl.ds(r0, q)
        lo = d * C_HEADS
        causal = (si <= ti) if d == 0 else (si >= ti)
        n_x = C_INNER // LANES
        bm = [xc_scr[n_x + g, rows, :] for g in range(C_GROUPS)]
        cm = [xc_scr[n_x + C_GROUPS + g, rows, :] for g in range(C_GROUPS)]
        bm_b = [b.astype(BF16) for b in bm]
        cm_b = [c.astype(BF16) for c in cm]
        bmt_b = [b.T.astype(BF16) for b in bm]
        gmat = [lax.dot_general(cm_b[g], bm_b[g], (((1,), (1,)), ((), ())), preferred_element_type=F32)
                for g in range(C_GROUPS)]

        dtr = dt_ref[0, rows, :] + dtb_ref[...]
        dt = jnp.maximum(dtr, 0.0) + jnp.log1p(jnp.exp(-jnp.abs(dtr)))
        a = dt * a_neg
        cs = _prefix_sums(tri_incl, a)
        tot = cs[q - 1:q, :]
        ev = cs if d == 0 else (tot - cs + a)
        e_in = jnp.exp(ev)
        e_out = jnp.exp(tot - ev)
        dec = jnp.exp(tot)
        ev_t = ev.T
        per_head = jnp.concatenate([dt, e_in, e_out], axis=0).astype(BF16)
        wide = jnp.dot(per_head, spread[d], preferred_element_type=F32)
        dt_w, e_in_w, e_out_w = wide[0:q], wide[q:2 * q], wide[2 * q:3 * q]

        st = st_scr.at[d]
        for j in range(C_INNER // LANES):
            sl = slice(j * LANES, (j + 1) * LANES)
            h0 = 2 * j
            g0, g1 = h0 // (C_HEADS // C_GROUPS), (h0 + 1) // (C_HEADS // C_GROUPS)
            xdt = xc_scr[j, rows, :] * dt_w[:, sl]
            xdt_b = xdt.astype(BF16)
            st_blk = st[:, sl]
            st_b = st_blk.astype(BF16)
            yd = []
            for e in range(2):
                hh = lo + h0 + e
                g = g0 if e == 0 else g1
                diff = ev[:, hh:hh + 1] - ev_t[hh:hh + 1, :]
                lmat = jnp.exp(jnp.where(causal, diff, NEG_BIG))
                yd.append(jnp.dot((gmat[g] * lmat).astype(BF16), xdt_b, preferred_element_type=F32))
            yo0 = jnp.dot(cm_b[g0], st_b, preferred_element_type=F32)
            yo1 = yo0 if g1 == g0 else jnp.dot(cm_b[g1], st_b, preferred_element_type=F32)
            y_blk = (jnp.where(left, yd[0], yd[1])
                     + jnp.where(left, yo0, yo1) * e_in_w[:, sl])
            y_scr[rows, sl] = y_scr[rows, sl] + y_blk
            xw = (xdt * e_out_w[:, sl]).astype(BF16)
            up0 = jnp.dot(bmt_b[g0], xw, preferred_element_type=F32)
            up1 = up0 if g1 == g0 else jnp.dot(bmt_b[g1], xw, preferred_element_type=F32)
            st[:, sl] = st_blk * _pair_cols(dec, lo + h0) + jnp.where(left, up0, up1)

    if nc <= 2:
        for c in range(nc):
            process(0, c * q)
            process(1, (nc - 1 - c) * q)
    else:
        def body(c, carry):
            process(0, pl.multiple_of(c * q, q))
            process(1, pl.multiple_of((nc - 1 - c) * q, q))
            return carry
        lax.fori_loop(0, nc, body, 0)

    if emit_state:
        for d in range(2):
            st_ref[0, d] = st_scr[d].T.reshape(C_HEADS, C_HEADDIM, C_STATE)

    for r0 in range(0, L, rb):
        zz = z_ref[0, r0:r0 + rb, :]
        yg = y_scr[r0:r0 + rb, :] * (zz * _sigmoid(zz))
        ms = jnp.mean(yg * yg, axis=-1, keepdims=True)
        co_ref[0, r0:r0 + rb, :] = (yg * lax.rsqrt(ms + EPS) * nw_ref[...]).astype(BF16)


def _ssd(xbc, z, dt, conv_w, conv_b, dtb, alog, dexp, norm_w, state, layer, *, emit_state):
    nb, n_slab, L, _ = xbc.shape
    has_init = state is not None
    seq_spec = lambda w: pl.BlockSpec((1, L, w), lambda b: (b, 0, 0))
    par_spec = lambda r, w: pl.BlockSpec((r, w), lambda b: (0, 0))
    in_specs = [pl.BlockSpec((1, n_slab, L, LANES), lambda b: (b, 0, 0, 0)), seq_spec(C_INNER), seq_spec(LANES),
                par_spec(3, C_CONV_CH), par_spec(1, C_CONV_CH), par_spec(1, LANES), par_spec(1, LANES),
                par_spec(1, C_INNER), par_spec(1, C_INNER)]
    args = [xbc, z, dt, conv_w, conv_b, dtb, alog, dexp, norm_w]
    if has_init:
        in_specs.append(pl.BlockSpec((1, 1, 2, C_HEADS, C_HEADDIM, C_STATE),
                                     lambda b: (b, layer, 0, 0, 0, 0)))
        args.append(state)
    out_shape = [jax.ShapeDtypeStruct((nb, L, C_INNER), BF16)]
    out_specs = [seq_spec(C_INNER)]
    if emit_state:
        out_shape.append(jax.ShapeDtypeStruct((nb, 2, C_HEADS, C_HEADDIM, C_STATE), F32))
        out_specs.append(pl.BlockSpec((1, 2, C_HEADS, C_HEADDIM, C_STATE), lambda b: (b, 0, 0, 0, 0)))
    return pl.pallas_call(
        functools.partial(_ssd_kernel, has_init=has_init, emit_state=emit_state, seq_len=L),
        out_shape=out_shape,
        grid=(nb,),
        in_specs=in_specs,
        out_specs=out_specs,
        scratch_shapes=[pltpu.VMEM((n_slab, L, LANES), F32), pltpu.VMEM((L, C_INNER), F32),
                        pltpu.VMEM((2, C_STATE, C_INNER), F32)],
        compiler_params=_cparams(1),
        name="ssd_init" if has_init else "ssd",
    )(*args)


def _outp_kernel(a_ref, b_ref, c_ref, x_ref, mod_ref, w_ref, x1_o, h2_o):
    mix = jnp.concatenate([a_ref[0], b_ref[0], c_ref[0]], axis=-1)
    o = jnp.dot(mix, w_ref[...], preferred_element_type=F32)
    gate1 = mod_ref[0, 2:3, :]
    shift2 = mod_ref[0, 3:4, :]
    scale2 = mod_ref[0, 4:5, :]
    x1 = x_ref[0] + gate1 * o
    ms = jnp.mean(x1 * x1, axis=-1, keepdims=True)
    x1_o[0] = x1
    h2_o[0] = ((x1 * lax.rsqrt(ms + EPS)) * (1.0 + scale2) + shift2).astype(BF16)


def _outp(a_out, b_out, c_out, x, mod, w_out_b, layer, *, per_batch_mod):
    nb, L, d = x.shape
    tm = min(PROJ_TM, L)
    mod_map = (lambda b, i: (b, 0, 0)) if per_batch_mod else (lambda b, i: (0, 0, 0))
    row_spec = lambda w: pl.BlockSpec((1, tm, w), lambda b, i: (b, i, 0))
    return pl.pallas_call(
        _outp_kernel,
        out_shape=[jax.ShapeDtypeStruct((nb, L, d), F32), jax.ShapeDtypeStruct((nb, L, d), BF16)],
        grid=(nb, L // tm),
        in_specs=[row_spec(A_WIDTH), row_spec(B_WIDTH), row_spec(C_INNER), row_spec(d),
                  pl.BlockSpec((1, 6, d), mod_map),
                  pl.BlockSpec((None, d, d), lambda b, i: (layer, 0, 0))],
        out_specs=[row_spec(d), row_spec(d)],
        compiler_params=_cparams(2),
        name="outp",
    )(a_out, b_out, c_out, x, mod, w_out_b)


HALO = 16


def _ffn_kernel(h_ref, hp_ref, hn_ref, x1_ref, mod_ref, wu_ref, cw_ref, cb_ref, wd_ref, fw_ref,
                o_ref, hext_scr, u_scr, acc_scr, *, seq_len, final_norm):
    tm = h_ref.shape[1]
    i = pl.program_id(1)
    n_chunks = wu_ref.shape[0]
    rb = min(FFN_RB, tm)

    has_prev = ((i * tm) & (seq_len - 1)) != 0
    has_next = ((i * tm + tm) & (seq_len - 1)) != 0
    hext_scr[0:HALO, :] = jnp.where(has_prev, hp_ref[0], jnp.zeros_like(hp_ref[0]))
    hext_scr[HALO:HALO + tm, :] = h_ref[0]
    hext_scr[HALO + tm:HALO + tm + HALO, :] = jnp.where(has_next, hn_ref[0], jnp.zeros_like(hn_ref[0]))
    acc_scr[...] = jnp.zeros_like(acc_scr)
    row8 = lax.broadcasted_iota(jnp.int32, (8, 1), 0)

    n_uslab = 2 * FFN_FC // LANES
    n_oslab = acc_scr.shape[0]
    half = rb // 2

    def up(c, slot):
        u = jnp.dot(hext_scr[...], wu_ref[c], preferred_element_type=F32)
        for s in range(n_uslab):
            u_scr[slot, s] = u[:, s * LANES:(s + 1) * LANES]

    def conv_down(c, slot):
        cw = cw_ref[c]
        bias = cb_ref[c]
        for r0 in range(0, tm, rb):
            base = HALO + r0
            ys = []
            for s in range(n_uslab):
                ls = slice(s * LANES, (s + 1) * LANES)
                even = u_scr[slot, s, pl.ds(base, half, stride=2), :]
                odd = u_scr[slot, s, pl.ds(base + 1, half, stride=2), :]
                odd_before = u_scr[slot, s, pl.ds(base - 1, half, stride=2), :]
                even_after = u_scr[slot, s, pl.ds(base + 2, half, stride=2), :]
                if r0 > 0 and r0 % seq_len == 0:
                    odd_before = jnp.concatenate(
                        [jnp.where(row8 == 0, 0.0, odd_before[0:8]), odd_before[8:]], axis=0)
                if r0 + rb < tm and (r0 + rb) % seq_len == 0:
                    even_after = jnp.concatenate(
                        [even_after[:half - 8], jnp.where(row8 == 7, 0.0, even_after[half - 8:])], axis=0)
                w0, w1, w2, b = cw[0:1, ls], cw[1:2, ls], cw[2:3, ls], bias[:, ls]
                y_even = odd_before * w0 + even * w1 + odd * w2 + b
                y_odd = even * w0 + odd * w1 + even_after * w2 + b
                ys.append(jnp.concatenate([y_even, y_odd], axis=0))
            n_g = n_uslab // 2
            act = jnp.concatenate([ys[s] * _sigmoid(ys[s]) * ys[n_g + s] for s in range(n_g)], axis=1)
            dn = jnp.dot(act.astype(BF16), wd_ref[c], preferred_element_type=F32)
            for s in range(n_oslab):
                ls = slice(s * LANES, (s + 1) * LANES)
                acc_scr[s, pl.ds(r0, half, stride=2), :] += dn[0:half, ls]
                acc_scr[s, pl.ds(r0 + 1, half, stride=2), :] += dn[half:rb, ls]

    up(0, 0)

    def pair(k, carry):
        c = 2 * k
        up(c + 1, 1)
        conv_down(c, 0)
        up(c + 2, 0)
        conv_down(c + 1, 1)
        return carry

    lax.fori_loop(0, (n_chunks - 1) // 2, pair, 0)
    if n_chunks % 2 == 1:
        conv_down(n_chunks - 1, 0)
    else:
        up(n_chunks - 1, 1)
        conv_down(n_chunks - 2, 0)
        conv_down(n_chunks - 1, 1)

    gate2 = mod_ref[0, 5:6, :]
    ffn_out = jnp.concatenate([acc_scr[s] for s in range(n_oslab)], axis=1)
    out = x1_ref[0] + gate2 * ffn_out
    if final_norm:
        ms = jnp.mean(out * out, axis=-1, keepdims=True)
        out = out * lax.rsqrt(ms + EPS) * fw_ref[...]
    o_ref[0] = out


def _ffn(h2, x1, mod, wu_c, cw_c, cb_c, wd_c, layer, final_w, *, per_batch_mod, seq_len, final_norm):
    nb, L, d = x1.shape
    tm = min(FFN_TM, L)
    n_tiles = L // tm
    rb = min(FFN_RB, tm)
    assert seq_len & (seq_len - 1) == 0 and L % seq_len == 0
    assert seq_len % tm == 0 or (tm % seq_len == 0 and seq_len % rb == 0)
    n_chunks = wu_c.shape[1]
    hb = tm // HALO
    n_hblk = L // HALO
    mod_map = (lambda b, i: (b, 0, 0)) if per_batch_mod else (lambda b, i: (0, 0, 0))
    row_spec = lambda w: pl.BlockSpec((1, tm, w), lambda b, i: (b, i, 0))
    const3 = lambda s: pl.BlockSpec(s, lambda b, i: (0, 0, 0))
    return pl.pallas_call(
        functools.partial(_ffn_kernel, seq_len=seq_len, final_norm=final_norm),
        out_shape=jax.ShapeDtypeStruct((nb, L, d), F32),
        grid=(nb, n_tiles),
        in_specs=[row_spec(d),
                  pl.BlockSpec((1, HALO, d), lambda b, i: (b, jnp.maximum(i * hb - 1, 0), 0)),
                  pl.BlockSpec((1, HALO, d), lambda b, i: (b, jnp.minimum((i + 1) * hb, n_hblk - 1), 0)),
                  row_spec(d),
                  pl.BlockSpec((1, 6, d), mod_map),
                  pl.BlockSpec((None, n_chunks, d, 2 * FFN_FC), lambda b, i: (layer, 0, 0, 0)),
                  const3((n_chunks, 3, 2 * FFN_FC)),
                  const3((n_chunks, 1, 2 * FFN_FC)),
                  pl.BlockSpec((None, n_chunks, FFN_FC, d), lambda b, i: (layer, 0, 0, 0)),
                  pl.BlockSpec((1, d), lambda b, i: (0, 0))],
        out_specs=row_spec(d),
        scratch_shapes=[pltpu.VMEM((tm + 2 * HALO, d), BF16),
                        pltpu.VMEM((2, 2 * FFN_FC // LANES, tm + 2 * HALO, LANES), F32),
                        pltpu.VMEM((d // LANES, tm, LANES), F32)],
        compiler_params=_cparams(2),
        name="ffn_final" if final_norm else "ffn",
    )(h2, h2, h2, x1, mod, wu_c, cw_c, cb_c, wd_c, final_w)


def _rope_tables(L, d):
    rows = L // GRID_W
    row = jnp.repeat(jnp.arange(rows), GRID_W).astype(F32)
    col = jnp.tile(jnp.arange(GRID_W), rows).astype(F32)
    quarter = d // 4
    inv = ROPE_THETA ** (-jnp.arange(quarter, dtype=F32) / quarter)
    ang_r = row[:, None] * inv[None, :]
    ang_c = col[:, None] * inv[None, :]
    ang = jnp.concatenate([ang_r, ang_r, ang_c, ang_c], axis=-1)
    cos, sin = jnp.cos(ang), jnp.sin(ang)
    even = ((jnp.arange(d) // quarter) % 2 == 0)[None, :]
    s_up = jnp.where(even, -sin, 0.0)
    s_dn = jnp.where(even, 0.0, sin)
    reps = LANES // d
    return tuple(jnp.tile(t, (1, reps)) for t in (cos, s_up, s_dn))


def _pad_lanes(v, width=LANES):
    v = v.reshape(1, -1).astype(F32)
    return jnp.pad(v, ((0, 0), (0, width - v.shape[1])))


def _layer_params(l, a_q_norm, a_k_norm, b_lambda, b_subln, ssm_conv_w, ssm_conv_b, ssm_A_log,
                  ssm_dt_bias, ssm_D, ssm_norm_w, ffn_conv_w, ffn_conv_b):
    n_chunks = D_FF // FFN_FC

    def gv_chunks(t):
        lead = t.shape[:-1]
        g = t[..., :D_FF].reshape(lead + (n_chunks, FFN_FC))
        v = t[..., D_FF:].reshape(lead + (n_chunks, FFN_FC))
        gv = jnp.concatenate([g, v], axis=-1)
        return jnp.moveaxis(gv, -2, 0)

    return dict(
        qn=jnp.tile(a_q_norm[l], 2).reshape(1, LANES),
        kn=jnp.tile(a_k_norm[l], 2).reshape(1, LANES),
        lamv=b_lambda[l],
        subln=jnp.tile(b_subln[l], 2).reshape(1, LANES),
        conv_w=ssm_conv_w[l],
        conv_b=ssm_conv_b[l].reshape(1, C_CONV_CH),
        dtb=_pad_lanes(ssm_dt_bias[l]),
        alog=_pad_lanes(ssm_A_log[l]),
        dexp=jnp.repeat(ssm_D[l], C_HEADDIM).reshape(1, C_INNER),
        norm_w=ssm_norm_w[l].reshape(1, C_INNER),
        cw=gv_chunks(ffn_conv_w[l]),
        cb=gv_chunks(ffn_conv_b[l].reshape(1, 2 * D_FF)),
    )


def _block(x, mod, p, wts, rope_tabs, caches, state, layer, *, seq_shape, per_batch_mod, is_ctx, lam_init,
           final_w, final_norm):
    n_seq, seq_len = seq_shape
    outs = _proj(x, mod, wts["w_in"], layer, p["qn"], p["kn"], rope_tabs,
                 per_batch_mod=per_batch_mod, kv_f32=is_ctx, seq_len=seq_len, values_t=caches is not None)
    seq = lambda t: t.reshape(n_seq, seq_len, t.shape[-1]) if t.ndim == 3 else t
    qa, k3, va, qb, kb, vb, xbc, z, dt = [seq(t) for t in outs[:9]]
    if caches is None:
        a_out, b_out = _attn_rows(qa, k3, va, qb, kb, vb, p["lamv"], p["subln"], lam_init=lam_init)
    else:
        a_out, b_out = _attn(qa, k3, va, qb, kb, vb, p["lamv"], p["subln"], caches, layer, lam_init=lam_init)
    ssd_out = _ssd(xbc, z, dt, p["conv_w"], p["conv_b"], p["dtb"], p["alog"], p["dexp"], p["norm_w"],
                   state, layer, emit_state=is_ctx)
    c_out = ssd_out[0]
    flat = lambda t: t.reshape(x.shape[0], x.shape[1], t.shape[-1])
    x1, h2 = _outp(flat(a_out), flat(b_out), flat(c_out), x, mod, wts["w_out"], layer,
                   per_batch_mod=per_batch_mod)
    x2 = _ffn(h2, x1, mod, wts["wu"], p["cw"], p["cb"], wts["wd"], layer, final_w,
              per_batch_mod=per_batch_mod, seq_len=seq_len, final_norm=final_norm)
    extras = None
    if is_ctx:
        extras = tuple(seq(t) for t in outs[9:13]) + (ssd_out[1],)
    return x2, extras


def kernel(x_prompt, x_sample, cache_a_k, cache_a_v, cache_b_k, cache_b_v, state_ssm, c, c_ctx, ada_w, ada_b, w_in, a_q_norm, a_k_norm, b_lambda, b_subln, ssm_conv_w, ssm_conv_b, ssm_A_log, ssm_dt_bias, ssm_D, ssm_norm_w, w_out, ffn_up, ffn_conv_w, ffn_conv_b, ffn_down, final_norm_w):
    depth = w_in.shape[0]
    nbp, lp, d = x_prompt.shape
    nbs, ls, _ = x_sample.shape
    past = cache_a_k.shape[2]

    mod_rows = 8
    assert 1 + nbs <= mod_rows
    cvecs = jnp.concatenate([c_ctx[None, :], c, jnp.zeros((mod_rows - 1 - nbs, d), F32)], axis=0)
    mod = _modulation(cvecs, ada_w, ada_b)

    rope_tabs = _rope_tables(ls, HEAD_DIM) + _rope_tables(ls, B_HALF)
    caches = (cache_a_k.reshape(nbs, depth, past, A_KV_HEADS * HEAD_DIM),
              cache_a_v.reshape(nbs, depth, past, A_KV_HEADS * HEAD_DIM),
              cache_b_k.reshape(nbs, depth, past, B_WIDTH),
              cache_b_v.reshape(nbs, depth, past, B_WIDTH))
    final_w = final_norm_w.reshape(1, d)
    n_chunks = D_FF // FFN_FC
    wts = dict(
        w_in=_pad_cast_bf16(w_in, PROJ_PAD),
        w_out=_cast_bf16(w_out),
        wu=_gv_cast_bf16(ffn_up),
        wd=_cast_bf16(ffn_down).reshape(depth, n_chunks, FFN_FC, d),
    )

    yp = x_prompt.reshape(1, nbp * lp, d)
    ys = x_sample
    ctx_extras = []
    for l in range(depth):
        lam_init = 0.8 - 0.6 * math.exp(-0.3 * l)
        p = _layer_params(l, a_q_norm, a_k_norm, b_lambda, b_subln, ssm_conv_w, ssm_conv_b,
                          ssm_A_log, ssm_dt_bias, ssm_D, ssm_norm_w, ffn_conv_w, ffn_conv_b)
        last = l == depth - 1
        mod_ctx = mod[l, 0].reshape(1, 6, d)
        mod_lat = mod[l, 1:1 + nbs].reshape(nbs, 6, d)
        yp, extras = _block(yp, mod_ctx, p, wts, None, None, None, l, seq_shape=(nbp, lp),
                            per_batch_mod=False, is_ctx=True, lam_init=lam_init,
                            final_w=final_w, final_norm=last)
        ctx_extras.append(extras)
        ys, _ = _block(ys, mod_lat, p, wts, rope_tabs, caches, state_ssm, l, seq_shape=(nbs, ls),
                       per_batch_mod=True, is_ctx=False, lam_init=lam_init,
                       final_w=final_w, final_norm=last)

    y_prompt = yp.reshape(nbp, lp, d)
    stack = lambda k: jnp.stack([e[k] for e in ctx_extras], axis=1)
    new_a_k = stack(0).reshape(nbp, depth, lp, A_KV_HEADS, HEAD_DIM)
    new_a_v = stack(1).reshape(nbp, depth, lp, A_KV_HEADS, HEAD_DIM)
    new_b_k = stack(2).reshape(nbp, depth, lp, B_HEADS, 2, B_HALF)
    new_b_v = stack(3).reshape(nbp, depth, lp, B_HEADS, 2 * B_HALF)
    new_ssm = stack(4)
    return (y_prompt, ys, new_a_k, new_a_v, new_b_k, new_b_v, new_ssm)
```

```python
import functools
import math

import jax
import jax.numpy as jnp
from jax import lax
from jax.experimental import pallas as pl
from jax.experimental.pallas import tpu as pltpu

F32 = jnp.float32
BF16 = jnp.bfloat16

LANES = 128
VMEM_LIMIT_BYTES = 56 * 1024 * 1024

D_MODEL = 1024
GRID_W = 64
SSD_CHUNK = 128
ROPE_THETA = 10000.0
EPS = 1e-6
HEAD_DIM = 64
A_Q_HEADS = 6
A_KV_HEADS = 2
A_GROUP = A_Q_HEADS // A_KV_HEADS
A_WIDTH = A_Q_HEADS * HEAD_DIM
B_HEADS = 4
B_HALF = 32
B_WIDTH = B_HEADS * 2 * B_HALF
C_HEADS = 6
C_HEADDIM = 64
C_INNER = C_HEADS * C_HEADDIM
C_GROUPS = 2
C_STATE = 128
C_CONV_CH = C_INNER + 2 * C_GROUPS * C_STATE
D_FF = 2816
PROJ_WIDTH = A_WIDTH + 4 * HEAD_DIM + 3 * B_WIDTH + 2 * C_INNER + 4 * C_STATE + 2 * C_HEADS
PROJ_PAD = ((PROJ_WIDTH + 2 * LANES - 1) // (2 * LANES)) * (2 * LANES)

OFF_AQ = 0
OFF_AK = OFF_AQ + A_WIDTH
OFF_AV = OFF_AK + A_KV_HEADS * HEAD_DIM
OFF_BQ = OFF_AV + A_KV_HEADS * HEAD_DIM
OFF_BK = OFF_BQ + B_WIDTH
OFF_BV = OFF_BK + B_WIDTH
OFF_CX = OFF_BV + B_WIDTH
OFF_CZ = OFF_CX + C_INNER
OFF_CB = OFF_CZ + C_INNER
OFF_CC = OFF_CB + C_GROUPS * C_STATE
OFF_DT = OFF_CC + C_GROUPS * C_STATE

MOD_BLOCK = 1024
PROJ_TM = 512
FFN_TM = 512
FFN_FC = 256
FFN_RB = 256
ATTN_TQ = 512
ATTN_KC = 512
ROW_BLOCK = 256
CONV_ROWS = 128
SSD_UNROLL = 4
NEG_BIG = -1e30
LOG2E = 1.4426950408889634


def _cparams(n_grid):
    return pltpu.CompilerParams(
        dimension_semantics=("parallel",) * n_grid,
        vmem_limit_bytes=VMEM_LIMIT_BYTES,
    )


def _sigmoid(x):
    return 1.0 / (1.0 + jnp.exp(-x))


def _left_mask():
    lane = lax.broadcasted_iota(jnp.int32, (1, LANES), 1)
    return lane < (LANES // 2)


def _pair_rmsnorm(y, w):
    left = _left_mask()
    sq = y * y
    s_l = jnp.sum(jnp.where(left, sq, 0.0), axis=-1, keepdims=True)
    s_r = jnp.sum(jnp.where(left, 0.0, sq), axis=-1, keepdims=True)
    ms = jnp.where(left, s_l, s_r) * (1.0 / HEAD_DIM)
    return y * lax.rsqrt(ms + EPS) * w


def _rope(y, cos, s_up, s_dn, quarter):
    return (y * cos
            + pltpu.roll(y, LANES - quarter, 1) * s_up
            + pltpu.roll(y, quarter, 1) * s_dn)


def _modulation_kernel(c_ref, w_ref, b_ref, o_ref):
    c = c_ref[...]
    s = c * _sigmoid(c)
    o_ref[0] = jnp.dot(s.astype(BF16), w_ref[0].astype(BF16),
                       preferred_element_type=F32) + b_ref[0]


def _modulation(cvecs, ada_w, ada_b):
    depth, d, width = ada_w.shape
    rows = cvecs.shape[0]
    return pl.pallas_call(
        _modulation_kernel,
        out_shape=jax.ShapeDtypeStruct((depth, rows, width), F32),
        grid=(depth, width // MOD_BLOCK),
        in_specs=[
            pl.BlockSpec((rows, d), lambda l, j: (0, 0)),
            pl.BlockSpec((1, d, MOD_BLOCK), lambda l, j: (l, 0, j)),
            pl.BlockSpec((1, 1, MOD_BLOCK), lambda l, j: (l, 0, j)),
        ],
        out_specs=pl.BlockSpec((1, rows, MOD_BLOCK), lambda l, j: (l, 0, j)),
        compiler_params=_cparams(2),
        name="modulation",
    )(cvecs, ada_w, ada_b.reshape(depth, 1, width))


CAST_ROWS = 256


def _cast_kernel(x_ref, o_ref):
    o_ref[...] = x_ref[...].astype(o_ref.dtype)


def _cast_bf16(x):
    depth, rows, cols = x.shape
    spec = pl.BlockSpec((1, CAST_ROWS, cols), lambda l, r: (l, r, 0))
    return pl.pallas_call(
        _cast_kernel, out_shape=jax.ShapeDtypeStruct(x.shape, BF16),
        grid=(depth, rows // CAST_ROWS), in_specs=[spec], out_specs=spec,
        compiler_params=_cparams(2), name="cast_bf16",
    )(x)


def _pad_cast_kernel(x_ref, o_ref, *, n_cols):
    tr = x_ref.shape[0]
    col = lax.broadcasted_iota(jnp.int32, (tr, 1), 0) + pl.program_id(0) * tr
    for l in range(x_ref.shape[1]):
        x = jnp.where(col < n_cols, x_ref[:, l, :], 0.0)
        o_ref[l] = x.T.astype(BF16)


def _pad_cast_bf16(x, width):
    depth, rows, cols = x.shape
    return pl.pallas_call(
        functools.partial(_pad_cast_kernel, n_cols=cols),
        out_shape=jax.ShapeDtypeStruct((depth, rows, width), BF16),
        grid=(width // LANES,),
        in_specs=[pl.BlockSpec((LANES, depth, rows), lambda i: (i, 0, 0))],
        out_specs=pl.BlockSpec((depth, rows, LANES), lambda i: (0, 0, i)),
        compiler_params=_cparams(1), name="pad_cast_bf16",
    )(jnp.transpose(x, (2, 0, 1)))


def _gv_cast_kernel(g_ref, v_ref, o_ref):
    o_ref[0, 0, :, 0:FFN_FC] = g_ref[0].astype(BF16)
    o_ref[0, 0, :, FFN_FC:2 * FFN_FC] = v_ref[0].astype(BF16)


def _gv_cast_bf16(up):
    depth, d, _ = up.shape
    n_chunks = D_FF // FFN_FC
    return pl.pallas_call(
        _gv_cast_kernel, out_shape=jax.ShapeDtypeStruct((depth, n_chunks, d, 2 * FFN_FC), BF16),
        grid=(depth, n_chunks),
        in_specs=[pl.BlockSpec((1, d, FFN_FC), lambda l, c: (l, 0, c)),
                  pl.BlockSpec((1, d, FFN_FC), lambda l, c: (l, 0, n_chunks + c))],
        out_specs=pl.BlockSpec((1, 1, d, 2 * FFN_FC), lambda l, c: (l, c, 0, 0)),
        compiler_params=_cparams(2), name="gv_cast_bf16",
    )(up, up)


def _proj_kernel(*refs, rope, kv_f32):
    x_ref, mod_ref, w_ref, qn_ref, kn_ref = refs[:5]
    pos = 5
    if rope:
        ca_ref, sau_ref, sad_ref, cb_ref, sbu_ref, sbd_ref = refs[pos:pos + 6]
        pos += 6
    (qa_o, k3_o, vta_o, qb_o, kb_o, vtb_o, xbc_o, z_o, dt_o) = refs[pos:pos + 9]
    pos += 9
    if kv_f32:
        ka32_o, va32_o, kb32_o, vb32_o = refs[pos:pos + 4]
        pos += 4
    p_scr = refs[pos]

    x = x_ref[0]
    ms = jnp.mean(x * x, axis=-1, keepdims=True)
    shift = mod_ref[0, 0:1, :]
    scale = mod_ref[0, 1:2, :]
    h = (x * lax.rsqrt(ms + EPS)) * (1.0 + scale) + shift
    p_scr[...] = jnp.dot(h.astype(BF16), w_ref[...], preferred_element_type=F32)

    left = _left_mask()

    def rope_a(y):
        if not rope:
            return y
        return _rope(y, ca_ref[...], sau_ref[...], sad_ref[...], HEAD_DIM // 4)

    def rope_b(y):
        if not rope:
            return y
        return _rope(y, cb_ref[...], sbu_ref[...], sbd_ref[...], B_HALF // 4)

    values_t = len(vta_o.shape) == 4
    if values_t:
        n_sub, sub_len = vta_o.shape[0], vta_o.shape[3]
        ones_half = jnp.ones((HEAD_DIM, sub_len), BF16)

    def store_vt(dst, h, vt_half, e):
        for s in range(n_sub):
            dst[s, h, e * HEAD_DIM:(e + 1) * HEAD_DIM, :] = vt_half[:, s * sub_len:(s + 1) * sub_len]
            dst[s, h, (1 - e) * HEAD_DIM:(2 - e) * HEAD_DIM, :] = ones_half

    a_scale = HEAD_DIM ** -0.5 * LOG2E
    for j in range(A_WIDTH // LANES):
        y = p_scr[:, OFF_AQ + j * LANES:OFF_AQ + (j + 1) * LANES]
        y = rope_a(_pair_rmsnorm(y, qn_ref[...]))
        qa_o[0, :, j * LANES:(j + 1) * LANES] = (y * a_scale).astype(BF16)

    k = rope_a(_pair_rmsnorm(p_scr[:, OFF_AK:OFF_AK + LANES], kn_ref[...]))
    v = p_scr[:, OFF_AV:OFF_AV + LANES]
    if kv_f32:
        ka32_o[0] = k
        va32_o[0] = v
    swapped = pltpu.roll(k, LANES // 2, 1)
    k3_o[0, :, 0:LANES] = jnp.where(left, k, swapped).astype(BF16)
    k3_o[0, :, LANES:2 * LANES] = k.astype(BF16)
    k3_o[0, :, 2 * LANES:3 * LANES] = jnp.where(left, swapped, k).astype(BF16)
    if values_t:
        vt = v.T.astype(BF16)
        for h in range(A_Q_HEADS):
            g = h // A_GROUP
            store_vt(vta_o, h, vt[g * HEAD_DIM:(g + 1) * HEAD_DIM], h % 2)
    else:
        v_sw = pltpu.roll(v, LANES // 2, 1)
        vta_o[0, :, 0:LANES] = jnp.where(left, v, v_sw).astype(BF16)
        vta_o[0, :, LANES:2 * LANES] = v.astype(BF16)
        vta_o[0, :, 2 * LANES:3 * LANES] = jnp.where(left, v_sw, v).astype(BF16)

    b_scale = B_HALF ** -0.5 * LOG2E
    for j in range(B_WIDTH // LANES):
        sl = slice(j * LANES, (j + 1) * LANES)
        q = rope_b(p_scr[:, OFF_BQ + j * LANES:OFF_BQ + (j + 1) * LANES])
        qb_o[0, :, sl] = (q * b_scale).astype(BF16)
        kb = rope_b(p_scr[:, OFF_BK + j * LANES:OFF_BK + (j + 1) * LANES])
        vb = p_scr[:, OFF_BV + j * LANES:OFF_BV + (j + 1) * LANES]
        kb_o[0, :, sl] = kb.astype(BF16)
        if values_t:
            vbt = vb.T.astype(BF16)
            for e in range(2):
                store_vt(vtb_o, 2 * j + e, vbt[e * HEAD_DIM:(e + 1) * HEAD_DIM], e)
        else:
            vtb_o[0, :, sl] = vb.astype(BF16)
        if kv_f32:
            kb32_o[0, :, sl] = kb
            vb32_o[0, :, sl] = vb

    xbc_sub, xbc_len = xbc_o.shape[0], xbc_o.shape[2]
    for s in range(C_CONV_CH // LANES):
        col = OFF_CX + s * LANES if s < C_INNER // LANES else OFF_CB + s * LANES - C_INNER
        for u in range(xbc_sub):
            xbc_o[u, s] = p_scr[u * xbc_len:(u + 1) * xbc_len, col:col + LANES]
    z_o[0] = p_scr[:, OFF_CZ:OFF_CZ + C_INNER]
    dt_o[0] = p_scr[:, OFF_DT:OFF_DT + LANES]


def _proj(x, mod, w_in_p, layer, qn, kn, rope_tabs, *, per_batch_mod, kv_f32, seq_len, values_t):
    nb, L, d = x.shape
    tm = min(PROJ_TM, L)
    rope = rope_tabs is not None
    mod_map = (lambda b, i: (b, 0, 0)) if per_batch_mod else (lambda b, i: (0, 0, 0))
    row_spec = lambda w: pl.BlockSpec((1, tm, w), lambda b, i: (b, i, 0))
    in_specs = [
        row_spec(d),
        pl.BlockSpec((1, 6, d), mod_map),
        pl.BlockSpec((None, d, PROJ_PAD), lambda b, i: (layer, 0, 0)),
        pl.BlockSpec((1, LANES), lambda b, i: (0, 0)),
        pl.BlockSpec((1, LANES), lambda b, i: (0, 0)),
    ]
    args = [x, mod, w_in_p, qn, kn]
    if rope:
        in_specs += [pl.BlockSpec((tm, LANES), lambda b, i: (i, 0))] * 6
        args += list(rope_tabs)
    widths = [(A_WIDTH, BF16), (A_WIDTH, BF16), (-A_Q_HEADS if values_t else A_WIDTH, BF16),
              (B_WIDTH, BF16), (B_WIDTH, BF16), (-B_HEADS if values_t else B_WIDTH, BF16),
              ("slabs", F32), (C_INNER, F32), (LANES, F32)]
    if kv_f32:
        widths += [(LANES, F32), (LANES, F32), (B_WIDTH, F32), (B_WIDTH, F32)]
    out_shape, out_specs = [], []
    n_seq = nb * L // seq_len
    in_seq = seq_len >= tm
    assert (L == seq_len) if in_seq else (nb == 1 and tm % seq_len == 0)
    n_slab = C_CONV_CH // LANES
    for w, dt in widths:
        if w == "slabs":
            out_shape.append(jax.ShapeDtypeStruct((n_seq, n_slab, seq_len, LANES), dt))
            out_specs.append(pl.BlockSpec((1, n_slab, tm, LANES), lambda b, i: (b, 0, i, 0)) if in_seq else
                             pl.BlockSpec((tm // seq_len, n_slab, seq_len, LANES), lambda b, i: (i, 0, 0, 0)))
        elif w > 0:
            out_shape.append(jax.ShapeDtypeStruct((nb, L, w), dt))
            out_specs.append(row_spec(w))
        else:
            out_shape.append(jax.ShapeDtypeStruct((n_seq, -w, LANES, seq_len), dt))
            out_specs.append(pl.BlockSpec((1, -w, LANES, tm), lambda b, i: (b, 0, 0, i)) if in_seq else
                             pl.BlockSpec((tm // seq_len, -w, LANES, seq_len), lambda b, i: (i, 0, 0, 0)))
    return pl.pallas_call(
        functools.partial(_proj_kernel, rope=rope, kv_f32=kv_f32),
        out_shape=out_shape,
        grid=(nb, L // tm),
        in_specs=in_specs,
        out_specs=out_specs,
        scratch_shapes=[pltpu.VMEM((tm, PROJ_PAD), F32)],
        compiler_params=_cparams(2),
        name="proj_rope" if rope else "proj",
    )(*args)


class _ScoreMap:
    def __init__(self, tag, qm, chunks, s_ref):
        self.tag, self.qm, self.chunks, self.s_ref = tag, qm, chunks, s_ref
        self.offsets = [sum(w for _, _, w in chunks[:c]) for c in range(len(chunks))]
        self.m_part = self.m = self.acc = None

    def pass1(self, c):
        k_fn, _, w = self.chunks[c]
        tq = self.qm.shape[0]
        s = lax.dot_general(k_fn(), self.qm, (((1,), (1,)), ((), ())), preferred_element_type=F32)
        self.s_ref[self.offsets[c]:self.offsets[c] + w, :] = s
        part = jnp.max(s.reshape(w // 8, 8, tq), axis=0)
        self.m_part = part if self.m_part is None else jnp.maximum(self.m_part, part)

    def finish_max(self):
        self.m = jnp.max(self.m_part, axis=0, keepdims=True)

    def pass2(self, c):
        _, vt_fn, w = self.chunks[c]
        e = jnp.exp2(self.s_ref[self.offsets[c]:self.offsets[c] + w, :] - self.m).astype(BF16)
        pv = jnp.dot(vt_fn(), e, preferred_element_type=F32)
        self.acc = pv if self.acc is None else self.acc + pv


def _run_score_maps(maps, on_done):
    first = maps[0]
    for c in range(len(first.chunks)):
        first.pass1(c)
    first.finish_max()
    for i, mp in enumerate(maps):
        nxt = maps[i + 1] if i + 1 < len(maps) else None
        n_next = len(nxt.chunks) if nxt is not None else 0
        for c in range(max(len(mp.chunks), n_next)):
            if c < len(mp.chunks):
                mp.pass2(c)
            if c < n_next:
                nxt.pass1(c)
        if nxt is not None:
            nxt.finish_max()
        on_done(mp)


def _normalised_half(acc, e):
    l = acc[(1 - e) * HEAD_DIM:(1 - e) * HEAD_DIM + 1, :]
    return acc[e * HEAD_DIM:(e + 1) * HEAD_DIM, :] * (1.0 / l)


def _attn_kernel(*refs, has_cache, lam_init, seq_len):
    qa_ref, k3_ref, vta_ref, qb_ref, kb_ref, vtb_ref, lamv_ref, subln_ref = refs[:8]
    pos = 8
    if has_cache:
        cka_ref, cva_ref, ckb_ref, cvb_ref = refs[pos:pos + 4]
        pos += 4
    ao_ref, bo_ref, s_scr = refs[pos:pos + 3]

    left = _left_mask()
    lane = lax.broadcasted_iota(jnp.int32, (1, LANES), 1)
    kc = min(ATTN_KC, seq_len)
    n_kc = seq_len // kc
    maps = []

    def add_map(tag, qm, chunks):
        maps.append(_ScoreMap(tag, qm, chunks, s_scr.at[len(maps) % 2]))

    def new_chunks(k_ref, vt_ref, j, h):
        sl = slice(j * LANES, (j + 1) * LANES)
        out = []
        for c in range(n_kc):
            rows = slice(c * kc, (c + 1) * kc)
            out.append((functools.partial(lambda r, s: k_ref[0, r, s], rows, sl),
                        functools.partial(lambda r, hh: vt_ref[0, hh, :, r], rows, h), kc))
        return out

    def value_block(vt_half, e):
        ones = jnp.ones_like(vt_half)
        return jnp.concatenate([vt_half, ones] if e == 0 else [ones, vt_half], axis=0).astype(BF16)

    if has_cache:
        ck = cka_ref[0, 0]
        ck_sw = pltpu.roll(ck, LANES // 2, 1)
        ck3 = [jnp.where(left, ck, ck_sw), ck, jnp.where(left, ck_sw, ck)]
        cvt = cva_ref[0, 0].T
        past = ck.shape[0]
    for j in range(A_WIDTH // LANES):
        q = qa_ref[0, :, j * LANES:(j + 1) * LANES]
        for e in range(2):
            h = 2 * j + e
            g = h // A_GROUP
            chunks = new_chunks(k3_ref, vta_ref, j, h)
            if has_cache:
                ckj = ck3[j].astype(BF16)
                cvj = value_block(cvt[g * HEAD_DIM:(g + 1) * HEAD_DIM], e)
                chunks.append((lambda a=ckj: a, lambda a=cvj: a, past))
            qm = jnp.where(left if e == 0 else jnp.logical_not(left), q, jnp.zeros_like(q))
            add_map(("a", j, e, 0), qm, chunks)

    lv = lamv_ref[...]
    lam = (jnp.exp(jnp.sum(lv[0:1] * lv[1:2], axis=-1, keepdims=True))
           - jnp.exp(jnp.sum(lv[2:3] * lv[3:4], axis=-1, keepdims=True)) + lam_init)
    for j in range(B_WIDTH // LANES):
        sl = slice(j * LANES, (j + 1) * LANES)
        q = qb_ref[0, :, sl]
        if has_cache:
            ckj = ckb_ref[0, 0, :, sl].astype(BF16)
            cvt_j = cvb_ref[0, 0, :, sl].T
        for e in range(2):
            h = 2 * j + e
            chunks = new_chunks(kb_ref, vtb_ref, j, h)
            if has_cache:
                cvj = value_block(cvt_j[e * HEAD_DIM:(e + 1) * HEAD_DIM], e)
                chunks.append((lambda a=ckj: a, lambda a=cvj: a, ckj.shape[0]))
            for mi in range(2):
                sel = (lane // B_HALF) == (2 * e + mi)
                add_map(("b", j, e, mi), jnp.where(sel, q, jnp.zeros_like(q)), chunks)

    done = {}

    def on_done(mp):
        kind, j, e, mi = mp.tag
        done[mp.tag] = _normalised_half(mp.acc, e)
        if kind == "a" and e == 1:
            o_t = jnp.concatenate([done[("a", j, 0, 0)], done[("a", j, 1, 0)]], axis=0)
            ao_ref[0, :, j * LANES:(j + 1) * LANES] = o_t.T.astype(BF16)
        if kind == "b" and e == 1 and mi == 1:
            halves = []
            for ee in range(2):
                o = done[("b", j, ee, 0)] - lam * done[("b", j, ee, 1)]
                ms = jnp.mean(o * o, axis=0, keepdims=True)
                halves.append(o * lax.rsqrt(ms + EPS))
            o_t = jnp.concatenate(halves, axis=0)
            bo_ref[0, :, j * LANES:(j + 1) * LANES] = (
                o_t.T * subln_ref[...] * (1.0 - lam_init)).astype(BF16)

    _run_score_maps(maps, on_done)


def _softmax_pv_rows(qm, k, v):
    s = lax.dot_general(qm, k, (((1,), (1,)), ((), ())), preferred_element_type=F32)
    m_part = s[:, 0:LANES]
    for t in range(1, s.shape[1] // LANES):
        m_part = jnp.maximum(m_part, s[:, t * LANES:(t + 1) * LANES])
    e = jnp.exp2(s - jnp.max(m_part, axis=-1, keepdims=True))
    l_part = e[:, 0:LANES]
    for t in range(1, s.shape[1] // LANES):
        l_part = l_part + e[:, t * LANES:(t + 1) * LANES]
    acc = jnp.dot(e.astype(BF16), v, preferred_element_type=F32)
    return acc, jnp.sum(l_part, axis=-1, keepdims=True)


def _attn_rows_kernel(qa_ref, k3_ref, v3_ref, qb_ref, kb_ref, vb_ref, lamv_ref, subln_ref,
                      ao_ref, bo_ref, *, lam_init):
    left = _left_mask()
    lane = lax.broadcasted_iota(jnp.int32, (1, LANES), 1)
    for j in range(A_WIDTH // LANES):
        sl = slice(j * LANES, (j + 1) * LANES)
        q = qa_ref[0, :, sl]
        halves = []
        for e in range(2):
            qm = jnp.where(left if e == 0 else jnp.logical_not(left), q, jnp.zeros_like(q))
            acc, l = _softmax_pv_rows(qm, k3_ref[0, :, sl], v3_ref[0, :, sl])
            halves.append(acc * (1.0 / l))
        ao_ref[0, :, sl] = jnp.where(left, halves[0], halves[1]).astype(BF16)

    lv = lamv_ref[...]
    lam = (jnp.exp(jnp.sum(lv[0:1] * lv[1:2], axis=-1, keepdims=True))
           - jnp.exp(jnp.sum(lv[2:3] * lv[3:4], axis=-1, keepdims=True)) + lam_init)
    for j in range(B_WIDTH // LANES):
        sl = slice(j * LANES, (j + 1) * LANES)
        q = qb_ref[0, :, sl]
        halves = []
        for e in range(2):
            maps = []
            for mi in range(2):
                sel = (lane // B_HALF) == (2 * e + mi)
                acc, l = _softmax_pv_rows(jnp.where(sel, q, jnp.zeros_like(q)), kb_ref[0, :, sl], vb_ref[0, :, sl])
                maps.append(acc * (1.0 / l))
            halves.append(maps[0] - lam * maps[1])
        o = jnp.where(left, halves[0], halves[1])
        bo_ref[0, :, sl] = (_pair_rmsnorm(o, subln_ref[...]) * (1.0 - lam_init)).astype(BF16)


def _attn_rows(qa, k3, v3, qb, kb, vb, lamv, subln, *, lam_init):
    nb, L, _ = qa.shape
    spec = lambda w: pl.BlockSpec((1, L, w), lambda b: (b, 0, 0))
    return pl.pallas_call(
        functools.partial(_attn_rows_kernel, lam_init=lam_init),
        out_shape=[jax.ShapeDtypeStruct((nb, L, A_WIDTH), BF16),
                   jax.ShapeDtypeStruct((nb, L, B_WIDTH), BF16)],
        grid=(nb,),
        in_specs=[spec(A_WIDTH), spec(A_WIDTH), spec(A_WIDTH), spec(B_WIDTH), spec(B_WIDTH), spec(B_WIDTH),
                  pl.BlockSpec((4, B_HALF), lambda b: (0, 0)),
                  pl.BlockSpec((1, LANES), lambda b: (0, 0))],
        out_specs=[spec(A_WIDTH), spec(B_WIDTH)],
        compiler_params=_cparams(1),
        name="attn",
    )(qa, k3, v3, qb, kb, vb, lamv, subln)


def _attn(qa, k3, vta, qb, kb, vtb, lamv, subln, caches, layer, *, lam_init):
    nb, L, _ = qa.shape
    tq = min(ATTN_TQ, L)
    has_cache = caches is not None
    q_spec = lambda w: pl.BlockSpec((1, tq, w), lambda b, i: (b, i, 0))
    kv_spec = lambda w: pl.BlockSpec((1, L, w), lambda b, i: (b, 0, 0))
    vt_spec = lambda n: pl.BlockSpec((1, n, LANES, L), lambda b, i: (b, 0, 0, 0))
    in_specs = [q_spec(A_WIDTH), kv_spec(A_WIDTH), vt_spec(A_Q_HEADS),
                q_spec(B_WIDTH), kv_spec(B_WIDTH), vt_spec(B_HEADS),
                pl.BlockSpec((4, B_HALF), lambda b, i: (0, 0)),
                pl.BlockSpec((1, LANES), lambda b, i: (0, 0))]
    args = [qa, k3, vta, qb, kb, vtb, lamv, subln]
    lk = L
    if has_cache:
        past = caches[0].shape[2]
        lk += past
        for c in caches:
            in_specs.append(pl.BlockSpec((1, 1, past, c.shape[-1]), lambda b, i: (b, layer, 0, 0)))
            args.append(c)
    return pl.pallas_call(
        functools.partial(_attn_kernel, has_cache=has_cache, lam_init=lam_init, seq_len=L),
        out_shape=[jax.ShapeDtypeStruct((nb, L, A_WIDTH), BF16),
                   jax.ShapeDtypeStruct((nb, L, B_WIDTH), BF16)],
        grid=(nb, L // tq),
        in_specs=in_specs,
        out_specs=[q_spec(A_WIDTH), q_spec(B_WIDTH)],
        scratch_shapes=[pltpu.VMEM((2, lk, tq), F32)],
        compiler_params=_cparams(2),
        name="attn_cache" if has_cache else "attn",
    )(*args)


def _prefix_sums(tri, a):
    hi = a.astype(BF16)
    lo = (a - hi.astype(F32)).astype(BF16)
    return (jnp.dot(tri, hi, preferred_element_type=F32)
            + jnp.dot(tri, lo, preferred_element_type=F32))


def _pair_cols(x, c0):
    return jnp.where(_left_mask(), x[:, c0:c0 + 1], x[:, c0 + 1:c0 + 2])


def _ssd_kernel(*refs, has_init, emit_state, seq_len):
    xbc_ref, z_ref, dt_ref, cw_ref, cb_ref, dtb_ref, alog_ref, dexp_ref, nw_ref = refs[:9]
    pos = 9
    if has_init:
        init_ref = refs[pos]
        pos += 1
    co_ref = refs[pos]
    pos += 1
    if emit_state:
        st_ref = refs[pos]
        pos += 1
    xc_scr, y_scr, st_scr = refs[pos:pos + 3]

    L = seq_len
    rb = min(ROW_BLOCK, L)
    q = SSD_CHUNK
    nc = L // q
    left = _left_mask()

    w0 = cw_ref[0:1, :]
    w1 = cw_ref[1:2, :]
    w2 = cw_ref[2:3, :]
    cb = CONV_ROWS
    half = cb // 2
    hrow = lax.broadcasted_iota(jnp.int32, (half, 1), 0)
    for s in range(C_CONV_CH // LANES):
        ls = slice(s * LANES, (s + 1) * LANES)
        w0s, w1s, w2s, bs = w0[:, ls], w1[:, ls], w2[:, ls], cb_ref[:, ls]
        for r0 in range(0, L, cb):
            even = xbc_ref[0, s, pl.ds(r0, half, stride=2), :]
            odd = xbc_ref[0, s, pl.ds(r0 + 1, half, stride=2), :]
            if r0 > 0:
                odd_before = xbc_ref[0, s, pl.ds(r0 - 1, half, stride=2), :]
            else:
                odd_before = jnp.where(hrow == 0, 0.0, pltpu.roll(odd, 1, 0))
            if r0 + cb < L:
                even_after = xbc_ref[0, s, pl.ds(r0 + 2, half, stride=2), :]
            else:
                even_after = jnp.where(hrow == half - 1, 0.0, pltpu.roll(even, half - 1, 0))
            y_even = odd_before * w0s + even * w1s + odd * w2s + bs
            y_odd = even * w0s + odd * w1s + even_after * w2s + bs
            xc_scr[s, pl.ds(r0, half, stride=2), :] = y_even * _sigmoid(y_even)
            xc_scr[s, pl.ds(r0 + 1, half, stride=2), :] = y_odd * _sigmoid(y_odd)
    for j in range(C_INNER // LANES):
        for r0 in range(0, L, rb):
            y_scr[r0:r0 + rb, j * LANES:(j + 1) * LANES] = (
                xc_scr[j, r0:r0 + rb, :] * dexp_ref[:, j * LANES:(j + 1) * LANES])

    for d in range(2):
        if has_init:
            st_scr[d] = init_ref[0, 0, d].reshape(C_INNER, C_STATE).T
        else:
            st_scr[d] = jnp.zeros((C_STATE, C_INNER), F32)

    a_neg = -jnp.exp(alog_ref[...])
    ti = lax.broadcasted_iota(jnp.int32, (q, q), 0)
    si = lax.broadcasted_iota(jnp.int32, (q, q), 1)
    tri_incl = (si <= ti).astype(BF16)
    src_lane = lax.broadcasted_iota(jnp.int32, (LANES, C_INNER), 0)
    dst_head = lax.broadcasted_iota(jnp.int32, (LANES, C_INNER), 1) // C_HEADDIM
    spread = [(src_lane == d * C_HEADS + dst_head).astype(BF16) for d in range(2)]

    def process(d, r0):
        rows = pl.ds(r0, q)
        lo = d * C_HEADS
        causal = (si <= ti) if d == 0 else (si >= ti)
        n_x = C_INNER // LANES
        bm = [xc_scr[n_x + g, rows, :] for g in range(C_GROUPS)]
        cm = [xc_scr[n_x + C_GROUPS + g, rows, :] for g in range(C_GROUPS)]
        bm_b = [b.astype(BF16) for b in bm]
        cm_b = [c.astype(BF16) for c in cm]
        bmt_b = [b.T.astype(BF16) for b in bm]
        gmat = [lax.dot_general(cm_b[g], bm_b[g], (((1,), (1,)), ((), ())), preferred_element_type=F32)
                for g in range(C_GROUPS)]

        dtr = dt_ref[0, rows, :] + dtb_ref[...]
        dt = jnp.maximum(dtr, 0.0) + jnp.log1p(jnp.exp(-jnp.abs(dtr)))
        a = dt * a_neg
        cs = _prefix_sums(tri_incl, a)
        tot = cs[q - 1:q, :]
        ev = cs if d == 0 else (tot - cs + a)
        e_in = jnp.exp(ev)
        e_out = jnp.exp(tot - ev)
        dec = jnp.exp(tot)
        ev_t = ev.T
        per_head = jnp.concatenate([dt, e_in, e_out], axis=0).astype(BF16)
        wide = jnp.dot(per_head, spread[d], preferred_element_type=F32)
        dt_w, e_in_w, e_out_w = wide[0:q], wide[q:2 * q], wide[2 * q:3 * q]

        st = st_scr.at[d]
        for j in range(C_INNER // LANES):
            sl = slice(j * LANES, (j + 1) * LANES)
            h0 = 2 * j
            g0, g1 = h0 // (C_HEADS // C_GROUPS), (h0 + 1) // (C_HEADS // C_GROUPS)
            xdt = xc_scr[j, rows, :] * dt_w[:, sl]
            xdt_b = xdt.astype(BF16)
            st_blk = st[:, sl]
            st_b = st_blk.astype(BF16)
            yd = []
            for e in range(2):
                hh = lo + h0 + e
                g = g0 if e == 0 else g1
                diff = ev[:, hh:hh + 1] - ev_t[hh:hh + 1, :]
                lmat = jnp.exp(jnp.where(causal, diff, NEG_BIG))
                yd.append(jnp.dot((gmat[g] * lmat).astype(BF16), xdt_b, preferred_element_type=F32))
            yo0 = jnp.dot(cm_b[g0], st_b, preferred_element_type=F32)
            yo1 = yo0 if g1 == g0 else jnp.dot(cm_b[g1], st_b, preferred_element_type=F32)
            y_blk = (jnp.where(left, yd[0], yd[1])
                     + jnp.where(left, yo0, yo1) * e_in_w[:, sl])
            y_scr[rows, sl] = y_scr[rows, sl] + y_blk
            xw = (xdt * e_out_w[:, sl]).astype(BF16)
            up0 = jnp.dot(bmt_b[g0], xw, preferred_element_type=F32)
            up1 = up0 if g1 == g0 else jnp.dot(bmt_b[g1], xw, preferred_element_type=F32)
            st[:, sl] = st_blk * _pair_cols(dec, lo + h0) + jnp.where(left, up0, up1)

    if nc <= 2:
        for c in range(nc):
            process(0, c * q)
            process(1, (nc - 1 - c) * q)
    else:
        def body(c, carry):
            process(0, pl.multiple_of(c * q, q))
            process(1, pl.multiple_of((nc - 1 - c) * q, q))
            return carry
        lax.fori_loop(0, nc, body, 0, unroll=SSD_UNROLL if nc % SSD_UNROLL == 0 else 1)

    if emit_state:
        for d in range(2):
            st_ref[0, d] = st_scr[d].T.reshape(C_HEADS, C_HEADDIM, C_STATE)

    for r0 in range(0, L, rb):
        zz = z_ref[0, r0:r0 + rb, :]
        yg = y_scr[r0:r0 + rb, :] * (zz * _sigmoid(zz))
        ms = jnp.mean(yg * yg, axis=-1, keepdims=True)
        co_ref[0, r0:r0 + rb, :] = (yg * lax.rsqrt(ms + EPS) * nw_ref[...]).astype(BF16)


def _ssd(xbc, z, dt, conv_w, conv_b, dtb, alog, dexp, norm_w, state, layer, *, emit_state):
    nb, n_slab, L, _ = xbc.shape
    has_init = state is not None
    seq_spec = lambda w: pl.BlockSpec((1, L, w), lambda b: (b, 0, 0))
    par_spec = lambda r, w: pl.BlockSpec((r, w), lambda b: (0, 0))
    in_specs = [pl.BlockSpec((1, n_slab, L, LANES), lambda b: (b, 0, 0, 0)), seq_spec(C_INNER), seq_spec(LANES),
                par_spec(3, C_CONV_CH), par_spec(1, C_CONV_CH), par_spec(1, LANES), par_spec(1, LANES),
                par_spec(1, C_INNER), par_spec(1, C_INNER)]
    args = [xbc, z, dt, conv_w, conv_b, dtb, alog, dexp, norm_w]
    if has_init:
        in_specs.append(pl.BlockSpec((1, 1, 2, C_HEADS, C_HEADDIM, C_STATE),
                                     lambda b: (b, layer, 0, 0, 0, 0)))
        args.append(state)
    out_shape = [jax.ShapeDtypeStruct((nb, L, C_INNER), BF16)]
    out_specs = [seq_spec(C_INNER)]
    if emit_state:
        out_shape.append(jax.ShapeDtypeStruct((nb, 2, C_HEADS, C_HEADDIM, C_STATE), F32))
        out_specs.append(pl.BlockSpec((1, 2, C_HEADS, C_HEADDIM, C_STATE), lambda b: (b, 0, 0, 0, 0)))
    return pl.pallas_call(
        functools.partial(_ssd_kernel, has_init=has_init, emit_state=emit_state, seq_len=L),
        out_shape=out_shape,
        grid=(nb,),
        in_specs=in_specs,
        out_specs=out_specs,
        scratch_shapes=[pltpu.VMEM((n_slab, L, LANES), F32), pltpu.VMEM((L, C_INNER), F32),
                        pltpu.VMEM((2, C_STATE, C_INNER), F32)],
        compiler_params=_cparams(1),
        name="ssd_init" if has_init else "ssd",
    )(*args)


def _outp_kernel(a_ref, b_ref, c_ref, x_ref, mod_ref, w_ref, x1_o, h2_o):
    mix = jnp.concatenate([a_ref[0], b_ref[0], c_ref[0]], axis=-1)
    o = jnp.dot(mix, w_ref[...], preferred_element_type=F32)
    gate1 = mod_ref[0, 2:3, :]
    shift2 = mod_ref[0, 3:4, :]
    scale2 = mod_ref[0, 4:5, :]
    x1 = x_ref[0] + gate1 * o
    ms = jnp.mean(x1 * x1, axis=-1, keepdims=True)
    x1_o[0] = x1
    h2_o[0] = ((x1 * lax.rsqrt(ms + EPS)) * (1.0 + scale2) + shift2).astype(BF16)


def _outp(a_out, b_out, c_out, x, mod, w_out_b, layer, *, per_batch_mod):
    nb, L, d = x.shape
    tm = min(PROJ_TM, L)
    mod_map = (lambda b, i: (b, 0, 0)) if per_batch_mod else (lambda b, i: (0, 0, 0))
    row_spec = lambda w: pl.BlockSpec((1, tm, w), lambda b, i: (b, i, 0))
    return pl.pallas_call(
        _outp_kernel,
        out_shape=[jax.ShapeDtypeStruct((nb, L, d), F32), jax.ShapeDtypeStruct((nb, L, d), BF16)],
        grid=(nb, L // tm),
        in_specs=[row_spec(A_WIDTH), row_spec(B_WIDTH), row_spec(C_INNER), row_spec(d),
                  pl.BlockSpec((1, 6, d), mod_map),
                  pl.BlockSpec((None, d, d), lambda b, i: (layer, 0, 0))],
        out_specs=[row_spec(d), row_spec(d)],
        compiler_params=_cparams(2),
        name="outp",
    )(a_out, b_out, c_out, x, mod, w_out_b)


HALO = 16


def _ffn_kernel(h_ref, hp_ref, hn_ref, x1_ref, mod_ref, wu_ref, cw_ref, cb_ref, wd_ref, fw_ref,
                o_ref, hext_scr, u_scr, acc_scr, *, seq_len, final_norm):
    tm = h_ref.shape[1]
    i = pl.program_id(1)
    n_chunks = wu_ref.shape[0]
    rb = min(FFN_RB, tm)

    has_prev = ((i * tm) & (seq_len - 1)) != 0
    has_next = ((i * tm + tm) & (seq_len - 1)) != 0
    hext_scr[0:HALO, :] = jnp.where(has_prev, hp_ref[0], jnp.zeros_like(hp_ref[0]))
    hext_scr[HALO:HALO + tm, :] = h_ref[0]
    hext_scr[HALO + tm:HALO + tm + HALO, :] = jnp.where(has_next, hn_ref[0], jnp.zeros_like(hn_ref[0]))
    acc_scr[...] = jnp.zeros_like(acc_scr)
    row8 = lax.broadcasted_iota(jnp.int32, (8, 1), 0)

    n_uslab = 2 * FFN_FC // LANES
    n_oslab = acc_scr.shape[0]
    half = rb // 2

    def up(c, slot):
        u = jnp.dot(hext_scr[...], wu_ref[c], preferred_element_type=F32)
        for s in range(n_uslab):
            u_scr[slot, s] = u[:, s * LANES:(s + 1) * LANES]

    def conv_down(c, slot):
        cw = cw_ref[c]
        bias = cb_ref[c]
        for r0 in range(0, tm, rb):
            base = HALO + r0
            ys = []
            for s in range(n_uslab):
                ls = slice(s * LANES, (s + 1) * LANES)
                even = u_scr[slot, s, pl.ds(base, half, stride=2), :]
                odd = u_scr[slot, s, pl.ds(base + 1, half, stride=2), :]
                odd_before = u_scr[slot, s, pl.ds(base - 1, half, stride=2), :]
                even_after = u_scr[slot, s, pl.ds(base + 2, half, stride=2), :]
                if r0 > 0 and r0 % seq_len == 0:
                    odd_before = jnp.concatenate(
                        [jnp.where(row8 == 0, 0.0, odd_before[0:8]), odd_before[8:]], axis=0)
                if r0 + rb < tm and (r0 + rb) % seq_len == 0:
                    even_after = jnp.concatenate(
                        [even_after[:half - 8], jnp.where(row8 == 7, 0.0, even_after[half - 8:])], axis=0)
                w0, w1, w2, b = cw[0:1, ls], cw[1:2, ls], cw[2:3, ls], bias[:, ls]
                y_even = odd_before * w0 + even * w1 + odd * w2 + b
                y_odd = even * w0 + odd * w1 + even_after * w2 + b
                ys.append(jnp.concatenate([y_even, y_odd], axis=0))
            n_g = n_uslab // 2
            act = jnp.concatenate([ys[s] * _sigmoid(ys[s]) * ys[n_g + s] for s in range(n_g)], axis=1)
            dn = jnp.dot(act.astype(BF16), wd_ref[c], preferred_element_type=F32)
            for s in range(n_oslab):
                ls = slice(s * LANES, (s + 1) * LANES)
                acc_scr[s, pl.ds(r0, half, stride=2), :] += dn[0:half, ls]
                acc_scr[s, pl.ds(r0 + 1, half, stride=2), :] += dn[half:rb, ls]

    up(0, 0)

    def pair(k, carry):
        c = 2 * k
        up(c + 1, 1)
        conv_down(c, 0)
        up(c + 2, 0)
        conv_down(c + 1, 1)
        return carry

    lax.fori_loop(0, (n_chunks - 1) // 2, pair, 0)
    if n_chunks % 2 == 1:
        conv_down(n_chunks - 1, 0)
    else:
        up(n_chunks - 1, 1)
        conv_down(n_chunks - 2, 0)
        conv_down(n_chunks - 1, 1)

    gate2 = mod_ref[0, 5:6, :]
    ffn_out = jnp.concatenate([acc_scr[s] for s in range(n_oslab)], axis=1)
    out = x1_ref[0] + gate2 * ffn_out
    if final_norm:
        ms = jnp.mean(out * out, axis=-1, keepdims=True)
        out = out * lax.rsqrt(ms + EPS) * fw_ref[...]
    o_ref[0] = out


def _ffn(h2, x1, mod, wu_c, cw_c, cb_c, wd_c, layer, final_w, *, per_batch_mod, seq_len, final_norm):
    nb, L, d = x1.shape
    tm = min(FFN_TM, L)
    n_tiles = L // tm
    rb = min(FFN_RB, tm)
    assert seq_len & (seq_len - 1) == 0 and L % seq_len == 0
    assert seq_len % tm == 0 or (tm % seq_len == 0 and seq_len % rb == 0)
    n_chunks = wu_c.shape[1]
    hb = tm // HALO
    n_hblk = L // HALO
    mod_map = (lambda b, i: (b, 0, 0)) if per_batch_mod else (lambda b, i: (0, 0, 0))
    row_spec = lambda w: pl.BlockSpec((1, tm, w), lambda b, i: (b, i, 0))
    const3 = lambda s: pl.BlockSpec(s, lambda b, i: (0, 0, 0))
    return pl.pallas_call(
        functools.partial(_ffn_kernel, seq_len=seq_len, final_norm=final_norm),
        out_shape=jax.ShapeDtypeStruct((nb, L, d), F32),
        grid=(nb, n_tiles),
        in_specs=[row_spec(d),
                  pl.BlockSpec((1, HALO, d), lambda b, i: (b, jnp.maximum(i * hb - 1, 0), 0)),
                  pl.BlockSpec((1, HALO, d), lambda b, i: (b, jnp.minimum((i + 1) * hb, n_hblk - 1), 0)),
                  row_spec(d),
                  pl.BlockSpec((1, 6, d), mod_map),
                  pl.BlockSpec((None, n_chunks, d, 2 * FFN_FC), lambda b, i: (layer, 0, 0, 0)),
                  const3((n_chunks, 3, 2 * FFN_FC)),
                  const3((n_chunks, 1, 2 * FFN_FC)),
                  pl.BlockSpec((None, n_chunks, FFN_FC, d), lambda b, i: (layer, 0, 0, 0)),
                  pl.BlockSpec((1, d), lambda b, i: (0, 0))],
        out_specs=row_spec(d),
        scratch_shapes=[pltpu.VMEM((tm + 2 * HALO, d), BF16),
                        pltpu.VMEM((2, 2 * FFN_FC // LANES, tm + 2 * HALO, LANES), F32),
                        pltpu.VMEM((d // LANES, tm, LANES), F32)],
        compiler_params=_cparams(2),
        name="ffn_final" if final_norm else "ffn",
    )(h2, h2, h2, x1, mod, wu_c, cw_c, cb_c, wd_c, final_w)


def _rope_tables(L, d):
    rows = L // GRID_W
    row = jnp.repeat(jnp.arange(rows), GRID_W).astype(F32)
    col = jnp.tile(jnp.arange(GRID_W), rows).astype(F32)
    quarter = d // 4
    inv = ROPE_THETA ** (-jnp.arange(quarter, dtype=F32) / quarter)
    ang_r = row[:, None] * inv[None, :]
    ang_c = col[:, None] * inv[None, :]
    ang = jnp.concatenate([ang_r, ang_r, ang_c, ang_c], axis=-1)
    cos, sin = jnp.cos(ang), jnp.sin(ang)
    even = ((jnp.arange(d) // quarter) % 2 == 0)[None, :]
    s_up = jnp.where(even, -sin, 0.0)
    s_dn = jnp.where(even, 0.0, sin)
    reps = LANES // d
    return tuple(jnp.tile(t, (1, reps)) for t in (cos, s_up, s_dn))


def _pad_lanes(v, width=LANES):
    v = v.reshape(1, -1).astype(F32)
    return jnp.pad(v, ((0, 0), (0, width - v.shape[1])))


def _layer_params(l, a_q_norm, a_k_norm, b_lambda, b_subln, ssm_conv_w, ssm_conv_b, ssm_A_log,
                  ssm_dt_bias, ssm_D, ssm_norm_w, ffn_conv_w, ffn_conv_b):
    n_chunks = D_FF // FFN_FC

    def gv_chunks(t):
        lead = t.shape[:-1]
        g = t[..., :D_FF].reshape(lead + (n_chunks, FFN_FC))
        v = t[..., D_FF:].reshape(lead + (n_chunks, FFN_FC))
        gv = jnp.concatenate([g, v], axis=-1)
        return jnp.moveaxis(gv, -2, 0)

    return dict(
        qn=jnp.tile(a_q_norm[l], 2).reshape(1, LANES),
        kn=jnp.tile(a_k_norm[l], 2).reshape(1, LANES),
        lamv=b_lambda[l],
        subln=jnp.tile(b_subln[l], 2).reshape(1, LANES),
        conv_w=ssm_conv_w[l],
        conv_b=ssm_conv_b[l].reshape(1, C_CONV_CH),
        dtb=_pad_lanes(ssm_dt_bias[l]),
        alog=_pad_lanes(ssm_A_log[l]),
        dexp=jnp.repeat(ssm_D[l], C_HEADDIM).reshape(1, C_INNER),
        norm_w=ssm_norm_w[l].reshape(1, C_INNER),
        cw=gv_chunks(ffn_conv_w[l]),
        cb=gv_chunks(ffn_conv_b[l].reshape(1, 2 * D_FF)),
    )


def _block(x, mod, p, wts, rope_tabs, caches, state, layer, *, seq_shape, per_batch_mod, is_ctx, lam_init,
           final_w, final_norm):
    n_seq, seq_len = seq_shape
    outs = _proj(x, mod, wts["w_in"], layer, p["qn"], p["kn"], rope_tabs,
                 per_batch_mod=per_batch_mod, kv_f32=is_ctx, seq_len=seq_len, values_t=caches is not None)
    seq = lambda t: t.reshape(n_seq, seq_len, t.shape[-1]) if t.ndim == 3 else t
    qa, k3, va, qb, kb, vb, xbc, z, dt = [seq(t) for t in outs[:9]]
    if caches is None:
        a_out, b_out = _attn_rows(qa, k3, va, qb, kb, vb, p["lamv"], p["subln"], lam_init=lam_init)
    else:
        a_out, b_out = _attn(qa, k3, va, qb, kb, vb, p["lamv"], p["subln"], caches, layer, lam_init=lam_init)
    ssd_out = _ssd(xbc, z, dt, p["conv_w"], p["conv_b"], p["dtb"], p["alog"], p["dexp"], p["norm_w"],
                   state, layer, emit_state=is_ctx)
    c_out = ssd_out[0]
    flat = lambda t: t.reshape(x.shape[0], x.shape[1], t.shape[-1])
    x1, h2 = _outp(flat(a_out), flat(b_out), flat(c_out), x, mod, wts["w_out"], layer,
                   per_batch_mod=per_batch_mod)
    x2 = _ffn(h2, x1, mod, wts["wu"], p["cw"], p["cb"], wts["wd"], layer, final_w,
              per_batch_mod=per_batch_mod, seq_len=seq_len, final_norm=final_norm)
    extras = None
    if is_ctx:
        extras = tuple(seq(t) for t in outs[9:13]) + (ssd_out[1],)
    return x2, extras


def kernel(x_prompt, x_sample, cache_a_k, cache_a_v, cache_b_k, cache_b_v, state_ssm, c, c_ctx, ada_w, ada_b, w_in, a_q_norm, a_k_norm, b_lambda, b_subln, ssm_conv_w, ssm_conv_b, ssm_A_log, ssm_dt_bias, ssm_D, ssm_norm_w, w_out, ffn_up, ffn_conv_w, ffn_conv_b, ffn_down, final_norm_w):
    depth = w_in.shape[0]
    nbp, lp, d = x_prompt.shape
    nbs, ls, _ = x_sample.shape
    past = cache_a_k.shape[2]

    mod_rows = 8
    assert 1 + nbs <= mod_rows
    cvecs = jnp.concatenate([c_ctx[None, :], c, jnp.zeros((mod_rows - 1 - nbs, d), F32)], axis=0)
    mod = _modulation(cvecs, ada_w, ada_b)

    rope_tabs = _rope_tables(ls, HEAD_DIM) + _rope_tables(ls, B_HALF)
    caches = (cache_a_k.reshape(nbs, depth, past, A_KV_HEADS * HEAD_DIM),
              cache_a_v.reshape(nbs, depth, past, A_KV_HEADS * HEAD_DIM),
              cache_b_k.reshape(nbs, depth, past, B_WIDTH),
              cache_b_v.reshape(nbs, depth, past, B_WIDTH))
    final_w = final_norm_w.reshape(1, d)
    n_chunks = D_FF // FFN_FC
    wts = dict(
        w_in=_pad_cast_bf16(w_in, PROJ_PAD),
        w_out=_cast_bf16(w_out),
        wu=_gv_cast_bf16(ffn_up),
        wd=_cast_bf16(ffn_down).reshape(depth, n_chunks, FFN_FC, d),
    )

    yp = x_prompt.reshape(1, nbp * lp, d)
    ys = x_sample
    ctx_extras = []
    for l in range(depth):
        lam_init = 0.8 - 0.6 * math.exp(-0.3 * l)
        p = _layer_params(l, a_q_norm, a_k_norm, b_lambda, b_subln, ssm_conv_w, ssm_conv_b,
                          ssm_A_log, ssm_dt_bias, ssm_D, ssm_norm_w, ffn_conv_w, ffn_conv_b)
        last = l == depth - 1
        mod_ctx = mod[l, 0].reshape(1, 6, d)
        mod_lat = mod[l, 1:1 + nbs].reshape(nbs, 6, d)
        yp, extras = _block(yp, mod_ctx, p, wts, None, None, None, l, seq_shape=(nbp, lp),
                            per_batch_mod=False, is_ctx=True, lam_init=lam_init,
                            final_w=final_w, final_norm=last)
        ctx_extras.append(extras)
        ys, _ = _block(ys, mod_lat, p, wts, rope_tabs, caches, state_ssm, l, seq_shape=(nbs, ls),
                       per_batch_mod=True, is_ctx=False, lam_init=lam_init,
                       final_w=final_w, final_norm=last)

    y_prompt = yp.reshape(nbp, lp, d)
    stack = lambda k: jnp.stack([e[k] for e in ctx_extras], axis=1)
    new_a_k = stack(0).reshape(nbp, depth, lp, A_KV_HEADS, HEAD_DIM)
    new_a_v = stack(1).reshape(nbp, depth, lp, A_KV_HEADS, HEAD_DIM)
    new_b_k = stack(2).reshape(nbp, depth, lp, B_HEADS, 2, B_HALF)
    new_b_v = stack(3).reshape(nbp, depth, lp, B_HEADS, 2 * B_HALF)
    new_ssm = stack(4)
    return (y_prompt, ys, new_a_k, new_a_v, new_b_k, new_b_v, new_ssm)
```

```python
import functools
import math

import jax
import jax.numpy as jnp
from jax import lax
from jax.experimental import pallas as pl
from jax.experimental.pallas import tpu as pltpu

F32 = jnp.float32
BF16 = jnp.bfloat16

LANES = 128
VMEM_LIMIT_BYTES = 56 * 1024 * 1024

D_MODEL = 1024
GRID_W = 64
SSD_CHUNK = 128
ROPE_THETA = 10000.0
EPS = 1e-6
HEAD_DIM = 64
A_Q_HEADS = 6
A_KV_HEADS = 2
A_GROUP = A_Q_HEADS // A_KV_HEADS
A_WIDTH = A_Q_HEADS * HEAD_DIM
B_HEADS = 4
B_HALF = 32
B_WIDTH = B_HEADS * 2 * B_HALF
C_HEADS = 6
C_HEADDIM = 64
C_INNER = C_HEADS * C_HEADDIM
C_GROUPS = 2
C_STATE = 128
C_CONV_CH = C_INNER + 2 * C_GROUPS * C_STATE
D_FF = 2816
PROJ_WIDTH = A_WIDTH + 4 * HEAD_DIM + 3 * B_WIDTH + 2 * C_INNER + 4 * C_STATE + 2 * C_HEADS
PROJ_PAD = ((PROJ_WIDTH + 2 * LANES - 1) // (2 * LANES)) * (2 * LANES)

OFF_AQ = 0
OFF_AK = OFF_AQ + A_WIDTH
OFF_AV = OFF_AK + A_KV_HEADS * HEAD_DIM
OFF_BQ = OFF_AV + A_KV_HEADS * HEAD_DIM
OFF_BK = OFF_BQ + B_WIDTH
OFF_BV = OFF_BK + B_WIDTH
OFF_CX = OFF_BV + B_WIDTH
OFF_CZ = OFF_CX + C_INNER
OFF_CB = OFF_CZ + C_INNER
OFF_CC = OFF_CB + C_GROUPS * C_STATE
OFF_DT = OFF_CC + C_GROUPS * C_STATE

MOD_BLOCK = 1024
PROJ_TM = 512
FFN_TM = 512
FFN_SUB = 512
FFN_FC = 256
FFN_RB = 256
ATTN_TQ = 512
ATTN_KC = 512
ATTN_AHEAD = 2
ROW_BLOCK = 256
CONV_ROWS = 128
SSD_UNROLL = 4
NEG_BIG = -1e30
LOG2E = 1.4426950408889634


def _cparams(n_grid):
    return pltpu.CompilerParams(
        dimension_semantics=("parallel",) * n_grid,
        vmem_limit_bytes=VMEM_LIMIT_BYTES,
    )


def _sigmoid(x):
    return 1.0 / (1.0 + jnp.exp(-x))


def _left_mask():
    lane = lax.broadcasted_iota(jnp.int32, (1, LANES), 1)
    return lane < (LANES // 2)


def _pair_rmsnorm(y, w):
    left = _left_mask()
    sq = y * y
    s_l = jnp.sum(jnp.where(left, sq, 0.0), axis=-1, keepdims=True)
    s_r = jnp.sum(jnp.where(left, 0.0, sq), axis=-1, keepdims=True)
    ms = jnp.where(left, s_l, s_r) * (1.0 / HEAD_DIM)
    return y * lax.rsqrt(ms + EPS) * w


def _rope(y, cos, s_up, s_dn, quarter):
    return (y * cos
            + pltpu.roll(y, LANES - quarter, 1) * s_up
            + pltpu.roll(y, quarter, 1) * s_dn)


def _modulation_kernel(c_ref, w_ref, b_ref, o_ref):
    c = c_ref[...]
    s = c * _sigmoid(c)
    o_ref[0] = jnp.dot(s.astype(BF16), w_ref[0].astype(BF16),
                       preferred_element_type=F32) + b_ref[0]


def _modulation(cvecs, ada_w, ada_b):
    depth, d, width = ada_w.shape
    rows = cvecs.shape[0]
    return pl.pallas_call(
        _modulation_kernel,
        out_shape=jax.ShapeDtypeStruct((depth, rows, width), F32),
        grid=(depth, width // MOD_BLOCK),
        in_specs=[
            pl.BlockSpec((rows, d), lambda l, j: (0, 0)),
            pl.BlockSpec((1, d, MOD_BLOCK), lambda l, j: (l, 0, j)),
            pl.BlockSpec((1, 1, MOD_BLOCK), lambda l, j: (l, 0, j)),
        ],
        out_specs=pl.BlockSpec((1, rows, MOD_BLOCK), lambda l, j: (l, 0, j)),
        compiler_params=_cparams(2),
        name="modulation",
    )(cvecs, ada_w, ada_b.reshape(depth, 1, width))


CAST_ROWS = 256


def _cast_kernel(x_ref, o_ref):
    o_ref[...] = x_ref[...].astype(o_ref.dtype)


def _cast_bf16(x):
    depth, rows, cols = x.shape
    spec = pl.BlockSpec((1, CAST_ROWS, cols), lambda l, r: (l, r, 0))
    return pl.pallas_call(
        _cast_kernel, out_shape=jax.ShapeDtypeStruct(x.shape, BF16),
        grid=(depth, rows // CAST_ROWS), in_specs=[spec], out_specs=spec,
        compiler_params=_cparams(2), name="cast_bf16",
    )(x)


def _pad_cast_kernel(x_ref, o_ref, *, n_cols):
    tr = x_ref.shape[0]
    col = lax.broadcasted_iota(jnp.int32, (tr, 1), 0) + pl.program_id(0) * tr
    for l in range(x_ref.shape[1]):
        x = jnp.where(col < n_cols, x_ref[:, l, :], 0.0)
        o_ref[l] = x.T.astype(BF16)


def _pad_cast_bf16(x, width):
    depth, rows, cols = x.shape
    return pl.pallas_call(
        functools.partial(_pad_cast_kernel, n_cols=cols),
        out_shape=jax.ShapeDtypeStruct((depth, rows, width), BF16),
        grid=(width // LANES,),
        in_specs=[pl.BlockSpec((LANES, depth, rows), lambda i: (i, 0, 0))],
        out_specs=pl.BlockSpec((depth, rows, LANES), lambda i: (0, 0, i)),
        compiler_params=_cparams(1), name="pad_cast_bf16",
    )(jnp.transpose(x, (2, 0, 1)))


def _gv_cast_kernel(g_ref, v_ref, o_ref):
    o_ref[0, 0, :, 0:FFN_FC] = g_ref[0].astype(BF16)
    o_ref[0, 0, :, FFN_FC:2 * FFN_FC] = v_ref[0].astype(BF16)


def _gv_cast_bf16(up):
    depth, d, _ = up.shape
    n_chunks = D_FF // FFN_FC
    return pl.pallas_call(
        _gv_cast_kernel, out_shape=jax.ShapeDtypeStruct((depth, n_chunks, d, 2 * FFN_FC), BF16),
        grid=(depth, n_chunks),
        in_specs=[pl.BlockSpec((1, d, FFN_FC), lambda l, c: (l, 0, c)),
                  pl.BlockSpec((1, d, FFN_FC), lambda l, c: (l, 0, n_chunks + c))],
        out_specs=pl.BlockSpec((1, 1, d, 2 * FFN_FC), lambda l, c: (l, c, 0, 0)),
        compiler_params=_cparams(2), name="gv_cast_bf16",
    )(up, up)


def _proj_kernel(*refs, rope, kv_f32):
    x_ref, mod_ref, w_ref, qn_ref, kn_ref = refs[:5]
    pos = 5
    if rope:
        ca_ref, sau_ref, sad_ref, cb_ref, sbu_ref, sbd_ref = refs[pos:pos + 6]
        pos += 6
    (qa_o, k3_o, vta_o, qb_o, kb_o, vtb_o, xbc_o, z_o, dt_o) = refs[pos:pos + 9]
    pos += 9
    if kv_f32:
        ka32_o, va32_o, kb32_o, vb32_o = refs[pos:pos + 4]
        pos += 4
    p_scr = refs[pos]

    x = x_ref[0]
    ms = jnp.mean(x * x, axis=-1, keepdims=True)
    shift = mod_ref[0, 0:1, :]
    scale = mod_ref[0, 1:2, :]
    h = (x * lax.rsqrt(ms + EPS)) * (1.0 + scale) + shift
    p_scr[...] = jnp.dot(h.astype(BF16), w_ref[...], preferred_element_type=F32)

    left = _left_mask()

    def rope_a(y):
        if not rope:
            return y
        return _rope(y, ca_ref[...], sau_ref[...], sad_ref[...], HEAD_DIM // 4)

    def rope_b(y):
        if not rope:
            return y
        return _rope(y, cb_ref[...], sbu_ref[...], sbd_ref[...], B_HALF // 4)

    values_t = len(vta_o.shape) == 4
    if values_t:
        n_sub, sub_len = vta_o.shape[0], vta_o.shape[3]
        ones_half = jnp.ones((HEAD_DIM, sub_len), BF16)

    def store_vt(dst, h, vt_half, e):
        for s in range(n_sub):
            dst[s, h, e * HEAD_DIM:(e + 1) * HEAD_DIM, :] = vt_half[:, s * sub_len:(s + 1) * sub_len]
            dst[s, h, (1 - e) * HEAD_DIM:(2 - e) * HEAD_DIM, :] = ones_half

    a_scale = HEAD_DIM ** -0.5 * LOG2E
    for j in range(A_WIDTH // LANES):
        y = p_scr[:, OFF_AQ + j * LANES:OFF_AQ + (j + 1) * LANES]
        y = rope_a(_pair_rmsnorm(y, qn_ref[...]))
        qa_o[0, :, j * LANES:(j + 1) * LANES] = (y * a_scale).astype(BF16)

    k = rope_a(_pair_rmsnorm(p_scr[:, OFF_AK:OFF_AK + LANES], kn_ref[...]))
    v = p_scr[:, OFF_AV:OFF_AV + LANES]
    if kv_f32:
        ka32_o[0] = k
        va32_o[0] = v
    swapped = pltpu.roll(k, LANES // 2, 1)
    k3_o[0, :, 0:LANES] = jnp.where(left, k, swapped).astype(BF16)
    k3_o[0, :, LANES:2 * LANES] = k.astype(BF16)
    k3_o[0, :, 2 * LANES:3 * LANES] = jnp.where(left, swapped, k).astype(BF16)
    if values_t:
        vt = v.T.astype(BF16)
        for h in range(A_Q_HEADS):
            g = h // A_GROUP
            store_vt(vta_o, h, vt[g * HEAD_DIM:(g + 1) * HEAD_DIM], h % 2)
    else:
        v_sw = pltpu.roll(v, LANES // 2, 1)
        vta_o[0, :, 0:LANES] = jnp.where(left, v, v_sw).astype(BF16)
        vta_o[0, :, LANES:2 * LANES] = v.astype(BF16)
        vta_o[0, :, 2 * LANES:3 * LANES] = jnp.where(left, v_sw, v).astype(BF16)

    b_scale = B_HALF ** -0.5 * LOG2E
    for j in range(B_WIDTH // LANES):
        sl = slice(j * LANES, (j + 1) * LANES)
        q = rope_b(p_scr[:, OFF_BQ + j * LANES:OFF_BQ + (j + 1) * LANES])
        qb_o[0, :, sl] = (q * b_scale).astype(BF16)
        kb = rope_b(p_scr[:, OFF_BK + j * LANES:OFF_BK + (j + 1) * LANES])
        vb = p_scr[:, OFF_BV + j * LANES:OFF_BV + (j + 1) * LANES]
        kb_o[0, :, sl] = kb.astype(BF16)
        if values_t:
            vbt = vb.T.astype(BF16)
            for e in range(2):
                store_vt(vtb_o, 2 * j + e, vbt[e * HEAD_DIM:(e + 1) * HEAD_DIM], e)
        else:
            vtb_o[0, :, sl] = vb.astype(BF16)
        if kv_f32:
            kb32_o[0, :, sl] = kb
            vb32_o[0, :, sl] = vb

    xbc_sub, xbc_len = xbc_o.shape[0], xbc_o.shape[2]
    for s in range(C_CONV_CH // LANES):
        col = OFF_CX + s * LANES if s < C_INNER // LANES else OFF_CB + s * LANES - C_INNER
        for u in range(xbc_sub):
            xbc_o[u, s] = p_scr[u * xbc_len:(u + 1) * xbc_len, col:col + LANES]
    z_o[0] = p_scr[:, OFF_CZ:OFF_CZ + C_INNER]
    dt_o[0] = p_scr[:, OFF_DT:OFF_DT + LANES]


def _proj(x, mod, w_in_p, layer, qn, kn, rope_tabs, *, per_batch_mod, kv_f32, seq_len, values_t):
    nb, L, d = x.shape
    tm = min(PROJ_TM, L)
    rope = rope_tabs is not None
    mod_map = (lambda b, i: (b, 0, 0)) if per_batch_mod else (lambda b, i: (0, 0, 0))
    row_spec = lambda w: pl.BlockSpec((1, tm, w), lambda b, i: (b, i, 0))
    in_specs = [
        row_spec(d),
        pl.BlockSpec((1, 6, d), mod_map),
        pl.BlockSpec((None, d, PROJ_PAD), lambda b, i: (layer, 0, 0)),
        pl.BlockSpec((1, LANES), lambda b, i: (0, 0)),
        pl.BlockSpec((1, LANES), lambda b, i: (0, 0)),
    ]
    args = [x, mod, w_in_p, qn, kn]
    if rope:
        in_specs += [pl.BlockSpec((tm, LANES), lambda b, i: (i, 0))] * 6
        args += list(rope_tabs)
    widths = [(A_WIDTH, BF16), (A_WIDTH, BF16), (-A_Q_HEADS if values_t else A_WIDTH, BF16),
              (B_WIDTH, BF16), (B_WIDTH, BF16), (-B_HEADS if values_t else B_WIDTH, BF16),
              ("slabs", F32), (C_INNER, F32), (LANES, F32)]
    if kv_f32:
        widths += [(LANES, F32), (LANES, F32), (B_WIDTH, F32), (B_WIDTH, F32)]
    out_shape, out_specs = [], []
    n_seq = nb * L // seq_len
    in_seq = seq_len >= tm
    assert (L == seq_len) if in_seq else (nb == 1 and tm % seq_len == 0)
    n_slab = C_CONV_CH // LANES
    for w, dt in widths:
        if w == "slabs":
            out_shape.append(jax.ShapeDtypeStruct((n_seq, n_slab, seq_len, LANES), dt))
            out_specs.append(pl.BlockSpec((1, n_slab, tm, LANES), lambda b, i: (b, 0, i, 0)) if in_seq else
                             pl.BlockSpec((tm // seq_len, n_slab, seq_len, LANES), lambda b, i: (i, 0, 0, 0)))
        elif w > 0:
            out_shape.append(jax.ShapeDtypeStruct((nb, L, w), dt))
            out_specs.append(row_spec(w))
        else:
            out_shape.append(jax.ShapeDtypeStruct((n_seq, -w, LANES, seq_len), dt))
            out_specs.append(pl.BlockSpec((1, -w, LANES, tm), lambda b, i: (b, 0, 0, i)) if in_seq else
                             pl.BlockSpec((tm // seq_len, -w, LANES, seq_len), lambda b, i: (i, 0, 0, 0)))
    return pl.pallas_call(
        functools.partial(_proj_kernel, rope=rope, kv_f32=kv_f32),
        out_shape=out_shape,
        grid=(nb, L // tm),
        in_specs=in_specs,
        out_specs=out_specs,
        scratch_shapes=[pltpu.VMEM((tm, PROJ_PAD), F32)],
        compiler_params=_cparams(2),
        name="proj_rope" if rope else "proj",
    )(*args)


class _ScoreMap:
    def __init__(self, tag, qm, chunks, s_ref):
        self.tag, self.qm, self.chunks, self.s_ref = tag, qm, chunks, s_ref
        self.offsets = [sum(w for _, _, w in chunks[:c]) for c in range(len(chunks))]
        self.m_part = self.m = self.acc = None

    def pass1(self, c):
        k_fn, _, w = self.chunks[c]
        tq = self.qm.shape[0]
        s = lax.dot_general(k_fn(), self.qm, (((1,), (1,)), ((), ())), preferred_element_type=F32)
        self.s_ref[self.offsets[c]:self.offsets[c] + w, :] = s
        part = jnp.max(s.reshape(w // 8, 8, tq), axis=0)
        self.m_part = part if self.m_part is None else jnp.maximum(self.m_part, part)

    def finish_max(self):
        self.m = jnp.max(self.m_part, axis=0, keepdims=True)

    def pass2(self, c):
        _, vt_fn, w = self.chunks[c]
        e = jnp.exp2(self.s_ref[self.offsets[c]:self.offsets[c] + w, :] - self.m).astype(BF16)
        pv = jnp.dot(vt_fn(), e, preferred_element_type=F32)
        self.acc = pv if self.acc is None else self.acc + pv


def _run_score_maps(maps, on_done):
    for mp in maps[:ATTN_AHEAD]:
        for c in range(len(mp.chunks)):
            mp.pass1(c)
        mp.finish_max()
    for i, mp in enumerate(maps):
        nxt = maps[i + ATTN_AHEAD] if i + ATTN_AHEAD < len(maps) else None
        n_next = len(nxt.chunks) if nxt is not None else 0
        for c in range(max(len(mp.chunks), n_next)):
            if c < len(mp.chunks):
                mp.pass2(c)
            if c < n_next:
                nxt.pass1(c)
        if nxt is not None:
            nxt.finish_max()
        on_done(mp)


def _normalised_half(acc, e):
    l = acc[(1 - e) * HEAD_DIM:(1 - e) * HEAD_DIM + 1, :]
    return acc[e * HEAD_DIM:(e + 1) * HEAD_DIM, :] * (1.0 / l)


def _attn_kernel(*refs, has_cache, lam_init, seq_len):
    qa_ref, k3_ref, vta_ref, qb_ref, kb_ref, vtb_ref, lamv_ref, subln_ref = refs[:8]
    pos = 8
    if has_cache:
        cka_ref, cva_ref, ckb_ref, cvb_ref = refs[pos:pos + 4]
        pos += 4
    ao_ref, bo_ref, s_scr = refs[pos:pos + 3]

    left = _left_mask()
    lane = lax.broadcasted_iota(jnp.int32, (1, LANES), 1)
    kc = min(ATTN_KC, seq_len)
    n_kc = seq_len // kc
    maps = []

    def add_map(tag, qm, chunks):
        maps.append(_ScoreMap(tag, qm, chunks, s_scr.at[len(maps) % (ATTN_AHEAD + 1)]))

    def new_chunks(k_ref, vt_ref, j, h):
        sl = slice(j * LANES, (j + 1) * LANES)
        out = []
        for c in range(n_kc):
            rows = slice(c * kc, (c + 1) * kc)
            out.append((functools.partial(lambda r, s: k_ref[0, r, s], rows, sl),
                        functools.partial(lambda r, hh: vt_ref[0, hh, :, r], rows, h), kc))
        return out

    def value_block(vt_half, e):
        ones = jnp.ones_like(vt_half)
        return jnp.concatenate([vt_half, ones] if e == 0 else [ones, vt_half], axis=0).astype(BF16)

    if has_cache:
        ck = cka_ref[0, 0]
        ck_sw = pltpu.roll(ck, LANES // 2, 1)
        ck3 = [jnp.where(left, ck, ck_sw), ck, jnp.where(left, ck_sw, ck)]
        cvt = cva_ref[0, 0].T
        past = ck.shape[0]
    for j in range(A_WIDTH // LANES):
        q = qa_ref[0, :, j * LANES:(j + 1) * LANES]
        for e in range(2):
            h = 2 * j + e
            g = h // A_GROUP
            chunks = new_chunks(k3_ref, vta_ref, j, h)
            if has_cache:
                ckj = ck3[j].astype(BF16)
                cvj = value_block(cvt[g * HEAD_DIM:(g + 1) * HEAD_DIM], e)
                chunks.append((lambda a=ckj: a, lambda a=cvj: a, past))
            qm = jnp.where(left if e == 0 else jnp.logical_not(left), q, jnp.zeros_like(q))
            add_map(("a", j, e, 0), qm, chunks)

    lv = lamv_ref[...]
    lam = (jnp.exp(jnp.sum(lv[0:1] * lv[1:2], axis=-1, keepdims=True))
           - jnp.exp(jnp.sum(lv[2:3] * lv[3:4], axis=-1, keepdims=True)) + lam_init)
    for j in range(B_WIDTH // LANES):
        sl = slice(j * LANES, (j + 1) * LANES)
        q = qb_ref[0, :, sl]
        if has_cache:
            ckj = ckb_ref[0, 0, :, sl].astype(BF16)
            cvt_j = cvb_ref[0, 0, :, sl].T
        for e in range(2):
            h = 2 * j + e
            chunks = new_chunks(kb_ref, vtb_ref, j, h)
            if has_cache:
                cvj = value_block(cvt_j[e * HEAD_DIM:(e + 1) * HEAD_DIM], e)
                chunks.append((lambda a=ckj: a, lambda a=cvj: a, ckj.shape[0]))
            for mi in range(2):
                sel = (lane // B_HALF) == (2 * e + mi)
                add_map(("b", j, e, mi), jnp.where(sel, q, jnp.zeros_like(q)), chunks)

    done = {}

    def on_done(mp):
        kind, j, e, mi = mp.tag
        done[mp.tag] = _normalised_half(mp.acc, e)
        if kind == "a" and e == 1:
            o_t = jnp.concatenate([done[("a", j, 0, 0)], done[("a", j, 1, 0)]], axis=0)
            ao_ref[0, :, j * LANES:(j + 1) * LANES] = o_t.T.astype(BF16)
        if kind == "b" and e == 1 and mi == 1:
            halves = []
            for ee in range(2):
                o = done[("b", j, ee, 0)] - lam * done[("b", j, ee, 1)]
                ms = jnp.mean(o * o, axis=0, keepdims=True)
                halves.append(o * lax.rsqrt(ms + EPS))
            o_t = jnp.concatenate(halves, axis=0)
            bo_ref[0, :, j * LANES:(j + 1) * LANES] = (
                o_t.T * subln_ref[...] * (1.0 - lam_init)).astype(BF16)

    _run_score_maps(maps, on_done)


def _softmax_pv_rows(qm, k, v):
    s = lax.dot_general(qm, k, (((1,), (1,)), ((), ())), preferred_element_type=F32)
    m_part = s[:, 0:LANES]
    for t in range(1, s.shape[1] // LANES):
        m_part = jnp.maximum(m_part, s[:, t * LANES:(t + 1) * LANES])
    e = jnp.exp2(s - jnp.max(m_part, axis=-1, keepdims=True))
    l_part = e[:, 0:LANES]
    for t in range(1, s.shape[1] // LANES):
        l_part = l_part + e[:, t * LANES:(t + 1) * LANES]
    acc = jnp.dot(e.astype(BF16), v, preferred_element_type=F32)
    return acc, jnp.sum(l_part, axis=-1, keepdims=True)


def _attn_rows_kernel(qa_ref, k3_ref, v3_ref, qb_ref, kb_ref, vb_ref, lamv_ref, subln_ref,
                      ao_ref, bo_ref, *, lam_init):
    left = _left_mask()
    lane = lax.broadcasted_iota(jnp.int32, (1, LANES), 1)
    for j in range(A_WIDTH // LANES):
        sl = slice(j * LANES, (j + 1) * LANES)
        q = qa_ref[0, :, sl]
        halves = []
        for e in range(2):
            qm = jnp.where(left if e == 0 else jnp.logical_not(left), q, jnp.zeros_like(q))
            acc, l = _softmax_pv_rows(qm, k3_ref[0, :, sl], v3_ref[0, :, sl])
            halves.append(acc * (1.0 / l))
        ao_ref[0, :, sl] = jnp.where(left, halves[0], halves[1]).astype(BF16)

    lv = lamv_ref[...]
    lam = (jnp.exp(jnp.sum(lv[0:1] * lv[1:2], axis=-1, keepdims=True))
           - jnp.exp(jnp.sum(lv[2:3] * lv[3:4], axis=-1, keepdims=True)) + lam_init)
    for j in range(B_WIDTH // LANES):
        sl = slice(j * LANES, (j + 1) * LANES)
        q = qb_ref[0, :, sl]
        halves = []
        for e in range(2):
            maps = []
            for mi in range(2):
                sel = (lane // B_HALF) == (2 * e + mi)
                acc, l = _softmax_pv_rows(jnp.where(sel, q, jnp.zeros_like(q)), kb_ref[0, :, sl], vb_ref[0, :, sl])
                maps.append(acc * (1.0 / l))
            halves.append(maps[0] - lam * maps[1])
        o = jnp.where(left, halves[0], halves[1])
        bo_ref[0, :, sl] = (_pair_rmsnorm(o, subln_ref[...]) * (1.0 - lam_init)).astype(BF16)


def _attn_rows(qa, k3, v3, qb, kb, vb, lamv, subln, *, lam_init):
    nb, L, _ = qa.shape
    spec = lambda w: pl.BlockSpec((1, L, w), lambda b: (b, 0, 0))
    return pl.pallas_call(
        functools.partial(_attn_rows_kernel, lam_init=lam_init),
        out_shape=[jax.ShapeDtypeStruct((nb, L, A_WIDTH), BF16),
                   jax.ShapeDtypeStruct((nb, L, B_WIDTH), BF16)],
        grid=(nb,),
        in_specs=[spec(A_WIDTH), spec(A_WIDTH), spec(A_WIDTH), spec(B_WIDTH), spec(B_WIDTH), spec(B_WIDTH),
                  pl.BlockSpec((4, B_HALF), lambda b: (0, 0)),
                  pl.BlockSpec((1, LANES), lambda b: (0, 0))],
        out_specs=[spec(A_WIDTH), spec(B_WIDTH)],
        compiler_params=_cparams(1),
        name="attn",
    )(qa, k3, v3, qb, kb, vb, lamv, subln)


def _attn(qa, k3, vta, qb, kb, vtb, lamv, subln, caches, layer, *, lam_init):
    nb, L, _ = qa.shape
    tq = min(ATTN_TQ, L)
    has_cache = caches is not None
    q_spec = lambda w: pl.BlockSpec((1, tq, w), lambda b, i: (b, i, 0))
    kv_spec = lambda w: pl.BlockSpec((1, L, w), lambda b, i: (b, 0, 0))
    vt_spec = lambda n: pl.BlockSpec((1, n, LANES, L), lambda b, i: (b, 0, 0, 0))
    in_specs = [q_spec(A_WIDTH), kv_spec(A_WIDTH), vt_spec(A_Q_HEADS),
                q_spec(B_WIDTH), kv_spec(B_WIDTH), vt_spec(B_HEADS),
                pl.BlockSpec((4, B_HALF), lambda b, i: (0, 0)),
                pl.BlockSpec((1, LANES), lambda b, i: (0, 0))]
    args = [qa, k3, vta, qb, kb, vtb, lamv, subln]
    lk = L
    if has_cache:
        past = caches[0].shape[2]
        lk += past
        for c in caches:
            in_specs.append(pl.BlockSpec((1, 1, past, c.shape[-1]), lambda b, i: (b, layer, 0, 0)))
            args.append(c)
    return pl.pallas_call(
        functools.partial(_attn_kernel, has_cache=has_cache, lam_init=lam_init, seq_len=L),
        out_shape=[jax.ShapeDtypeStruct((nb, L, A_WIDTH), BF16),
                   jax.ShapeDtypeStruct((nb, L, B_WIDTH), BF16)],
        grid=(nb, L // tq),
        in_specs=in_specs,
        out_specs=[q_spec(A_WIDTH), q_spec(B_WIDTH)],
        scratch_shapes=[pltpu.VMEM((ATTN_AHEAD + 1, lk, tq), F32)],
        compiler_params=_cparams(2),
        name="attn_cache" if has_cache else "attn",
    )(*args)


def _prefix_sums(tri, a):
    hi = a.astype(BF16)
    lo = (a - hi.astype(F32)).astype(BF16)
    return (jnp.dot(tri, hi, preferred_element_type=F32)
            + jnp.dot(tri, lo, preferred_element_type=F32))


def _pair_cols(x, c0):
    return jnp.where(_left_mask(), x[:, c0:c0 + 1], x[:, c0 + 1:c0 + 2])


def _ssd_kernel(*refs, has_init, emit_state, seq_len):
    xbc_ref, z_ref, dt_ref, cw_ref, cb_ref, dtb_ref, alog_ref, dexp_ref, nw_ref = refs[:9]
    pos = 9
    if has_init:
        init_ref = refs[pos]
        pos += 1
    co_ref = refs[pos]
    pos += 1
    if emit_state:
        st_ref = refs[pos]
        pos += 1
    xc_scr, y_scr, st_scr = refs[pos:pos + 3]

    L = seq_len
    rb = min(ROW_BLOCK, L)
    q = SSD_CHUNK
    nc = L // q
    left = _left_mask()

    w0 = cw_ref[0:1, :]
    w1 = cw_ref[1:2, :]
    w2 = cw_ref[2:3, :]
    cb = CONV_ROWS
    half = cb // 2
    hrow = lax.broadcasted_iota(jnp.int32, (half, 1), 0)
    for s in range(C_CONV_CH // LANES):
        ls = slice(s * LANES, (s + 1) * LANES)
        w0s, w1s, w2s, bs = w0[:, ls], w1[:, ls], w2[:, ls], cb_ref[:, ls]
        for r0 in range(0, L, cb):
            even = xbc_ref[0, s, pl.ds(r0, half, stride=2), :]
            odd = xbc_ref[0, s, pl.ds(r0 + 1, half, stride=2), :]
            if r0 > 0:
                odd_before = xbc_ref[0, s, pl.ds(r0 - 1, half, stride=2), :]
            else:
                odd_before = jnp.where(hrow == 0, 0.0, pltpu.roll(odd, 1, 0))
            if r0 + cb < L:
                even_after = xbc_ref[0, s, pl.ds(r0 + 2, half, stride=2), :]
            else:
                even_after = jnp.where(hrow == half - 1, 0.0, pltpu.roll(even, half - 1, 0))
            y_even = odd_before * w0s + even * w1s + odd * w2s + bs
            y_odd = even * w0s + odd * w1s + even_after * w2s + bs
            xc_scr[s, pl.ds(r0, half, stride=2), :] = y_even * _sigmoid(y_even)
            xc_scr[s, pl.ds(r0 + 1, half, stride=2), :] = y_odd * _sigmoid(y_odd)
    for j in range(C_INNER // LANES):
        for r0 in range(0, L, rb):
            y_scr[r0:r0 + rb, j * LANES:(j + 1) * LANES] = (
                xc_scr[j, r0:r0 + rb, :] * dexp_ref[:, j * LANES:(j + 1) * LANES])

    for d in range(2):
        if has_init:
            st_scr[d] = init_ref[0, 0, d].reshape(C_INNER, C_STATE).T
        else:
            st_scr[d] = jnp.zeros((C_STATE, C_INNER), F32)

    a_neg = -jnp.exp(alog_ref[...])
    ti = lax.broadcasted_iota(jnp.int32, (q, q), 0)
    si = lax.broadcasted_iota(jnp.int32, (q, q), 1)
    tri_incl = (si <= ti).astype(BF16)
    src_lane = lax.broadcasted_iota(jnp.int32, (LANES, C_INNER), 0)
    dst_head = lax.broadcasted_iota(jnp.int32, (LANES, C_INNER), 1) // C_HEADDIM
    spread = [(src_lane == d * C_HEADS + dst_head).astype(BF16) for d in range(2)]

    def process(d, r0):
        rows = pl.ds(r0, q)
        lo = d * C_HEADS
        causal = (si <= ti) if d == 0 else (si >= ti)
        n_x = C_INNER // LANES
        bm = [xc_scr[n_x + g, rows, :] for g in range(C_GROUPS)]
        cm = [xc_scr[n_x + C_GROUPS + g, rows, :] for g in range(C_GROUPS)]
        bm_b = [b.astype(BF16) for b in bm]
        cm_b = [c.astype(BF16) for c in cm]
        bmt_b = [b.T.astype(BF16) for b in bm]
        gmat = [lax.dot_general(cm_b[g], bm_b[g], (((1,), (1,)), ((), ())), preferred_element_type=F32)
                for g in range(C_GROUPS)]

        dtr = dt_ref[0, rows, :] + dtb_ref[...]
        dt = jnp.maximum(dtr, 0.0) + jnp.log1p(jnp.exp(-jnp.abs(dtr)))
        a = dt * a_neg
        cs = _prefix_sums(tri_incl, a)
        tot = cs[q - 1:q, :]
        ev = cs if d == 0 else (tot - cs + a)
        e_in = jnp.exp(ev)
        e_out = jnp.exp(tot - ev)
        dec = jnp.exp(tot)
        ev_t = ev.T
        per_head = jnp.concatenate([dt, e_in, e_out], axis=0).astype(BF16)
        wide = jnp.dot(per_head, spread[d], preferred_element_type=F32)
        dt_w, e_in_w, e_out_w = wide[0:q], wide[q:2 * q], wide[2 * q:3 * q]

        st = st_scr.at[d]
        for j in range(C_INNER // LANES):
            sl = slice(j * LANES, (j + 1) * LANES)
            h0 = 2 * j
            g0, g1 = h0 // (C_HEADS // C_GROUPS), (h0 + 1) // (C_HEADS // C_GROUPS)
            xdt = xc_scr[j, rows, :] * dt_w[:, sl]
            xdt_b = xdt.astype(BF16)
            st_blk = st[:, sl]
            st_b = st_blk.astype(BF16)
            yd = []
            for e in range(2):
                hh = lo + h0 + e
                g = g0 if e == 0 else g1
                diff = ev[:, hh:hh + 1] - ev_t[hh:hh + 1, :]
                lmat = jnp.exp(jnp.where(causal, diff, NEG_BIG))
                yd.append(jnp.dot((gmat[g] * lmat).astype(BF16), xdt_b, preferred_element_type=F32))
            yo0 = jnp.dot(cm_b[g0], st_b, preferred_element_type=F32)
            yo1 = yo0 if g1 == g0 else jnp.dot(cm_b[g1], st_b, preferred_element_type=F32)
            y_blk = (jnp.where(left, yd[0], yd[1])
                     + jnp.where(left, yo0, yo1) * e_in_w[:, sl])
            y_scr[rows, sl] = y_scr[rows, sl] + y_blk
            xw = (xdt * e_out_w[:, sl]).astype(BF16)
            up0 = jnp.dot(bmt_b[g0], xw, preferred_element_type=F32)
            up1 = up0 if g1 == g0 else jnp.dot(bmt_b[g1], xw, preferred_element_type=F32)
            st[:, sl] = st_blk * _pair_cols(dec, lo + h0) + jnp.where(left, up0, up1)

    if nc <= 2:
        for c in range(nc):
            process(0, c * q)
            process(1, (nc - 1 - c) * q)
    else:
        def body(c, carry):
            process(0, pl.multiple_of(c * q, q))
            process(1, pl.multiple_of((nc - 1 - c) * q, q))
            return carry
        lax.fori_loop(0, nc, body, 0, unroll=SSD_UNROLL if nc % SSD_UNROLL == 0 else 1)

    if emit_state:
        for d in range(2):
            st_ref[0, d] = st_scr[d].T.reshape(C_HEADS, C_HEADDIM, C_STATE)

    for r0 in range(0, L, rb):
        zz = z_ref[0, r0:r0 + rb, :]
        yg = y_scr[r0:r0 + rb, :] * (zz * _sigmoid(zz))
        ms = jnp.mean(yg * yg, axis=-1, keepdims=True)
        co_ref[0, r0:r0 + rb, :] = (yg * lax.rsqrt(ms + EPS) * nw_ref[...]).astype(BF16)


def _ssd(xbc, z, dt, conv_w, conv_b, dtb, alog, dexp, norm_w, state, layer, *, emit_state):
    nb, n_slab, L, _ = xbc.shape
    has_init = state is not None
    seq_spec = lambda w: pl.BlockSpec((1, L, w), lambda b: (b, 0, 0))
    par_spec = lambda r, w: pl.BlockSpec((r, w), lambda b: (0, 0))
    in_specs = [pl.BlockSpec((1, n_slab, L, LANES), lambda b: (b, 0, 0, 0)), seq_spec(C_INNER), seq_spec(LANES),
                par_spec(3, C_CONV_CH), par_spec(1, C_CONV_CH), par_spec(1, LANES), par_spec(1, LANES),
                par_spec(1, C_INNER), par_spec(1, C_INNER)]
    args = [xbc, z, dt, conv_w, conv_b, dtb, alog, dexp, norm_w]
    if has_init:
        in_specs.append(pl.BlockSpec((1, 1, 2, C_HEADS, C_HEADDIM, C_STATE),
                                     lambda b: (b, layer, 0, 0, 0, 0)))
        args.append(state)
    out_shape = [jax.ShapeDtypeStruct((nb, L, C_INNER), BF16)]
    out_specs = [seq_spec(C_INNER)]
    if emit_state:
        out_shape.append(jax.ShapeDtypeStruct((nb, 2, C_HEADS, C_HEADDIM, C_STATE), F32))
        out_specs.append(pl.BlockSpec((1, 2, C_HEADS, C_HEADDIM, C_STATE), lambda b: (b, 0, 0, 0, 0)))
    return pl.pallas_call(
        functools.partial(_ssd_kernel, has_init=has_init, emit_state=emit_state, seq_len=L),
        out_shape=out_shape,
        grid=(nb,),
        in_specs=in_specs,
        out_specs=out_specs,
        scratch_shapes=[pltpu.VMEM((n_slab, L, LANES), F32), pltpu.VMEM((L, C_INNER), F32),
                        pltpu.VMEM((2, C_STATE, C_INNER), F32)],
        compiler_params=_cparams(1),
        name="ssd_init" if has_init else "ssd",
    )(*args)


def _outp_kernel(a_ref, b_ref, c_ref, x_ref, mod_ref, w_ref, x1_o, h2_o):
    mix = jnp.concatenate([a_ref[0], b_ref[0], c_ref[0]], axis=-1)
    o = jnp.dot(mix, w_ref[...], preferred_element_type=F32)
    gate1 = mod_ref[0, 2:3, :]
    shift2 = mod_ref[0, 3:4, :]
    scale2 = mod_ref[0, 4:5, :]
    x1 = x_ref[0] + gate1 * o
    ms = jnp.mean(x1 * x1, axis=-1, keepdims=True)
    x1_o[0] = x1
    h2_o[0] = ((x1 * lax.rsqrt(ms + EPS)) * (1.0 + scale2) + shift2).astype(BF16)


def _outp(a_out, b_out, c_out, x, mod, w_out_b, layer, *, per_batch_mod):
    nb, L, d = x.shape
    tm = min(PROJ_TM, L)
    mod_map = (lambda b, i: (b, 0, 0)) if per_batch_mod else (lambda b, i: (0, 0, 0))
    row_spec = lambda w: pl.BlockSpec((1, tm, w), lambda b, i: (b, i, 0))
    return pl.pallas_call(
        _outp_kernel,
        out_shape=[jax.ShapeDtypeStruct((nb, L, d), F32), jax.ShapeDtypeStruct((nb, L, d), BF16)],
        grid=(nb, L // tm),
        in_specs=[row_spec(A_WIDTH), row_spec(B_WIDTH), row_spec(C_INNER), row_spec(d),
                  pl.BlockSpec((1, 6, d), mod_map),
                  pl.BlockSpec((None, d, d), lambda b, i: (layer, 0, 0))],
        out_specs=[row_spec(d), row_spec(d)],
        compiler_params=_cparams(2),
        name="outp",
    )(a_out, b_out, c_out, x, mod, w_out_b)


HALO = 16


def _ffn_kernel(h_ref, hp_ref, hn_ref, x1_ref, mod_ref, wu_ref, cw_ref, cb_ref, wd_ref, fw_ref,
                o_ref, hext_scr, u_scr, acc_scr, *, seq_len, final_norm):
    tm = h_ref.shape[1]
    i = pl.program_id(1)
    n_chunks = wu_ref.shape[0]
    sub = min(FFN_SUB, tm)
    n_sub = tm // sub
    rb = min(FFN_RB, sub)
    n_items = n_sub * n_chunks

    has_prev = ((i * tm) & (seq_len - 1)) != 0
    has_next = ((i * tm + tm) & (seq_len - 1)) != 0
    hext_scr[0:HALO, :] = jnp.where(has_prev, hp_ref[0], jnp.zeros_like(hp_ref[0]))
    hext_scr[HALO:HALO + tm, :] = h_ref[0]
    hext_scr[HALO + tm:HALO + tm + HALO, :] = jnp.where(has_next, hn_ref[0], jnp.zeros_like(hn_ref[0]))
    acc_scr[...] = jnp.zeros_like(acc_scr)
    row8 = lax.broadcasted_iota(jnp.int32, (8, 1), 0)

    n_uslab = 2 * FFN_FC // LANES
    n_oslab = acc_scr.shape[0]
    half = rb // 2

    def split(n):
        if isinstance(n, int):
            return (n // n_chunks) * sub, n % n_chunks
        if n_sub == 1:
            return 0, n
        t = sum((n >= k * n_chunks).astype(jnp.int32) for k in range(1, n_sub))
        return pl.multiple_of(t * sub, sub), n - t * n_chunks

    def up(n, slot):
        row0, c = split(n)
        h_rows = hext_scr[pl.ds(row0, sub + 2 * HALO), :]
        u = jnp.dot(h_rows, wu_ref[c], preferred_element_type=F32)
        for s in range(n_uslab):
            u_scr[slot, s] = u[:, s * LANES:(s + 1) * LANES]

    def conv_down(n, slot):
        row0, c = split(n)
        cw = cw_ref[c]
        bias = cb_ref[c]
        for r0 in range(0, sub, rb):
            base = HALO + r0
            ys = []
            for s in range(n_uslab):
                ls = slice(s * LANES, (s + 1) * LANES)
                even = u_scr[slot, s, pl.ds(base, half, stride=2), :]
                odd = u_scr[slot, s, pl.ds(base + 1, half, stride=2), :]
                odd_before = u_scr[slot, s, pl.ds(base - 1, half, stride=2), :]
                even_after = u_scr[slot, s, pl.ds(base + 2, half, stride=2), :]
                if r0 % seq_len == 0 and (r0 > 0 or seq_len <= sub):
                    odd_before = jnp.concatenate(
                        [jnp.where(row8 == 0, 0.0, odd_before[0:8]), odd_before[8:]], axis=0)
                if (r0 + rb) % seq_len == 0 and (r0 + rb < sub or seq_len <= sub):
                    even_after = jnp.concatenate(
                        [even_after[:half - 8], jnp.where(row8 == 7, 0.0, even_after[half - 8:])], axis=0)
                w0, w1, w2, b = cw[0:1, ls], cw[1:2, ls], cw[2:3, ls], bias[:, ls]
                y_even = odd_before * w0 + even * w1 + odd * w2 + b
                y_odd = even * w0 + odd * w1 + even_after * w2 + b
                ys.append(jnp.concatenate([y_even, y_odd], axis=0))
            n_g = n_uslab // 2
            act = jnp.concatenate([ys[s] * _sigmoid(ys[s]) * ys[n_g + s] for s in range(n_g)], axis=1)
            dn = jnp.dot(act.astype(BF16), wd_ref[c], preferred_element_type=F32)
            for s in range(n_oslab):
                ls = slice(s * LANES, (s + 1) * LANES)
                acc_scr[s, pl.ds(row0 + r0, half, stride=2), :] += dn[0:half, ls]
                acc_scr[s, pl.ds(row0 + r0 + 1, half, stride=2), :] += dn[half:rb, ls]

    up(0, 0)

    def pair(k, carry):
        n = 2 * k
        up(n + 1, 1)
        conv_down(n, 0)
        up(n + 2, 0)
        conv_down(n + 1, 1)
        return carry

    lax.fori_loop(0, (n_items - 1) // 2, pair, 0)
    if n_items % 2 == 1:
        conv_down(n_items - 1, 0)
    else:
        up(n_items - 1, 1)
        conv_down(n_items - 2, 0)
        conv_down(n_items - 1, 1)

    gate2 = mod_ref[0, 5:6, :]
    ffn_out = jnp.concatenate([acc_scr[s] for s in range(n_oslab)], axis=1)
    out = x1_ref[0] + gate2 * ffn_out
    if final_norm:
        ms = jnp.mean(out * out, axis=-1, keepdims=True)
        out = out * lax.rsqrt(ms + EPS) * fw_ref[...]
    o_ref[0] = out


def _ffn(h2, x1, mod, wu_c, cw_c, cb_c, wd_c, layer, final_w, *, per_batch_mod, seq_len, final_norm):
    nb, L, d = x1.shape
    tm = min(FFN_TM, L)
    n_tiles = L // tm
    sub = min(FFN_SUB, tm)
    rb = min(FFN_RB, sub)
    assert seq_len & (seq_len - 1) == 0 and L % seq_len == 0 and tm % sub == 0
    assert seq_len % tm == 0 or (tm % seq_len == 0 and sub % seq_len == 0 and seq_len % rb == 0)
    n_chunks = wu_c.shape[1]
    hb = tm // HALO
    n_hblk = L // HALO
    mod_map = (lambda b, i: (b, 0, 0)) if per_batch_mod else (lambda b, i: (0, 0, 0))
    row_spec = lambda w: pl.BlockSpec((1, tm, w), lambda b, i: (b, i, 0))
    const3 = lambda s: pl.BlockSpec(s, lambda b, i: (0, 0, 0))
    return pl.pallas_call(
        functools.partial(_ffn_kernel, seq_len=seq_len, final_norm=final_norm),
        out_shape=jax.ShapeDtypeStruct((nb, L, d), F32),
        grid=(nb, n_tiles),
        in_specs=[row_spec(d),
                  pl.BlockSpec((1, HALO, d), lambda b, i: (b, jnp.maximum(i * hb - 1, 0), 0)),
                  pl.BlockSpec((1, HALO, d), lambda b, i: (b, jnp.minimum((i + 1) * hb, n_hblk - 1), 0)),
                  row_spec(d),
                  pl.BlockSpec((1, 6, d), mod_map),
                  pl.BlockSpec((None, n_chunks, d, 2 * FFN_FC), lambda b, i: (layer, 0, 0, 0)),
                  const3((n_chunks, 3, 2 * FFN_FC)),
                  const3((n_chunks, 1, 2 * FFN_FC)),
                  pl.BlockSpec((None, n_chunks, FFN_FC, d), lambda b, i: (layer, 0, 0, 0)),
                  pl.BlockSpec((1, d), lambda b, i: (0, 0))],
        out_specs=row_spec(d),
        scratch_shapes=[pltpu.VMEM((tm + 2 * HALO, d), BF16),
                        pltpu.VMEM((2, 2 * FFN_FC // LANES, sub + 2 * HALO, LANES), F32),
                        pltpu.VMEM((d // LANES, tm, LANES), F32)],
        compiler_params=_cparams(2),
        name="ffn_final" if final_norm else "ffn",
    )(h2, h2, h2, x1, mod, wu_c, cw_c, cb_c, wd_c, final_w)


def _rope_tables(L, d):
    rows = L // GRID_W
    row = jnp.repeat(jnp.arange(rows), GRID_W).astype(F32)
    col = jnp.tile(jnp.arange(GRID_W), rows).astype(F32)
    quarter = d // 4
    inv = ROPE_THETA ** (-jnp.arange(quarter, dtype=F32) / quarter)
    ang_r = row[:, None] * inv[None, :]
    ang_c = col[:, None] * inv[None, :]
    ang = jnp.concatenate([ang_r, ang_r, ang_c, ang_c], axis=-1)
    cos, sin = jnp.cos(ang), jnp.sin(ang)
    even = ((jnp.arange(d) // quarter) % 2 == 0)[None, :]
    s_up = jnp.where(even, -sin, 0.0)
    s_dn = jnp.where(even, 0.0, sin)
    reps = LANES // d
    return tuple(jnp.tile(t, (1, reps)) for t in (cos, s_up, s_dn))


def _pad_lanes(v, width=LANES):
    v = v.reshape(1, -1).astype(F32)
    return jnp.pad(v, ((0, 0), (0, width - v.shape[1])))


def _layer_params(l, a_q_norm, a_k_norm, b_lambda, b_subln, ssm_conv_w, ssm_conv_b, ssm_A_log,
                  ssm_dt_bias, ssm_D, ssm_norm_w, ffn_conv_w, ffn_conv_b):
    n_chunks = D_FF // FFN_FC

    def gv_chunks(t):
        lead = t.shape[:-1]
        g = t[..., :D_FF].reshape(lead + (n_chunks, FFN_FC))
        v = t[..., D_FF:].reshape(lead + (n_chunks, FFN_FC))
        gv = jnp.concatenate([g, v], axis=-1)
        return jnp.moveaxis(gv, -2, 0)

    return dict(
        qn=jnp.tile(a_q_norm[l], 2).reshape(1, LANES),
        kn=jnp.tile(a_k_norm[l], 2).reshape(1, LANES),
        lamv=b_lambda[l],
        subln=jnp.tile(b_subln[l], 2).reshape(1, LANES),
        conv_w=ssm_conv_w[l],
        conv_b=ssm_conv_b[l].reshape(1, C_CONV_CH),
        dtb=_pad_lanes(ssm_dt_bias[l]),
        alog=_pad_lanes(ssm_A_log[l]),
        dexp=jnp.repeat(ssm_D[l], C_HEADDIM).reshape(1, C_INNER),
        norm_w=ssm_norm_w[l].reshape(1, C_INNER),
        cw=gv_chunks(ffn_conv_w[l]),
        cb=gv_chunks(ffn_conv_b[l].reshape(1, 2 * D_FF)),
    )


def _block(x, mod, p, wts, rope_tabs, caches, state, layer, *, seq_shape, per_batch_mod, is_ctx, lam_init,
           final_w, final_norm):
    n_seq, seq_len = seq_shape
    outs = _proj(x, mod, wts["w_in"], layer, p["qn"], p["kn"], rope_tabs,
                 per_batch_mod=per_batch_mod, kv_f32=is_ctx, seq_len=seq_len, values_t=caches is not None)
    seq = lambda t: t.reshape(n_seq, seq_len, t.shape[-1]) if t.ndim == 3 else t
    qa, k3, va, qb, kb, vb, xbc, z, dt = [seq(t) for t in outs[:9]]
    if caches is None:
        a_out, b_out = _attn_rows(qa, k3, va, qb, kb, vb, p["lamv"], p["subln"], lam_init=lam_init)
    else:
        a_out, b_out = _attn(qa, k3, va, qb, kb, vb, p["lamv"], p["subln"], caches, layer, lam_init=lam_init)
    ssd_out = _ssd(xbc, z, dt, p["conv_w"], p["conv_b"], p["dtb"], p["alog"], p["dexp"], p["norm_w"],
                   state, layer, emit_state=is_ctx)
    c_out = ssd_out[0]
    flat = lambda t: t.reshape(x.shape[0], x.shape[1], t.shape[-1])
    x1, h2 = _outp(flat(a_out), flat(b_out), flat(c_out), x, mod, wts["w_out"], layer,
                   per_batch_mod=per_batch_mod)
    x2 = _ffn(h2, x1, mod, wts["wu"], p["cw"], p["cb"], wts["wd"], layer, final_w,
              per_batch_mod=per_batch_mod, seq_len=seq_len, final_norm=final_norm)
    extras = None
    if is_ctx:
        extras = tuple(seq(t) for t in outs[9:13]) + (ssd_out[1],)
    return x2, extras


def kernel(x_prompt, x_sample, cache_a_k, cache_a_v, cache_b_k, cache_b_v, state_ssm, c, c_ctx, ada_w, ada_b, w_in, a_q_norm, a_k_norm, b_lambda, b_subln, ssm_conv_w, ssm_conv_b, ssm_A_log, ssm_dt_bias, ssm_D, ssm_norm_w, w_out, ffn_up, ffn_conv_w, ffn_conv_b, ffn_down, final_norm_w):
    depth = w_in.shape[0]
    nbp, lp, d = x_prompt.shape
    nbs, ls, _ = x_sample.shape
    past = cache_a_k.shape[2]

    mod_rows = 8
    assert 1 + nbs <= mod_rows
    cvecs = jnp.concatenate([c_ctx[None, :], c, jnp.zeros((mod_rows - 1 - nbs, d), F32)], axis=0)
    mod = _modulation(cvecs, ada_w, ada_b)

    rope_tabs = _rope_tables(ls, HEAD_DIM) + _rope_tables(ls, B_HALF)
    caches = (cache_a_k.reshape(nbs, depth, past, A_KV_HEADS * HEAD_DIM),
              cache_a_v.reshape(nbs, depth, past, A_KV_HEADS * HEAD_DIM),
              cache_b_k.reshape(nbs, depth, past, B_WIDTH),
              cache_b_v.reshape(nbs, depth, past, B_WIDTH))
    final_w = final_norm_w.reshape(1, d)
    n_chunks = D_FF // FFN_FC
    wts = dict(
        w_in=_pad_cast_bf16(w_in, PROJ_PAD),
        w_out=_cast_bf16(w_out),
        wu=_gv_cast_bf16(ffn_up),
        wd=_cast_bf16(ffn_down).reshape(depth, n_chunks, FFN_FC, d),
    )

    yp = x_prompt.reshape(1, nbp * lp, d)
    ys = x_sample
    ctx_extras = []
    for l in range(depth):
        lam_init = 0.8 - 0.6 * math.exp(-0.3 * l)
        p = _layer_params(l, a_q_norm, a_k_norm, b_lambda, b_subln, ssm_conv_w, ssm_conv_b,
                          ssm_A_log, ssm_dt_bias, ssm_D, ssm_norm_w, ffn_conv_w, ffn_conv_b)
        last = l == depth - 1
        mod_ctx = mod[l, 0].reshape(1, 6, d)
        mod_lat = mod[l, 1:1 + nbs].reshape(nbs, 6, d)
        yp, extras = _block(yp, mod_ctx, p, wts, None, None, None, l, seq_shape=(nbp, lp),
                            per_batch_mod=False, is_ctx=True, lam_init=lam_init,
                            final_w=final_w, final_norm=last)
        ctx_extras.append(extras)
        ys, _ = _block(ys, mod_lat, p, wts, rope_tabs, caches, state_ssm, l, seq_shape=(nbs, ls),
                       per_batch_mod=True, is_ctx=False, lam_init=lam_init,
                       final_w=final_w, final_norm=last)

    y_prompt = yp.reshape(nbp, lp, d)
    stack = lambda k: jnp.stack([e[k] for e in ctx_extras], axis=1)
    new_a_k = stack(0).reshape(nbp, depth, lp, A_KV_HEADS, HEAD_DIM)
    new_a_v = stack(1).reshape(nbp, depth, lp, A_KV_HEADS, HEAD_DIM)
    new_b_k = stack(2).reshape(nbp, depth, lp, B_HEADS, 2, B_HALF)
    new_b_v = stack(3).reshape(nbp, depth, lp, B_HEADS, 2 * B_HALF)
    new_ssm = stack(4)
    return (y_prompt, ys, new_a_k, new_a_v, new_b_k, new_b_v, new_ssm)
```

```python
import functools
import math

import jax
import jax.numpy as jnp
from jax import lax
from jax.experimental import pallas as pl
from jax.experimental.pallas import tpu as pltpu

F32 = jnp.float32
BF16 = jnp.bfloat16

LANES = 128
VMEM_LIMIT_BYTES = 56 * 1024 * 1024

D_MODEL = 1024
GRID_W = 64
SSD_CHUNK = 128
ROPE_THETA = 10000.0
EPS = 1e-6
HEAD_DIM = 64
A_Q_HEADS = 6
A_KV_HEADS = 2
A_GROUP = A_Q_HEADS // A_KV_HEADS
A_WIDTH = A_Q_HEADS * HEAD_DIM
B_HEADS = 4
B_HALF = 32
B_WIDTH = B_HEADS * 2 * B_HALF
C_HEADS = 6
C_HEADDIM = 64
C_INNER = C_HEADS * C_HEADDIM
C_GROUPS = 2
C_STATE = 128
C_CONV_CH = C_INNER + 2 * C_GROUPS * C_STATE
D_FF = 2816
PROJ_WIDTH = A_WIDTH + 4 * HEAD_DIM + 3 * B_WIDTH + 2 * C_INNER + 4 * C_STATE + 2 * C_HEADS
PROJ_PAD = ((PROJ_WIDTH + 2 * LANES - 1) // (2 * LANES)) * (2 * LANES)

OFF_AQ = 0
OFF_AK = OFF_AQ + A_WIDTH
OFF_AV = OFF_AK + A_KV_HEADS * HEAD_DIM
OFF_BQ = OFF_AV + A_KV_HEADS * HEAD_DIM
OFF_BK = OFF_BQ + B_WIDTH
OFF_BV = OFF_BK + B_WIDTH
OFF_CX = OFF_BV + B_WIDTH
OFF_CZ = OFF_CX + C_INNER
OFF_CB = OFF_CZ + C_INNER
OFF_CC = OFF_CB + C_GROUPS * C_STATE
OFF_DT = OFF_CC + C_GROUPS * C_STATE

MOD_BLOCK = 1024
PROJ_TM = 512
FFN_TM = 512
FFN_SUB = 512
FFN_FC = 256
FFN_RB = 256
ATTN_TQ = 512
ATTN_KC = 512
ATTN_AHEAD = 2
ROW_BLOCK = 256
CONV_ROWS = 128
SSD_UNROLL = 4
NEG_BIG = -1e30
LOG2E = 1.4426950408889634


def _cparams(n_grid):
    return pltpu.CompilerParams(
        dimension_semantics=("parallel",) * n_grid,
        vmem_limit_bytes=VMEM_LIMIT_BYTES,
    )


def _sigmoid(x):
    return 1.0 / (1.0 + jnp.exp(-x))


def _left_mask():
    lane = lax.broadcasted_iota(jnp.int32, (1, LANES), 1)
    return lane < (LANES // 2)


def _pair_rmsnorm(y, w):
    left = _left_mask()
    sq = y * y
    s_l = jnp.sum(jnp.where(left, sq, 0.0), axis=-1, keepdims=True)
    s_r = jnp.sum(jnp.where(left, 0.0, sq), axis=-1, keepdims=True)
    ms = jnp.where(left, s_l, s_r) * (1.0 / HEAD_DIM)
    return y * lax.rsqrt(ms + EPS) * w


def _rope(y, cos, s_up, s_dn, quarter):
    return (y * cos
            + pltpu.roll(y, LANES - quarter, 1) * s_up
            + pltpu.roll(y, quarter, 1) * s_dn)


def _modulation_kernel(c_ref, w_ref, b_ref, o_ref):
    c = c_ref[...]
    s = c * _sigmoid(c)
    o_ref[0] = jnp.dot(s.astype(BF16), w_ref[0].astype(BF16),
                       preferred_element_type=F32) + b_ref[0]


def _modulation(cvecs, ada_w, ada_b):
    depth, d, width = ada_w.shape
    rows = cvecs.shape[0]
    return pl.pallas_call(
        _modulation_kernel,
        out_shape=jax.ShapeDtypeStruct((depth, rows, width), F32),
        grid=(depth, width // MOD_BLOCK),
        in_specs=[
            pl.BlockSpec((rows, d), lambda l, j: (0, 0)),
            pl.BlockSpec((1, d, MOD_BLOCK), lambda l, j: (l, 0, j)),
            pl.BlockSpec((1, 1, MOD_BLOCK), lambda l, j: (l, 0, j)),
        ],
        out_specs=pl.BlockSpec((1, rows, MOD_BLOCK), lambda l, j: (l, 0, j)),
        compiler_params=_cparams(2),
        name="modulation",
    )(cvecs, ada_w, ada_b.reshape(depth, 1, width))


CAST_ROWS = 256


def _cast_kernel(x_ref, o_ref):
    o_ref[...] = x_ref[...].astype(o_ref.dtype)


def _cast_bf16(x):
    depth, rows, cols = x.shape
    spec = pl.BlockSpec((1, CAST_ROWS, cols), lambda l, r: (l, r, 0))
    return pl.pallas_call(
        _cast_kernel, out_shape=jax.ShapeDtypeStruct(x.shape, BF16),
        grid=(depth, rows // CAST_ROWS), in_specs=[spec], out_specs=spec,
        compiler_params=_cparams(2), name="cast_bf16",
    )(x)


def _pad_cast_kernel(x_ref, o_ref, *, n_cols):
    tr = x_ref.shape[0]
    col = lax.broadcasted_iota(jnp.int32, (tr, 1), 0) + pl.program_id(0) * tr
    for l in range(x_ref.shape[1]):
        x = jnp.where(col < n_cols, x_ref[:, l, :], 0.0)
        o_ref[l] = x.T.astype(BF16)


def _pad_cast_bf16(x, width):
    depth, rows, cols = x.shape
    return pl.pallas_call(
        functools.partial(_pad_cast_kernel, n_cols=cols),
        out_shape=jax.ShapeDtypeStruct((depth, rows, width), BF16),
        grid=(width // LANES,),
        in_specs=[pl.BlockSpec((LANES, depth, rows), lambda i: (i, 0, 0))],
        out_specs=pl.BlockSpec((depth, rows, LANES), lambda i: (0, 0, i)),
        compiler_params=_cparams(1), name="pad_cast_bf16",
    )(jnp.transpose(x, (2, 0, 1)))


def _gv_cast_kernel(g_ref, v_ref, o_ref):
    o_ref[0, 0, :, 0:FFN_FC] = g_ref[0].astype(BF16)
    o_ref[0, 0, :, FFN_FC:2 * FFN_FC] = v_ref[0].astype(BF16)


def _gv_cast_bf16(up):
    depth, d, _ = up.shape
    n_chunks = D_FF // FFN_FC
    return pl.pallas_call(
        _gv_cast_kernel, out_shape=jax.ShapeDtypeStruct((depth, n_chunks, d, 2 * FFN_FC), BF16),
        grid=(depth, n_chunks),
        in_specs=[pl.BlockSpec((1, d, FFN_FC), lambda l, c: (l, 0, c)),
                  pl.BlockSpec((1, d, FFN_FC), lambda l, c: (l, 0, n_chunks + c))],
        out_specs=pl.BlockSpec((1, 1, d, 2 * FFN_FC), lambda l, c: (l, c, 0, 0)),
        compiler_params=_cparams(2), name="gv_cast_bf16",
    )(up, up)


def _proj_kernel(*refs, rope, kv_f32, n_alias):
    x_ref, mod_ref, w_ref, qn_ref, kn_ref = refs[:5]
    pos = 5
    if rope:
        ca_ref, sau_ref, sad_ref, cb_ref, sbu_ref, sbd_ref = refs[pos:pos + 6]
        pos += 6
    pos += n_alias
    (qa_o, k3_o, vta_o, qb_o, kb_o, vtb_o, xbc_o, z_o, dt_o) = refs[pos:pos + 9]
    pos += 9
    if kv_f32:
        ka32_o, va32_o, kb32_o, vb32_o = refs[pos:pos + 4]
        pos += 4
    p_scr = refs[pos]

    x = x_ref[0]
    ms = jnp.mean(x * x, axis=-1, keepdims=True)
    shift = mod_ref[0, 0:1, :]
    scale = mod_ref[0, 1:2, :]
    h = (x * lax.rsqrt(ms + EPS)) * (1.0 + scale) + shift
    p_scr[...] = jnp.dot(h.astype(BF16), w_ref[...], preferred_element_type=F32)

    left = _left_mask()

    def rope_a(y):
        if not rope:
            return y
        return _rope(y, ca_ref[...], sau_ref[...], sad_ref[...], HEAD_DIM // 4)

    def rope_b(y):
        if not rope:
            return y
        return _rope(y, cb_ref[...], sbu_ref[...], sbd_ref[...], B_HALF // 4)

    values_t = len(vta_o.shape) == 4
    if values_t:
        n_sub, sub_len = vta_o.shape[0], vta_o.shape[3]
        ones_half = jnp.ones((HEAD_DIM, sub_len), BF16)

    def store_vt(dst, h, vt_half, e):
        for s in range(n_sub):
            dst[s, h, e * HEAD_DIM:(e + 1) * HEAD_DIM, :] = vt_half[:, s * sub_len:(s + 1) * sub_len]
            dst[s, h, (1 - e) * HEAD_DIM:(2 - e) * HEAD_DIM, :] = ones_half

    a_scale = HEAD_DIM ** -0.5 * LOG2E
    for j in range(A_WIDTH // LANES):
        y = p_scr[:, OFF_AQ + j * LANES:OFF_AQ + (j + 1) * LANES]
        y = rope_a(_pair_rmsnorm(y, qn_ref[...]))
        qa_o[0, :, j * LANES:(j + 1) * LANES] = (y * a_scale).astype(BF16)

    k = rope_a(_pair_rmsnorm(p_scr[:, OFF_AK:OFF_AK + LANES], kn_ref[...]))
    v = p_scr[:, OFF_AV:OFF_AV + LANES]
    if kv_f32:
        kv_sub, kv_len = ka32_o.shape[0], ka32_o.shape[2]
        for u in range(kv_sub):
            ka32_o[u, 0] = k[u * kv_len:(u + 1) * kv_len]
            va32_o[u, 0] = v[u * kv_len:(u + 1) * kv_len]
    swapped = pltpu.roll(k, LANES // 2, 1)
    k3_o[0, :, 0:LANES] = jnp.where(left, k, swapped).astype(BF16)
    k3_o[0, :, LANES:2 * LANES] = k.astype(BF16)
    k3_o[0, :, 2 * LANES:3 * LANES] = jnp.where(left, swapped, k).astype(BF16)
    if values_t:
        vt = v.T.astype(BF16)
        for h in range(A_Q_HEADS):
            g = h // A_GROUP
            store_vt(vta_o, h, vt[g * HEAD_DIM:(g + 1) * HEAD_DIM], h % 2)
    else:
        v_sw = pltpu.roll(v, LANES // 2, 1)
        vta_o[0, :, 0:LANES] = jnp.where(left, v, v_sw).astype(BF16)
        vta_o[0, :, LANES:2 * LANES] = v.astype(BF16)
        vta_o[0, :, 2 * LANES:3 * LANES] = jnp.where(left, v_sw, v).astype(BF16)

    b_scale = B_HALF ** -0.5 * LOG2E
    for j in range(B_WIDTH // LANES):
        sl = slice(j * LANES, (j + 1) * LANES)
        q = rope_b(p_scr[:, OFF_BQ + j * LANES:OFF_BQ + (j + 1) * LANES])
        qb_o[0, :, sl] = (q * b_scale).astype(BF16)
        kb = rope_b(p_scr[:, OFF_BK + j * LANES:OFF_BK + (j + 1) * LANES])
        vb = p_scr[:, OFF_BV + j * LANES:OFF_BV + (j + 1) * LANES]
        kb_o[0, :, sl] = kb.astype(BF16)
        if values_t:
            vbt = vb.T.astype(BF16)
            for e in range(2):
                store_vt(vtb_o, 2 * j + e, vbt[e * HEAD_DIM:(e + 1) * HEAD_DIM], e)
        else:
            vtb_o[0, :, sl] = vb.astype(BF16)
        if kv_f32:
            for u in range(kv_sub):
                kb32_o[u, 0, :, sl] = kb[u * kv_len:(u + 1) * kv_len]
                vb32_o[u, 0, :, sl] = vb[u * kv_len:(u + 1) * kv_len]

    xbc_sub, xbc_len = xbc_o.shape[0], xbc_o.shape[2]
    for s in range(C_CONV_CH // LANES):
        col = OFF_CX + s * LANES if s < C_INNER // LANES else OFF_CB + s * LANES - C_INNER
        for u in range(xbc_sub):
            xbc_o[u, s] = p_scr[u * xbc_len:(u + 1) * xbc_len, col:col + LANES]
    z_o[0] = p_scr[:, OFF_CZ:OFF_CZ + C_INNER]
    dt_o[0] = p_scr[:, OFF_DT:OFF_DT + LANES]


def _proj(x, mod, w_in_p, layer, qn, kn, rope_tabs, *, per_batch_mod, kv_f32, seq_len, values_t,
          kv_prev=None):
    depth = w_in_p.shape[0]
    nb, L, d = x.shape
    tm = min(PROJ_TM, L)
    rope = rope_tabs is not None
    mod_map = (lambda b, i: (b, 0, 0)) if per_batch_mod else (lambda b, i: (0, 0, 0))
    row_spec = lambda w: pl.BlockSpec((1, tm, w), lambda b, i: (b, i, 0))
    in_specs = [
        row_spec(d),
        pl.BlockSpec((1, 6, d), mod_map),
        pl.BlockSpec((None, d, PROJ_PAD), lambda b, i: (layer, 0, 0)),
        pl.BlockSpec((1, LANES), lambda b, i: (0, 0)),
        pl.BlockSpec((1, LANES), lambda b, i: (0, 0)),
    ]
    args = [x, mod, w_in_p, qn, kn]
    if rope:
        in_specs += [pl.BlockSpec((tm, LANES), lambda b, i: (i, 0))] * 6
        args += list(rope_tabs)
    widths = [(A_WIDTH, BF16), (A_WIDTH, BF16), (-A_Q_HEADS if values_t else A_WIDTH, BF16),
              (B_WIDTH, BF16), (B_WIDTH, BF16), (-B_HEADS if values_t else B_WIDTH, BF16),
              ("slabs", F32), (C_INNER, F32), (LANES, F32)]
    kv_widths = [LANES, LANES, B_WIDTH, B_WIDTH] if kv_f32 else []
    widths += [("cache", w) for w in kv_widths]
    aliases = {}
    if kv_prev is not None:
        for k_idx, prev in enumerate(kv_prev):
            aliases[len(args)] = len(widths) - len(kv_widths) + k_idx
            in_specs.append(pl.BlockSpec(memory_space=pl.ANY))
            args.append(prev)
    out_shape, out_specs = [], []
    n_seq = nb * L // seq_len
    in_seq = seq_len >= tm
    assert (L == seq_len) if in_seq else (nb == 1 and tm % seq_len == 0)
    n_slab = C_CONV_CH // LANES
    for w, dt in widths:
        if w == "slabs":
            out_shape.append(jax.ShapeDtypeStruct((n_seq, n_slab, seq_len, LANES), dt))
            out_specs.append(pl.BlockSpec((1, n_slab, tm, LANES), lambda b, i: (b, 0, i, 0)) if in_seq else
                             pl.BlockSpec((tm // seq_len, n_slab, seq_len, LANES), lambda b, i: (i, 0, 0, 0)))
        elif w == "cache":
            assert not in_seq
            out_shape.append(jax.ShapeDtypeStruct((n_seq, depth, seq_len, dt), F32))
            out_specs.append(pl.BlockSpec((tm // seq_len, 1, seq_len, dt), lambda b, i: (i, layer, 0, 0)))
        elif w > 0:
            out_shape.append(jax.ShapeDtypeStruct((nb, L, w), dt))
            out_specs.append(row_spec(w))
        else:
            out_shape.append(jax.ShapeDtypeStruct((n_seq, -w, LANES, seq_len), dt))
            out_specs.append(pl.BlockSpec((1, -w, LANES, tm), lambda b, i: (b, 0, 0, i)) if in_seq else
                             pl.BlockSpec((tm // seq_len, -w, LANES, seq_len), lambda b, i: (i, 0, 0, 0)))
    return pl.pallas_call(
        functools.partial(_proj_kernel, rope=rope, kv_f32=kv_f32, n_alias=len(aliases)),
        out_shape=out_shape,
        grid=(nb, L // tm),
        in_specs=in_specs,
        out_specs=out_specs,
        input_output_aliases=aliases,
        scratch_shapes=[pltpu.VMEM((tm, PROJ_PAD), F32)],
        compiler_params=_cparams(2),
        name="proj_rope" if rope else "proj",
    )(*args)


class _ScoreMap:
    def __init__(self, tag, qm, chunks, s_ref):
        self.tag, self.qm, self.chunks, self.s_ref = tag, qm, chunks, s_ref
        self.offsets = [sum(w for _, _, w in chunks[:c]) for c in range(len(chunks))]
        self.m_part = self.m = self.acc = None

    def pass1(self, c):
        k_fn, _, w = self.chunks[c]
        tq = self.qm.shape[0]
        s = lax.dot_general(k_fn(), self.qm, (((1,), (1,)), ((), ())), preferred_element_type=F32)
        self.s_ref[self.offsets[c]:self.offsets[c] + w, :] = s
        part = jnp.max(s.reshape(w // 8, 8, tq), axis=0)
        self.m_part = part if self.m_part is None else jnp.maximum(self.m_part, part)

    def finish_max(self):
        self.m = jnp.max(self.m_part, axis=0, keepdims=True)

    def pass2(self, c):
        _, vt_fn, w = self.chunks[c]
        e = jnp.exp2(self.s_ref[self.offsets[c]:self.offsets[c] + w, :] - self.m).astype(BF16)
        pv = jnp.dot(vt_fn(), e, preferred_element_type=F32)
        self.acc = pv if self.acc is None else self.acc + pv


def _run_score_maps(maps, on_done):
    for mp in maps[:ATTN_AHEAD]:
        for c in range(len(mp.chunks)):
            mp.pass1(c)
        mp.finish_max()
    for i, mp in enumerate(maps):
        nxt = maps[i + ATTN_AHEAD] if i + ATTN_AHEAD < len(maps) else None
        n_next = len(nxt.chunks) if nxt is not None else 0
        for c in range(max(len(mp.chunks), n_next)):
            if c < len(mp.chunks):
                mp.pass2(c)
            if c < n_next:
                nxt.pass1(c)
        if nxt is not None:
            nxt.finish_max()
        on_done(mp)


def _normalised_half(acc, e):
    l = acc[(1 - e) * HEAD_DIM:(1 - e) * HEAD_DIM + 1, :]
    return acc[e * HEAD_DIM:(e + 1) * HEAD_DIM, :] * (1.0 / l)


def _attn_kernel(*refs, has_cache, lam_init, seq_len):
    qa_ref, k3_ref, vta_ref, qb_ref, kb_ref, vtb_ref, lamv_ref, subln_ref = refs[:8]
    pos = 8
    if has_cache:
        cka_ref, cva_ref, ckb_ref, cvb_ref = refs[pos:pos + 4]
        pos += 4
    ao_ref, bo_ref, s_scr = refs[pos:pos + 3]

    left = _left_mask()
    lane = lax.broadcasted_iota(jnp.int32, (1, LANES), 1)
    kc = min(ATTN_KC, seq_len)
    n_kc = seq_len // kc
    maps = []

    def add_map(tag, qm, chunks):
        maps.append(_ScoreMap(tag, qm, chunks, s_scr.at[len(maps) % (ATTN_AHEAD + 1)]))

    def new_chunks(k_ref, vt_ref, j, h):
        sl = slice(j * LANES, (j + 1) * LANES)
        out = []
        for c in range(n_kc):
            rows = slice(c * kc, (c + 1) * kc)
            out.append((functools.partial(lambda r, s: k_ref[0, r, s], rows, sl),
                        functools.partial(lambda r, hh: vt_ref[0, hh, :, r], rows, h), kc))
        return out

    def value_block(vt_half, e):
        ones = jnp.ones_like(vt_half)
        return jnp.concatenate([vt_half, ones] if e == 0 else [ones, vt_half], axis=0).astype(BF16)

    if has_cache:
        ck = cka_ref[0, 0]
        ck_sw = pltpu.roll(ck, LANES // 2, 1)
        ck3 = [jnp.where(left, ck, ck_sw), ck, jnp.where(left, ck_sw, ck)]
        cvt = cva_ref[0, 0].T
        past = ck.shape[0]
    for j in range(A_WIDTH // LANES):
        q = qa_ref[0, :, j * LANES:(j + 1) * LANES]
        for e in range(2):
            h = 2 * j + e
            g = h // A_GROUP
            chunks = new_chunks(k3_ref, vta_ref, j, h)
            if has_cache:
                ckj = ck3[j].astype(BF16)
                cvj = value_block(cvt[g * HEAD_DIM:(g + 1) * HEAD_DIM], e)
                chunks.append((lambda a=ckj: a, lambda a=cvj: a, past))
            qm = jnp.where(left if e == 0 else jnp.logical_not(left), q, jnp.zeros_like(q))
            add_map(("a", j, e, 0), qm, chunks)

    lv = lamv_ref[...]
    lam = (jnp.exp(jnp.sum(lv[0:1] * lv[1:2], axis=-1, keepdims=True))
           - jnp.exp(jnp.sum(lv[2:3] * lv[3:4], axis=-1, keepdims=True)) + lam_init)
    for j in range(B_WIDTH // LANES):
        sl = slice(j * LANES, (j + 1) * LANES)
        q = qb_ref[0, :, sl]
        if has_cache:
            ckj = ckb_ref[0, 0, :, sl].astype(BF16)
            cvt_j = cvb_ref[0, 0, :, sl].T
        for e in range(2):
            h = 2 * j + e
            chunks = new_chunks(kb_ref, vtb_ref, j, h)
            if has_cache:
                cvj = value_block(cvt_j[e * HEAD_DIM:(e + 1) * HEAD_DIM], e)
                chunks.append((lambda a=ckj: a, lambda a=cvj: a, ckj.shape[0]))
            for mi in range(2):
                sel = (lane // B_HALF) == (2 * e + mi)
                add_map(("b", j, e, mi), jnp.where(sel, q, jnp.zeros_like(q)), chunks)

    done = {}

    def on_done(mp):
        kind, j, e, mi = mp.tag
        done[mp.tag] = _normalised_half(mp.acc, e)
        if kind == "a" and e == 1:
            o_t = jnp.concatenate([done[("a", j, 0, 0)], done[("a", j, 1, 0)]], axis=0)
            ao_ref[0, :, j * LANES:(j + 1) * LANES] = o_t.T.astype(BF16)
        if kind == "b" and e == 1 and mi == 1:
            halves = []
            for ee in range(2):
                o = done[("b", j, ee, 0)] - lam * done[("b", j, ee, 1)]
                ms = jnp.mean(o * o, axis=0, keepdims=True)
                halves.append(o * lax.rsqrt(ms + EPS))
            o_t = jnp.concatenate(halves, axis=0)
            bo_ref[0, :, j * LANES:(j + 1) * LANES] = (
                o_t.T * subln_ref[...] * (1.0 - lam_init)).astype(BF16)

    _run_score_maps(maps, on_done)


def _softmax_pv_rows(qm, k, v):
    s = lax.dot_general(qm, k, (((1,), (1,)), ((), ())), preferred_element_type=F32)
    m_part = s[:, 0:LANES]
    for t in range(1, s.shape[1] // LANES):
        m_part = jnp.maximum(m_part, s[:, t * LANES:(t + 1) * LANES])
    e = jnp.exp2(s - jnp.max(m_part, axis=-1, keepdims=True))
    l_part = e[:, 0:LANES]
    for t in range(1, s.shape[1] // LANES):
        l_part = l_part + e[:, t * LANES:(t + 1) * LANES]
    acc = jnp.dot(e.astype(BF16), v, preferred_element_type=F32)
    return acc, jnp.sum(l_part, axis=-1, keepdims=True)


def _attn_rows_kernel(qa_ref, k3_ref, v3_ref, qb_ref, kb_ref, vb_ref, lamv_ref, subln_ref,
                      ao_ref, bo_ref, *, lam_init):
    left = _left_mask()
    lane = lax.broadcasted_iota(jnp.int32, (1, LANES), 1)
    for j in range(A_WIDTH // LANES):
        sl = slice(j * LANES, (j + 1) * LANES)
        q = qa_ref[0, :, sl]
        halves = []
        for e in range(2):
            qm = jnp.where(left if e == 0 else jnp.logical_not(left), q, jnp.zeros_like(q))
            acc, l = _softmax_pv_rows(qm, k3_ref[0, :, sl], v3_ref[0, :, sl])
            halves.append(acc * (1.0 / l))
        ao_ref[0, :, sl] = jnp.where(left, halves[0], halves[1]).astype(BF16)

    lv = lamv_ref[...]
    lam = (jnp.exp(jnp.sum(lv[0:1] * lv[1:2], axis=-1, keepdims=True))
           - jnp.exp(jnp.sum(lv[2:3] * lv[3:4], axis=-1, keepdims=True)) + lam_init)
    for j in range(B_WIDTH // LANES):
        sl = slice(j * LANES, (j + 1) * LANES)
        q = qb_ref[0, :, sl]
        halves = []
        for e in range(2):
            maps = []
            for mi in range(2):
                sel = (lane // B_HALF) == (2 * e + mi)
                acc, l = _softmax_pv_rows(jnp.where(sel, q, jnp.zeros_like(q)), kb_ref[0, :, sl], vb_ref[0, :, sl])
                maps.append(acc * (1.0 / l))
            halves.append(maps[0] - lam * maps[1])
        o = jnp.where(left, halves[0], halves[1])
        bo_ref[0, :, sl] = (_pair_rmsnorm(o, subln_ref[...]) * (1.0 - lam_init)).astype(BF16)


def _attn_rows(qa, k3, v3, qb, kb, vb, lamv, subln, *, lam_init):
    nb, L, _ = qa.shape
    spec = lambda w: pl.BlockSpec((1, L, w), lambda b: (b, 0, 0))
    return pl.pallas_call(
        functools.partial(_attn_rows_kernel, lam_init=lam_init),
        out_shape=[jax.ShapeDtypeStruct((nb, L, A_WIDTH), BF16),
                   jax.ShapeDtypeStruct((nb, L, B_WIDTH), BF16)],
        grid=(nb,),
        in_specs=[spec(A_WIDTH), spec(A_WIDTH), spec(A_WIDTH), spec(B_WIDTH), spec(B_WIDTH), spec(B_WIDTH),
                  pl.BlockSpec((4, B_HALF), lambda b: (0, 0)),
                  pl.BlockSpec((1, LANES), lambda b: (0, 0))],
        out_specs=[spec(A_WIDTH), spec(B_WIDTH)],
        compiler_params=_cparams(1),
        name="attn",
    )(qa, k3, v3, qb, kb, vb, lamv, subln)


def _attn(qa, k3, vta, qb, kb, vtb, lamv, subln, caches, layer, *, lam_init):
    nb, L, _ = qa.shape
    tq = min(ATTN_TQ, L)
    has_cache = caches is not None
    q_spec = lambda w: pl.BlockSpec((1, tq, w), lambda b, i: (b, i, 0))
    kv_spec = lambda w: pl.BlockSpec((1, L, w), lambda b, i: (b, 0, 0))
    vt_spec = lambda n: pl.BlockSpec((1, n, LANES, L), lambda b, i: (b, 0, 0, 0))
    in_specs = [q_spec(A_WIDTH), kv_spec(A_WIDTH), vt_spec(A_Q_HEADS),
                q_spec(B_WIDTH), kv_spec(B_WIDTH), vt_spec(B_HEADS),
                pl.BlockSpec((4, B_HALF), lambda b, i: (0, 0)),
                pl.BlockSpec((1, LANES), lambda b, i: (0, 0))]
    args = [qa, k3, vta, qb, kb, vtb, lamv, subln]
    lk = L
    if has_cache:
        past = caches[0].shape[2]
        lk += past
        for c in caches:
            in_specs.append(pl.BlockSpec((1, 1, past, c.shape[-1]), lambda b, i: (b, layer, 0, 0)))
            args.append(c)
    return pl.pallas_call(
        functools.partial(_attn_kernel, has_cache=has_cache, lam_init=lam_init, seq_len=L),
        out_shape=[jax.ShapeDtypeStruct((nb, L, A_WIDTH), BF16),
                   jax.ShapeDtypeStruct((nb, L, B_WIDTH), BF16)],
        grid=(nb, L // tq),
        in_specs=in_specs,
        out_specs=[q_spec(A_WIDTH), q_spec(B_WIDTH)],
        scratch_shapes=[pltpu.VMEM((ATTN_AHEAD + 1, lk, tq), F32)],
        compiler_params=_cparams(2),
        name="attn_cache" if has_cache else "attn",
    )(*args)


def _prefix_sums(tri, a):
    hi = a.astype(BF16)
    lo = (a - hi.astype(F32)).astype(BF16)
    return (jnp.dot(tri, hi, preferred_element_type=F32)
            + jnp.dot(tri, lo, preferred_element_type=F32))


def _pair_cols(x, c0):
    return jnp.where(_left_mask(), x[:, c0:c0 + 1], x[:, c0 + 1:c0 + 2])


def _ssd_kernel(*refs, has_init, emit_state, seq_len, n_alias):
    xbc_ref, z_ref, dt_ref, cw_ref, cb_ref, dtb_ref, alog_ref, dexp_ref, nw_ref = refs[:9]
    pos = 9
    if has_init:
        init_ref = refs[pos]
        pos += 1
    pos += n_alias
    co_ref = refs[pos]
    pos += 1
    if emit_state:
        st_ref = refs[pos]
        pos += 1
    xc_scr, y_scr, st_scr = refs[pos:pos + 3]

    L = seq_len
    rb = min(ROW_BLOCK, L)
    q = SSD_CHUNK
    nc = L // q
    left = _left_mask()

    w0 = cw_ref[0:1, :]
    w1 = cw_ref[1:2, :]
    w2 = cw_ref[2:3, :]
    cb = CONV_ROWS
    half = cb // 2
    hrow = lax.broadcasted_iota(jnp.int32, (half, 1), 0)
    for s in range(C_CONV_CH // LANES):
        ls = slice(s * LANES, (s + 1) * LANES)
        w0s, w1s, w2s, bs = w0[:, ls], w1[:, ls], w2[:, ls], cb_ref[:, ls]
        for r0 in range(0, L, cb):
            even = xbc_ref[0, s, pl.ds(r0, half, stride=2), :]
            odd = xbc_ref[0, s, pl.ds(r0 + 1, half, stride=2), :]
            if r0 > 0:
                odd_before = xbc_ref[0, s, pl.ds(r0 - 1, half, stride=2), :]
            else:
                odd_before = jnp.where(hrow == 0, 0.0, pltpu.roll(odd, 1, 0))
            if r0 + cb < L:
                even_after = xbc_ref[0, s, pl.ds(r0 + 2, half, stride=2), :]
            else:
                even_after = jnp.where(hrow == half - 1, 0.0, pltpu.roll(even, half - 1, 0))
            y_even = odd_before * w0s + even * w1s + odd * w2s + bs
            y_odd = even * w0s + odd * w1s + even_after * w2s + bs
            xc_scr[s, pl.ds(r0, half, stride=2), :] = y_even * _sigmoid(y_even)
            xc_scr[s, pl.ds(r0 + 1, half, stride=2), :] = y_odd * _sigmoid(y_odd)
    for j in range(C_INNER // LANES):
        for r0 in range(0, L, rb):
            y_scr[r0:r0 + rb, j * LANES:(j + 1) * LANES] = (
                xc_scr[j, r0:r0 + rb, :] * dexp_ref[:, j * LANES:(j + 1) * LANES])

    for d in range(2):
        if has_init:
            st_scr[d] = init_ref[0, 0, d].reshape(C_INNER, C_STATE).T
        else:
            st_scr[d] = jnp.zeros((C_STATE, C_INNER), F32)

    a_neg = -jnp.exp(alog_ref[...])
    ti = lax.broadcasted_iota(jnp.int32, (q, q), 0)
    si = lax.broadcasted_iota(jnp.int32, (q, q), 1)
    tri_incl = (si <= ti).astype(BF16)
    src_lane = lax.broadcasted_iota(jnp.int32, (LANES, C_INNER), 0)
    dst_head = lax.broadcasted_iota(jnp.int32, (LANES, C_INNER), 1) // C_HEADDIM
    spread = [(src_lane == d * C_HEADS + dst_head).astype(BF16) for d in range(2)]

    def process(d, r0):
        rows = pl.ds(r0, q)
        lo = d * C_HEADS
        causal = (si <= ti) if d == 0 else (si >= ti)
        n_x = C_INNER // LANES
        bm = [xc_scr[n_x + g, rows, :] for g in range(C_GROUPS)]
        cm = [xc_scr[n_x + C_GROUPS + g, rows, :] for g in range(C_GROUPS)]
        bm_b = [b.astype(BF16) for b in bm]
        cm_b = [c.astype(BF16) for c in cm]
        bmt_b = [b.T.astype(BF16) for b in bm]
        gmat = [lax.dot_general(cm_b[g], bm_b[g], (((1,), (1,)), ((), ())), preferred_element_type=F32)
                for g in range(C_GROUPS)]

        dtr = dt_ref[0, rows, :] + dtb_ref[...]
        dt = jnp.maximum(dtr, 0.0) + jnp.log1p(jnp.exp(-jnp.abs(dtr)))
        a = dt * a_neg
        cs = _prefix_sums(tri_incl, a)
        tot = cs[q - 1:q, :]
        ev = cs if d == 0 else (tot - cs + a)
        e_in = jnp.exp(ev)
        e_out = jnp.exp(tot - ev)
        dec = jnp.exp(tot)
        ev_t = ev.T
        per_head = jnp.concatenate([dt, e_in, e_out], axis=0).astype(BF16)
        wide = jnp.dot(per_head, spread[d], preferred_element_type=F32)
        dt_w, e_in_w, e_out_w = wide[0:q], wide[q:2 * q], wide[2 * q:3 * q]

        st = st_scr.at[d]
        for j in range(C_INNER // LANES):
            sl = slice(j * LANES, (j + 1) * LANES)
            h0 = 2 * j
            g0, g1 = h0 // (C_HEADS // C_GROUPS), (h0 + 1) // (C_HEADS // C_GROUPS)
            xdt = xc_scr[j, rows, :] * dt_w[:, sl]
            xdt_b = xdt.astype(BF16)
            st_blk = st[:, sl]
            st_b = st_blk.astype(BF16)
            yd = []
            for e in range(2):
                hh = lo + h0 + e
                g = g0 if e == 0 else g1
                diff = ev[:, hh:hh + 1] - ev_t[hh:hh + 1, :]
                lmat = jnp.exp(jnp.where(causal, diff, NEG_BIG))
                yd.append(jnp.dot((gmat[g] * lmat).astype(BF16), xdt_b, preferred_element_type=F32))
            yo0 = jnp.dot(cm_b[g0], st_b, preferred_element_type=F32)
            yo1 = yo0 if g1 == g0 else jnp.dot(cm_b[g1], st_b, preferred_element_type=F32)
            y_blk = (jnp.where(left, yd[0], yd[1])
                     + jnp.where(left, yo0, yo1) * e_in_w[:, sl])
            y_scr[rows, sl] = y_scr[rows, sl] + y_blk
            xw = (xdt * e_out_w[:, sl]).astype(BF16)
            up0 = jnp.dot(bmt_b[g0], xw, preferred_element_type=F32)
            up1 = up0 if g1 == g0 else jnp.dot(bmt_b[g1], xw, preferred_element_type=F32)
            st[:, sl] = st_blk * _pair_cols(dec, lo + h0) + jnp.where(left, up0, up1)

    if nc <= 2:
        for c in range(nc):
            process(0, c * q)
            process(1, (nc - 1 - c) * q)
    else:
        def body(c, carry):
            process(0, pl.multiple_of(c * q, q))
            process(1, pl.multiple_of((nc - 1 - c) * q, q))
            return carry
        lax.fori_loop(0, nc, body, 0, unroll=SSD_UNROLL if nc % SSD_UNROLL == 0 else 1)

    if emit_state:
        for d in range(2):
            st_ref[0, 0, d] = st_scr[d].T.reshape(C_HEADS, C_HEADDIM, C_STATE)

    for r0 in range(0, L, rb):
        zz = z_ref[0, r0:r0 + rb, :]
        yg = y_scr[r0:r0 + rb, :] * (zz * _sigmoid(zz))
        ms = jnp.mean(yg * yg, axis=-1, keepdims=True)
        co_ref[0, r0:r0 + rb, :] = (yg * lax.rsqrt(ms + EPS) * nw_ref[...]).astype(BF16)


def _ssd(xbc, z, dt, conv_w, conv_b, dtb, alog, dexp, norm_w, state, layer, *, emit_state, depth=1,
         st_prev=None):
    nb, n_slab, L, _ = xbc.shape
    has_init = state is not None
    seq_spec = lambda w: pl.BlockSpec((1, L, w), lambda b: (b, 0, 0))
    par_spec = lambda r, w: pl.BlockSpec((r, w), lambda b: (0, 0))
    in_specs = [pl.BlockSpec((1, n_slab, L, LANES), lambda b: (b, 0, 0, 0)), seq_spec(C_INNER), seq_spec(LANES),
                par_spec(3, C_CONV_CH), par_spec(1, C_CONV_CH), par_spec(1, LANES), par_spec(1, LANES),
                par_spec(1, C_INNER), par_spec(1, C_INNER)]
    args = [xbc, z, dt, conv_w, conv_b, dtb, alog, dexp, norm_w]
    if has_init:
        in_specs.append(pl.BlockSpec((1, 1, 2, C_HEADS, C_HEADDIM, C_STATE),
                                     lambda b: (b, layer, 0, 0, 0, 0)))
        args.append(state)
    out_shape = [jax.ShapeDtypeStruct((nb, L, C_INNER), BF16)]
    out_specs = [seq_spec(C_INNER)]
    aliases = {}
    if emit_state:
        out_shape.append(jax.ShapeDtypeStruct((nb, depth, 2, C_HEADS, C_HEADDIM, C_STATE), F32))
        out_specs.append(pl.BlockSpec((1, 1, 2, C_HEADS, C_HEADDIM, C_STATE), lambda b: (b, layer, 0, 0, 0, 0)))
        if st_prev is not None:
            aliases[len(args)] = 1
            in_specs.append(pl.BlockSpec(memory_space=pl.ANY))
            args.append(st_prev)
    return pl.pallas_call(
        functools.partial(_ssd_kernel, has_init=has_init, emit_state=emit_state, seq_len=L,
                          n_alias=len(aliases)),
        input_output_aliases=aliases,
        out_shape=out_shape,
        grid=(nb,),
        in_specs=in_specs,
        out_specs=out_specs,
        scratch_shapes=[pltpu.VMEM((n_slab, L, LANES), F32), pltpu.VMEM((L, C_INNER), F32),
                        pltpu.VMEM((2, C_STATE, C_INNER), F32)],
        compiler_params=_cparams(1),
        name="ssd_init" if has_init else "ssd",
    )(*args)


def _outp_kernel(a_ref, b_ref, c_ref, x_ref, mod_ref, w_ref, x1_o, h2_o):
    mix = jnp.concatenate([a_ref[0], b_ref[0], c_ref[0]], axis=-1)
    o = jnp.dot(mix, w_ref[...], preferred_element_type=F32)
    gate1 = mod_ref[0, 2:3, :]
    shift2 = mod_ref[0, 3:4, :]
    scale2 = mod_ref[0, 4:5, :]
    x1 = x_ref[0] + gate1 * o
    ms = jnp.mean(x1 * x1, axis=-1, keepdims=True)
    x1_o[0] = x1
    h2_o[0] = ((x1 * lax.rsqrt(ms + EPS)) * (1.0 + scale2) + shift2).astype(BF16)


def _outp(a_out, b_out, c_out, x, mod, w_out_b, layer, *, per_batch_mod):
    nb, L, d = x.shape
    tm = min(PROJ_TM, L)
    mod_map = (lambda b, i: (b, 0, 0)) if per_batch_mod else (lambda b, i: (0, 0, 0))
    row_spec = lambda w: pl.BlockSpec((1, tm, w), lambda b, i: (b, i, 0))
    return pl.pallas_call(
        _outp_kernel,
        out_shape=[jax.ShapeDtypeStruct((nb, L, d), F32), jax.ShapeDtypeStruct((nb, L, d), BF16)],
        grid=(nb, L // tm),
        in_specs=[row_spec(A_WIDTH), row_spec(B_WIDTH), row_spec(C_INNER), row_spec(d),
                  pl.BlockSpec((1, 6, d), mod_map),
                  pl.BlockSpec((None, d, d), lambda b, i: (layer, 0, 0))],
        out_specs=[row_spec(d), row_spec(d)],
        compiler_params=_cparams(2),
        name="outp",
    )(a_out, b_out, c_out, x, mod, w_out_b)


HALO = 16


def _ffn_kernel(h_ref, hp_ref, hn_ref, x1_ref, mod_ref, wu_ref, cw_ref, cb_ref, wd_ref, fw_ref,
                o_ref, hext_scr, u_scr, acc_scr, *, seq_len, final_norm):
    tm = h_ref.shape[1]
    i = pl.program_id(1)
    n_chunks = wu_ref.shape[0]
    sub = min(FFN_SUB, tm)
    n_sub = tm // sub
    rb = min(FFN_RB, sub)
    n_items = n_sub * n_chunks

    has_prev = ((i * tm) & (seq_len - 1)) != 0
    has_next = ((i * tm + tm) & (seq_len - 1)) != 0
    hext_scr[0:HALO, :] = jnp.where(has_prev, hp_ref[0], jnp.zeros_like(hp_ref[0]))
    hext_scr[HALO:HALO + tm, :] = h_ref[0]
    hext_scr[HALO + tm:HALO + tm + HALO, :] = jnp.where(has_next, hn_ref[0], jnp.zeros_like(hn_ref[0]))
    acc_scr[...] = jnp.zeros_like(acc_scr)
    row8 = lax.broadcasted_iota(jnp.int32, (8, 1), 0)

    n_uslab = 2 * FFN_FC // LANES
    n_oslab = acc_scr.shape[0]
    half = rb // 2

    def split(n):
        if isinstance(n, int):
            return (n // n_chunks) * sub, n % n_chunks
        if n_sub == 1:
            return 0, n
        t = sum((n >= k * n_chunks).astype(jnp.int32) for k in range(1, n_sub))
        return pl.multiple_of(t * sub, sub), n - t * n_chunks

    def up(n, slot):
        row0, c = split(n)
        h_rows = hext_scr[pl.ds(row0, sub + 2 * HALO), :]
        u = jnp.dot(h_rows, wu_ref[c], preferred_element_type=F32)
        for s in range(n_uslab):
            u_scr[slot, s] = u[:, s * LANES:(s + 1) * LANES]

    def conv_down(n, slot):
        row0, c = split(n)
        cw = cw_ref[c]
        bias = cb_ref[c]
        for r0 in range(0, sub, rb):
            base = HALO + r0
            ys = []
            for s in range(n_uslab):
                ls = slice(s * LANES, (s + 1) * LANES)
                even = u_scr[slot, s, pl.ds(base, half, stride=2), :]
                odd = u_scr[slot, s, pl.ds(base + 1, half, stride=2), :]
                odd_before = u_scr[slot, s, pl.ds(base - 1, half, stride=2), :]
                even_after = u_scr[slot, s, pl.ds(base + 2, half, stride=2), :]
                if r0 % seq_len == 0 and (r0 > 0 or seq_len <= sub):
                    odd_before = jnp.concatenate(
                        [jnp.where(row8 == 0, 0.0, odd_before[0:8]), odd_before[8:]], axis=0)
                if (r0 + rb) % seq_len == 0 and (r0 + rb < sub or seq_len <= sub):
                    even_after = jnp.concatenate(
                        [even_after[:half - 8], jnp.where(row8 == 7, 0.0, even_after[half - 8:])], axis=0)
                w0, w1, w2, b = cw[0:1, ls], cw[1:2, ls], cw[2:3, ls], bias[:, ls]
                y_even = odd_before * w0 + even * w1 + odd * w2 + b
                y_odd = even * w0 + odd * w1 + even_after * w2 + b
                ys.append(jnp.concatenate([y_even, y_odd], axis=0))
            n_g = n_uslab // 2
            act = jnp.concatenate([ys[s] * _sigmoid(ys[s]) * ys[n_g + s] for s in range(n_g)], axis=1)
            dn = jnp.dot(act.astype(BF16), wd_ref[c], preferred_element_type=F32)
            for s in range(n_oslab):
                ls = slice(s * LANES, (s + 1) * LANES)
                acc_scr[s, pl.ds(row0 + r0, half, stride=2), :] += dn[0:half, ls]
                acc_scr[s, pl.ds(row0 + r0 + 1, half, stride=2), :] += dn[half:rb, ls]

    up(0, 0)

    def pair(k, carry):
        n = 2 * k
        up(n + 1, 1)
        conv_down(n, 0)
        up(n + 2, 0)
        conv_down(n + 1, 1)
        return carry

    lax.fori_loop(0, (n_items - 1) // 2, pair, 0)
    if n_items % 2 == 1:
        conv_down(n_items - 1, 0)
    else:
        up(n_items - 1, 1)
        conv_down(n_items - 2, 0)
        conv_down(n_items - 1, 1)

    gate2 = mod_ref[0, 5:6, :]
    ffn_out = jnp.concatenate([acc_scr[s] for s in range(n_oslab)], axis=1)
    out = x1_ref[0] + gate2 * ffn_out
    if final_norm:
        ms = jnp.mean(out * out, axis=-1, keepdims=True)
        out = out * lax.rsqrt(ms + EPS) * fw_ref[...]
    o_ref[0] = out


def _ffn(h2, x1, mod, wu_c, cw_c, cb_c, wd_c, layer, final_w, *, per_batch_mod, seq_len, final_norm):
    nb, L, d = x1.shape
    tm = min(FFN_TM, L)
    n_tiles = L // tm
    sub = min(FFN_SUB, tm)
    rb = min(FFN_RB, sub)
    assert seq_len & (seq_len - 1) == 0 and L % seq_len == 0 and tm % sub == 0
    assert seq_len % tm == 0 or (tm % seq_len == 0 and sub % seq_len == 0 and seq_len % rb == 0)
    n_chunks = wu_c.shape[1]
    hb = tm // HALO
    n_hblk = L // HALO
    mod_map = (lambda b, i: (b, 0, 0)) if per_batch_mod else (lambda b, i: (0, 0, 0))
    row_spec = lambda w: pl.BlockSpec((1, tm, w), lambda b, i: (b, i, 0))
    const3 = lambda s: pl.BlockSpec(s, lambda b, i: (0, 0, 0))
    return pl.pallas_call(
        functools.partial(_ffn_kernel, seq_len=seq_len, final_norm=final_norm),
        out_shape=jax.ShapeDtypeStruct((nb, L, d), F32),
        grid=(nb, n_tiles),
        in_specs=[row_spec(d),
                  pl.BlockSpec((1, HALO, d), lambda b, i: (b, jnp.maximum(i * hb - 1, 0), 0)),
                  pl.BlockSpec((1, HALO, d), lambda b, i: (b, jnp.minimum((i + 1) * hb, n_hblk - 1), 0)),
                  row_spec(d),
                  pl.BlockSpec((1, 6, d), mod_map),
                  pl.BlockSpec((None, n_chunks, d, 2 * FFN_FC), lambda b, i: (layer, 0, 0, 0)),
                  const3((n_chunks, 3, 2 * FFN_FC)),
                  const3((n_chunks, 1, 2 * FFN_FC)),
                  pl.BlockSpec((None, n_chunks, FFN_FC, d), lambda b, i: (layer, 0, 0, 0)),
                  pl.BlockSpec((1, d), lambda b, i: (0, 0))],
        out_specs=row_spec(d),
        scratch_shapes=[pltpu.VMEM((tm + 2 * HALO, d), BF16),
                        pltpu.VMEM((2, 2 * FFN_FC // LANES, sub + 2 * HALO, LANES), F32),
                        pltpu.VMEM((d // LANES, tm, LANES), F32)],
        compiler_params=_cparams(2),
        name="ffn_final" if final_norm else "ffn",
    )(h2, h2, h2, x1, mod, wu_c, cw_c, cb_c, wd_c, final_w)


def _rope_tables(L, d):
    rows = L // GRID_W
    row = jnp.repeat(jnp.arange(rows), GRID_W).astype(F32)
    col = jnp.tile(jnp.arange(GRID_W), rows).astype(F32)
    quarter = d // 4
    inv = ROPE_THETA ** (-jnp.arange(quarter, dtype=F32) / quarter)
    ang_r = row[:, None] * inv[None, :]
    ang_c = col[:, None] * inv[None, :]
    ang = jnp.concatenate([ang_r, ang_r, ang_c, ang_c], axis=-1)
    cos, sin = jnp.cos(ang), jnp.sin(ang)
    even = ((jnp.arange(d) // quarter) % 2 == 0)[None, :]
    s_up = jnp.where(even, -sin, 0.0)
    s_dn = jnp.where(even, 0.0, sin)
    reps = LANES // d
    return tuple(jnp.tile(t, (1, reps)) for t in (cos, s_up, s_dn))


def _pad_lanes(v, width=LANES):
    v = v.reshape(1, -1).astype(F32)
    return jnp.pad(v, ((0, 0), (0, width - v.shape[1])))


def _layer_params(l, a_q_norm, a_k_norm, b_lambda, b_subln, ssm_conv_w, ssm_conv_b, ssm_A_log,
                  ssm_dt_bias, ssm_D, ssm_norm_w, ffn_conv_w, ffn_conv_b):
    n_chunks = D_FF // FFN_FC

    def gv_chunks(t):
        lead = t.shape[:-1]
        g = t[..., :D_FF].reshape(lead + (n_chunks, FFN_FC))
        v = t[..., D_FF:].reshape(lead + (n_chunks, FFN_FC))
        gv = jnp.concatenate([g, v], axis=-1)
        return jnp.moveaxis(gv, -2, 0)

    return dict(
        qn=jnp.tile(a_q_norm[l], 2).reshape(1, LANES),
        kn=jnp.tile(a_k_norm[l], 2).reshape(1, LANES),
        lamv=b_lambda[l],
        subln=jnp.tile(b_subln[l], 2).reshape(1, LANES),
        conv_w=ssm_conv_w[l],
        conv_b=ssm_conv_b[l].reshape(1, C_CONV_CH),
        dtb=_pad_lanes(ssm_dt_bias[l]),
        alog=_pad_lanes(ssm_A_log[l]),
        dexp=jnp.repeat(ssm_D[l], C_HEADDIM).reshape(1, C_INNER),
        norm_w=ssm_norm_w[l].reshape(1, C_INNER),
        cw=gv_chunks(ffn_conv_w[l]),
        cb=gv_chunks(ffn_conv_b[l].reshape(1, 2 * D_FF)),
    )


def _block(x, mod, p, wts, rope_tabs, caches, state, layer, *, seq_shape, per_batch_mod, is_ctx, lam_init,
           final_w, final_norm, new_caches=None):
    n_seq, seq_len = seq_shape
    depth = wts["w_in"].shape[0]
    outs = _proj(x, mod, wts["w_in"], layer, p["qn"], p["kn"], rope_tabs,
                 per_batch_mod=per_batch_mod, kv_f32=is_ctx, seq_len=seq_len, values_t=caches is not None,
                 kv_prev=None if new_caches is None else new_caches[:4])
    seq = lambda t: t.reshape(n_seq, seq_len, t.shape[-1]) if t.ndim == 3 else t
    qa, k3, va, qb, kb, vb, xbc, z, dt = [seq(t) for t in outs[:9]]
    if caches is None:
        a_out, b_out = _attn_rows(qa, k3, va, qb, kb, vb, p["lamv"], p["subln"], lam_init=lam_init)
    else:
        a_out, b_out = _attn(qa, k3, va, qb, kb, vb, p["lamv"], p["subln"], caches, layer, lam_init=lam_init)
    ssd_out = _ssd(xbc, z, dt, p["conv_w"], p["conv_b"], p["dtb"], p["alog"], p["dexp"], p["norm_w"],
                   state, layer, emit_state=is_ctx, depth=depth,
                   st_prev=None if new_caches is None else new_caches[4])
    c_out = ssd_out[0]
    flat = lambda t: t.reshape(x.shape[0], x.shape[1], t.shape[-1])
    x1, h2 = _outp(flat(a_out), flat(b_out), flat(c_out), x, mod, wts["w_out"], layer,
                   per_batch_mod=per_batch_mod)
    x2 = _ffn(h2, x1, mod, wts["wu"], p["cw"], p["cb"], wts["wd"], layer, final_w,
              per_batch_mod=per_batch_mod, seq_len=seq_len, final_norm=final_norm)
    extras = None
    if is_ctx:
        extras = tuple(outs[9:13]) + (ssd_out[1],)
    return x2, extras


def kernel(x_prompt, x_sample, cache_a_k, cache_a_v, cache_b_k, cache_b_v, state_ssm, c, c_ctx, ada_w, ada_b, w_in, a_q_norm, a_k_norm, b_lambda, b_subln, ssm_conv_w, ssm_conv_b, ssm_A_log, ssm_dt_bias, ssm_D, ssm_norm_w, w_out, ffn_up, ffn_conv_w, ffn_conv_b, ffn_down, final_norm_w):
    depth = w_in.shape[0]
    nbp, lp, d = x_prompt.shape
    nbs, ls, _ = x_sample.shape
    past = cache_a_k.shape[2]

    mod_rows = 8
    assert 1 + nbs <= mod_rows
    cvecs = jnp.concatenate([c_ctx[None, :], c, jnp.zeros((mod_rows - 1 - nbs, d), F32)], axis=0)
    mod = _modulation(cvecs, ada_w, ada_b)

    rope_tabs = _rope_tables(ls, HEAD_DIM) + _rope_tables(ls, B_HALF)
    caches = (cache_a_k.reshape(nbs, depth, past, A_KV_HEADS * HEAD_DIM),
              cache_a_v.reshape(nbs, depth, past, A_KV_HEADS * HEAD_DIM),
              cache_b_k.reshape(nbs, depth, past, B_WIDTH),
              cache_b_v.reshape(nbs, depth, past, B_WIDTH))
    final_w = final_norm_w.reshape(1, d)
    n_chunks = D_FF // FFN_FC
    wts = dict(
        w_in=_pad_cast_bf16(w_in, PROJ_PAD),
        w_out=_cast_bf16(w_out),
        wu=_gv_cast_bf16(ffn_up),
        wd=_cast_bf16(ffn_down).reshape(depth, n_chunks, FFN_FC, d),
    )

    yp = x_prompt.reshape(1, nbp * lp, d)
    ys = x_sample
    new_caches = None
    for l in range(depth):
        lam_init = 0.8 - 0.6 * math.exp(-0.3 * l)
        p = _layer_params(l, a_q_norm, a_k_norm, b_lambda, b_subln, ssm_conv_w, ssm_conv_b,
                          ssm_A_log, ssm_dt_bias, ssm_D, ssm_norm_w, ffn_conv_w, ffn_conv_b)
        last = l == depth - 1
        mod_ctx = mod[l, 0].reshape(1, 6, d)
        mod_lat = mod[l, 1:1 + nbs].reshape(nbs, 6, d)
        yp, new_caches = _block(yp, mod_ctx, p, wts, None, None, None, l, seq_shape=(nbp, lp),
                                per_batch_mod=False, is_ctx=True, lam_init=lam_init,
                                final_w=final_w, final_norm=last, new_caches=new_caches)
        ys, _ = _block(ys, mod_lat, p, wts, rope_tabs, caches, state_ssm, l, seq_shape=(nbs, ls),
                       per_batch_mod=True, is_ctx=False, lam_init=lam_init,
                       final_w=final_w, final_norm=last)

    y_prompt = yp.reshape(nbp, lp, d)
    new_a_k = new_caches[0].reshape(nbp, depth, lp, A_KV_HEADS, HEAD_DIM)
    new_a_v = new_caches[1].reshape(nbp, depth, lp, A_KV_HEADS, HEAD_DIM)
    new_b_k = new_caches[2].reshape(nbp, depth, lp, B_HEADS, 2, B_HALF)
    new_b_v = new_caches[3].reshape(nbp, depth, lp, B_HEADS, 2 * B_HALF)
    new_ssm = new_caches[4]
    return (y_prompt, ys, new_a_k, new_a_v, new_b_k, new_b_v, new_ssm)
```

```python
import functools
import math

import jax
import jax.numpy as jnp
from jax import lax
from jax.experimental import pallas as pl
from jax.experimental.pallas import tpu as pltpu

F32 = jnp.float32
BF16 = jnp.bfloat16

LANES = 128
VMEM_LIMIT_BYTES = 56 * 1024 * 1024

D_MODEL = 1024
GRID_W = 64
SSD_CHUNK = 128
ROPE_THETA = 10000.0
EPS = 1e-6
HEAD_DIM = 64
A_Q_HEADS = 6
A_KV_HEADS = 2
A_GROUP = A_Q_HEADS // A_KV_HEADS
A_WIDTH = A_Q_HEADS * HEAD_DIM
B_HEADS = 4
B_HALF = 32
B_WIDTH = B_HEADS * 2 * B_HALF
C_HEADS = 6
C_HEADDIM = 64
C_INNER = C_HEADS * C_HEADDIM
C_GROUPS = 2
C_STATE = 128
C_CONV_CH = C_INNER + 2 * C_GROUPS * C_STATE
D_FF = 2816
PROJ_WIDTH = A_WIDTH + 4 * HEAD_DIM + 3 * B_WIDTH + 2 * C_INNER + 4 * C_STATE + 2 * C_HEADS
PROJ_PAD = ((PROJ_WIDTH + 2 * LANES - 1) // (2 * LANES)) * (2 * LANES)

OFF_AQ = 0
OFF_AK = OFF_AQ + A_WIDTH
OFF_AV = OFF_AK + A_KV_HEADS * HEAD_DIM
OFF_BQ = OFF_AV + A_KV_HEADS * HEAD_DIM
OFF_BK = OFF_BQ + B_WIDTH
OFF_BV = OFF_BK + B_WIDTH
OFF_CX = OFF_BV + B_WIDTH
OFF_CZ = OFF_CX + C_INNER
OFF_CB = OFF_CZ + C_INNER
OFF_CC = OFF_CB + C_GROUPS * C_STATE
OFF_DT = OFF_CC + C_GROUPS * C_STATE

MOD_BLOCK = 1024
PROJ_TM = 512
FFN_TM = 512
FFN_SUB = 512
FFN_FC = 256
FFN_RB = 256
ATTN_TQ = 512
ATTN_KC = 512
ATTN_AHEAD = 2
ROW_BLOCK = 256
CONV_ROWS = 128
SSD_UNROLL = 4
NEG_BIG = -1e30
LOG2E = 1.4426950408889634


def _cparams(n_grid):
    return pltpu.CompilerParams(
        dimension_semantics=("parallel",) * n_grid,
        vmem_limit_bytes=VMEM_LIMIT_BYTES,
    )


def _sigmoid(x):
    return 1.0 / (1.0 + jnp.exp(-x))


def _left_mask():
    lane = lax.broadcasted_iota(jnp.int32, (1, LANES), 1)
    return lane < (LANES // 2)


def _pair_rmsnorm(y, w):
    left = _left_mask()
    sq = y * y
    s_l = jnp.sum(jnp.where(left, sq, 0.0), axis=-1, keepdims=True)
    s_r = jnp.sum(jnp.where(left, 0.0, sq), axis=-1, keepdims=True)
    ms = jnp.where(left, s_l, s_r) * (1.0 / HEAD_DIM)
    return y * lax.rsqrt(ms + EPS) * w


def _rope(y, cos, s_up, s_dn, quarter):
    return (y * cos
            + pltpu.roll(y, LANES - quarter, 1) * s_up
            + pltpu.roll(y, quarter, 1) * s_dn)


def _modulation_kernel(c_ref, w_ref, b_ref, o_ref):
    c = c_ref[...]
    s = c * _sigmoid(c)
    o_ref[0] = jnp.dot(s.astype(BF16), w_ref[0].astype(BF16),
                       preferred_element_type=F32) + b_ref[0]


def _modulation(cvecs, ada_w, ada_b):
    depth, d, width = ada_w.shape
    rows = cvecs.shape[0]
    return pl.pallas_call(
        _modulation_kernel,
        out_shape=jax.ShapeDtypeStruct((depth, rows, width), F32),
        grid=(depth, width // MOD_BLOCK),
        in_specs=[
            pl.BlockSpec((rows, d), lambda l, j: (0, 0)),
            pl.BlockSpec((1, d, MOD_BLOCK), lambda l, j: (l, 0, j)),
            pl.BlockSpec((1, 1, MOD_BLOCK), lambda l, j: (l, 0, j)),
        ],
        out_specs=pl.BlockSpec((1, rows, MOD_BLOCK), lambda l, j: (l, 0, j)),
        compiler_params=_cparams(2),
        name="modulation",
    )(cvecs, ada_w, ada_b.reshape(depth, 1, width))


CAST_ROWS = 256


def _cast_kernel(x_ref, o_ref):
    o_ref[...] = x_ref[...].astype(o_ref.dtype)


def _cast_bf16(x):
    depth, rows, cols = x.shape
    spec = pl.BlockSpec((1, CAST_ROWS, cols), lambda l, r: (l, r, 0))
    return pl.pallas_call(
        _cast_kernel, out_shape=jax.ShapeDtypeStruct(x.shape, BF16),
        grid=(depth, rows // CAST_ROWS), in_specs=[spec], out_specs=spec,
        compiler_params=_cparams(2), name="cast_bf16",
    )(x)


def _pad_cast_kernel(x_ref, o_ref, *, n_cols):
    tr = x_ref.shape[0]
    col = lax.broadcasted_iota(jnp.int32, (tr, 1), 0) + pl.program_id(0) * tr
    for l in range(x_ref.shape[1]):
        x = jnp.where(col < n_cols, x_ref[:, l, :], 0.0)
        o_ref[l] = x.T.astype(BF16)


def _pad_cast_bf16(x, width):
    depth, rows, cols = x.shape
    return pl.pallas_call(
        functools.partial(_pad_cast_kernel, n_cols=cols),
        out_shape=jax.ShapeDtypeStruct((depth, rows, width), BF16),
        grid=(width // LANES,),
        in_specs=[pl.BlockSpec((LANES, depth, rows), lambda i: (i, 0, 0))],
        out_specs=pl.BlockSpec((depth, rows, LANES), lambda i: (0, 0, i)),
        compiler_params=_cparams(1), name="pad_cast_bf16",
    )(jnp.transpose(x, (2, 0, 1)))


def _gv_cast_kernel(g_ref, v_ref, o_ref):
    o_ref[0, 0, :, 0:FFN_FC] = g_ref[0].astype(BF16)
    o_ref[0, 0, :, FFN_FC:2 * FFN_FC] = v_ref[0].astype(BF16)


def _gv_cast_bf16(up):
    depth, d, _ = up.shape
    n_chunks = D_FF // FFN_FC
    return pl.pallas_call(
        _gv_cast_kernel, out_shape=jax.ShapeDtypeStruct((depth, n_chunks, d, 2 * FFN_FC), BF16),
        grid=(depth, n_chunks),
        in_specs=[pl.BlockSpec((1, d, FFN_FC), lambda l, c: (l, 0, c)),
                  pl.BlockSpec((1, d, FFN_FC), lambda l, c: (l, 0, n_chunks + c))],
        out_specs=pl.BlockSpec((1, 1, d, 2 * FFN_FC), lambda l, c: (l, c, 0, 0)),
        compiler_params=_cparams(2), name="gv_cast_bf16",
    )(up, up)


def _proj_kernel(*refs, rope, kv_f32, n_alias, cache_slot):
    x_ref, mod_ref, w_ref, qn_ref, kn_ref = refs[:5]
    pos = 5
    if rope:
        ca_ref, sau_ref, sad_ref, cb_ref, sbu_ref, sbd_ref = refs[pos:pos + 6]
        pos += 6
    pos += n_alias
    (qa_o, k3_o, vta_o, qb_o, kb_o, vtb_o, xbc_o, z_o, dt_o) = refs[pos:pos + 9]
    pos += 9
    if kv_f32:
        ka32_o, va32_o, kb32_o, vb32_o = refs[pos:pos + 4]
        pos += 4
    p_scr = refs[pos]

    x = x_ref[0]
    ms = jnp.mean(x * x, axis=-1, keepdims=True)
    shift = mod_ref[0, 0:1, :]
    scale = mod_ref[0, 1:2, :]
    h = (x * lax.rsqrt(ms + EPS)) * (1.0 + scale) + shift
    p_scr[...] = jnp.dot(h.astype(BF16), w_ref[...], preferred_element_type=F32)

    left = _left_mask()

    def rope_a(y):
        if not rope:
            return y
        return _rope(y, ca_ref[...], sau_ref[...], sad_ref[...], HEAD_DIM // 4)

    def rope_b(y):
        if not rope:
            return y
        return _rope(y, cb_ref[...], sbu_ref[...], sbd_ref[...], B_HALF // 4)

    values_t = len(vta_o.shape) == 4
    if values_t:
        n_sub, sub_len = vta_o.shape[0], vta_o.shape[3]
        ones_half = jnp.ones((HEAD_DIM, sub_len), BF16)

    def store_vt(dst, h, vt_half, e):
        for s in range(n_sub):
            dst[s, h, e * HEAD_DIM:(e + 1) * HEAD_DIM, :] = vt_half[:, s * sub_len:(s + 1) * sub_len]
            dst[s, h, (1 - e) * HEAD_DIM:(2 - e) * HEAD_DIM, :] = ones_half

    a_scale = HEAD_DIM ** -0.5 * LOG2E
    for j in range(A_WIDTH // LANES):
        y = p_scr[:, OFF_AQ + j * LANES:OFF_AQ + (j + 1) * LANES]
        y = rope_a(_pair_rmsnorm(y, qn_ref[...]))
        qa_o[0, :, j * LANES:(j + 1) * LANES] = (y * a_scale).astype(BF16)

    k = rope_a(_pair_rmsnorm(p_scr[:, OFF_AK:OFF_AK + LANES], kn_ref[...]))
    v = p_scr[:, OFF_AV:OFF_AV + LANES]
    if kv_f32:
        kv_sub, kv_len = ka32_o.shape[0], ka32_o.shape[2]
        for u in range(kv_sub):
            ka32_o[u, cache_slot] = k[u * kv_len:(u + 1) * kv_len]
            va32_o[u, cache_slot] = v[u * kv_len:(u + 1) * kv_len]
        for ref in (ka32_o, va32_o, kb32_o, vb32_o):
            for other in range(ref.shape[1]):
                if other != cache_slot:
                    ref[:, other] = jnp.zeros((kv_sub, kv_len, ref.shape[3]), F32)
    swapped = pltpu.roll(k, LANES // 2, 1)
    k3_o[0, :, 0:LANES] = jnp.where(left, k, swapped).astype(BF16)
    k3_o[0, :, LANES:2 * LANES] = k.astype(BF16)
    k3_o[0, :, 2 * LANES:3 * LANES] = jnp.where(left, swapped, k).astype(BF16)
    if values_t:
        vt = v.T.astype(BF16)
        for h in range(A_Q_HEADS):
            g = h // A_GROUP
            store_vt(vta_o, h, vt[g * HEAD_DIM:(g + 1) * HEAD_DIM], h % 2)
    else:
        v_sw = pltpu.roll(v, LANES // 2, 1)
        vta_o[0, :, 0:LANES] = jnp.where(left, v, v_sw).astype(BF16)
        vta_o[0, :, LANES:2 * LANES] = v.astype(BF16)
        vta_o[0, :, 2 * LANES:3 * LANES] = jnp.where(left, v_sw, v).astype(BF16)

    b_scale = B_HALF ** -0.5 * LOG2E
    for j in range(B_WIDTH // LANES):
        sl = slice(j * LANES, (j + 1) * LANES)
        q = rope_b(p_scr[:, OFF_BQ + j * LANES:OFF_BQ + (j + 1) * LANES])
        qb_o[0, :, sl] = (q * b_scale).astype(BF16)
        kb = rope_b(p_scr[:, OFF_BK + j * LANES:OFF_BK + (j + 1) * LANES])
        vb = p_scr[:, OFF_BV + j * LANES:OFF_BV + (j + 1) * LANES]
        kb_o[0, :, sl] = kb.astype(BF16)
        if values_t:
            vbt = vb.T.astype(BF16)
            for e in range(2):
                store_vt(vtb_o, 2 * j + e, vbt[e * HEAD_DIM:(e + 1) * HEAD_DIM], e)
        else:
            vtb_o[0, :, sl] = vb.astype(BF16)
        if kv_f32:
            for u in range(kv_sub):
                kb32_o[u, cache_slot, :, sl] = kb[u * kv_len:(u + 1) * kv_len]
                vb32_o[u, cache_slot, :, sl] = vb[u * kv_len:(u + 1) * kv_len]

    xbc_sub, xbc_len = xbc_o.shape[0], xbc_o.shape[2]
    for s in range(C_CONV_CH // LANES):
        col = OFF_CX + s * LANES if s < C_INNER // LANES else OFF_CB + s * LANES - C_INNER
        for u in range(xbc_sub):
            xbc_o[u, s] = p_scr[u * xbc_len:(u + 1) * xbc_len, col:col + LANES]
    z_o[0] = p_scr[:, OFF_CZ:OFF_CZ + C_INNER]
    dt_o[0] = p_scr[:, OFF_DT:OFF_DT + LANES]


def _proj(x, mod, w_in_p, layer, qn, kn, rope_tabs, *, per_batch_mod, kv_f32, seq_len, values_t,
          kv_prev=None):
    depth = w_in_p.shape[0]
    nb, L, d = x.shape
    tm = min(PROJ_TM, L)
    rope = rope_tabs is not None
    mod_map = (lambda b, i: (b, 0, 0)) if per_batch_mod else (lambda b, i: (0, 0, 0))
    row_spec = lambda w: pl.BlockSpec((1, tm, w), lambda b, i: (b, i, 0))
    in_specs = [
        row_spec(d),
        pl.BlockSpec((1, 6, d), mod_map),
        pl.BlockSpec((None, d, PROJ_PAD), lambda b, i: (layer, 0, 0)),
        pl.BlockSpec((1, LANES), lambda b, i: (0, 0)),
        pl.BlockSpec((1, LANES), lambda b, i: (0, 0)),
    ]
    args = [x, mod, w_in_p, qn, kn]
    if rope:
        in_specs += [pl.BlockSpec((tm, LANES), lambda b, i: (i, 0))] * 6
        args += list(rope_tabs)
    widths = [(A_WIDTH, BF16), (A_WIDTH, BF16), (-A_Q_HEADS if values_t else A_WIDTH, BF16),
              (B_WIDTH, BF16), (B_WIDTH, BF16), (-B_HEADS if values_t else B_WIDTH, BF16),
              ("slabs", F32), (C_INNER, F32), (LANES, F32)]
    kv_widths = [LANES, LANES, B_WIDTH, B_WIDTH] if kv_f32 else []
    widths += [("cache", w) for w in kv_widths]
    aliases = {}
    if kv_prev is not None:
        for k_idx, prev in enumerate(kv_prev):
            aliases[len(args)] = len(widths) - len(kv_widths) + k_idx
            in_specs.append(pl.BlockSpec(memory_space=pl.ANY))
            args.append(prev)
    out_shape, out_specs = [], []
    n_seq = nb * L // seq_len
    in_seq = seq_len >= tm
    assert (L == seq_len) if in_seq else (nb == 1 and tm % seq_len == 0)
    n_slab = C_CONV_CH // LANES
    for w, dt in widths:
        if w == "slabs":
            out_shape.append(jax.ShapeDtypeStruct((n_seq, n_slab, seq_len, LANES), dt))
            out_specs.append(pl.BlockSpec((1, n_slab, tm, LANES), lambda b, i: (b, 0, i, 0)) if in_seq else
                             pl.BlockSpec((tm // seq_len, n_slab, seq_len, LANES), lambda b, i: (i, 0, 0, 0)))
        elif w == "cache":
            assert not in_seq
            out_shape.append(jax.ShapeDtypeStruct((n_seq, depth, seq_len, dt), F32))
            out_specs.append(
                pl.BlockSpec((tm // seq_len, depth, seq_len, dt), lambda b, i: (i, 0, 0, 0)) if kv_prev is None
                else pl.BlockSpec((tm // seq_len, 1, seq_len, dt), lambda b, i: (i, layer, 0, 0)))
        elif w > 0:
            out_shape.append(jax.ShapeDtypeStruct((nb, L, w), dt))
            out_specs.append(row_spec(w))
        else:
            out_shape.append(jax.ShapeDtypeStruct((n_seq, -w, LANES, seq_len), dt))
            out_specs.append(pl.BlockSpec((1, -w, LANES, tm), lambda b, i: (b, 0, 0, i)) if in_seq else
                             pl.BlockSpec((tm // seq_len, -w, LANES, seq_len), lambda b, i: (i, 0, 0, 0)))
    return pl.pallas_call(
        functools.partial(_proj_kernel, rope=rope, kv_f32=kv_f32, n_alias=len(aliases),
                          cache_slot=layer if kv_prev is None else 0),
        out_shape=out_shape,
        grid=(nb, L // tm),
        in_specs=in_specs,
        out_specs=out_specs,
        input_output_aliases=aliases,
        scratch_shapes=[pltpu.VMEM((tm, PROJ_PAD), F32)],
        compiler_params=_cparams(2),
        name="proj_rope" if rope else "proj",
    )(*args)


class _ScoreMap:
    def __init__(self, tag, qm, chunks, s_ref):
        self.tag, self.qm, self.chunks, self.s_ref = tag, qm, chunks, s_ref
        self.offsets = [sum(w for _, _, w in chunks[:c]) for c in range(len(chunks))]
        self.m_part = self.m = self.acc = None

    def pass1(self, c):
        k_fn, _, w = self.chunks[c]
        tq = self.qm.shape[0]
        s = lax.dot_general(k_fn(), self.qm, (((1,), (1,)), ((), ())), preferred_element_type=F32)
        self.s_ref[self.offsets[c]:self.offsets[c] + w, :] = s
        part = jnp.max(s.reshape(w // 8, 8, tq), axis=0)
        self.m_part = part if self.m_part is None else jnp.maximum(self.m_part, part)

    def finish_max(self):
        self.m = jnp.max(self.m_part, axis=0, keepdims=True)

    def pass2(self, c):
        _, vt_fn, w = self.chunks[c]
        e = jnp.exp2(self.s_ref[self.offsets[c]:self.offsets[c] + w, :] - self.m).astype(BF16)
        pv = jnp.dot(vt_fn(), e, preferred_element_type=F32)
        self.acc = pv if self.acc is None else self.acc + pv


def _run_score_maps(maps, on_done):
    for mp in maps[:ATTN_AHEAD]:
        for c in range(len(mp.chunks)):
            mp.pass1(c)
        mp.finish_max()
    for i, mp in enumerate(maps):
        nxt = maps[i + ATTN_AHEAD] if i + ATTN_AHEAD < len(maps) else None
        n_next = len(nxt.chunks) if nxt is not None else 0
        for c in range(max(len(mp.chunks), n_next)):
            if c < len(mp.chunks):
                mp.pass2(c)
            if c < n_next:
                nxt.pass1(c)
        if nxt is not None:
            nxt.finish_max()
        on_done(mp)


def _normalised_half(acc, e):
    l = acc[(1 - e) * HEAD_DIM:(1 - e) * HEAD_DIM + 1, :]
    return acc[e * HEAD_DIM:(e + 1) * HEAD_DIM, :] * (1.0 / l)


def _attn_kernel(*refs, has_cache, lam_init, seq_len):
    qa_ref, k3_ref, vta_ref, qb_ref, kb_ref, vtb_ref, lamv_ref, subln_ref = refs[:8]
    pos = 8
    if has_cache:
        cka_ref, cva_ref, ckb_ref, cvb_ref = refs[pos:pos + 4]
        pos += 4
    ao_ref, bo_ref, s_scr = refs[pos:pos + 3]

    left = _left_mask()
    lane = lax.broadcasted_iota(jnp.int32, (1, LANES), 1)
    kc = min(ATTN_KC, seq_len)
    n_kc = seq_len // kc
    maps = []

    def add_map(tag, qm, chunks):
        maps.append(_ScoreMap(tag, qm, chunks, s_scr.at[len(maps) % (ATTN_AHEAD + 1)]))

    def new_chunks(k_ref, vt_ref, j, h):
        sl = slice(j * LANES, (j + 1) * LANES)
        out = []
        for c in range(n_kc):
            rows = slice(c * kc, (c + 1) * kc)
            out.append((functools.partial(lambda r, s: k_ref[0, r, s], rows, sl),
                        functools.partial(lambda r, hh: vt_ref[0, hh, :, r], rows, h), kc))
        return out

    def value_block(vt_half, e):
        ones = jnp.ones_like(vt_half)
        return jnp.concatenate([vt_half, ones] if e == 0 else [ones, vt_half], axis=0).astype(BF16)

    if has_cache:
        ck = cka_ref[0, 0]
        ck_sw = pltpu.roll(ck, LANES // 2, 1)
        ck3 = [jnp.where(left, ck, ck_sw), ck, jnp.where(left, ck_sw, ck)]
        cvt = cva_ref[0, 0].T
        past = ck.shape[0]
    for j in range(A_WIDTH // LANES):
        q = qa_ref[0, :, j * LANES:(j + 1) * LANES]
        for e in range(2):
            h = 2 * j + e
            g = h // A_GROUP
            chunks = new_chunks(k3_ref, vta_ref, j, h)
            if has_cache:
                ckj = ck3[j].astype(BF16)
                cvj = value_block(cvt[g * HEAD_DIM:(g + 1) * HEAD_DIM], e)
                chunks.append((lambda a=ckj: a, lambda a=cvj: a, past))
            qm = jnp.where(left if e == 0 else jnp.logical_not(left), q, jnp.zeros_like(q))
            add_map(("a", j, e, 0), qm, chunks)

    lv = lamv_ref[...]
    lam = (jnp.exp(jnp.sum(lv[0:1] * lv[1:2], axis=-1, keepdims=True))
           - jnp.exp(jnp.sum(lv[2:3] * lv[3:4], axis=-1, keepdims=True)) + lam_init)
    for j in range(B_WIDTH // LANES):
        sl = slice(j * LANES, (j + 1) * LANES)
        q = qb_ref[0, :, sl]
        if has_cache:
            ckj = ckb_ref[0, 0, :, sl].astype(BF16)
            cvt_j = cvb_ref[0, 0, :, sl].T
        for e in range(2):
            h = 2 * j + e
            chunks = new_chunks(kb_ref, vtb_ref, j, h)
            if has_cache:
                cvj = value_block(cvt_j[e * HEAD_DIM:(e + 1) * HEAD_DIM], e)
                chunks.append((lambda a=ckj: a, lambda a=cvj: a, ckj.shape[0]))
            for mi in range(2):
                sel = (lane // B_HALF) == (2 * e + mi)
                add_map(("b", j, e, mi), jnp.where(sel, q, jnp.zeros_like(q)), chunks)

    done = {}

    def on_done(mp):
        kind, j, e, mi = mp.tag
        done[mp.tag] = _normalised_half(mp.acc, e)
        if kind == "a" and e == 1:
            o_t = jnp.concatenate([done[("a", j, 0, 0)], done[("a", j, 1, 0)]], axis=0)
            ao_ref[0, :, j * LANES:(j + 1) * LANES] = o_t.T.astype(BF16)
        if kind == "b" and e == 1 and mi == 1:
            halves = []
            for ee in range(2):
                o = done[("b", j, ee, 0)] - lam * done[("b", j, ee, 1)]
                ms = jnp.mean(o * o, axis=0, keepdims=True)
                halves.append(o * lax.rsqrt(ms + EPS))
            o_t = jnp.concatenate(halves, axis=0)
            bo_ref[0, :, j * LANES:(j + 1) * LANES] = (
                o_t.T * subln_ref[...] * (1.0 - lam_init)).astype(BF16)

    _run_score_maps(maps, on_done)


def _softmax_pv_rows(qm, k, v):
    s = lax.dot_general(qm, k, (((1,), (1,)), ((), ())), preferred_element_type=F32)
    m_part = s[:, 0:LANES]
    for t in range(1, s.shape[1] // LANES):
        m_part = jnp.maximum(m_part, s[:, t * LANES:(t + 1) * LANES])
    e = jnp.exp2(s - jnp.max(m_part, axis=-1, keepdims=True))
    l_part = e[:, 0:LANES]
    for t in range(1, s.shape[1] // LANES):
        l_part = l_part + e[:, t * LANES:(t + 1) * LANES]
    acc = jnp.dot(e.astype(BF16), v, preferred_element_type=F32)
    return acc, jnp.sum(l_part, axis=-1, keepdims=True)


def _attn_rows_kernel(qa_ref, k3_ref, v3_ref, qb_ref, kb_ref, vb_ref, lamv_ref, subln_ref,
                      ao_ref, bo_ref, *, lam_init):
    left = _left_mask()
    lane = lax.broadcasted_iota(jnp.int32, (1, LANES), 1)
    for j in range(A_WIDTH // LANES):
        sl = slice(j * LANES, (j + 1) * LANES)
        q = qa_ref[0, :, sl]
        halves = []
        for e in range(2):
            qm = jnp.where(left if e == 0 else jnp.logical_not(left), q, jnp.zeros_like(q))
            acc, l = _softmax_pv_rows(qm, k3_ref[0, :, sl], v3_ref[0, :, sl])
            halves.append(acc * (1.0 / l))
        ao_ref[0, :, sl] = jnp.where(left, halves[0], halves[1]).astype(BF16)

    lv = lamv_ref[...]
    lam = (jnp.exp(jnp.sum(lv[0:1] * lv[1:2], axis=-1, keepdims=True))
           - jnp.exp(jnp.sum(lv[2:3] * lv[3:4], axis=-1, keepdims=True)) + lam_init)
    for j in range(B_WIDTH // LANES):
        sl = slice(j * LANES, (j + 1) * LANES)
        q = qb_ref[0, :, sl]
        halves = []
        for e in range(2):
            maps = []
            for mi in range(2):
                sel = (lane // B_HALF) == (2 * e + mi)
                acc, l = _softmax_pv_rows(jnp.where(sel, q, jnp.zeros_like(q)), kb_ref[0, :, sl], vb_ref[0, :, sl])
                maps.append(acc * (1.0 / l))
            halves.append(maps[0] - lam * maps[1])
        o = jnp.where(left, halves[0], halves[1])
        bo_ref[0, :, sl] = (_pair_rmsnorm(o, subln_ref[...]) * (1.0 - lam_init)).astype(BF16)


def _attn_rows(qa, k3, v3, qb, kb, vb, lamv, subln, *, lam_init):
    nb, L, _ = qa.shape
    spec = lambda w: pl.BlockSpec((1, L, w), lambda b: (b, 0, 0))
    return pl.pallas_call(
        functools.partial(_attn_rows_kernel, lam_init=lam_init),
        out_shape=[jax.ShapeDtypeStruct((nb, L, A_WIDTH), BF16),
                   jax.ShapeDtypeStruct((nb, L, B_WIDTH), BF16)],
        grid=(nb,),
        in_specs=[spec(A_WIDTH), spec(A_WIDTH), spec(A_WIDTH), spec(B_WIDTH), spec(B_WIDTH), spec(B_WIDTH),
                  pl.BlockSpec((4, B_HALF), lambda b: (0, 0)),
                  pl.BlockSpec((1, LANES), lambda b: (0, 0))],
        out_specs=[spec(A_WIDTH), spec(B_WIDTH)],
        compiler_params=_cparams(1),
        name="attn",
    )(qa, k3, v3, qb, kb, vb, lamv, subln)


def _attn(qa, k3, vta, qb, kb, vtb, lamv, subln, caches, layer, *, lam_init):
    nb, L, _ = qa.shape
    tq = min(ATTN_TQ, L)
    has_cache = caches is not None
    q_spec = lambda w: pl.BlockSpec((1, tq, w), lambda b, i: (b, i, 0))
    kv_spec = lambda w: pl.BlockSpec((1, L, w), lambda b, i: (b, 0, 0))
    vt_spec = lambda n: pl.BlockSpec((1, n, LANES, L), lambda b, i: (b, 0, 0, 0))
    in_specs = [q_spec(A_WIDTH), kv_spec(A_WIDTH), vt_spec(A_Q_HEADS),
                q_spec(B_WIDTH), kv_spec(B_WIDTH), vt_spec(B_HEADS),
                pl.BlockSpec((4, B_HALF), lambda b, i: (0, 0)),
                pl.BlockSpec((1, LANES), lambda b, i: (0, 0))]
    args = [qa, k3, vta, qb, kb, vtb, lamv, subln]
    lk = L
    if has_cache:
        past = caches[0].shape[2]
        lk += past
        for c in caches:
            in_specs.append(pl.BlockSpec((1, 1, past, c.shape[-1]), lambda b, i: (b, layer, 0, 0)))
            args.append(c)
    return pl.pallas_call(
        functools.partial(_attn_kernel, has_cache=has_cache, lam_init=lam_init, seq_len=L),
        out_shape=[jax.ShapeDtypeStruct((nb, L, A_WIDTH), BF16),
                   jax.ShapeDtypeStruct((nb, L, B_WIDTH), BF16)],
        grid=(nb, L // tq),
        in_specs=in_specs,
        out_specs=[q_spec(A_WIDTH), q_spec(B_WIDTH)],
        scratch_shapes=[pltpu.VMEM((ATTN_AHEAD + 1, lk, tq), F32)],
        compiler_params=_cparams(2),
        name="attn_cache" if has_cache else "attn",
    )(*args)


def _prefix_sums(tri, a):
    hi = a.astype(BF16)
    lo = (a - hi.astype(F32)).astype(BF16)
    return (jnp.dot(tri, hi, preferred_element_type=F32)
            + jnp.dot(tri, lo, preferred_element_type=F32))


def _pair_cols(x, c0):
    return jnp.where(_left_mask(), x[:, c0:c0 + 1], x[:, c0 + 1:c0 + 2])


def _ssd_kernel(*refs, has_init, emit_state, seq_len, n_alias, state_slot):
    xbc_ref, z_ref, dt_ref, cw_ref, cb_ref, dtb_ref, alog_ref, dexp_ref, nw_ref = refs[:9]
    pos = 9
    if has_init:
        init_ref = refs[pos]
        pos += 1
    pos += n_alias
    co_ref = refs[pos]
    pos += 1
    if emit_state:
        st_ref = refs[pos]
        pos += 1
    xc_scr, y_scr, st_scr = refs[pos:pos + 3]

    L = seq_len
    rb = min(ROW_BLOCK, L)
    q = SSD_CHUNK
    nc = L // q
    left = _left_mask()

    w0 = cw_ref[0:1, :]
    w1 = cw_ref[1:2, :]
    w2 = cw_ref[2:3, :]
    cb = CONV_ROWS
    half = cb // 2
    hrow = lax.broadcasted_iota(jnp.int32, (half, 1), 0)
    for s in range(C_CONV_CH // LANES):
        ls = slice(s * LANES, (s + 1) * LANES)
        w0s, w1s, w2s, bs = w0[:, ls], w1[:, ls], w2[:, ls], cb_ref[:, ls]
        for r0 in range(0, L, cb):
            even = xbc_ref[0, s, pl.ds(r0, half, stride=2), :]
            odd = xbc_ref[0, s, pl.ds(r0 + 1, half, stride=2), :]
            if r0 > 0:
                odd_before = xbc_ref[0, s, pl.ds(r0 - 1, half, stride=2), :]
            else:
                odd_before = jnp.where(hrow == 0, 0.0, pltpu.roll(odd, 1, 0))
            if r0 + cb < L:
                even_after = xbc_ref[0, s, pl.ds(r0 + 2, half, stride=2), :]
            else:
                even_after = jnp.where(hrow == half - 1, 0.0, pltpu.roll(even, half - 1, 0))
            y_even = odd_before * w0s + even * w1s + odd * w2s + bs
            y_odd = even * w0s + odd * w1s + even_after * w2s + bs
            xc_scr[s, pl.ds(r0, half, stride=2), :] = y_even * _sigmoid(y_even)
            xc_scr[s, pl.ds(r0 + 1, half, stride=2), :] = y_odd * _sigmoid(y_odd)
    for j in range(C_INNER // LANES):
        for r0 in range(0, L, rb):
            y_scr[r0:r0 + rb, j * LANES:(j + 1) * LANES] = (
                xc_scr[j, r0:r0 + rb, :] * dexp_ref[:, j * LANES:(j + 1) * LANES])

    for d in range(2):
        if has_init:
            st_scr[d] = init_ref[0, 0, d].reshape(C_INNER, C_STATE).T
        else:
            st_scr[d] = jnp.zeros((C_STATE, C_INNER), F32)

    a_neg = -jnp.exp(alog_ref[...])
    ti = lax.broadcasted_iota(jnp.int32, (q, q), 0)
    si = lax.broadcasted_iota(jnp.int32, (q, q), 1)
    tri_incl = (si <= ti).astype(BF16)
    src_lane = lax.broadcasted_iota(jnp.int32, (LANES, C_INNER), 0)
    dst_head = lax.broadcasted_iota(jnp.int32, (LANES, C_INNER), 1) // C_HEADDIM
    spread = [(src_lane == d * C_HEADS + dst_head).astype(BF16) for d in range(2)]

    def process(d, r0):
        rows = pl.ds(r0, q)
        lo = d * C_HEADS
        causal = (si <= ti) if d == 0 else (si >= ti)
        n_x = C_INNER // LANES
        bm = [xc_scr[n_x + g, rows, :] for g in range(C_GROUPS)]
        cm = [xc_scr[n_x + C_GROUPS + g, rows, :] for g in range(C_GROUPS)]
        bm_b = [b.astype(BF16) for b in bm]
        cm_b = [c.astype(BF16) for c in cm]
        bmt_b = [b.T.astype(BF16) for b in bm]
        gmat = [lax.dot_general(cm_b[g], bm_b[g], (((1,), (1,)), ((), ())), preferred_element_type=F32)
                for g in range(C_GROUPS)]

        dtr = dt_ref[0, rows, :] + dtb_ref[...]
        dt = jnp.maximum(dtr, 0.0) + jnp.log1p(jnp.exp(-jnp.abs(dtr)))
        a = dt * a_neg
        cs = _prefix_sums(tri_incl, a)
        tot = cs[q - 1:q, :]
        ev = cs if d == 0 else (tot - cs + a)
        e_in = jnp.exp(ev)
        e_out = jnp.exp(tot - ev)
        dec = jnp.exp(tot)
        ev_t = ev.T
        per_head = jnp.concatenate([dt, e_in, e_out], axis=0).astype(BF16)
        wide = jnp.dot(per_head, spread[d], preferred_element_type=F32)
        dt_w, e_in_w, e_out_w = wide[0:q], wide[q:2 * q], wide[2 * q:3 * q]

        st = st_scr.at[d]
        for j in range(C_INNER // LANES):
            sl = slice(j * LANES, (j + 1) * LANES)
            h0 = 2 * j
            g0, g1 = h0 // (C_HEADS // C_GROUPS), (h0 + 1) // (C_HEADS // C_GROUPS)
            xdt = xc_scr[j, rows, :] * dt_w[:, sl]
            xdt_b = xdt.astype(BF16)
            st_blk = st[:, sl]
            st_b = st_blk.astype(BF16)
            yd = []
            for e in range(2):
                hh = lo + h0 + e
                g = g0 if e == 0 else g1
                diff = ev[:, hh:hh + 1] - ev_t[hh:hh + 1, :]
                lmat = jnp.exp(jnp.where(causal, diff, NEG_BIG))
                yd.append(jnp.dot((gmat[g] * lmat).astype(BF16), xdt_b, preferred_element_type=F32))
            yo0 = jnp.dot(cm_b[g0], st_b, preferred_element_type=F32)
            yo1 = yo0 if g1 == g0 else jnp.dot(cm_b[g1], st_b, preferred_element_type=F32)
            y_blk = (jnp.where(left, yd[0], yd[1])
                     + jnp.where(left, yo0, yo1) * e_in_w[:, sl])
            y_scr[rows, sl] = y_scr[rows, sl] + y_blk
            xw = (xdt * e_out_w[:, sl]).astype(BF16)
            up0 = jnp.dot(bmt_b[g0], xw, preferred_element_type=F32)
            up1 = up0 if g1 == g0 else jnp.dot(bmt_b[g1], xw, preferred_element_type=F32)
            st[:, sl] = st_blk * _pair_cols(dec, lo + h0) + jnp.where(left, up0, up1)

    if nc <= 2:
        for c in range(nc):
            process(0, c * q)
            process(1, (nc - 1 - c) * q)
    else:
        def body(c, carry):
            process(0, pl.multiple_of(c * q, q))
            process(1, pl.multiple_of((nc - 1 - c) * q, q))
            return carry
        lax.fori_loop(0, nc, body, 0, unroll=SSD_UNROLL if nc % SSD_UNROLL == 0 else 1)

    if emit_state:
        for d in range(2):
            st_ref[0, state_slot, d] = st_scr[d].T.reshape(C_HEADS, C_HEADDIM, C_STATE)
        for other in range(st_ref.shape[1]):
            if other != state_slot:
                st_ref[0, other] = jnp.zeros(st_ref.shape[2:], F32)

    for r0 in range(0, L, rb):
        zz = z_ref[0, r0:r0 + rb, :]
        yg = y_scr[r0:r0 + rb, :] * (zz * _sigmoid(zz))
        ms = jnp.mean(yg * yg, axis=-1, keepdims=True)
        co_ref[0, r0:r0 + rb, :] = (yg * lax.rsqrt(ms + EPS) * nw_ref[...]).astype(BF16)


def _ssd(xbc, z, dt, conv_w, conv_b, dtb, alog, dexp, norm_w, state, layer, *, emit_state, depth=1,
         st_prev=None):
    nb, n_slab, L, _ = xbc.shape
    has_init = state is not None
    seq_spec = lambda w: pl.BlockSpec((1, L, w), lambda b: (b, 0, 0))
    par_spec = lambda r, w: pl.BlockSpec((r, w), lambda b: (0, 0))
    in_specs = [pl.BlockSpec((1, n_slab, L, LANES), lambda b: (b, 0, 0, 0)), seq_spec(C_INNER), seq_spec(LANES),
                par_spec(3, C_CONV_CH), par_spec(1, C_CONV_CH), par_spec(1, LANES), par_spec(1, LANES),
                par_spec(1, C_INNER), par_spec(1, C_INNER)]
    args = [xbc, z, dt, conv_w, conv_b, dtb, alog, dexp, norm_w]
    if has_init:
        in_specs.append(pl.BlockSpec((1, 1, 2, C_HEADS, C_HEADDIM, C_STATE),
                                     lambda b: (b, layer, 0, 0, 0, 0)))
        args.append(state)
    out_shape = [jax.ShapeDtypeStruct((nb, L, C_INNER), BF16)]
    out_specs = [seq_spec(C_INNER)]
    aliases = {}
    if emit_state:
        out_shape.append(jax.ShapeDtypeStruct((nb, depth, 2, C_HEADS, C_HEADDIM, C_STATE), F32))
        out_specs.append(
            pl.BlockSpec((1, depth, 2, C_HEADS, C_HEADDIM, C_STATE), lambda b: (b, 0, 0, 0, 0, 0)) if st_prev is None
            else pl.BlockSpec((1, 1, 2, C_HEADS, C_HEADDIM, C_STATE), lambda b: (b, layer, 0, 0, 0, 0)))
        if st_prev is not None:
            aliases[len(args)] = 1
            in_specs.append(pl.BlockSpec(memory_space=pl.ANY))
            args.append(st_prev)
    return pl.pallas_call(
        functools.partial(_ssd_kernel, has_init=has_init, emit_state=emit_state, seq_len=L,
                          n_alias=len(aliases), state_slot=layer if st_prev is None else 0),
        input_output_aliases=aliases,
        out_shape=out_shape,
        grid=(nb,),
        in_specs=in_specs,
        out_specs=out_specs,
        scratch_shapes=[pltpu.VMEM((n_slab, L, LANES), F32), pltpu.VMEM((L, C_INNER), F32),
                        pltpu.VMEM((2, C_STATE, C_INNER), F32)],
        compiler_params=_cparams(1),
        name="ssd_init" if has_init else "ssd",
    )(*args)


def _outp_kernel(a_ref, b_ref, c_ref, x_ref, mod_ref, w_ref, x1_o, h2_o):
    mix = jnp.concatenate([a_ref[0], b_ref[0], c_ref[0]], axis=-1)
    o = jnp.dot(mix, w_ref[...], preferred_element_type=F32)
    gate1 = mod_ref[0, 2:3, :]
    shift2 = mod_ref[0, 3:4, :]
    scale2 = mod_ref[0, 4:5, :]
    x1 = x_ref[0] + gate1 * o
    ms = jnp.mean(x1 * x1, axis=-1, keepdims=True)
    x1_o[0] = x1
    h2_o[0] = ((x1 * lax.rsqrt(ms + EPS)) * (1.0 + scale2) + shift2).astype(BF16)


def _outp(a_out, b_out, c_out, x, mod, w_out_b, layer, *, per_batch_mod):
    nb, L, d = x.shape
    tm = min(PROJ_TM, L)
    mod_map = (lambda b, i: (b, 0, 0)) if per_batch_mod else (lambda b, i: (0, 0, 0))
    row_spec = lambda w: pl.BlockSpec((1, tm, w), lambda b, i: (b, i, 0))
    return pl.pallas_call(
        _outp_kernel,
        out_shape=[jax.ShapeDtypeStruct((nb, L, d), F32), jax.ShapeDtypeStruct((nb, L, d), BF16)],
        grid=(nb, L // tm),
        in_specs=[row_spec(A_WIDTH), row_spec(B_WIDTH), row_spec(C_INNER), row_spec(d),
                  pl.BlockSpec((1, 6, d), mod_map),
                  pl.BlockSpec((None, d, d), lambda b, i: (layer, 0, 0))],
        out_specs=[row_spec(d), row_spec(d)],
        compiler_params=_cparams(2),
        name="outp",
    )(a_out, b_out, c_out, x, mod, w_out_b)


HALO = 16


def _ffn_kernel(h_ref, hp_ref, hn_ref, x1_ref, mod_ref, wu_ref, cw_ref, cb_ref, wd_ref, fw_ref,
                o_ref, hext_scr, u_scr, acc_scr, *, seq_len, final_norm):
    tm = h_ref.shape[1]
    i = pl.program_id(1)
    n_chunks = wu_ref.shape[0]
    sub = min(FFN_SUB, tm)
    n_sub = tm // sub
    rb = min(FFN_RB, sub)
    n_items = n_sub * n_chunks

    has_prev = ((i * tm) & (seq_len - 1)) != 0
    has_next = ((i * tm + tm) & (seq_len - 1)) != 0
    hext_scr[0:HALO, :] = jnp.where(has_prev, hp_ref[0], jnp.zeros_like(hp_ref[0]))
    hext_scr[HALO:HALO + tm, :] = h_ref[0]
    hext_scr[HALO + tm:HALO + tm + HALO, :] = jnp.where(has_next, hn_ref[0], jnp.zeros_like(hn_ref[0]))
    acc_scr[...] = jnp.zeros_like(acc_scr)
    row8 = lax.broadcasted_iota(jnp.int32, (8, 1), 0)

    n_uslab = 2 * FFN_FC // LANES
    n_oslab = acc_scr.shape[0]
    half = rb // 2

    def split(n):
        if isinstance(n, int):
            return (n // n_chunks) * sub, n % n_chunks
        if n_sub == 1:
            return 0, n
        t = sum((n >= k * n_chunks).astype(jnp.int32) for k in range(1, n_sub))
        return pl.multiple_of(t * sub, sub), n - t * n_chunks

    def up(n, slot):
        row0, c = split(n)
        h_rows = hext_scr[pl.ds(row0, sub + 2 * HALO), :]
        u = jnp.dot(h_rows, wu_ref[c], preferred_element_type=F32)
        for s in range(n_uslab):
            u_scr[slot, s] = u[:, s * LANES:(s + 1) * LANES]

    def conv_down(n, slot):
        row0, c = split(n)
        cw = cw_ref[c]
        bias = cb_ref[c]
        for r0 in range(0, sub, rb):
            base = HALO + r0
            ys = []
            for s in range(n_uslab):
                ls = slice(s * LANES, (s + 1) * LANES)
                even = u_scr[slot, s, pl.ds(base, half, stride=2), :]
                odd = u_scr[slot, s, pl.ds(base + 1, half, stride=2), :]
                odd_before = u_scr[slot, s, pl.ds(base - 1, half, stride=2), :]
                even_after = u_scr[slot, s, pl.ds(base + 2, half, stride=2), :]
                if r0 % seq_len == 0 and (r0 > 0 or seq_len <= sub):
                    odd_before = jnp.concatenate(
                        [jnp.where(row8 == 0, 0.0, odd_before[0:8]), odd_before[8:]], axis=0)
                if (r0 + rb) % seq_len == 0 and (r0 + rb < sub or seq_len <= sub):
                    even_after = jnp.concatenate(
                        [even_after[:half - 8], jnp.where(row8 == 7, 0.0, even_after[half - 8:])], axis=0)
                w0, w1, w2, b = cw[0:1, ls], cw[1:2, ls], cw[2:3, ls], bias[:, ls]
                y_even = odd_before * w0 + even * w1 + odd * w2 + b
                y_odd = even * w0 + odd * w1 + even_after * w2 + b
                ys.append(jnp.concatenate([y_even, y_odd], axis=0))
            n_g = n_uslab // 2
            act = jnp.concatenate([ys[s] * _sigmoid(ys[s]) * ys[n_g + s] for s in range(n_g)], axis=1)
            dn = jnp.dot(act.astype(BF16), wd_ref[c], preferred_element_type=F32)
            for s in range(n_oslab):
                ls = slice(s * LANES, (s + 1) * LANES)
                acc_scr[s, pl.ds(row0 + r0, half, stride=2), :] += dn[0:half, ls]
                acc_scr[s, pl.ds(row0 + r0 + 1, half, stride=2), :] += dn[half:rb, ls]

    up(0, 0)

    def pair(k, carry):
        n = 2 * k
        up(n + 1, 1)
        conv_down(n, 0)
        up(n + 2, 0)
        conv_down(n + 1, 1)
        return carry

    lax.fori_loop(0, (n_items - 1) // 2, pair, 0)
    if n_items % 2 == 1:
        conv_down(n_items - 1, 0)
    else:
        up(n_items - 1, 1)
        conv_down(n_items - 2, 0)
        conv_down(n_items - 1, 1)

    gate2 = mod_ref[0, 5:6, :]
    ffn_out = jnp.concatenate([acc_scr[s] for s in range(n_oslab)], axis=1)
    out = x1_ref[0] + gate2 * ffn_out
    if final_norm:
        ms = jnp.mean(out * out, axis=-1, keepdims=True)
        out = out * lax.rsqrt(ms + EPS) * fw_ref[...]
    o_ref[0] = out


def _ffn(h2, x1, mod, wu_c, cw_c, cb_c, wd_c, layer, final_w, *, per_batch_mod, seq_len, final_norm):
    nb, L, d = x1.shape
    tm = min(FFN_TM, L)
    n_tiles = L // tm
    sub = min(FFN_SUB, tm)
    rb = min(FFN_RB, sub)
    assert seq_len & (seq_len - 1) == 0 and L % seq_len == 0 and tm % sub == 0
    assert seq_len % tm == 0 or (tm % seq_len == 0 and sub % seq_len == 0 and seq_len % rb == 0)
    n_chunks = wu_c.shape[1]
    hb = tm // HALO
    n_hblk = L // HALO
    mod_map = (lambda b, i: (b, 0, 0)) if per_batch_mod else (lambda b, i: (0, 0, 0))
    row_spec = lambda w: pl.BlockSpec((1, tm, w), lambda b, i: (b, i, 0))
    const3 = lambda s: pl.BlockSpec(s, lambda b, i: (0, 0, 0))
    return pl.pallas_call(
        functools.partial(_ffn_kernel, seq_len=seq_len, final_norm=final_norm),
        out_shape=jax.ShapeDtypeStruct((nb, L, d), F32),
        grid=(nb, n_tiles),
        in_specs=[row_spec(d),
                  pl.BlockSpec((1, HALO, d), lambda b, i: (b, jnp.maximum(i * hb - 1, 0), 0)),
                  pl.BlockSpec((1, HALO, d), lambda b, i: (b, jnp.minimum((i + 1) * hb, n_hblk - 1), 0)),
                  row_spec(d),
                  pl.BlockSpec((1, 6, d), mod_map),
                  pl.BlockSpec((None, n_chunks, d, 2 * FFN_FC), lambda b, i: (layer, 0, 0, 0)),
                  const3((n_chunks, 3, 2 * FFN_FC)),
                  const3((n_chunks, 1, 2 * FFN_FC)),
                  pl.BlockSpec((None, n_chunks, FFN_FC, d), lambda b, i: (layer, 0, 0, 0)),
                  pl.BlockSpec((1, d), lambda b, i: (0, 0))],
        out_specs=row_spec(d),
        scratch_shapes=[pltpu.VMEM((tm + 2 * HALO, d), BF16),
                        pltpu.VMEM((2, 2 * FFN_FC // LANES, sub + 2 * HALO, LANES), F32),
                        pltpu.VMEM((d // LANES, tm, LANES), F32)],
        compiler_params=_cparams(2),
        name="ffn_final" if final_norm else "ffn",
    )(h2, h2, h2, x1, mod, wu_c, cw_c, cb_c, wd_c, final_w)


def _rope_tables(L, d):
    rows = L // GRID_W
    row = jnp.repeat(jnp.arange(rows), GRID_W).astype(F32)
    col = jnp.tile(jnp.arange(GRID_W), rows).astype(F32)
    quarter = d // 4
    inv = ROPE_THETA ** (-jnp.arange(quarter, dtype=F32) / quarter)
    ang_r = row[:, None] * inv[None, :]
    ang_c = col[:, None] * inv[None, :]
    ang = jnp.concatenate([ang_r, ang_r, ang_c, ang_c], axis=-1)
    cos, sin = jnp.cos(ang), jnp.sin(ang)
    even = ((jnp.arange(d) // quarter) % 2 == 0)[None, :]
    s_up = jnp.where(even, -sin, 0.0)
    s_dn = jnp.where(even, 0.0, sin)
    reps = LANES // d
    return tuple(jnp.tile(t, (1, reps)) for t in (cos, s_up, s_dn))


def _pad_lanes(v, width=LANES):
    v = v.reshape(1, -1).astype(F32)
    return jnp.pad(v, ((0, 0), (0, width - v.shape[1])))


def _layer_params(l, a_q_norm, a_k_norm, b_lambda, b_subln, ssm_conv_w, ssm_conv_b, ssm_A_log,
                  ssm_dt_bias, ssm_D, ssm_norm_w, ffn_conv_w, ffn_conv_b):
    n_chunks = D_FF // FFN_FC

    def gv_chunks(t):
        lead = t.shape[:-1]
        g = t[..., :D_FF].reshape(lead + (n_chunks, FFN_FC))
        v = t[..., D_FF:].reshape(lead + (n_chunks, FFN_FC))
        gv = jnp.concatenate([g, v], axis=-1)
        return jnp.moveaxis(gv, -2, 0)

    return dict(
        qn=jnp.tile(a_q_norm[l], 2).reshape(1, LANES),
        kn=jnp.tile(a_k_norm[l], 2).reshape(1, LANES),
        lamv=b_lambda[l],
        subln=jnp.tile(b_subln[l], 2).reshape(1, LANES),
        conv_w=ssm_conv_w[l],
        conv_b=ssm_conv_b[l].reshape(1, C_CONV_CH),
        dtb=_pad_lanes(ssm_dt_bias[l]),
        alog=_pad_lanes(ssm_A_log[l]),
        dexp=jnp.repeat(ssm_D[l], C_HEADDIM).reshape(1, C_INNER),
        norm_w=ssm_norm_w[l].reshape(1, C_INNER),
        cw=gv_chunks(ffn_conv_w[l]),
        cb=gv_chunks(ffn_conv_b[l].reshape(1, 2 * D_FF)),
    )


def _block(x, mod, p, wts, rope_tabs, caches, state, layer, *, seq_shape, per_batch_mod, is_ctx, lam_init,
           final_w, final_norm, new_caches=None):
    n_seq, seq_len = seq_shape
    depth = wts["w_in"].shape[0]
    outs = _proj(x, mod, wts["w_in"], layer, p["qn"], p["kn"], rope_tabs,
                 per_batch_mod=per_batch_mod, kv_f32=is_ctx, seq_len=seq_len, values_t=caches is not None,
                 kv_prev=None if new_caches is None else new_caches[:4])
    seq = lambda t: t.reshape(n_seq, seq_len, t.shape[-1]) if t.ndim == 3 else t
    qa, k3, va, qb, kb, vb, xbc, z, dt = [seq(t) for t in outs[:9]]
    if caches is None:
        a_out, b_out = _attn_rows(qa, k3, va, qb, kb, vb, p["lamv"], p["subln"], lam_init=lam_init)
    else:
        a_out, b_out = _attn(qa, k3, va, qb, kb, vb, p["lamv"], p["subln"], caches, layer, lam_init=lam_init)
    ssd_out = _ssd(xbc, z, dt, p["conv_w"], p["conv_b"], p["dtb"], p["alog"], p["dexp"], p["norm_w"],
                   state, layer, emit_state=is_ctx, depth=depth,
                   st_prev=None if new_caches is None else new_caches[4])
    c_out = ssd_out[0]
    flat = lambda t: t.reshape(x.shape[0], x.shape[1], t.shape[-1])
    x1, h2 = _outp(flat(a_out), flat(b_out), flat(c_out), x, mod, wts["w_out"], layer,
                   per_batch_mod=per_batch_mod)
    x2 = _ffn(h2, x1, mod, wts["wu"], p["cw"], p["cb"], wts["wd"], layer, final_w,
              per_batch_mod=per_batch_mod, seq_len=seq_len, final_norm=final_norm)
    extras = None
    if is_ctx:
        extras = tuple(outs[9:13]) + (ssd_out[1],)
    return x2, extras


def kernel(x_prompt, x_sample, cache_a_k, cache_a_v, cache_b_k, cache_b_v, state_ssm, c, c_ctx, ada_w, ada_b, w_in, a_q_norm, a_k_norm, b_lambda, b_subln, ssm_conv_w, ssm_conv_b, ssm_A_log, ssm_dt_bias, ssm_D, ssm_norm_w, w_out, ffn_up, ffn_conv_w, ffn_conv_b, ffn_down, final_norm_w):
    depth = w_in.shape[0]
    nbp, lp, d = x_prompt.shape
    nbs, ls, _ = x_sample.shape
    past = cache_a_k.shape[2]

    mod_rows = 8
    assert 1 + nbs <= mod_rows
    cvecs = jnp.concatenate([c_ctx[None, :], c, jnp.zeros((mod_rows - 1 - nbs, d), F32)], axis=0)
    mod = _modulation(cvecs, ada_w, ada_b)

    rope_tabs = _rope_tables(ls, HEAD_DIM) + _rope_tables(ls, B_HALF)
    caches = (cache_a_k.reshape(nbs, depth, past, A_KV_HEADS * HEAD_DIM),
              cache_a_v.reshape(nbs, depth, past, A_KV_HEADS * HEAD_DIM),
              cache_b_k.reshape(nbs, depth, past, B_WIDTH),
              cache_b_v.reshape(nbs, depth, past, B_WIDTH))
    final_w = final_norm_w.reshape(1, d)
    n_chunks = D_FF // FFN_FC
    wts = dict(
        w_in=_pad_cast_bf16(w_in, PROJ_PAD),
        w_out=_cast_bf16(w_out),
        wu=_gv_cast_bf16(ffn_up),
        wd=_cast_bf16(ffn_down).reshape(depth, n_chunks, FFN_FC, d),
    )

    yp = x_prompt.reshape(1, nbp * lp, d)
    ys = x_sample
    new_caches = None
    for l in range(depth):
        lam_init = 0.8 - 0.6 * math.exp(-0.3 * l)
        p = _layer_params(l, a_q_norm, a_k_norm, b_lambda, b_subln, ssm_conv_w, ssm_conv_b,
                          ssm_A_log, ssm_dt_bias, ssm_D, ssm_norm_w, ffn_conv_w, ffn_conv_b)
        last = l == depth - 1
        mod_ctx = mod[l, 0].reshape(1, 6, d)
        mod_lat = mod[l, 1:1 + nbs].reshape(nbs, 6, d)
        yp, new_caches = _block(yp, mod_ctx, p, wts, None, None, None, l, seq_shape=(nbp, lp),
                                per_batch_mod=False, is_ctx=True, lam_init=lam_init,
                                final_w=final_w, final_norm=last, new_caches=new_caches)
        ys, _ = _block(ys, mod_lat, p, wts, rope_tabs, caches, state_ssm, l, seq_shape=(nbs, ls),
                       per_batch_mod=True, is_ctx=False, lam_init=lam_init,
                       final_w=final_w, final_norm=last)

    y_prompt = yp.reshape(nbp, lp, d)
    new_a_k = new_caches[0].reshape(nbp, depth, lp, A_KV_HEADS, HEAD_DIM)
    new_a_v = new_caches[1].reshape(nbp, depth, lp, A_KV_HEADS, HEAD_DIM)
    new_b_k = new_caches[2].reshape(nbp, depth, lp, B_HEADS, 2, B_HALF)
    new_b_v = new_caches[3].reshape(nbp, depth, lp, B_HEADS, 2 * B_HALF)
    new_ssm = new_caches[4]
    return (y_prompt, ys, new_a_k, new_a_v, new_b_k, new_b_v, new_ssm)
```

```python
import functools
import math

import jax
import jax.numpy as jnp
from jax import lax
from jax.experimental import pallas as pl
from jax.experimental.pallas import tpu as pltpu

F32 = jnp.float32
BF16 = jnp.bfloat16

LANES = 128
VMEM_LIMIT_BYTES = 56 * 1024 * 1024

D_MODEL = 1024
GRID_W = 64
SSD_CHUNK = 128
ROPE_THETA = 10000.0
EPS = 1e-6
HEAD_DIM = 64
A_Q_HEADS = 6
A_KV_HEADS = 2
A_GROUP = A_Q_HEADS // A_KV_HEADS
A_WIDTH = A_Q_HEADS * HEAD_DIM
B_HEADS = 4
B_HALF = 32
B_WIDTH = B_HEADS * 2 * B_HALF
C_HEADS = 6
C_HEADDIM = 64
C_INNER = C_HEADS * C_HEADDIM
C_GROUPS = 2
C_STATE = 128
C_CONV_CH = C_INNER + 2 * C_GROUPS * C_STATE
D_FF = 2816
PROJ_WIDTH = A_WIDTH + 4 * HEAD_DIM + 3 * B_WIDTH + 2 * C_INNER + 4 * C_STATE + 2 * C_HEADS
PROJ_PAD = ((PROJ_WIDTH + 2 * LANES - 1) // (2 * LANES)) * (2 * LANES)

OFF_AQ = 0
OFF_AK = OFF_AQ + A_WIDTH
OFF_AV = OFF_AK + A_KV_HEADS * HEAD_DIM
OFF_BQ = OFF_AV + A_KV_HEADS * HEAD_DIM
OFF_BK = OFF_BQ + B_WIDTH
OFF_BV = OFF_BK + B_WIDTH
OFF_CX = OFF_BV + B_WIDTH
OFF_CZ = OFF_CX + C_INNER
OFF_CB = OFF_CZ + C_INNER
OFF_CC = OFF_CB + C_GROUPS * C_STATE
OFF_DT = OFF_CC + C_GROUPS * C_STATE

MOD_BLOCK = 1024
PROJ_TM = 512
OUTP_TM = 1024
FFN_TM = 512
FFN_SUB = 512
FFN_FC = 256
FFN_RB = 256
ATTN_TQ = 512
ATTN_KC = 512
ATTN_AHEAD = 2
ROW_BLOCK = 256
CONV_ROWS = 128
SSD_UNROLL = 4
SSD_STEP_ROWS = 512
NEG_BIG = -1e30
LOG2E = 1.4426950408889634


def _cparams(n_grid):
    return pltpu.CompilerParams(
        dimension_semantics=("parallel",) * n_grid,
        vmem_limit_bytes=VMEM_LIMIT_BYTES,
    )


def _sigmoid(x):
    return 1.0 / (1.0 + jnp.exp(-x))


def _left_mask():
    lane = lax.broadcasted_iota(jnp.int32, (1, LANES), 1)
    return lane < (LANES // 2)


def _pair_rmsnorm(y, w):
    left = _left_mask()
    sq = y * y
    s_l = jnp.sum(jnp.where(left, sq, 0.0), axis=-1, keepdims=True)
    s_r = jnp.sum(jnp.where(left, 0.0, sq), axis=-1, keepdims=True)
    ms = jnp.where(left, s_l, s_r) * (1.0 / HEAD_DIM)
    return y * lax.rsqrt(ms + EPS) * w


def _rope(y, cos, s_up, s_dn, quarter):
    return (y * cos
            + pltpu.roll(y, LANES - quarter, 1) * s_up
            + pltpu.roll(y, quarter, 1) * s_dn)


def _modulation_kernel(c_ref, w_ref, b_ref, o_ref):
    c = c_ref[...]
    s = c * _sigmoid(c)
    o_ref[0] = jnp.dot(s.astype(BF16), w_ref[0].astype(BF16),
                       preferred_element_type=F32) + b_ref[0]


def _modulation(cvecs, ada_w, ada_b):
    depth, d, width = ada_w.shape
    rows = cvecs.shape[0]
    return pl.pallas_call(
        _modulation_kernel,
        out_shape=jax.ShapeDtypeStruct((depth, rows, width), F32),
        grid=(depth, width // MOD_BLOCK),
        in_specs=[
            pl.BlockSpec((rows, d), lambda l, j: (0, 0)),
            pl.BlockSpec((1, d, MOD_BLOCK), lambda l, j: (l, 0, j)),
            pl.BlockSpec((1, 1, MOD_BLOCK), lambda l, j: (l, 0, j)),
        ],
        out_specs=pl.BlockSpec((1, rows, MOD_BLOCK), lambda l, j: (l, 0, j)),
        compiler_params=_cparams(2),
        name="modulation",
    )(cvecs, ada_w, ada_b.reshape(depth, 1, width))


CAST_ROWS = 256
PAD_CAST_COLS = 256


def _cast_kernel(x_ref, o_ref):
    o_ref[...] = x_ref[...].astype(o_ref.dtype)


def _cast_bf16(x):
    depth, rows, cols = x.shape
    spec = pl.BlockSpec((1, CAST_ROWS, cols), lambda l, r: (l, r, 0))
    return pl.pallas_call(
        _cast_kernel, out_shape=jax.ShapeDtypeStruct(x.shape, BF16),
        grid=(depth, rows // CAST_ROWS), in_specs=[spec], out_specs=spec,
        compiler_params=_cparams(2), name="cast_bf16",
    )(x)


def _pad_cast_kernel(x_ref, o_ref, *, n_cols):
    tr = x_ref.shape[0]
    col = lax.broadcasted_iota(jnp.int32, (tr, 1), 0) + pl.program_id(0) * tr
    for l in range(x_ref.shape[1]):
        x = jnp.where(col < n_cols, x_ref[:, l, :], 0.0)
        o_ref[l] = x.T.astype(BF16)


def _pad_cast_bf16(x, width):
    depth, rows, cols = x.shape
    return pl.pallas_call(
        functools.partial(_pad_cast_kernel, n_cols=cols),
        out_shape=jax.ShapeDtypeStruct((depth, rows, width), BF16),
        grid=(width // PAD_CAST_COLS,),
        in_specs=[pl.BlockSpec((PAD_CAST_COLS, depth, rows), lambda i: (i, 0, 0))],
        out_specs=pl.BlockSpec((depth, rows, PAD_CAST_COLS), lambda i: (0, 0, i)),
        compiler_params=_cparams(1), name="pad_cast_bf16",
    )(jnp.transpose(x, (2, 0, 1)))


def _gv_cast_kernel(g_ref, v_ref, o_ref):
    o_ref[0, 0, :, 0:FFN_FC] = g_ref[0].astype(BF16)
    o_ref[0, 0, :, FFN_FC:2 * FFN_FC] = v_ref[0].astype(BF16)


def _gv_cast_bf16(up):
    depth, d, _ = up.shape
    n_chunks = D_FF // FFN_FC
    return pl.pallas_call(
        _gv_cast_kernel, out_shape=jax.ShapeDtypeStruct((depth, n_chunks, d, 2 * FFN_FC), BF16),
        grid=(depth, n_chunks),
        in_specs=[pl.BlockSpec((1, d, FFN_FC), lambda l, c: (l, 0, c)),
                  pl.BlockSpec((1, d, FFN_FC), lambda l, c: (l, 0, n_chunks + c))],
        out_specs=pl.BlockSpec((1, 1, d, 2 * FFN_FC), lambda l, c: (l, c, 0, 0)),
        compiler_params=_cparams(2), name="gv_cast_bf16",
    )(up, up)


def _proj_kernel(*refs, rope, kv_f32, n_alias, cache_slot):
    x_ref, mod_ref, w_ref, qn_ref, kn_ref = refs[:5]
    pos = 5
    if rope:
        ca_ref, sau_ref, sad_ref, cb_ref, sbu_ref, sbd_ref = refs[pos:pos + 6]
        pos += 6
    pos += n_alias
    (qa_o, k3_o, vta_o, qb_o, kb_o, vtb_o, xbc_o, z_o, dt_o) = refs[pos:pos + 9]
    pos += 9
    if kv_f32:
        ka32_o, va32_o, kb32_o, vb32_o = refs[pos:pos + 4]
        pos += 4
    p_scr = refs[pos]

    x = x_ref[0]
    ms = jnp.mean(x * x, axis=-1, keepdims=True)
    shift = mod_ref[0, 0:1, :]
    scale = mod_ref[0, 1:2, :]
    h = (x * lax.rsqrt(ms + EPS)) * (1.0 + scale) + shift
    p_scr[...] = jnp.dot(h.astype(BF16), w_ref[...], preferred_element_type=F32)

    left = _left_mask()

    def rope_a(y):
        if not rope:
            return y
        return _rope(y, ca_ref[...], sau_ref[...], sad_ref[...], HEAD_DIM // 4)

    def rope_b(y):
        if not rope:
            return y
        return _rope(y, cb_ref[...], sbu_ref[...], sbd_ref[...], B_HALF // 4)

    values_t = len(vta_o.shape) == 4
    if values_t:
        n_sub, sub_len = vta_o.shape[0], vta_o.shape[3]
        ones_half = jnp.ones((HEAD_DIM, sub_len), BF16)

    def store_vt(dst, h, vt_half, e):
        for s in range(n_sub):
            dst[s, h, e * HEAD_DIM:(e + 1) * HEAD_DIM, :] = vt_half[:, s * sub_len:(s + 1) * sub_len]
            dst[s, h, (1 - e) * HEAD_DIM:(2 - e) * HEAD_DIM, :] = ones_half

    a_scale = HEAD_DIM ** -0.5 * LOG2E
    for j in range(A_WIDTH // LANES):
        y = p_scr[:, OFF_AQ + j * LANES:OFF_AQ + (j + 1) * LANES]
        y = rope_a(_pair_rmsnorm(y, qn_ref[...]))
        qa_o[0, :, j * LANES:(j + 1) * LANES] = (y * a_scale).astype(BF16)

    k = rope_a(_pair_rmsnorm(p_scr[:, OFF_AK:OFF_AK + LANES], kn_ref[...]))
    v = p_scr[:, OFF_AV:OFF_AV + LANES]
    if kv_f32:
        kv_sub, kv_len = ka32_o.shape[0], ka32_o.shape[2]
        for u in range(kv_sub):
            ka32_o[u, cache_slot] = k[u * kv_len:(u + 1) * kv_len]
            va32_o[u, cache_slot] = v[u * kv_len:(u + 1) * kv_len]
        for ref in (ka32_o, va32_o, kb32_o, vb32_o):
            for other in range(ref.shape[1]):
                if other != cache_slot:
                    ref[:, other] = jnp.zeros((kv_sub, kv_len, ref.shape[3]), F32)
    swapped = pltpu.roll(k, LANES // 2, 1)
    k3_o[0, :, 0:LANES] = jnp.where(left, k, swapped).astype(BF16)
    k3_o[0, :, LANES:2 * LANES] = k.astype(BF16)
    k3_o[0, :, 2 * LANES:3 * LANES] = jnp.where(left, swapped, k).astype(BF16)
    if values_t:
        vt = v.T.astype(BF16)
        for h in range(A_Q_HEADS):
            g = h // A_GROUP
            store_vt(vta_o, h, vt[g * HEAD_DIM:(g + 1) * HEAD_DIM], h % 2)
    else:
        v_sw = pltpu.roll(v, LANES // 2, 1)
        vta_o[0, :, 0:LANES] = jnp.where(left, v, v_sw).astype(BF16)
        vta_o[0, :, LANES:2 * LANES] = v.astype(BF16)
        vta_o[0, :, 2 * LANES:3 * LANES] = jnp.where(left, v_sw, v).astype(BF16)

    b_scale = B_HALF ** -0.5 * LOG2E
    for j in range(B_WIDTH // LANES):
        sl = slice(j * LANES, (j + 1) * LANES)
        q = rope_b(p_scr[:, OFF_BQ + j * LANES:OFF_BQ + (j + 1) * LANES])
        qb_o[0, :, sl] = (q * b_scale).astype(BF16)
        kb = rope_b(p_scr[:, OFF_BK + j * LANES:OFF_BK + (j + 1) * LANES])
        vb = p_scr[:, OFF_BV + j * LANES:OFF_BV + (j + 1) * LANES]
        kb_o[0, :, sl] = kb.astype(BF16)
        if values_t:
            vbt = vb.T.astype(BF16)
            for e in range(2):
                store_vt(vtb_o, 2 * j + e, vbt[e * HEAD_DIM:(e + 1) * HEAD_DIM], e)
        else:
            vtb_o[0, :, sl] = vb.astype(BF16)
        if kv_f32:
            for u in range(kv_sub):
                kb32_o[u, cache_slot, :, sl] = kb[u * kv_len:(u + 1) * kv_len]
                vb32_o[u, cache_slot, :, sl] = vb[u * kv_len:(u + 1) * kv_len]

    xbc_sub, xbc_len = xbc_o.shape[0], xbc_o.shape[2]
    for s in range(C_CONV_CH // LANES):
        col = OFF_CX + s * LANES if s < C_INNER // LANES else OFF_CB + s * LANES - C_INNER
        for u in range(xbc_sub):
            xbc_o[u, s] = p_scr[u * xbc_len:(u + 1) * xbc_len, col:col + LANES]
    z_o[0] = p_scr[:, OFF_CZ:OFF_CZ + C_INNER]
    dt_o[0] = p_scr[:, OFF_DT:OFF_DT + LANES]


def _proj(x, mod, w_in_p, layer, qn, kn, rope_tabs, *, per_batch_mod, kv_f32, seq_len, values_t,
          kv_prev=None):
    depth = w_in_p.shape[0]
    nb, L, d = x.shape
    tm = min(PROJ_TM, L)
    rope = rope_tabs is not None
    mod_map = (lambda b, i: (b, 0, 0)) if per_batch_mod else (lambda b, i: (0, 0, 0))
    row_spec = lambda w: pl.BlockSpec((1, tm, w), lambda b, i: (b, i, 0))
    in_specs = [
        row_spec(d),
        pl.BlockSpec((1, 6, d), mod_map),
        pl.BlockSpec((None, d, PROJ_PAD), lambda b, i: (layer, 0, 0)),
        pl.BlockSpec((1, LANES), lambda b, i: (0, 0)),
        pl.BlockSpec((1, LANES), lambda b, i: (0, 0)),
    ]
    args = [x, mod, w_in_p, qn, kn]
    if rope:
        in_specs += [pl.BlockSpec((tm, LANES), lambda b, i: (i, 0))] * 6
        args += list(rope_tabs)
    widths = [(A_WIDTH, BF16), (A_WIDTH, BF16), (-A_Q_HEADS if values_t else A_WIDTH, BF16),
              (B_WIDTH, BF16), (B_WIDTH, BF16), (-B_HEADS if values_t else B_WIDTH, BF16),
              ("slabs", F32), (C_INNER, F32), (LANES, F32)]
    kv_widths = [LANES, LANES, B_WIDTH, B_WIDTH] if kv_f32 else []
    widths += [("cache", w) for w in kv_widths]
    aliases = {}
    if kv_prev is not None:
        for k_idx, prev in enumerate(kv_prev):
            aliases[len(args)] = len(widths) - len(kv_widths) + k_idx
            in_specs.append(pl.BlockSpec(memory_space=pl.ANY))
            args.append(prev)
    out_shape, out_specs = [], []
    n_seq = nb * L // seq_len
    in_seq = seq_len >= tm
    assert (L == seq_len) if in_seq else (nb == 1 and tm % seq_len == 0)
    n_slab = C_CONV_CH // LANES
    for w, dt in widths:
        if w == "slabs":
            out_shape.append(jax.ShapeDtypeStruct((n_seq, n_slab, seq_len, LANES), dt))
            out_specs.append(pl.BlockSpec((1, n_slab, tm, LANES), lambda b, i: (b, 0, i, 0)) if in_seq else
                             pl.BlockSpec((tm // seq_len, n_slab, seq_len, LANES), lambda b, i: (i, 0, 0, 0)))
        elif w == "cache":
            assert not in_seq
            out_shape.append(jax.ShapeDtypeStruct((n_seq, depth, seq_len, dt), F32))
            out_specs.append(
                pl.BlockSpec((tm // seq_len, depth, seq_len, dt), lambda b, i: (i, 0, 0, 0)) if kv_prev is None
                else pl.BlockSpec((tm // seq_len, 1, seq_len, dt), lambda b, i: (i, layer, 0, 0)))
        elif w > 0:
            out_shape.append(jax.ShapeDtypeStruct((nb, L, w), dt))
            out_specs.append(row_spec(w))
        else:
            out_shape.append(jax.ShapeDtypeStruct((n_seq, -w, LANES, seq_len), dt))
            out_specs.append(pl.BlockSpec((1, -w, LANES, tm), lambda b, i: (b, 0, 0, i)) if in_seq else
                             pl.BlockSpec((tm // seq_len, -w, LANES, seq_len), lambda b, i: (i, 0, 0, 0)))
    return pl.pallas_call(
        functools.partial(_proj_kernel, rope=rope, kv_f32=kv_f32, n_alias=len(aliases),
                          cache_slot=layer if kv_prev is None else 0),
        out_shape=out_shape,
        grid=(nb, L // tm),
        in_specs=in_specs,
        out_specs=out_specs,
        input_output_aliases=aliases,
        scratch_shapes=[pltpu.VMEM((tm, PROJ_PAD), F32)],
        compiler_params=_cparams(2),
        name="proj_rope" if rope else "proj",
    )(*args)


class _ScoreMap:
    def __init__(self, tag, qm, chunks, s_ref):
        self.tag, self.qm, self.chunks, self.s_ref = tag, qm, chunks, s_ref
        self.offsets = [sum(w for _, _, w in chunks[:c]) for c in range(len(chunks))]
        self.m_part = self.m = self.acc = None

    def pass1(self, c):
        k_fn, _, w = self.chunks[c]
        tq = self.qm.shape[0]
        s = lax.dot_general(k_fn(), self.qm, (((1,), (1,)), ((), ())), preferred_element_type=F32)
        self.s_ref[self.offsets[c]:self.offsets[c] + w, :] = s
        part = jnp.max(s.reshape(w // 8, 8, tq), axis=0)
        self.m_part = part if self.m_part is None else jnp.maximum(self.m_part, part)

    def finish_max(self):
        self.m = jnp.max(self.m_part, axis=0, keepdims=True)

    def pass2(self, c):
        _, vt_fn, w = self.chunks[c]
        e = jnp.exp2(self.s_ref[self.offsets[c]:self.offsets[c] + w, :] - self.m).astype(BF16)
        pv = jnp.dot(vt_fn(), e, preferred_element_type=F32)
        self.acc = pv if self.acc is None else self.acc + pv


def _run_score_maps(maps, on_done):
    for mp in maps[:ATTN_AHEAD]:
        for c in range(len(mp.chunks)):
            mp.pass1(c)
        mp.finish_max()
    for i, mp in enumerate(maps):
        nxt = maps[i + ATTN_AHEAD] if i + ATTN_AHEAD < len(maps) else None
        n_next = len(nxt.chunks) if nxt is not None else 0
        for c in range(max(len(mp.chunks), n_next)):
            if c < len(mp.chunks):
                mp.pass2(c)
            if c < n_next:
                nxt.pass1(c)
        if nxt is not None:
            nxt.finish_max()
        on_done(mp)


def _normalised_half(acc, e):
    l = acc[(1 - e) * HEAD_DIM:(1 - e) * HEAD_DIM + 1, :]
    return acc[e * HEAD_DIM:(e + 1) * HEAD_DIM, :] * (1.0 / l)


def _attn_kernel(*refs, has_cache, lam_init, seq_len):
    qa_ref, k3_ref, vta_ref, qb_ref, kb_ref, vtb_ref, lamv_ref, subln_ref = refs[:8]
    pos = 8
    if has_cache:
        cka_ref, cva_ref, ckb_ref, cvb_ref = refs[pos:pos + 4]
        pos += 4
    ao_ref, bo_ref, s_scr = refs[pos:pos + 3]

    left = _left_mask()
    lane = lax.broadcasted_iota(jnp.int32, (1, LANES), 1)
    kc = min(ATTN_KC, seq_len)
    n_kc = seq_len // kc
    maps = []

    def add_map(tag, qm, chunks):
        maps.append(_ScoreMap(tag, qm, chunks, s_scr.at[len(maps) % (ATTN_AHEAD + 1)]))

    def new_chunks(k_ref, vt_ref, j, h):
        sl = slice(j * LANES, (j + 1) * LANES)
        out = []
        for c in range(n_kc):
            rows = slice(c * kc, (c + 1) * kc)
            out.append((functools.partial(lambda r, s: k_ref[0, r, s], rows, sl),
                        functools.partial(lambda r, hh: vt_ref[0, hh, :, r], rows, h), kc))
        return out

    def value_block(vt_half, e):
        ones = jnp.ones_like(vt_half)
        return jnp.concatenate([vt_half, ones] if e == 0 else [ones, vt_half], axis=0).astype(BF16)

    if has_cache:
        ck = cka_ref[0, 0]
        ck_sw = pltpu.roll(ck, LANES // 2, 1)
        ck3 = [jnp.where(left, ck, ck_sw), ck, jnp.where(left, ck_sw, ck)]
        cvt = cva_ref[0, 0].T
        past = ck.shape[0]
    for j in range(A_WIDTH // LANES):
        q = qa_ref[0, :, j * LANES:(j + 1) * LANES]
        for e in range(2):
            h = 2 * j + e
            g = h // A_GROUP
            chunks = new_chunks(k3_ref, vta_ref, j, h)
            if has_cache:
                ckj = ck3[j].astype(BF16)
                cvj = value_block(cvt[g * HEAD_DIM:(g + 1) * HEAD_DIM], e)
                chunks.append((lambda a=ckj: a, lambda a=cvj: a, past))
            qm = jnp.where(left if e == 0 else jnp.logical_not(left), q, jnp.zeros_like(q))
            add_map(("a", j, e, 0), qm, chunks)

    lv = lamv_ref[...]
    lam = (jnp.exp(jnp.sum(lv[0:1] * lv[1:2], axis=-1, keepdims=True))
           - jnp.exp(jnp.sum(lv[2:3] * lv[3:4], axis=-1, keepdims=True)) + lam_init)
    for j in range(B_WIDTH // LANES):
        sl = slice(j * LANES, (j + 1) * LANES)
        q = qb_ref[0, :, sl]
        if has_cache:
            ckj = ckb_ref[0, 0, :, sl].astype(BF16)
            cvt_j = cvb_ref[0, 0, :, sl].T
        for e in range(2):
            h = 2 * j + e
            chunks = new_chunks(kb_ref, vtb_ref, j, h)
            if has_cache:
                cvj = value_block(cvt_j[e * HEAD_DIM:(e + 1) * HEAD_DIM], e)
                chunks.append((lambda a=ckj: a, lambda a=cvj: a, ckj.shape[0]))
            for mi in range(2):
                sel = (lane // B_HALF) == (2 * e + mi)
                add_map(("b", j, e, mi), jnp.where(sel, q, jnp.zeros_like(q)), chunks)

    done = {}

    def on_done(mp):
        kind, j, e, mi = mp.tag
        done[mp.tag] = _normalised_half(mp.acc, e)
        if kind == "a" and e == 1:
            o_t = jnp.concatenate([done[("a", j, 0, 0)], done[("a", j, 1, 0)]], axis=0)
            ao_ref[0, :, j * LANES:(j + 1) * LANES] = o_t.T.astype(BF16)
        if kind == "b" and e == 1 and mi == 1:
            halves = []
            for ee in range(2):
                o = done[("b", j, ee, 0)] - lam * done[("b", j, ee, 1)]
                ms = jnp.mean(o * o, axis=0, keepdims=True)
                halves.append(o * lax.rsqrt(ms + EPS))
            o_t = jnp.concatenate(halves, axis=0)
            bo_ref[0, :, j * LANES:(j + 1) * LANES] = (
                o_t.T * subln_ref[...] * (1.0 - lam_init)).astype(BF16)

    _run_score_maps(maps, on_done)


def _softmax_pv_rows(qm, k, v):
    s = lax.dot_general(qm, k, (((1,), (1,)), ((), ())), preferred_element_type=F32)
    m_part = s[:, 0:LANES]
    for t in range(1, s.shape[1] // LANES):
        m_part = jnp.maximum(m_part, s[:, t * LANES:(t + 1) * LANES])
    e = jnp.exp2(s - jnp.max(m_part, axis=-1, keepdims=True))
    l_part = e[:, 0:LANES]
    for t in range(1, s.shape[1] // LANES):
        l_part = l_part + e[:, t * LANES:(t + 1) * LANES]
    acc = jnp.dot(e.astype(BF16), v, preferred_element_type=F32)
    return acc, jnp.sum(l_part, axis=-1, keepdims=True)


def _attn_rows_kernel(qa_ref, k3_ref, v3_ref, qb_ref, kb_ref, vb_ref, lamv_ref, subln_ref,
                      ao_ref, bo_ref, *, lam_init):
    left = _left_mask()
    lane = lax.broadcasted_iota(jnp.int32, (1, LANES), 1)
    for j in range(A_WIDTH // LANES):
        sl = slice(j * LANES, (j + 1) * LANES)
        q = qa_ref[0, :, sl]
        halves = []
        for e in range(2):
            qm = jnp.where(left if e == 0 else jnp.logical_not(left), q, jnp.zeros_like(q))
            acc, l = _softmax_pv_rows(qm, k3_ref[0, :, sl], v3_ref[0, :, sl])
            halves.append(acc * (1.0 / l))
        ao_ref[0, :, sl] = jnp.where(left, halves[0], halves[1]).astype(BF16)

    lv = lamv_ref[...]
    lam = (jnp.exp(jnp.sum(lv[0:1] * lv[1:2], axis=-1, keepdims=True))
           - jnp.exp(jnp.sum(lv[2:3] * lv[3:4], axis=-1, keepdims=True)) + lam_init)
    for j in range(B_WIDTH // LANES):
        sl = slice(j * LANES, (j + 1) * LANES)
        q = qb_ref[0, :, sl]
        halves = []
        for e in range(2):
            maps = []
            for mi in range(2):
                sel = (lane // B_HALF) == (2 * e + mi)
                acc, l = _softmax_pv_rows(jnp.where(sel, q, jnp.zeros_like(q)), kb_ref[0, :, sl], vb_ref[0, :, sl])
                maps.append(acc * (1.0 / l))
            halves.append(maps[0] - lam * maps[1])
        o = jnp.where(left, halves[0], halves[1])
        bo_ref[0, :, sl] = (_pair_rmsnorm(o, subln_ref[...]) * (1.0 - lam_init)).astype(BF16)


def _attn_rows(qa, k3, v3, qb, kb, vb, lamv, subln, *, lam_init):
    nb, L, _ = qa.shape
    spec = lambda w: pl.BlockSpec((1, L, w), lambda b: (b, 0, 0))
    return pl.pallas_call(
        functools.partial(_attn_rows_kernel, lam_init=lam_init),
        out_shape=[jax.ShapeDtypeStruct((nb, L, A_WIDTH), BF16),
                   jax.ShapeDtypeStruct((nb, L, B_WIDTH), BF16)],
        grid=(nb,),
        in_specs=[spec(A_WIDTH), spec(A_WIDTH), spec(A_WIDTH), spec(B_WIDTH), spec(B_WIDTH), spec(B_WIDTH),
                  pl.BlockSpec((4, B_HALF), lambda b: (0, 0)),
                  pl.BlockSpec((1, LANES), lambda b: (0, 0))],
        out_specs=[spec(A_WIDTH), spec(B_WIDTH)],
        compiler_params=_cparams(1),
        name="attn",
    )(qa, k3, v3, qb, kb, vb, lamv, subln)


def _attn(qa, k3, vta, qb, kb, vtb, lamv, subln, caches, layer, *, lam_init):
    nb, L, _ = qa.shape
    tq = min(ATTN_TQ, L)
    has_cache = caches is not None
    q_spec = lambda w: pl.BlockSpec((1, tq, w), lambda b, i: (b, i, 0))
    kv_spec = lambda w: pl.BlockSpec((1, L, w), lambda b, i: (b, 0, 0))
    vt_spec = lambda n: pl.BlockSpec((1, n, LANES, L), lambda b, i: (b, 0, 0, 0))
    in_specs = [q_spec(A_WIDTH), kv_spec(A_WIDTH), vt_spec(A_Q_HEADS),
                q_spec(B_WIDTH), kv_spec(B_WIDTH), vt_spec(B_HEADS),
                pl.BlockSpec((4, B_HALF), lambda b, i: (0, 0)),
                pl.BlockSpec((1, LANES), lambda b, i: (0, 0))]
    args = [qa, k3, vta, qb, kb, vtb, lamv, subln]
    lk = L
    if has_cache:
        past = caches[0].shape[2]
        lk += past
        for c in caches:
            in_specs.append(pl.BlockSpec((1, 1, past, c.shape[-1]), lambda b, i: (b, layer, 0, 0)))
            args.append(c)
    return pl.pallas_call(
        functools.partial(_attn_kernel, has_cache=has_cache, lam_init=lam_init, seq_len=L),
        out_shape=[jax.ShapeDtypeStruct((nb, L, A_WIDTH), BF16),
                   jax.ShapeDtypeStruct((nb, L, B_WIDTH), BF16)],
        grid=(nb, L // tq),
        in_specs=in_specs,
        out_specs=[q_spec(A_WIDTH), q_spec(B_WIDTH)],
        scratch_shapes=[pltpu.VMEM((ATTN_AHEAD + 1, lk, tq), F32)],
        compiler_params=_cparams(2),
        name="attn_cache" if has_cache else "attn",
    )(*args)


def _prefix_sums(tri, a):
    hi = a.astype(BF16)
    lo = (a - hi.astype(F32)).astype(BF16)
    return (jnp.dot(tri, hi, preferred_element_type=F32)
            + jnp.dot(tri, lo, preferred_element_type=F32))


def _pair_cols(x, c0):
    return jnp.where(_left_mask(), x[:, c0:c0 + 1], x[:, c0 + 1:c0 + 2])


def _ssd_kernel(*refs, has_init, emit_state, seq_len, n_alias, state_slot):
    xbc_ref, z_ref, dt_ref, cw_ref, cb_ref, dtb_ref, alog_ref, dexp_ref, nw_ref = refs[:9]
    pos = 9
    if has_init:
        init_ref = refs[pos]
        pos += 1
    pos += n_alias
    co_ref = refs[pos]
    pos += 1
    if emit_state:
        st_ref = refs[pos]
        pos += 1
    xc_scr, y_scr, st_scr = refs[pos:pos + 3]

    L = seq_len
    rb = min(ROW_BLOCK, L)
    q = SSD_CHUNK
    nc = L // q
    left = _left_mask()

    w0 = cw_ref[0:1, :]
    w1 = cw_ref[1:2, :]
    w2 = cw_ref[2:3, :]
    cb = CONV_ROWS
    half = cb // 2
    hrow = lax.broadcasted_iota(jnp.int32, (half, 1), 0)
    n_u = xbc_ref.shape[0]
    for u, s in [(u, s) for u in range(n_u) for s in range(C_CONV_CH // LANES)]:
        ls = slice(s * LANES, (s + 1) * LANES)
        w0s, w1s, w2s, bs = w0[:, ls], w1[:, ls], w2[:, ls], cb_ref[:, ls]
        for r0 in range(0, L, cb):
            even = xbc_ref[u, s, pl.ds(r0, half, stride=2), :]
            odd = xbc_ref[u, s, pl.ds(r0 + 1, half, stride=2), :]
            if r0 > 0:
                odd_before = xbc_ref[u, s, pl.ds(r0 - 1, half, stride=2), :]
            else:
                odd_before = jnp.where(hrow == 0, 0.0, pltpu.roll(odd, 1, 0))
            if r0 + cb < L:
                even_after = xbc_ref[u, s, pl.ds(r0 + 2, half, stride=2), :]
            else:
                even_after = jnp.where(hrow == half - 1, 0.0, pltpu.roll(even, half - 1, 0))
            y_even = odd_before * w0s + even * w1s + odd * w2s + bs
            y_odd = even * w0s + odd * w1s + even_after * w2s + bs
            xc_scr[u, s, pl.ds(r0, half, stride=2), :] = y_even * _sigmoid(y_even)
            xc_scr[u, s, pl.ds(r0 + 1, half, stride=2), :] = y_odd * _sigmoid(y_odd)
    for u, j in [(u, j) for u in range(n_u) for j in range(C_INNER // LANES)]:
        for r0 in range(0, L, rb):
            y_scr[u, r0:r0 + rb, j * LANES:(j + 1) * LANES] = (
                xc_scr[u, j, r0:r0 + rb, :] * dexp_ref[:, j * LANES:(j + 1) * LANES])

    for u, d in [(u, d) for u in range(n_u) for d in range(2)]:
        if has_init:
            st_scr[u, d] = init_ref[u, 0, d].reshape(C_INNER, C_STATE).T
        else:
            st_scr[u, d] = jnp.zeros((C_STATE, C_INNER), F32)

    a_neg = -jnp.exp(alog_ref[...])
    ti = lax.broadcasted_iota(jnp.int32, (q, q), 0)
    si = lax.broadcasted_iota(jnp.int32, (q, q), 1)
    tri_incl = (si <= ti).astype(BF16)
    src_lane = lax.broadcasted_iota(jnp.int32, (LANES, C_INNER), 0)
    dst_head = lax.broadcasted_iota(jnp.int32, (LANES, C_INNER), 1) // C_HEADDIM
    spread = [(src_lane == d * C_HEADS + dst_head).astype(BF16) for d in range(2)]

    def process(u, d, r0):
        rows = pl.ds(r0, q)
        lo = d * C_HEADS
        causal = (si <= ti) if d == 0 else (si >= ti)
        n_x = C_INNER // LANES
        bm = [xc_scr[u, n_x + g, rows, :] for g in range(C_GROUPS)]
        cm = [xc_scr[u, n_x + C_GROUPS + g, rows, :] for g in range(C_GROUPS)]
        bm_b = [b.astype(BF16) for b in bm]
        cm_b = [c.astype(BF16) for c in cm]
        bmt_b = [b.T.astype(BF16) for b in bm]
        gmat = [lax.dot_general(cm_b[g], bm_b[g], (((1,), (1,)), ((), ())), preferred_element_type=F32)
                for g in range(C_GROUPS)]

        dtr = dt_ref[u, rows, :] + dtb_ref[...]
        dt = jnp.maximum(dtr, 0.0) + jnp.log1p(jnp.exp(-jnp.abs(dtr)))
        a = dt * a_neg
        cs = _prefix_sums(tri_incl, a)
        tot = cs[q - 1:q, :]
        ev = cs if d == 0 else (tot - cs + a)
        e_in = jnp.exp(ev)
        e_out = jnp.exp(tot - ev)
        dec = jnp.exp(tot)
        ev_t = ev.T
        per_head = jnp.concatenate([dt, e_in, e_out], axis=0).astype(BF16)
        wide = jnp.dot(per_head, spread[d], preferred_element_type=F32)
        dt_w, e_in_w, e_out_w = wide[0:q], wide[q:2 * q], wide[2 * q:3 * q]

        st = st_scr.at[u, d]
        for j in range(C_INNER // LANES):
            sl = slice(j * LANES, (j + 1) * LANES)
            h0 = 2 * j
            g0, g1 = h0 // (C_HEADS // C_GROUPS), (h0 + 1) // (C_HEADS // C_GROUPS)
            xdt = xc_scr[u, j, rows, :] * dt_w[:, sl]
            xdt_b = xdt.astype(BF16)
            st_blk = st[:, sl]
            st_b = st_blk.astype(BF16)
            yd = []
            for e in range(2):
                hh = lo + h0 + e
                g = g0 if e == 0 else g1
                diff = ev[:, hh:hh + 1] - ev_t[hh:hh + 1, :]
                lmat = jnp.exp(jnp.where(causal, diff, NEG_BIG))
                yd.append(jnp.dot((gmat[g] * lmat).astype(BF16), xdt_b, preferred_element_type=F32))
            yo0 = jnp.dot(cm_b[g0], st_b, preferred_element_type=F32)
            yo1 = yo0 if g1 == g0 else jnp.dot(cm_b[g1], st_b, preferred_element_type=F32)
            y_blk = (jnp.where(left, yd[0], yd[1])
                     + jnp.where(left, yo0, yo1) * e_in_w[:, sl])
            y_scr[u, rows, sl] = y_scr[u, rows, sl] + y_blk
            xw = (xdt * e_out_w[:, sl]).astype(BF16)
            up0 = jnp.dot(bmt_b[g0], xw, preferred_element_type=F32)
            up1 = up0 if g1 == g0 else jnp.dot(bmt_b[g1], xw, preferred_element_type=F32)
            st[:, sl] = st_blk * _pair_cols(dec, lo + h0) + jnp.where(left, up0, up1)

    if nc <= 2:
        for c in range(nc):
            for u in range(n_u):
                process(u, 0, c * q)
                process(u, 1, (nc - 1 - c) * q)
    else:
        def body(c, carry):
            for u in range(n_u):
                process(u, 0, pl.multiple_of(c * q, q))
                process(u, 1, pl.multiple_of((nc - 1 - c) * q, q))
            return carry
        lax.fori_loop(0, nc, body, 0, unroll=SSD_UNROLL if nc % SSD_UNROLL == 0 else 1)

    for u in range(n_u):
        if emit_state:
            for d in range(2):
                st_ref[u, state_slot, d] = st_scr[u, d].T.reshape(C_HEADS, C_HEADDIM, C_STATE)
            for other in range(st_ref.shape[1]):
                if other != state_slot:
                    st_ref[u, other] = jnp.zeros(st_ref.shape[2:], F32)

        for r0 in range(0, L, rb):
            zz = z_ref[u, r0:r0 + rb, :]
            yg = y_scr[u, r0:r0 + rb, :] * (zz * _sigmoid(zz))
            ms = jnp.mean(yg * yg, axis=-1, keepdims=True)
            co_ref[u, r0:r0 + rb, :] = (yg * lax.rsqrt(ms + EPS) * nw_ref[...]).astype(BF16)


def _ssd(xbc, z, dt, conv_w, conv_b, dtb, alog, dexp, norm_w, state, layer, *, emit_state, depth=1,
         st_prev=None):
    nb, n_slab, L, _ = xbc.shape
    has_init = state is not None
    sb = max(1, min(nb, SSD_STEP_ROWS // L))
    assert nb % sb == 0
    seq_spec = lambda w: pl.BlockSpec((sb, L, w), lambda b: (b, 0, 0))
    par_spec = lambda r, w: pl.BlockSpec((r, w), lambda b: (0, 0))
    in_specs = [pl.BlockSpec((sb, n_slab, L, LANES), lambda b: (b, 0, 0, 0)), seq_spec(C_INNER), seq_spec(LANES),
                par_spec(3, C_CONV_CH), par_spec(1, C_CONV_CH), par_spec(1, LANES), par_spec(1, LANES),
                par_spec(1, C_INNER), par_spec(1, C_INNER)]
    args = [xbc, z, dt, conv_w, conv_b, dtb, alog, dexp, norm_w]
    if has_init:
        in_specs.append(pl.BlockSpec((sb, 1, 2, C_HEADS, C_HEADDIM, C_STATE),
                                     lambda b: (b, layer, 0, 0, 0, 0)))
        args.append(state)
    out_shape = [jax.ShapeDtypeStruct((nb, L, C_INNER), BF16)]
    out_specs = [seq_spec(C_INNER)]
    aliases = {}
    if emit_state:
        out_shape.append(jax.ShapeDtypeStruct((nb, depth, 2, C_HEADS, C_HEADDIM, C_STATE), F32))
        out_specs.append(
            pl.BlockSpec((sb, depth, 2, C_HEADS, C_HEADDIM, C_STATE), lambda b: (b, 0, 0, 0, 0, 0)) if st_prev is None
            else pl.BlockSpec((sb, 1, 2, C_HEADS, C_HEADDIM, C_STATE), lambda b: (b, layer, 0, 0, 0, 0)))
        if st_prev is not None:
            aliases[len(args)] = 1
            in_specs.append(pl.BlockSpec(memory_space=pl.ANY))
            args.append(st_prev)
    return pl.pallas_call(
        functools.partial(_ssd_kernel, has_init=has_init, emit_state=emit_state, seq_len=L,
                          n_alias=len(aliases), state_slot=layer if st_prev is None else 0),
        input_output_aliases=aliases,
        out_shape=out_shape,
        grid=(nb // sb,),
        in_specs=in_specs,
        out_specs=out_specs,
        scratch_shapes=[pltpu.VMEM((sb, n_slab, L, LANES), F32), pltpu.VMEM((sb, L, C_INNER), F32),
                        pltpu.VMEM((sb, 2, C_STATE, C_INNER), F32)],
        compiler_params=_cparams(1),
        name="ssd_init" if has_init else "ssd",
    )(*args)


def _outp_kernel(a_ref, b_ref, c_ref, x_ref, mod_ref, w_ref, x1_o, h2_o):
    mix = jnp.concatenate([a_ref[0], b_ref[0], c_ref[0]], axis=-1)
    o = jnp.dot(mix, w_ref[...], preferred_element_type=F32)
    gate1 = mod_ref[0, 2:3, :]
    shift2 = mod_ref[0, 3:4, :]
    scale2 = mod_ref[0, 4:5, :]
    x1 = x_ref[0] + gate1 * o
    ms = jnp.mean(x1 * x1, axis=-1, keepdims=True)
    x1_o[0] = x1
    h2_o[0] = ((x1 * lax.rsqrt(ms + EPS)) * (1.0 + scale2) + shift2).astype(BF16)


def _outp(a_out, b_out, c_out, x, mod, w_out_b, layer, *, per_batch_mod):
    nb, L, d = x.shape
    tm = min(OUTP_TM, L)
    mod_map = (lambda b, i: (b, 0, 0)) if per_batch_mod else (lambda b, i: (0, 0, 0))
    row_spec = lambda w: pl.BlockSpec((1, tm, w), lambda b, i: (b, i, 0))
    return pl.pallas_call(
        _outp_kernel,
        out_shape=[jax.ShapeDtypeStruct((nb, L, d), F32), jax.ShapeDtypeStruct((nb, L, d), BF16)],
        grid=(nb, L // tm),
        in_specs=[row_spec(A_WIDTH), row_spec(B_WIDTH), row_spec(C_INNER), row_spec(d),
                  pl.BlockSpec((1, 6, d), mod_map),
                  pl.BlockSpec((None, d, d), lambda b, i: (layer, 0, 0))],
        out_specs=[row_spec(d), row_spec(d)],
        compiler_params=_cparams(2),
        name="outp",
    )(a_out, b_out, c_out, x, mod, w_out_b)


HALO = 16


def _ffn_kernel(h_ref, hp_ref, hn_ref, x1_ref, mod_ref, wu_ref, cw_ref, cb_ref, wd_ref, fw_ref,
                o_ref, hext_scr, u_scr, acc_scr, *, seq_len, final_norm):
    tm = h_ref.shape[1]
    i = pl.program_id(1)
    n_chunks = wu_ref.shape[0]
    sub = min(FFN_SUB, tm)
    n_sub = tm // sub
    rb = min(FFN_RB, sub)
    n_items = n_sub * n_chunks

    has_prev = ((i * tm) & (seq_len - 1)) != 0
    has_next = ((i * tm + tm) & (seq_len - 1)) != 0
    hext_scr[0:HALO, :] = jnp.where(has_prev, hp_ref[0], jnp.zeros_like(hp_ref[0]))
    hext_scr[HALO:HALO + tm, :] = h_ref[0]
    hext_scr[HALO + tm:HALO + tm + HALO, :] = jnp.where(has_next, hn_ref[0], jnp.zeros_like(hn_ref[0]))
    acc_scr[...] = jnp.zeros_like(acc_scr)
    row8 = lax.broadcasted_iota(jnp.int32, (8, 1), 0)

    n_uslab = 2 * FFN_FC // LANES
    n_oslab = acc_scr.shape[0]
    half = rb // 2

    def split(n):
        if isinstance(n, int):
            return (n // n_chunks) * sub, n % n_chunks
        if n_sub == 1:
            return 0, n
        t = sum((n >= k * n_chunks).astype(jnp.int32) for k in range(1, n_sub))
        return pl.multiple_of(t * sub, sub), n - t * n_chunks

    def up(n, slot):
        row0, c = split(n)
        h_rows = hext_scr[pl.ds(row0, sub + 2 * HALO), :]
        u = jnp.dot(h_rows, wu_ref[c], preferred_element_type=F32)
        for s in range(n_uslab):
            u_scr[slot, s] = u[:, s * LANES:(s + 1) * LANES]

    def conv_down(n, slot):
        row0, c = split(n)
        cw = cw_ref[c]
        bias = cb_ref[c]
        for r0 in range(0, sub, rb):
            base = HALO + r0
            ys = []
            for s in range(n_uslab):
                ls = slice(s * LANES, (s + 1) * LANES)
                even = u_scr[slot, s, pl.ds(base, half, stride=2), :]
                odd = u_scr[slot, s, pl.ds(base + 1, half, stride=2), :]
                odd_before = u_scr[slot, s, pl.ds(base - 1, half, stride=2), :]
                even_after = u_scr[slot, s, pl.ds(base + 2, half, stride=2), :]
                if r0 % seq_len == 0 and (r0 > 0 or seq_len <= sub):
                    odd_before = jnp.concatenate(
                        [jnp.where(row8 == 0, 0.0, odd_before[0:8]), odd_before[8:]], axis=0)
                if (r0 + rb) % seq_len == 0 and (r0 + rb < sub or seq_len <= sub):
                    even_after = jnp.concatenate(
                        [even_after[:half - 8], jnp.where(row8 == 7, 0.0, even_after[half - 8:])], axis=0)
                w0, w1, w2, b = cw[0:1, ls], cw[1:2, ls], cw[2:3, ls], bias[:, ls]
                y_even = odd_before * w0 + even * w1 + odd * w2 + b
                y_odd = even * w0 + odd * w1 + even_after * w2 + b
                ys.append(jnp.concatenate([y_even, y_odd], axis=0))
            n_g = n_uslab // 2
            act = jnp.concatenate([ys[s] * _sigmoid(ys[s]) * ys[n_g + s] for s in range(n_g)], axis=1)
            dn = jnp.dot(act.astype(BF16), wd_ref[c], preferred_element_type=F32)
            for s in range(n_oslab):
                ls = slice(s * LANES, (s + 1) * LANES)
                acc_scr[s, pl.ds(row0 + r0, half, stride=2), :] += dn[0:half, ls]
                acc_scr[s, pl.ds(row0 + r0 + 1, half, stride=2), :] += dn[half:rb, ls]

    up(0, 0)

    def pair(k, carry):
        n = 2 * k
        up(n + 1, 1)
        conv_down(n, 0)
        up(n + 2, 0)
        conv_down(n + 1, 1)
        return carry

    lax.fori_loop(0, (n_items - 1) // 2, pair, 0)
    if n_items % 2 == 1:
        conv_down(n_items - 1, 0)
    else:
        up(n_items - 1, 1)
        conv_down(n_items - 2, 0)
        conv_down(n_items - 1, 1)

    gate2 = mod_ref[0, 5:6, :]
    ffn_out = jnp.concatenate([acc_scr[s] for s in range(n_oslab)], axis=1)
    out = x1_ref[0] + gate2 * ffn_out
    if final_norm:
        ms = jnp.mean(out * out, axis=-1, keepdims=True)
        out = out * lax.rsqrt(ms + EPS) * fw_ref[...]
    o_ref[0] = out


def _ffn(h2, x1, mod, wu_c, cw_c, cb_c, wd_c, layer, final_w, *, per_batch_mod, seq_len, final_norm):
    nb, L, d = x1.shape
    tm = min(FFN_TM, L)
    n_tiles = L // tm
    sub = min(FFN_SUB, tm)
    rb = min(FFN_RB, sub)
    assert seq_len & (seq_len - 1) == 0 and L % seq_len == 0 and tm % sub == 0
    assert seq_len % tm == 0 or (tm % seq_len == 0 and sub % seq_len == 0 and seq_len % rb == 0)
    n_chunks = wu_c.shape[1]
    hb = tm // HALO
    n_hblk = L // HALO
    mod_map = (lambda b, i: (b, 0, 0)) if per_batch_mod else (lambda b, i: (0, 0, 0))
    row_spec = lambda w: pl.BlockSpec((1, tm, w), lambda b, i: (b, i, 0))
    const3 = lambda s: pl.BlockSpec(s, lambda b, i: (0, 0, 0))
    return pl.pallas_call(
        functools.partial(_ffn_kernel, seq_len=seq_len, final_norm=final_norm),
        out_shape=jax.ShapeDtypeStruct((nb, L, d), F32),
        grid=(nb, n_tiles),
        in_specs=[row_spec(d),
                  pl.BlockSpec((1, HALO, d), lambda b, i: (b, jnp.maximum(i * hb - 1, 0), 0)),
                  pl.BlockSpec((1, HALO, d), lambda b, i: (b, jnp.minimum((i + 1) * hb, n_hblk - 1), 0)),
                  row_spec(d),
                  pl.BlockSpec((1, 6, d), mod_map),
                  pl.BlockSpec((None, n_chunks, d, 2 * FFN_FC), lambda b, i: (layer, 0, 0, 0)),
                  const3((n_chunks, 3, 2 * FFN_FC)),
                  const3((n_chunks, 1, 2 * FFN_FC)),
                  pl.BlockSpec((None, n_chunks, FFN_FC, d), lambda b, i: (layer, 0, 0, 0)),
                  pl.BlockSpec((1, d), lambda b, i: (0, 0))],
        out_specs=row_spec(d),
        scratch_shapes=[pltpu.VMEM((tm + 2 * HALO, d), BF16),
                        pltpu.VMEM((2, 2 * FFN_FC // LANES, sub + 2 * HALO, LANES), F32),
                        pltpu.VMEM((d // LANES, tm, LANES), F32)],
        compiler_params=_cparams(2),
        name="ffn_final" if final_norm else "ffn",
    )(h2, h2, h2, x1, mod, wu_c, cw_c, cb_c, wd_c, final_w)


def _rope_tables(L, d):
    rows = L // GRID_W
    row = jnp.repeat(jnp.arange(rows), GRID_W).astype(F32)
    col = jnp.tile(jnp.arange(GRID_W), rows).astype(F32)
    quarter = d // 4
    inv = ROPE_THETA ** (-jnp.arange(quarter, dtype=F32) / quarter)
    ang_r = row[:, None] * inv[None, :]
    ang_c = col[:, None] * inv[None, :]
    ang = jnp.concatenate([ang_r, ang_r, ang_c, ang_c], axis=-1)
    cos, sin = jnp.cos(ang), jnp.sin(ang)
    even = ((jnp.arange(d) // quarter) % 2 == 0)[None, :]
    s_up = jnp.where(even, -sin, 0.0)
    s_dn = jnp.where(even, 0.0, sin)
    reps = LANES // d
    return tuple(jnp.tile(t, (1, reps)) for t in (cos, s_up, s_dn))


def _pad_lanes(v, width=LANES):
    v = v.reshape(1, -1).astype(F32)
    return jnp.pad(v, ((0, 0), (0, width - v.shape[1])))


def _layer_params(l, a_q_norm, a_k_norm, b_lambda, b_subln, ssm_conv_w, ssm_conv_b, ssm_A_log,
                  ssm_dt_bias, ssm_D, ssm_norm_w, ffn_conv_w, ffn_conv_b):
    n_chunks = D_FF // FFN_FC

    def gv_chunks(t):
        lead = t.shape[:-1]
        g = t[..., :D_FF].reshape(lead + (n_chunks, FFN_FC))
        v = t[..., D_FF:].reshape(lead + (n_chunks, FFN_FC))
        gv = jnp.concatenate([g, v], axis=-1)
        return jnp.moveaxis(gv, -2, 0)

    return dict(
        qn=jnp.tile(a_q_norm[l], 2).reshape(1, LANES),
        kn=jnp.tile(a_k_norm[l], 2).reshape(1, LANES),
        lamv=b_lambda[l],
        subln=jnp.tile(b_subln[l], 2).reshape(1, LANES),
        conv_w=ssm_conv_w[l],
        conv_b=ssm_conv_b[l].reshape(1, C_CONV_CH),
        dtb=_pad_lanes(ssm_dt_bias[l]),
        alog=_pad_lanes(ssm_A_log[l]),
        dexp=jnp.repeat(ssm_D[l], C_HEADDIM).reshape(1, C_INNER),
        norm_w=ssm_norm_w[l].reshape(1, C_INNER),
        cw=gv_chunks(ffn_conv_w[l]),
        cb=gv_chunks(ffn_conv_b[l].reshape(1, 2 * D_FF)),
    )


def _block(x, mod, p, wts, rope_tabs, caches, state, layer, *, seq_shape, per_batch_mod, is_ctx, lam_init,
           final_w, final_norm, new_caches=None):
    n_seq, seq_len = seq_shape
    depth = wts["w_in"].shape[0]
    outs = _proj(x, mod, wts["w_in"], layer, p["qn"], p["kn"], rope_tabs,
                 per_batch_mod=per_batch_mod, kv_f32=is_ctx, seq_len=seq_len, values_t=caches is not None,
                 kv_prev=None if new_caches is None else new_caches[:4])
    seq = lambda t: t.reshape(n_seq, seq_len, t.shape[-1]) if t.ndim == 3 else t
    qa, k3, va, qb, kb, vb, xbc, z, dt = [seq(t) for t in outs[:9]]
    if caches is None:
        a_out, b_out = _attn_rows(qa, k3, va, qb, kb, vb, p["lamv"], p["subln"], lam_init=lam_init)
    else:
        a_out, b_out = _attn(qa, k3, va, qb, kb, vb, p["lamv"], p["subln"], caches, layer, lam_init=lam_init)
    ssd_out = _ssd(xbc, z, dt, p["conv_w"], p["conv_b"], p["dtb"], p["alog"], p["dexp"], p["norm_w"],
                   state, layer, emit_state=is_ctx, depth=depth,
                   st_prev=None if new_caches is None else new_caches[4])
    c_out = ssd_out[0]
    flat = lambda t: t.reshape(x.shape[0], x.shape[1], t.shape[-1])
    x1, h2 = _outp(flat(a_out), flat(b_out), flat(c_out), x, mod, wts["w_out"], layer,
                   per_batch_mod=per_batch_mod)
    x2 = _ffn(h2, x1, mod, wts["wu"], p["cw"], p["cb"], wts["wd"], layer, final_w,
              per_batch_mod=per_batch_mod, seq_len=seq_len, final_norm=final_norm)
    extras = None
    if is_ctx:
        extras = tuple(outs[9:13]) + (ssd_out[1],)
    return x2, extras


def kernel(x_prompt, x_sample, cache_a_k, cache_a_v, cache_b_k, cache_b_v, state_ssm, c, c_ctx, ada_w, ada_b, w_in, a_q_norm, a_k_norm, b_lambda, b_subln, ssm_conv_w, ssm_conv_b, ssm_A_log, ssm_dt_bias, ssm_D, ssm_norm_w, w_out, ffn_up, ffn_conv_w, ffn_conv_b, ffn_down, final_norm_w):
    depth = w_in.shape[0]
    nbp, lp, d = x_prompt.shape
    nbs, ls, _ = x_sample.shape
    past = cache_a_k.shape[2]

    mod_rows = 8
    assert 1 + nbs <= mod_rows
    cvecs = jnp.concatenate([c_ctx[None, :], c, jnp.zeros((mod_rows - 1 - nbs, d), F32)], axis=0)
    mod = _modulation(cvecs, ada_w, ada_b)

    rope_tabs = _rope_tables(ls, HEAD_DIM) + _rope_tables(ls, B_HALF)
    caches = (cache_a_k.reshape(nbs, depth, past, A_KV_HEADS * HEAD_DIM),
              cache_a_v.reshape(nbs, depth, past, A_KV_HEADS * HEAD_DIM),
              cache_b_k.reshape(nbs, depth, past, B_WIDTH),
              cache_b_v.reshape(nbs, depth, past, B_WIDTH))
    final_w = final_norm_w.reshape(1, d)
    n_chunks = D_FF // FFN_FC
    wts = dict(
        w_in=_pad_cast_bf16(w_in, PROJ_PAD),
        w_out=_cast_bf16(w_out),
        wu=_gv_cast_bf16(ffn_up),
        wd=_cast_bf16(ffn_down).reshape(depth, n_chunks, FFN_FC, d),
    )

    yp = x_prompt.reshape(1, nbp * lp, d)
    ys = x_sample
    new_caches = None
    for l in range(depth):
        lam_init = 0.8 - 0.6 * math.exp(-0.3 * l)
        p = _layer_params(l, a_q_norm, a_k_norm, b_lambda, b_subln, ssm_conv_w, ssm_conv_b,
                          ssm_A_log, ssm_dt_bias, ssm_D, ssm_norm_w, ffn_conv_w, ffn_conv_b)
        last = l == depth - 1
        mod_ctx = mod[l, 0].reshape(1, 6, d)
        mod_lat = mod[l, 1:1 + nbs].reshape(nbs, 6, d)
        yp, new_caches = _block(yp, mod_ctx, p, wts, None, None, None, l, seq_shape=(nbp, lp),
                                per_batch_mod=False, is_ctx=True, lam_init=lam_init,
                                final_w=final_w, final_norm=last, new_caches=new_caches)
        ys, _ = _block(ys, mod_lat, p, wts, rope_tabs, caches, state_ssm, l, seq_shape=(nbs, ls),
                       per_batch_mod=True, is_ctx=False, lam_init=lam_init,
                       final_w=final_w, final_norm=last)

    y_prompt = yp.reshape(nbp, lp, d)
    new_a_k = new_caches[0].reshape(nbp, depth, lp, A_KV_HEADS, HEAD_DIM)
    new_a_v = new_caches[1].reshape(nbp, depth, lp, A_KV_HEADS, HEAD_DIM)
    new_b_k = new_caches[2].reshape(nbp, depth, lp, B_HEADS, 2, B_HALF)
    new_b_v = new_caches[3].reshape(nbp, depth, lp, B_HEADS, 2 * B_HALF)
    new_ssm = new_caches[4]
    return (y_prompt, ys, new_a_k, new_a_v, new_b_k, new_b_v, new_ssm)
```

```python
import functools
import math

import jax
import jax.numpy as jnp
from jax import lax
from jax.experimental import pallas as pl
from jax.experimental.pallas import tpu as pltpu

F32 = jnp.float32
BF16 = jnp.bfloat16

LANES = 128
VMEM_LIMIT_BYTES = 56 * 1024 * 1024

D_MODEL = 1024
GRID_W = 64
SSD_CHUNK = 128
ROPE_THETA = 10000.0
EPS = 1e-6
HEAD_DIM = 64
A_Q_HEADS = 6
A_KV_HEADS = 2
A_GROUP = A_Q_HEADS // A_KV_HEADS
A_WIDTH = A_Q_HEADS * HEAD_DIM
B_HEADS = 4
B_HALF = 32
B_WIDTH = B_HEADS * 2 * B_HALF
C_HEADS = 6
C_HEADDIM = 64
C_INNER = C_HEADS * C_HEADDIM
C_GROUPS = 2
C_STATE = 128
C_CONV_CH = C_INNER + 2 * C_GROUPS * C_STATE
D_FF = 2816
PROJ_WIDTH = A_WIDTH + 4 * HEAD_DIM + 3 * B_WIDTH + 2 * C_INNER + 4 * C_STATE + 2 * C_HEADS
PROJ_PAD = ((PROJ_WIDTH + 2 * LANES - 1) // (2 * LANES)) * (2 * LANES)

OFF_AQ = 0
OFF_AK = OFF_AQ + A_WIDTH
OFF_AV = OFF_AK + A_KV_HEADS * HEAD_DIM
OFF_BQ = OFF_AV + A_KV_HEADS * HEAD_DIM
OFF_BK = OFF_BQ + B_WIDTH
OFF_BV = OFF_BK + B_WIDTH
OFF_CX = OFF_BV + B_WIDTH
OFF_CZ = OFF_CX + C_INNER
OFF_CB = OFF_CZ + C_INNER
OFF_CC = OFF_CB + C_GROUPS * C_STATE
OFF_DT = OFF_CC + C_GROUPS * C_STATE

MOD_BLOCK = 1024
PROJ_TM = 512
OUTP_TM = 1024
FFN_TM = 512
FFN_FC = 256
FFN_RB = 256
ATTN_TQ = 512
ATTN_KC = 512
ATTN_AHEAD = 2
ROW_BLOCK = 256
CONV_ROWS = 128
SSD_UNROLL = 4
SSD_STEP_ROWS = 512
NEG_BIG = -1e30
LOG2E = 1.4426950408889634


def _cparams(n_grid):
    return pltpu.CompilerParams(
        dimension_semantics=("parallel",) * n_grid,
        vmem_limit_bytes=VMEM_LIMIT_BYTES,
    )


def _sigmoid(x):
    return 1.0 / (1.0 + jnp.exp(-x))


def _left_mask():
    lane = lax.broadcasted_iota(jnp.int32, (1, LANES), 1)
    return lane < (LANES // 2)


def _pair_rmsnorm(y, w):
    left = _left_mask()
    sq = y * y
    s_l = jnp.sum(jnp.where(left, sq, 0.0), axis=-1, keepdims=True)
    s_r = jnp.sum(jnp.where(left, 0.0, sq), axis=-1, keepdims=True)
    ms = jnp.where(left, s_l, s_r) * (1.0 / HEAD_DIM)
    return y * lax.rsqrt(ms + EPS) * w


def _rope(y, cos, s_up, s_dn, quarter):
    return (y * cos
            + pltpu.roll(y, LANES - quarter, 1) * s_up
            + pltpu.roll(y, quarter, 1) * s_dn)


def _modulation_kernel(c_ref, w_ref, b_ref, o_ref):
    c = c_ref[...]
    s = c * _sigmoid(c)
    o_ref[0] = jnp.dot(s.astype(BF16), w_ref[0].astype(BF16),
                       preferred_element_type=F32) + b_ref[0]


def _modulation(cvecs, ada_w, ada_b):
    depth, d, width = ada_w.shape
    rows = cvecs.shape[0]
    return pl.pallas_call(
        _modulation_kernel,
        out_shape=jax.ShapeDtypeStruct((depth, rows, width), F32),
        grid=(depth, width // MOD_BLOCK),
        in_specs=[
            pl.BlockSpec((rows, d), lambda l, j: (0, 0)),
            pl.BlockSpec((1, d, MOD_BLOCK), lambda l, j: (l, 0, j)),
            pl.BlockSpec((1, 1, MOD_BLOCK), lambda l, j: (l, 0, j)),
        ],
        out_specs=pl.BlockSpec((1, rows, MOD_BLOCK), lambda l, j: (l, 0, j)),
        compiler_params=_cparams(2),
        name="modulation",
    )(cvecs, ada_w, ada_b.reshape(depth, 1, width))


CAST_ROWS = 1024
PAD_CAST_COLS = 256


def _cast_kernel(x_ref, o_ref):
    o_ref[...] = x_ref[...].astype(o_ref.dtype)


def _cast_bf16(x):
    depth, rows, cols = x.shape
    rows_blk = max(r for r in range(8, CAST_ROWS + 1, 8) if rows % r == 0)
    spec = pl.BlockSpec((1, rows_blk, cols), lambda l, r: (l, r, 0))
    return pl.pallas_call(
        _cast_kernel, out_shape=jax.ShapeDtypeStruct(x.shape, BF16),
        grid=(depth, rows // rows_blk), in_specs=[spec], out_specs=spec,
        compiler_params=_cparams(2), name="cast_bf16",
    )(x)


def _pad_cast_kernel(x_ref, o_ref, *, n_cols):
    tr = x_ref.shape[0]
    col = lax.broadcasted_iota(jnp.int32, (tr, 1), 0) + pl.program_id(0) * tr
    for l in range(x_ref.shape[1]):
        x = jnp.where(col < n_cols, x_ref[:, l, :], 0.0)
        o_ref[l] = x.T.astype(BF16)


def _pad_cast_bf16(x, width):
    depth, rows, cols = x.shape
    return pl.pallas_call(
        functools.partial(_pad_cast_kernel, n_cols=cols),
        out_shape=jax.ShapeDtypeStruct((depth, rows, width), BF16),
        grid=(width // PAD_CAST_COLS,),
        in_specs=[pl.BlockSpec((PAD_CAST_COLS, depth, rows), lambda i: (i, 0, 0))],
        out_specs=pl.BlockSpec((depth, rows, PAD_CAST_COLS), lambda i: (0, 0, i)),
        compiler_params=_cparams(1), name="pad_cast_bf16",
    )(jnp.transpose(x, (2, 0, 1)))


def _gv_cast_kernel(g_ref, v_ref, o_ref):
    o_ref[0, 0, :, 0:FFN_FC] = g_ref[0].astype(BF16)
    o_ref[0, 0, :, FFN_FC:2 * FFN_FC] = v_ref[0].astype(BF16)


def _gv_cast_bf16(up):
    depth, d, _ = up.shape
    n_chunks = D_FF // FFN_FC
    return pl.pallas_call(
        _gv_cast_kernel, out_shape=jax.ShapeDtypeStruct((depth, n_chunks, d, 2 * FFN_FC), BF16),
        grid=(depth, n_chunks),
        in_specs=[pl.BlockSpec((1, d, FFN_FC), lambda l, c: (l, 0, c)),
                  pl.BlockSpec((1, d, FFN_FC), lambda l, c: (l, 0, n_chunks + c))],
        out_specs=pl.BlockSpec((1, 1, d, 2 * FFN_FC), lambda l, c: (l, c, 0, 0)),
        compiler_params=_cparams(2), name="gv_cast_bf16",
    )(up, up)


def _proj_kernel(*refs, rope, kv_f32, n_alias, cache_slot):
    x_ref, mod_ref, w_ref, qn_ref, kn_ref = refs[:5]
    pos = 5
    if rope:
        ca_ref, sau_ref, sad_ref, cb_ref, sbu_ref, sbd_ref = refs[pos:pos + 6]
        pos += 6
    pos += n_alias
    (qa_o, k3_o, vta_o, qb_o, kb_o, vtb_o, xbc_o, z_o, dt_o) = refs[pos:pos + 9]
    pos += 9
    if kv_f32:
        ka32_o, va32_o, kb32_o, vb32_o = refs[pos:pos + 4]
        pos += 4
    p_scr = refs[pos]

    x = x_ref[0]
    ms = jnp.mean(x * x, axis=-1, keepdims=True)
    shift = mod_ref[0, 0:1, :]
    scale = mod_ref[0, 1:2, :]
    h = (x * lax.rsqrt(ms + EPS)) * (1.0 + scale) + shift
    p_scr[...] = jnp.dot(h.astype(BF16), w_ref[...], preferred_element_type=F32)

    left = _left_mask()

    def rope_a(y):
        if not rope:
            return y
        return _rope(y, ca_ref[...], sau_ref[...], sad_ref[...], HEAD_DIM // 4)

    def rope_b(y):
        if not rope:
            return y
        return _rope(y, cb_ref[...], sbu_ref[...], sbd_ref[...], B_HALF // 4)

    values_t = len(vta_o.shape) == 4
    if values_t:
        n_sub, sub_len = vta_o.shape[0], vta_o.shape[3]
        ones_half = jnp.ones((HEAD_DIM, sub_len), BF16)

    def store_vt(dst, h, vt_half, e):
        for s in range(n_sub):
            dst[s, h, e * HEAD_DIM:(e + 1) * HEAD_DIM, :] = vt_half[:, s * sub_len:(s + 1) * sub_len]
            dst[s, h, (1 - e) * HEAD_DIM:(2 - e) * HEAD_DIM, :] = ones_half

    a_scale = HEAD_DIM ** -0.5 * LOG2E
    for j in range(A_WIDTH // LANES):
        y = p_scr[:, OFF_AQ + j * LANES:OFF_AQ + (j + 1) * LANES]
        y = rope_a(_pair_rmsnorm(y, qn_ref[...]))
        qa_o[0, :, j * LANES:(j + 1) * LANES] = (y * a_scale).astype(BF16)

    k = rope_a(_pair_rmsnorm(p_scr[:, OFF_AK:OFF_AK + LANES], kn_ref[...]))
    v = p_scr[:, OFF_AV:OFF_AV + LANES]
    if kv_f32:
        kv_sub, kv_len = ka32_o.shape[0], ka32_o.shape[2]
        for u in range(kv_sub):
            ka32_o[u, cache_slot] = k[u * kv_len:(u + 1) * kv_len]
            va32_o[u, cache_slot] = v[u * kv_len:(u + 1) * kv_len]
        for ref in (ka32_o, va32_o, kb32_o, vb32_o):
            for other in range(ref.shape[1]):
                if other != cache_slot:
                    ref[:, other] = jnp.zeros((kv_sub, kv_len, ref.shape[3]), F32)
    swapped = pltpu.roll(k, LANES // 2, 1)
    k3_o[0, :, 0:LANES] = jnp.where(left, k, swapped).astype(BF16)
    k3_o[0, :, LANES:2 * LANES] = k.astype(BF16)
    k3_o[0, :, 2 * LANES:3 * LANES] = jnp.where(left, swapped, k).astype(BF16)
    if values_t:
        vt = v.T.astype(BF16)
        for h in range(A_Q_HEADS):
            g = h // A_GROUP
            store_vt(vta_o, h, vt[g * HEAD_DIM:(g + 1) * HEAD_DIM], h % 2)
    else:
        v_sw = pltpu.roll(v, LANES // 2, 1)
        vta_o[0, :, 0:LANES] = jnp.where(left, v, v_sw).astype(BF16)
        vta_o[0, :, LANES:2 * LANES] = v.astype(BF16)
        vta_o[0, :, 2 * LANES:3 * LANES] = jnp.where(left, v_sw, v).astype(BF16)

    b_scale = B_HALF ** -0.5 * LOG2E
    for j in range(B_WIDTH // LANES):
        sl = slice(j * LANES, (j + 1) * LANES)
        q = rope_b(p_scr[:, OFF_BQ + j * LANES:OFF_BQ + (j + 1) * LANES])
        qb_o[0, :, sl] = (q * b_scale).astype(BF16)
        kb = rope_b(p_scr[:, OFF_BK + j * LANES:OFF_BK + (j + 1) * LANES])
        vb = p_scr[:, OFF_BV + j * LANES:OFF_BV + (j + 1) * LANES]
        kb_o[0, :, sl] = kb.astype(BF16)
        if values_t:
            vbt = vb.T.astype(BF16)
            for e in range(2):
                store_vt(vtb_o, 2 * j + e, vbt[e * HEAD_DIM:(e + 1) * HEAD_DIM], e)
        else:
            vtb_o[0, :, sl] = vb.astype(BF16)
        if kv_f32:
            for u in range(kv_sub):
                kb32_o[u, cache_slot, :, sl] = kb[u * kv_len:(u + 1) * kv_len]
                vb32_o[u, cache_slot, :, sl] = vb[u * kv_len:(u + 1) * kv_len]

    xbc_sub, xbc_len = xbc_o.shape[0], xbc_o.shape[2]
    for s in range(C_CONV_CH // LANES):
        col = OFF_CX + s * LANES if s < C_INNER // LANES else OFF_CB + s * LANES - C_INNER
        for u in range(xbc_sub):
            xbc_o[u, s] = p_scr[u * xbc_len:(u + 1) * xbc_len, col:col + LANES]
    z_o[0] = p_scr[:, OFF_CZ:OFF_CZ + C_INNER]
    dt_o[0] = p_scr[:, OFF_DT:OFF_DT + LANES]


def _proj(x, mod, w_in_p, layer, qn, kn, rope_tabs, *, per_batch_mod, kv_f32, seq_len, values_t,
          kv_prev=None):
    depth = w_in_p.shape[0]
    nb, L, d = x.shape
    tm = min(PROJ_TM, L)
    rope = rope_tabs is not None
    mod_map = (lambda b, i: (b, 0, 0)) if per_batch_mod else (lambda b, i: (0, 0, 0))
    row_spec = lambda w: pl.BlockSpec((1, tm, w), lambda b, i: (b, i, 0))
    in_specs = [
        row_spec(d),
        pl.BlockSpec((1, 6, d), mod_map),
        pl.BlockSpec((None, d, PROJ_PAD), lambda b, i: (layer, 0, 0)),
        pl.BlockSpec((1, LANES), lambda b, i: (0, 0)),
        pl.BlockSpec((1, LANES), lambda b, i: (0, 0)),
    ]
    args = [x, mod, w_in_p, qn, kn]
    if rope:
        in_specs += [pl.BlockSpec((tm, LANES), lambda b, i: (i, 0))] * 6
        args += list(rope_tabs)
    widths = [(A_WIDTH, BF16), (A_WIDTH, BF16), (-A_Q_HEADS if values_t else A_WIDTH, BF16),
              (B_WIDTH, BF16), (B_WIDTH, BF16), (-B_HEADS if values_t else B_WIDTH, BF16),
              ("slabs", F32), (C_INNER, F32), (LANES, F32)]
    kv_widths = [LANES, LANES, B_WIDTH, B_WIDTH] if kv_f32 else []
    widths += [("cache", w) for w in kv_widths]
    aliases = {}
    if kv_prev is not None:
        for k_idx, prev in enumerate(kv_prev):
            aliases[len(args)] = len(widths) - len(kv_widths) + k_idx
            in_specs.append(pl.BlockSpec(memory_space=pl.ANY))
            args.append(prev)
    out_shape, out_specs = [], []
    n_seq = nb * L // seq_len
    in_seq = seq_len >= tm
    assert (L == seq_len) if in_seq else (nb == 1 and tm % seq_len == 0)
    n_slab = C_CONV_CH // LANES
    for w, dt in widths:
        if w == "slabs":
            out_shape.append(jax.ShapeDtypeStruct((n_seq, n_slab, seq_len, LANES), dt))
            out_specs.append(pl.BlockSpec((1, n_slab, tm, LANES), lambda b, i: (b, 0, i, 0)) if in_seq else
                             pl.BlockSpec((tm // seq_len, n_slab, seq_len, LANES), lambda b, i: (i, 0, 0, 0)))
        elif w == "cache":
            assert not in_seq
            out_shape.append(jax.ShapeDtypeStruct((n_seq, depth, seq_len, dt), F32))
            out_specs.append(
                pl.BlockSpec((tm // seq_len, depth, seq_len, dt), lambda b, i: (i, 0, 0, 0)) if kv_prev is None
                else pl.BlockSpec((tm // seq_len, 1, seq_len, dt), lambda b, i: (i, layer, 0, 0)))
        elif w > 0:
            out_shape.append(jax.ShapeDtypeStruct((nb, L, w), dt))
            out_specs.append(row_spec(w))
        else:
            out_shape.append(jax.ShapeDtypeStruct((n_seq, -w, LANES, seq_len), dt))
            out_specs.append(pl.BlockSpec((1, -w, LANES, tm), lambda b, i: (b, 0, 0, i)) if in_seq else
                             pl.BlockSpec((tm // seq_len, -w, LANES, seq_len), lambda b, i: (i, 0, 0, 0)))
    return pl.pallas_call(
        functools.partial(_proj_kernel, rope=rope, kv_f32=kv_f32, n_alias=len(aliases),
                          cache_slot=layer if kv_prev is None else 0),
        out_shape=out_shape,
        grid=(nb, L // tm),
        in_specs=in_specs,
        out_specs=out_specs,
        input_output_aliases=aliases,
        scratch_shapes=[pltpu.VMEM((tm, PROJ_PAD), F32)],
        compiler_params=_cparams(2),
        name="proj_rope" if rope else "proj",
    )(*args)


class _ScoreMap:
    def __init__(self, tag, qm, chunks, s_ref):
        self.tag, self.qm, self.chunks, self.s_ref = tag, qm, chunks, s_ref
        self.offsets = [sum(w for _, _, w in chunks[:c]) for c in range(len(chunks))]
        self.m_part = self.m = self.acc = None

    def pass1(self, c):
        k_fn, _, w = self.chunks[c]
        tq = self.qm.shape[0]
        s = lax.dot_general(k_fn(), self.qm, (((1,), (1,)), ((), ())), preferred_element_type=F32)
        self.s_ref[self.offsets[c]:self.offsets[c] + w, :] = s
        part = jnp.max(s.reshape(w // 8, 8, tq), axis=0)
        self.m_part = part if self.m_part is None else jnp.maximum(self.m_part, part)

    def finish_max(self):
        self.m = jnp.max(self.m_part, axis=0, keepdims=True)

    def pass2(self, c):
        _, vt_fn, w = self.chunks[c]
        e = jnp.exp2(self.s_ref[self.offsets[c]:self.offsets[c] + w, :] - self.m).astype(BF16)
        pv = jnp.dot(vt_fn(), e, preferred_element_type=F32)
        self.acc = pv if self.acc is None else self.acc + pv


def _run_score_maps(maps, on_done):
    for mp in maps[:ATTN_AHEAD]:
        for c in range(len(mp.chunks)):
            mp.pass1(c)
        mp.finish_max()
    for i, mp in enumerate(maps):
        nxt = maps[i + ATTN_AHEAD] if i + ATTN_AHEAD < len(maps) else None
        n_next = len(nxt.chunks) if nxt is not None else 0
        for c in range(max(len(mp.chunks), n_next)):
            if c < len(mp.chunks):
                mp.pass2(c)
            if c < n_next:
                nxt.pass1(c)
        if nxt is not None:
            nxt.finish_max()
        on_done(mp)


def _normalised_half(acc, e):
    l = acc[(1 - e) * HEAD_DIM:(1 - e) * HEAD_DIM + 1, :]
    return acc[e * HEAD_DIM:(e + 1) * HEAD_DIM, :] * (1.0 / l)


def _attn_kernel(*refs, has_cache, lam_init, seq_len):
    qa_ref, k3_ref, vta_ref, qb_ref, kb_ref, vtb_ref, lamv_ref, subln_ref = refs[:8]
    pos = 8
    if has_cache:
        cka_ref, cva_ref, ckb_ref, cvb_ref = refs[pos:pos + 4]
        pos += 4
    ao_ref, bo_ref, s_scr = refs[pos:pos + 3]

    left = _left_mask()
    lane = lax.broadcasted_iota(jnp.int32, (1, LANES), 1)
    kc = min(ATTN_KC, seq_len)
    n_kc = seq_len // kc
    maps = []

    def add_map(tag, qm, chunks):
        maps.append(_ScoreMap(tag, qm, chunks, s_scr.at[len(maps) % (ATTN_AHEAD + 1)]))

    def new_chunks(k_ref, vt_ref, j, h):
        sl = slice(j * LANES, (j + 1) * LANES)
        out = []
        for c in range(n_kc):
            rows = slice(c * kc, (c + 1) * kc)
            out.append((functools.partial(lambda r, s: k_ref[0, r, s], rows, sl),
                        functools.partial(lambda r, hh: vt_ref[0, hh, :, r], rows, h), kc))
        return out

    def value_block(vt_half, e):
        ones = jnp.ones_like(vt_half)
        return jnp.concatenate([vt_half, ones] if e == 0 else [ones, vt_half], axis=0).astype(BF16)

    if has_cache:
        ck = cka_ref[0, 0]
        ck_sw = pltpu.roll(ck, LANES // 2, 1)
        ck3 = [jnp.where(left, ck, ck_sw), ck, jnp.where(left, ck_sw, ck)]
        cvt = cva_ref[0, 0].T
        past = ck.shape[0]
    for j in range(A_WIDTH // LANES):
        q = qa_ref[0, :, j * LANES:(j + 1) * LANES]
        for e in range(2):
            h = 2 * j + e
            g = h // A_GROUP
            chunks = new_chunks(k3_ref, vta_ref, j, h)
            if has_cache:
                ckj = ck3[j].astype(BF16)
                cvj = value_block(cvt[g * HEAD_DIM:(g + 1) * HEAD_DIM], e)
                chunks.append((lambda a=ckj: a, lambda a=cvj: a, past))
            qm = jnp.where(left if e == 0 else jnp.logical_not(left), q, jnp.zeros_like(q))
            add_map(("a", j, e, 0), qm, chunks)

    lv = lamv_ref[...]
    lam = (jnp.exp(jnp.sum(lv[0:1] * lv[1:2], axis=-1, keepdims=True))
           - jnp.exp(jnp.sum(lv[2:3] * lv[3:4], axis=-1, keepdims=True)) + lam_init)
    for j in range(B_WIDTH // LANES):
        sl = slice(j * LANES, (j + 1) * LANES)
        q = qb_ref[0, :, sl]
        if has_cache:
            ckj = ckb_ref[0, 0, :, sl].astype(BF16)
            cvt_j = cvb_ref[0, 0, :, sl].T
        for e in range(2):
            h = 2 * j + e
            chunks = new_chunks(kb_ref, vtb_ref, j, h)
            if has_cache:
                cvj = value_block(cvt_j[e * HEAD_DIM:(e + 1) * HEAD_DIM], e)
                chunks.append((lambda a=ckj: a, lambda a=cvj: a, ckj.shape[0]))
            for mi in range(2):
                sel = (lane // B_HALF) == (2 * e + mi)
                add_map(("b", j, e, mi), jnp.where(sel, q, jnp.zeros_like(q)), chunks)

    done = {}

    def on_done(mp):
        kind, j, e, mi = mp.tag
        done[mp.tag] = _normalised_half(mp.acc, e)
        if kind == "a" and e == 1:
            o_t = jnp.concatenate([done[("a", j, 0, 0)], done[("a", j, 1, 0)]], axis=0)
            ao_ref[0, :, j * LANES:(j + 1) * LANES] = o_t.T.astype(BF16)
        if kind == "b" and e == 1 and mi == 1:
            halves = []
            for ee in range(2):
                o = done[("b", j, ee, 0)] - lam * done[("b", j, ee, 1)]
                ms = jnp.mean(o * o, axis=0, keepdims=True)
                halves.append(o * lax.rsqrt(ms + EPS))
            o_t = jnp.concatenate(halves, axis=0)
            bo_ref[0, :, j * LANES:(j + 1) * LANES] = (
                o_t.T * subln_ref[...] * (1.0 - lam_init)).astype(BF16)

    _run_score_maps(maps, on_done)


def _softmax_pv_rows(qm, k, v):
    s = lax.dot_general(qm, k, (((1,), (1,)), ((), ())), preferred_element_type=F32)
    m_part = s[:, 0:LANES]
    for t in range(1, s.shape[1] // LANES):
        m_part = jnp.maximum(m_part, s[:, t * LANES:(t + 1) * LANES])
    e = jnp.exp2(s - jnp.max(m_part, axis=-1, keepdims=True))
    l_part = e[:, 0:LANES]
    for t in range(1, s.shape[1] // LANES):
        l_part = l_part + e[:, t * LANES:(t + 1) * LANES]
    acc = jnp.dot(e.astype(BF16), v, preferred_element_type=F32)
    return acc, jnp.sum(l_part, axis=-1, keepdims=True)


def _attn_rows_kernel(qa_ref, k3_ref, v3_ref, qb_ref, kb_ref, vb_ref, lamv_ref, subln_ref,
                      ao_ref, bo_ref, *, lam_init):
    left = _left_mask()
    lane = lax.broadcasted_iota(jnp.int32, (1, LANES), 1)
    for j in range(A_WIDTH // LANES):
        sl = slice(j * LANES, (j + 1) * LANES)
        q = qa_ref[0, :, sl]
        halves = []
        for e in range(2):
            qm = jnp.where(left if e == 0 else jnp.logical_not(left), q, jnp.zeros_like(q))
            acc, l = _softmax_pv_rows(qm, k3_ref[0, :, sl], v3_ref[0, :, sl])
            halves.append(acc * (1.0 / l))
        ao_ref[0, :, sl] = jnp.where(left, halves[0], halves[1]).astype(BF16)

    lv = lamv_ref[...]
    lam = (jnp.exp(jnp.sum(lv[0:1] * lv[1:2], axis=-1, keepdims=True))
           - jnp.exp(jnp.sum(lv[2:3] * lv[3:4], axis=-1, keepdims=True)) + lam_init)
    for j in range(B_WIDTH // LANES):
        sl = slice(j * LANES, (j + 1) * LANES)
        q = qb_ref[0, :, sl]
        halves = []
        for e in range(2):
            maps = []
            for mi in range(2):
                sel = (lane // B_HALF) == (2 * e + mi)
                acc, l = _softmax_pv_rows(jnp.where(sel, q, jnp.zeros_like(q)), kb_ref[0, :, sl], vb_ref[0, :, sl])
                maps.append(acc * (1.0 / l))
            halves.append(maps[0] - lam * maps[1])
        o = jnp.where(left, halves[0], halves[1])
        bo_ref[0, :, sl] = (_pair_rmsnorm(o, subln_ref[...]) * (1.0 - lam_init)).astype(BF16)


def _attn_rows(qa, k3, v3, qb, kb, vb, lamv, subln, *, lam_init):
    nb, L, _ = qa.shape
    spec = lambda w: pl.BlockSpec((1, L, w), lambda b: (b, 0, 0))
    return pl.pallas_call(
        functools.partial(_attn_rows_kernel, lam_init=lam_init),
        out_shape=[jax.ShapeDtypeStruct((nb, L, A_WIDTH), BF16),
                   jax.ShapeDtypeStruct((nb, L, B_WIDTH), BF16)],
        grid=(nb,),
        in_specs=[spec(A_WIDTH), spec(A_WIDTH), spec(A_WIDTH), spec(B_WIDTH), spec(B_WIDTH), spec(B_WIDTH),
                  pl.BlockSpec((4, B_HALF), lambda b: (0, 0)),
                  pl.BlockSpec((1, LANES), lambda b: (0, 0))],
        out_specs=[spec(A_WIDTH), spec(B_WIDTH)],
        compiler_params=_cparams(1),
        name="attn",
    )(qa, k3, v3, qb, kb, vb, lamv, subln)


def _attn(qa, k3, vta, qb, kb, vtb, lamv, subln, caches, layer, *, lam_init):
    nb, L, _ = qa.shape
    tq = min(ATTN_TQ, L)
    has_cache = caches is not None
    q_spec = lambda w: pl.BlockSpec((1, tq, w), lambda b, i: (b, i, 0))
    kv_spec = lambda w: pl.BlockSpec((1, L, w), lambda b, i: (b, 0, 0))
    vt_spec = lambda n: pl.BlockSpec((1, n, LANES, L), lambda b, i: (b, 0, 0, 0))
    in_specs = [q_spec(A_WIDTH), kv_spec(A_WIDTH), vt_spec(A_Q_HEADS),
                q_spec(B_WIDTH), kv_spec(B_WIDTH), vt_spec(B_HEADS),
                pl.BlockSpec((4, B_HALF), lambda b, i: (0, 0)),
                pl.BlockSpec((1, LANES), lambda b, i: (0, 0))]
    args = [qa, k3, vta, qb, kb, vtb, lamv, subln]
    lk = L
    if has_cache:
        past = caches[0].shape[2]
        lk += past
        for c in caches:
            in_specs.append(pl.BlockSpec((1, 1, past, c.shape[-1]), lambda b, i: (b, layer, 0, 0)))
            args.append(c)
    return pl.pallas_call(
        functools.partial(_attn_kernel, has_cache=has_cache, lam_init=lam_init, seq_len=L),
        out_shape=[jax.ShapeDtypeStruct((nb, L, A_WIDTH), BF16),
                   jax.ShapeDtypeStruct((nb, L, B_WIDTH), BF16)],
        grid=(nb, L // tq),
        in_specs=in_specs,
        out_specs=[q_spec(A_WIDTH), q_spec(B_WIDTH)],
        scratch_shapes=[pltpu.VMEM((ATTN_AHEAD + 1, lk, tq), F32)],
        compiler_params=_cparams(2),
        name="attn_cache" if has_cache else "attn",
    )(*args)


def _prefix_sums(tri, a):
    hi = a.astype(BF16)
    lo = (a - hi.astype(F32)).astype(BF16)
    return (jnp.dot(tri, hi, preferred_element_type=F32)
            + jnp.dot(tri, lo, preferred_element_type=F32))


def _pair_cols(x, c0):
    return jnp.where(_left_mask(), x[:, c0:c0 + 1], x[:, c0 + 1:c0 + 2])


def _ssd_kernel(*refs, has_init, emit_state, seq_len, n_alias, state_slot):
    xbc_ref, z_ref, dt_ref, cw_ref, cb_ref, dtb_ref, alog_ref, dexp_ref, nw_ref = refs[:9]
    pos = 9
    if has_init:
        init_ref = refs[pos]
        pos += 1
    pos += n_alias
    co_ref = refs[pos]
    pos += 1
    if emit_state:
        st_ref = refs[pos]
        pos += 1
    xc_scr, y_scr, st_scr = refs[pos:pos + 3]

    L = seq_len
    rb = min(ROW_BLOCK, L)
    q = SSD_CHUNK
    nc = L // q
    left = _left_mask()

    w0 = cw_ref[0:1, :]
    w1 = cw_ref[1:2, :]
    w2 = cw_ref[2:3, :]
    cb = CONV_ROWS
    half = cb // 2
    hrow = lax.broadcasted_iota(jnp.int32, (half, 1), 0)
    n_u = xbc_ref.shape[0]
    for u, s in [(u, s) for u in range(n_u) for s in range(C_CONV_CH // LANES)]:
        ls = slice(s * LANES, (s + 1) * LANES)
        w0s, w1s, w2s, bs = w0[:, ls], w1[:, ls], w2[:, ls], cb_ref[:, ls]
        for r0 in range(0, L, cb):
            even = xbc_ref[u, s, pl.ds(r0, half, stride=2), :]
            odd = xbc_ref[u, s, pl.ds(r0 + 1, half, stride=2), :]
            if r0 > 0:
                odd_before = xbc_ref[u, s, pl.ds(r0 - 1, half, stride=2), :]
            else:
                odd_before = jnp.where(hrow == 0, 0.0, pltpu.roll(odd, 1, 0))
            if r0 + cb < L:
                even_after = xbc_ref[u, s, pl.ds(r0 + 2, half, stride=2), :]
            else:
                even_after = jnp.where(hrow == half - 1, 0.0, pltpu.roll(even, half - 1, 0))
            y_even = odd_before * w0s + even * w1s + odd * w2s + bs
            y_odd = even * w0s + odd * w1s + even_after * w2s + bs
            xc_scr[u, s, pl.ds(r0, half, stride=2), :] = y_even * _sigmoid(y_even)
            xc_scr[u, s, pl.ds(r0 + 1, half, stride=2), :] = y_odd * _sigmoid(y_odd)
    for u, j in [(u, j) for u in range(n_u) for j in range(C_INNER // LANES)]:
        for r0 in range(0, L, rb):
            y_scr[u, r0:r0 + rb, j * LANES:(j + 1) * LANES] = (
                xc_scr[u, j, r0:r0 + rb, :] * dexp_ref[:, j * LANES:(j + 1) * LANES])

    for u, d in [(u, d) for u in range(n_u) for d in range(2)]:
        if has_init:
            st_scr[u, d] = init_ref[u, 0, d].reshape(C_INNER, C_STATE).T
        else:
            st_scr[u, d] = jnp.zeros((C_STATE, C_INNER), F32)

    a_neg = -jnp.exp(alog_ref[...])
    ti = lax.broadcasted_iota(jnp.int32, (q, q), 0)
    si = lax.broadcasted_iota(jnp.int32, (q, q), 1)
    tri_incl = (si <= ti).astype(BF16)
    src_lane = lax.broadcasted_iota(jnp.int32, (LANES, C_INNER), 0)
    dst_head = lax.broadcasted_iota(jnp.int32, (LANES, C_INNER), 1) // C_HEADDIM
    spread = [(src_lane == d * C_HEADS + dst_head).astype(BF16) for d in range(2)]

    def process(u, d, r0):
        rows = pl.ds(r0, q)
        lo = d * C_HEADS
        causal = (si <= ti) if d == 0 else (si >= ti)
        n_x = C_INNER // LANES
        bm = [xc_scr[u, n_x + g, rows, :] for g in range(C_GROUPS)]
        cm = [xc_scr[u, n_x + C_GROUPS + g, rows, :] for g in range(C_GROUPS)]
        bm_b = [b.astype(BF16) for b in bm]
        cm_b = [c.astype(BF16) for c in cm]
        bmt_b = [b.T.astype(BF16) for b in bm]
        gmat = [lax.dot_general(cm_b[g], bm_b[g], (((1,), (1,)), ((), ())), preferred_element_type=F32)
                for g in range(C_GROUPS)]

        dtr = dt_ref[u, rows, :] + dtb_ref[...]
        dt = jnp.maximum(dtr, 0.0) + jnp.log1p(jnp.exp(-jnp.abs(dtr)))
        a = dt * a_neg
        cs = _prefix_sums(tri_incl, a)
        tot = cs[q - 1:q, :]
        ev = cs if d == 0 else (tot - cs + a)
        e_in = jnp.exp(ev)
        e_out = jnp.exp(tot - ev)
        dec = jnp.exp(tot)
        ev_t = ev.T
        per_head = jnp.concatenate([dt, e_in, e_out], axis=0).astype(BF16)
        wide = jnp.dot(per_head, spread[d], preferred_element_type=F32)
        dt_w, e_in_w, e_out_w = wide[0:q], wide[q:2 * q], wide[2 * q:3 * q]

        st = st_scr.at[u, d]
        for j in range(C_INNER // LANES):
            sl = slice(j * LANES, (j + 1) * LANES)
            h0 = 2 * j
            g0, g1 = h0 // (C_HEADS // C_GROUPS), (h0 + 1) // (C_HEADS // C_GROUPS)
            xdt = xc_scr[u, j, rows, :] * dt_w[:, sl]
            xdt_b = xdt.astype(BF16)
            st_blk = st[:, sl]
            st_b = st_blk.astype(BF16)
            yd = []
            for e in range(2):
                hh = lo + h0 + e
                g = g0 if e == 0 else g1
                diff = ev[:, hh:hh + 1] - ev_t[hh:hh + 1, :]
                lmat = jnp.exp(jnp.where(causal, diff, NEG_BIG))
                yd.append(jnp.dot((gmat[g] * lmat).astype(BF16), xdt_b, preferred_element_type=F32))
            yo0 = jnp.dot(cm_b[g0], st_b, preferred_element_type=F32)
            yo1 = yo0 if g1 == g0 else jnp.dot(cm_b[g1], st_b, preferred_element_type=F32)
            y_blk = (jnp.where(left, yd[0], yd[1])
                     + jnp.where(left, yo0, yo1) * e_in_w[:, sl])
            y_scr[u, rows, sl] = y_scr[u, rows, sl] + y_blk
            xw = (xdt * e_out_w[:, sl]).astype(BF16)
            up0 = jnp.dot(bmt_b[g0], xw, preferred_element_type=F32)
            up1 = up0 if g1 == g0 else jnp.dot(bmt_b[g1], xw, preferred_element_type=F32)
            st[:, sl] = st_blk * _pair_cols(dec, lo + h0) + jnp.where(left, up0, up1)

    if nc <= 2:
        for c in range(nc):
            for u in range(n_u):
                process(u, 0, c * q)
                process(u, 1, (nc - 1 - c) * q)
    else:
        def body(c, carry):
            for u in range(n_u):
                process(u, 0, pl.multiple_of(c * q, q))
                process(u, 1, pl.multiple_of((nc - 1 - c) * q, q))
            return carry
        lax.fori_loop(0, nc, body, 0, unroll=SSD_UNROLL if nc % SSD_UNROLL == 0 else 1)

    for u in range(n_u):
        if emit_state:
            for d in range(2):
                st_ref[u, state_slot, d] = st_scr[u, d].T.reshape(C_HEADS, C_HEADDIM, C_STATE)
            for other in range(st_ref.shape[1]):
                if other != state_slot:
                    st_ref[u, other] = jnp.zeros(st_ref.shape[2:], F32)

        for r0 in range(0, L, rb):
            zz = z_ref[u, r0:r0 + rb, :]
            yg = y_scr[u, r0:r0 + rb, :] * (zz * _sigmoid(zz))
            ms = jnp.mean(yg * yg, axis=-1, keepdims=True)
            co_ref[u, r0:r0 + rb, :] = (yg * lax.rsqrt(ms + EPS) * nw_ref[...]).astype(BF16)


def _ssd(xbc, z, dt, conv_w, conv_b, dtb, alog, dexp, norm_w, state, layer, *, emit_state, depth=1,
         st_prev=None):
    nb, n_slab, L, _ = xbc.shape
    has_init = state is not None
    sb = max(1, min(nb, SSD_STEP_ROWS // L))
    assert nb % sb == 0
    seq_spec = lambda w: pl.BlockSpec((sb, L, w), lambda b: (b, 0, 0))
    par_spec = lambda r, w: pl.BlockSpec((r, w), lambda b: (0, 0))
    in_specs = [pl.BlockSpec((sb, n_slab, L, LANES), lambda b: (b, 0, 0, 0)), seq_spec(C_INNER), seq_spec(LANES),
                par_spec(3, C_CONV_CH), par_spec(1, C_CONV_CH), par_spec(1, LANES), par_spec(1, LANES),
                par_spec(1, C_INNER), par_spec(1, C_INNER)]
    args = [xbc, z, dt, conv_w, conv_b, dtb, alog, dexp, norm_w]
    if has_init:
        in_specs.append(pl.BlockSpec((sb, 1, 2, C_HEADS, C_HEADDIM, C_STATE),
                                     lambda b: (b, layer, 0, 0, 0, 0)))
        args.append(state)
    out_shape = [jax.ShapeDtypeStruct((nb, L, C_INNER), BF16)]
    out_specs = [seq_spec(C_INNER)]
    aliases = {}
    if emit_state:
        out_shape.append(jax.ShapeDtypeStruct((nb, depth, 2, C_HEADS, C_HEADDIM, C_STATE), F32))
        out_specs.append(
            pl.BlockSpec((sb, depth, 2, C_HEADS, C_HEADDIM, C_STATE), lambda b: (b, 0, 0, 0, 0, 0)) if st_prev is None
            else pl.BlockSpec((sb, 1, 2, C_HEADS, C_HEADDIM, C_STATE), lambda b: (b, layer, 0, 0, 0, 0)))
        if st_prev is not None:
            aliases[len(args)] = 1
            in_specs.append(pl.BlockSpec(memory_space=pl.ANY))
            args.append(st_prev)
    return pl.pallas_call(
        functools.partial(_ssd_kernel, has_init=has_init, emit_state=emit_state, seq_len=L,
                          n_alias=len(aliases), state_slot=layer if st_prev is None else 0),
        input_output_aliases=aliases,
        out_shape=out_shape,
        grid=(nb // sb,),
        in_specs=in_specs,
        out_specs=out_specs,
        scratch_shapes=[pltpu.VMEM((sb, n_slab, L, LANES), F32), pltpu.VMEM((sb, L, C_INNER), F32),
                        pltpu.VMEM((sb, 2, C_STATE, C_INNER), F32)],
        compiler_params=_cparams(1),
        name="ssd_init" if has_init else "ssd",
    )(*args)


def _outp_kernel(a_ref, b_ref, c_ref, x_ref, mod_ref, w_ref, x1_o, h2_o):
    mix = jnp.concatenate([a_ref[0], b_ref[0], c_ref[0]], axis=-1)
    o = jnp.dot(mix, w_ref[...], preferred_element_type=F32)
    gate1 = mod_ref[0, 2:3, :]
    shift2 = mod_ref[0, 3:4, :]
    scale2 = mod_ref[0, 4:5, :]
    x1 = x_ref[0] + gate1 * o
    ms = jnp.mean(x1 * x1, axis=-1, keepdims=True)
    x1_o[0] = x1
    h2_o[0] = ((x1 * lax.rsqrt(ms + EPS)) * (1.0 + scale2) + shift2).astype(BF16)


def _outp(a_out, b_out, c_out, x, mod, w_out_b, layer, *, per_batch_mod):
    nb, L, d = x.shape
    tm = min(OUTP_TM, L)
    mod_map = (lambda b, i: (b, 0, 0)) if per_batch_mod else (lambda b, i: (0, 0, 0))
    row_spec = lambda w: pl.BlockSpec((1, tm, w), lambda b, i: (b, i, 0))
    return pl.pallas_call(
        _outp_kernel,
        out_shape=[jax.ShapeDtypeStruct((nb, L, d), F32), jax.ShapeDtypeStruct((nb, L, d), BF16)],
        grid=(nb, L // tm),
        in_specs=[row_spec(A_WIDTH), row_spec(B_WIDTH), row_spec(C_INNER), row_spec(d),
                  pl.BlockSpec((1, 6, d), mod_map),
                  pl.BlockSpec((None, d, d), lambda b, i: (layer, 0, 0))],
        out_specs=[row_spec(d), row_spec(d)],
        compiler_params=_cparams(2),
        name="outp",
    )(a_out, b_out, c_out, x, mod, w_out_b)


HALO = 16


def _ffn_kernel(h_ref, hp_ref, hn_ref, x1_ref, mod_ref, wu_ref, cw_ref, cb_ref, wd_ref, fw_ref,
                o_ref, hext_scr, u_scr, acc_scr, *, seq_len, final_norm):
    tm = h_ref.shape[1]
    i = pl.program_id(1)
    n_chunks = wu_ref.shape[0]
    rb = min(FFN_RB, tm)

    has_prev = ((i * tm) & (seq_len - 1)) != 0
    has_next = ((i * tm + tm) & (seq_len - 1)) != 0
    hext_scr[0:HALO, :] = jnp.where(has_prev, hp_ref[0], jnp.zeros_like(hp_ref[0]))
    hext_scr[HALO:HALO + tm, :] = h_ref[0]
    hext_scr[HALO + tm:HALO + tm + HALO, :] = jnp.where(has_next, hn_ref[0], jnp.zeros_like(hn_ref[0]))
    acc_scr[...] = jnp.zeros_like(acc_scr)
    row8 = lax.broadcasted_iota(jnp.int32, (8, 1), 0)

    n_uslab = 2 * FFN_FC // LANES
    n_oslab = acc_scr.shape[0]
    half = rb // 2

    def up(c, slot):
        u = jnp.dot(hext_scr[...], wu_ref[c], preferred_element_type=F32)
        for s in range(n_uslab):
            u_scr[slot, s] = u[:, s * LANES:(s + 1) * LANES]

    def conv_down(c, slot):
        cw = cw_ref[c]
        bias = cb_ref[c]
        for r0 in range(0, tm, rb):
            base = HALO + r0
            ys = []
            for s in range(n_uslab):
                ls = slice(s * LANES, (s + 1) * LANES)
                even = u_scr[slot, s, pl.ds(base, half, stride=2), :]
                odd = u_scr[slot, s, pl.ds(base + 1, half, stride=2), :]
                odd_before = u_scr[slot, s, pl.ds(base - 1, half, stride=2), :]
                even_after = u_scr[slot, s, pl.ds(base + 2, half, stride=2), :]
                if r0 > 0 and r0 % seq_len == 0:
                    odd_before = jnp.concatenate(
                        [jnp.where(row8 == 0, 0.0, odd_before[0:8]), odd_before[8:]], axis=0)
                if r0 + rb < tm and (r0 + rb) % seq_len == 0:
                    even_after = jnp.concatenate(
                        [even_after[:half - 8], jnp.where(row8 == 7, 0.0, even_after[half - 8:])], axis=0)
                w0, w1, w2, b = cw[0:1, ls], cw[1:2, ls], cw[2:3, ls], bias[:, ls]
                y_even = odd_before * w0 + even * w1 + odd * w2 + b
                y_odd = even * w0 + odd * w1 + even_after * w2 + b
                ys.append(jnp.concatenate([y_even, y_odd], axis=0))
            n_g = n_uslab // 2
            act = jnp.concatenate([ys[s] * _sigmoid(ys[s]) * ys[n_g + s] for s in range(n_g)], axis=1)
            dn = jnp.dot(act.astype(BF16), wd_ref[c], preferred_element_type=F32)
            for s in range(n_oslab):
                ls = slice(s * LANES, (s + 1) * LANES)
                acc_scr[s, pl.ds(r0, half, stride=2), :] += dn[0:half, ls]
                acc_scr[s, pl.ds(r0 + 1, half, stride=2), :] += dn[half:rb, ls]

    up(0, 0)

    def pair(k, carry):
        c = 2 * k
        up(c + 1, 1)
        conv_down(c, 0)
        up(c + 2, 0)
        conv_down(c + 1, 1)
        return carry

    lax.fori_loop(0, (n_chunks - 1) // 2, pair, 0)
    if n_chunks % 2 == 1:
        conv_down(n_chunks - 1, 0)
    else:
        up(n_chunks - 1, 1)
        conv_down(n_chunks - 2, 0)
        conv_down(n_chunks - 1, 1)

    gate2 = mod_ref[0, 5:6, :]
    ffn_out = jnp.concatenate([acc_scr[s] for s in range(n_oslab)], axis=1)
    out = x1_ref[0] + gate2 * ffn_out
    if final_norm:
        ms = jnp.mean(out * out, axis=-1, keepdims=True)
        out = out * lax.rsqrt(ms + EPS) * fw_ref[...]
    o_ref[0] = out


def _ffn(h2, x1, mod, wu_c, cw_c, cb_c, wd_c, layer, final_w, *, per_batch_mod, seq_len, final_norm):
    nb, L, d = x1.shape
    tm = min(FFN_TM, L)
    n_tiles = L // tm
    rb = min(FFN_RB, tm)
    assert seq_len & (seq_len - 1) == 0 and L % seq_len == 0
    assert seq_len % tm == 0 or (tm % seq_len == 0 and seq_len % rb == 0)
    n_chunks = wu_c.shape[1]
    hb = tm // HALO
    n_hblk = L // HALO
    mod_map = (lambda b, i: (b, 0, 0)) if per_batch_mod else (lambda b, i: (0, 0, 0))
    row_spec = lambda w: pl.BlockSpec((1, tm, w), lambda b, i: (b, i, 0))
    const3 = lambda s: pl.BlockSpec(s, lambda b, i: (0, 0, 0))
    return pl.pallas_call(
        functools.partial(_ffn_kernel, seq_len=seq_len, final_norm=final_norm),
        out_shape=jax.ShapeDtypeStruct((nb, L, d), F32),
        grid=(nb, n_tiles),
        in_specs=[row_spec(d),
                  pl.BlockSpec((1, HALO, d), lambda b, i: (b, jnp.maximum(i * hb - 1, 0), 0)),
                  pl.BlockSpec((1, HALO, d), lambda b, i: (b, jnp.minimum((i + 1) * hb, n_hblk - 1), 0)),
                  row_spec(d),
                  pl.BlockSpec((1, 6, d), mod_map),
                  pl.BlockSpec((None, n_chunks, d, 2 * FFN_FC), lambda b, i: (layer, 0, 0, 0)),
                  const3((n_chunks, 3, 2 * FFN_FC)),
                  const3((n_chunks, 1, 2 * FFN_FC)),
                  pl.BlockSpec((None, n_chunks, FFN_FC, d), lambda b, i: (layer, 0, 0, 0)),
                  pl.BlockSpec((1, d), lambda b, i: (0, 0))],
        out_specs=row_spec(d),
        scratch_shapes=[pltpu.VMEM((tm + 2 * HALO, d), BF16),
                        pltpu.VMEM((2, 2 * FFN_FC // LANES, tm + 2 * HALO, LANES), F32),
                        pltpu.VMEM((d // LANES, tm, LANES), F32)],
        compiler_params=_cparams(2),
        name="ffn_final" if final_norm else "ffn",
    )(h2, h2, h2, x1, mod, wu_c, cw_c, cb_c, wd_c, final_w)


def _rope_tables(L, d):
    rows = L // GRID_W
    row = jnp.repeat(jnp.arange(rows), GRID_W).astype(F32)
    col = jnp.tile(jnp.arange(GRID_W), rows).astype(F32)
    quarter = d // 4
    inv = ROPE_THETA ** (-jnp.arange(quarter, dtype=F32) / quarter)
    ang_r = row[:, None] * inv[None, :]
    ang_c = col[:, None] * inv[None, :]
    ang = jnp.concatenate([ang_r, ang_r, ang_c, ang_c], axis=-1)
    cos, sin = jnp.cos(ang), jnp.sin(ang)
    even = ((jnp.arange(d) // quarter) % 2 == 0)[None, :]
    s_up = jnp.where(even, -sin, 0.0)
    s_dn = jnp.where(even, 0.0, sin)
    reps = LANES // d
    return tuple(jnp.tile(t, (1, reps)) for t in (cos, s_up, s_dn))


def _pad_lanes(v, width=LANES):
    v = v.reshape(1, -1).astype(F32)
    return jnp.pad(v, ((0, 0), (0, width - v.shape[1])))


def _layer_params(l, a_q_norm, a_k_norm, b_lambda, b_subln, ssm_conv_w, ssm_conv_b, ssm_A_log,
                  ssm_dt_bias, ssm_D, ssm_norm_w, ffn_conv_w, ffn_conv_b):
    n_chunks = D_FF // FFN_FC

    def gv_chunks(t):
        lead = t.shape[:-1]
        g = t[..., :D_FF].reshape(lead + (n_chunks, FFN_FC))
        v = t[..., D_FF:].reshape(lead + (n_chunks, FFN_FC))
        gv = jnp.concatenate([g, v], axis=-1)
        return jnp.moveaxis(gv, -2, 0)

    return dict(
        qn=jnp.tile(a_q_norm[l], 2).reshape(1, LANES),
        kn=jnp.tile(a_k_norm[l], 2).reshape(1, LANES),
        lamv=b_lambda[l],
        subln=jnp.tile(b_subln[l], 2).reshape(1, LANES),
        conv_w=ssm_conv_w[l],
        conv_b=ssm_conv_b[l].reshape(1, C_CONV_CH),
        dtb=_pad_lanes(ssm_dt_bias[l]),
        alog=_pad_lanes(ssm_A_log[l]),
        dexp=jnp.repeat(ssm_D[l], C_HEADDIM).reshape(1, C_INNER),
        norm_w=ssm_norm_w[l].reshape(1, C_INNER),
        cw=gv_chunks(ffn_conv_w[l]),
        cb=gv_chunks(ffn_conv_b[l].reshape(1, 2 * D_FF)),
    )


def _block(x, mod, p, wts, rope_tabs, caches, state, layer, *, seq_shape, per_batch_mod, is_ctx, lam_init,
           final_w, final_norm, new_caches=None):
    n_seq, seq_len = seq_shape
    depth = wts["w_in"].shape[0]
    outs = _proj(x, mod, wts["w_in"], layer, p["qn"], p["kn"], rope_tabs,
                 per_batch_mod=per_batch_mod, kv_f32=is_ctx, seq_len=seq_len, values_t=caches is not None,
                 kv_prev=None if new_caches is None else new_caches[:4])
    seq = lambda t: t.reshape(n_seq, seq_len, t.shape[-1]) if t.ndim == 3 else t
    qa, k3, va, qb, kb, vb, xbc, z, dt = [seq(t) for t in outs[:9]]
    if caches is None:
        a_out, b_out = _attn_rows(qa, k3, va, qb, kb, vb, p["lamv"], p["subln"], lam_init=lam_init)
    else:
        a_out, b_out = _attn(qa, k3, va, qb, kb, vb, p["lamv"], p["subln"], caches, layer, lam_init=lam_init)
    ssd_out = _ssd(xbc, z, dt, p["conv_w"], p["conv_b"], p["dtb"], p["alog"], p["dexp"], p["norm_w"],
                   state, layer, emit_state=is_ctx, depth=depth,
                   st_prev=None if new_caches is None else new_caches[4])
    c_out = ssd_out[0]
    flat = lambda t: t.reshape(x.shape[0], x.shape[1], t.shape[-1])
    x1, h2 = _outp(flat(a_out), flat(b_out), flat(c_out), x, mod, wts["w_out"], layer,
                   per_batch_mod=per_batch_mod)
    x2 = _ffn(h2, x1, mod, wts["wu"], p["cw"], p["cb"], wts["wd"], layer, final_w,
              per_batch_mod=per_batch_mod, seq_len=seq_len, final_norm=final_norm)
    extras = None
    if is_ctx:
        extras = tuple(outs[9:13]) + (ssd_out[1],)
    return x2, extras


def kernel(x_prompt, x_sample, cache_a_k, cache_a_v, cache_b_k, cache_b_v, state_ssm, c, c_ctx, ada_w, ada_b, w_in, a_q_norm, a_k_norm, b_lambda, b_subln, ssm_conv_w, ssm_conv_b, ssm_A_log, ssm_dt_bias, ssm_D, ssm_norm_w, w_out, ffn_up, ffn_conv_w, ffn_conv_b, ffn_down, final_norm_w):
    depth = w_in.shape[0]
    nbp, lp, d = x_prompt.shape
    nbs, ls, _ = x_sample.shape
    past = cache_a_k.shape[2]

    mod_rows = 8
    assert 1 + nbs <= mod_rows
    cvecs = jnp.concatenate([c_ctx[None, :], c, jnp.zeros((mod_rows - 1 - nbs, d), F32)], axis=0)
    mod = _modulation(cvecs, ada_w, ada_b)

    rope_tabs = _rope_tables(ls, HEAD_DIM) + _rope_tables(ls, B_HALF)
    caches = (cache_a_k.reshape(nbs, depth, past, A_KV_HEADS * HEAD_DIM),
              cache_a_v.reshape(nbs, depth, past, A_KV_HEADS * HEAD_DIM),
              cache_b_k.reshape(nbs, depth, past, B_WIDTH),
              cache_b_v.reshape(nbs, depth, past, B_WIDTH))
    final_w = final_norm_w.reshape(1, d)
    n_chunks = D_FF // FFN_FC
    wts = dict(
        w_in=_pad_cast_bf16(w_in, PROJ_PAD),
        w_out=_cast_bf16(w_out),
        wu=_gv_cast_bf16(ffn_up),
        wd=_cast_bf16(ffn_down).reshape(depth, n_chunks, FFN_FC, d),
    )

    yp = x_prompt.reshape(1, nbp * lp, d)
    ys = x_sample
    new_caches = None
    for l in range(depth):
        lam_init = 0.8 - 0.6 * math.exp(-0.3 * l)
        p = _layer_params(l, a_q_norm, a_k_norm, b_lambda, b_subln, ssm_conv_w, ssm_conv_b,
                          ssm_A_log, ssm_dt_bias, ssm_D, ssm_norm_w, ffn_conv_w, ffn_conv_b)
        last = l == depth - 1
        mod_ctx = mod[l, 0].reshape(1, 6, d)
        mod_lat = mod[l, 1:1 + nbs].reshape(nbs, 6, d)
        yp, new_caches = _block(yp, mod_ctx, p, wts, None, None, None, l, seq_shape=(nbp, lp),
                                per_batch_mod=False, is_ctx=True, lam_init=lam_init,
                                final_w=final_w, final_norm=last, new_caches=new_caches)
        ys, _ = _block(ys, mod_lat, p, wts, rope_tabs, caches, state_ssm, l, seq_shape=(nbs, ls),
                       per_batch_mod=True, is_ctx=False, lam_init=lam_init,
                       final_w=final_w, final_norm=last)

    y_prompt = yp.reshape(nbp, lp, d)
    new_a_k = new_caches[0].reshape(nbp, depth, lp, A_KV_HEADS, HEAD_DIM)
    new_a_v = new_caches[1].reshape(nbp, depth, lp, A_KV_HEADS, HEAD_DIM)
    new_b_k = new_caches[2].reshape(nbp, depth, lp, B_HEADS, 2, B_HALF)
    new_b_v = new_caches[3].reshape(nbp, depth, lp, B_HEADS, 2 * B_HALF)
    new_ssm = new_caches[4]
    return (y_prompt, ys, new_a_k, new_a_v, new_b_k, new_b_v, new_ssm)
```

```python
import functools
import math

import jax
import jax.numpy as jnp
from jax import lax
from jax.experimental import pallas as pl
from jax.experimental.pallas import tpu as pltpu

F32 = jnp.float32
BF16 = jnp.bfloat16

LANES = 128
VMEM_LIMIT_BYTES = 56 * 1024 * 1024

D_MODEL = 1024
GRID_W = 64
SSD_CHUNK = 128
ROPE_THETA = 10000.0
EPS = 1e-6
HEAD_DIM = 64
A_Q_HEADS = 6
A_KV_HEADS = 2
A_GROUP = A_Q_HEADS // A_KV_HEADS
A_WIDTH = A_Q_HEADS * HEAD_DIM
B_HEADS = 4
B_HALF = 32
B_WIDTH = B_HEADS * 2 * B_HALF
C_HEADS = 6
C_HEADDIM = 64
C_INNER = C_HEADS * C_HEADDIM
C_GROUPS = 2
C_STATE = 128
C_CONV_CH = C_INNER + 2 * C_GROUPS * C_STATE
D_FF = 2816
PROJ_WIDTH = A_WIDTH + 4 * HEAD_DIM + 3 * B_WIDTH + 2 * C_INNER + 4 * C_STATE + 2 * C_HEADS
PROJ_PAD = ((PROJ_WIDTH + 2 * LANES - 1) // (2 * LANES)) * (2 * LANES)

OFF_AQ = 0
OFF_AK = OFF_AQ + A_WIDTH
OFF_AV = OFF_AK + A_KV_HEADS * HEAD_DIM
OFF_BQ = OFF_AV + A_KV_HEADS * HEAD_DIM
OFF_BK = OFF_BQ + B_WIDTH
OFF_BV = OFF_BK + B_WIDTH
OFF_CX = OFF_BV + B_WIDTH
OFF_CZ = OFF_CX + C_INNER
OFF_CB = OFF_CZ + C_INNER
OFF_CC = OFF_CB + C_GROUPS * C_STATE
OFF_DT = OFF_CC + C_GROUPS * C_STATE

MOD_BLOCK = 1024
PROJ_TM = 512
OUTP_TM = 1024
FFN_TM = 512
FFN_FC = 256
FFN_RB = 256
ATTN_TQ = 512
ATTN_KC = 512
ATTN_AHEAD = 2
ROW_BLOCK = 256
CONV_ROWS = 128
SSD_UNROLL = 4
SSD_STEP_ROWS = 512
NEG_BIG = -1e30
LOG2E = 1.4426950408889634


def _cparams(n_grid):
    return pltpu.CompilerParams(
        dimension_semantics=("parallel",) * n_grid,
        vmem_limit_bytes=VMEM_LIMIT_BYTES,
    )


def _sigmoid(x):
    return 1.0 / (1.0 + jnp.exp(-x))


def _left_mask():
    lane = lax.broadcasted_iota(jnp.int32, (1, LANES), 1)
    return lane < (LANES // 2)


def _pair_rmsnorm(y, w):
    left = _left_mask()
    sq = y * y
    s_l = jnp.sum(jnp.where(left, sq, 0.0), axis=-1, keepdims=True)
    s_r = jnp.sum(jnp.where(left, 0.0, sq), axis=-1, keepdims=True)
    ms = jnp.where(left, s_l, s_r) * (1.0 / HEAD_DIM)
    return y * lax.rsqrt(ms + EPS) * w


def _rope(y, cos, s_up, s_dn, quarter):
    return (y * cos
            + pltpu.roll(y, LANES - quarter, 1) * s_up
            + pltpu.roll(y, quarter, 1) * s_dn)


def _modulation_kernel(c_ref, w_ref, b_ref, o_ref):
    c = c_ref[...]
    s = c * _sigmoid(c)
    o_ref[0] = jnp.dot(s.astype(BF16), w_ref[0].astype(BF16),
                       preferred_element_type=F32) + b_ref[0]


def _modulation(cvecs, ada_w, ada_b):
    depth, d, width = ada_w.shape
    rows = cvecs.shape[0]
    return pl.pallas_call(
        _modulation_kernel,
        out_shape=jax.ShapeDtypeStruct((depth, rows, width), F32),
        grid=(depth, width // MOD_BLOCK),
        in_specs=[
            pl.BlockSpec((rows, d), lambda l, j: (0, 0)),
            pl.BlockSpec((1, d, MOD_BLOCK), lambda l, j: (l, 0, j)),
            pl.BlockSpec((1, 1, MOD_BLOCK), lambda l, j: (l, 0, j)),
        ],
        out_specs=pl.BlockSpec((1, rows, MOD_BLOCK), lambda l, j: (l, 0, j)),
        compiler_params=_cparams(2),
        name="modulation",
    )(cvecs, ada_w, ada_b.reshape(depth, 1, width))


CAST_ROWS = 1024
PAD_CAST_COLS = 256


def _cast_kernel(x_ref, o_ref):
    o_ref[...] = x_ref[...].astype(o_ref.dtype)


def _cast_bf16(x):
    depth, rows, cols = x.shape
    rows_blk = max(r for r in range(8, CAST_ROWS + 1, 8) if rows % r == 0)
    spec = pl.BlockSpec((1, rows_blk, cols), lambda l, r: (l, r, 0))
    return pl.pallas_call(
        _cast_kernel, out_shape=jax.ShapeDtypeStruct(x.shape, BF16),
        grid=(depth, rows // rows_blk), in_specs=[spec], out_specs=spec,
        compiler_params=_cparams(2), name="cast_bf16",
    )(x)


def _pad_cast_kernel(x_ref, o_ref, *, n_cols):
    tr = x_ref.shape[0]
    col = lax.broadcasted_iota(jnp.int32, (tr, 1), 0) + pl.program_id(0) * tr
    for l in range(x_ref.shape[1]):
        x = jnp.where(col < n_cols, x_ref[:, l, :], 0.0)
        o_ref[l] = x.T.astype(BF16)


def _pad_cast_bf16(x, width):
    depth, rows, cols = x.shape
    return pl.pallas_call(
        functools.partial(_pad_cast_kernel, n_cols=cols),
        out_shape=jax.ShapeDtypeStruct((depth, rows, width), BF16),
        grid=(width // PAD_CAST_COLS,),
        in_specs=[pl.BlockSpec((PAD_CAST_COLS, depth, rows), lambda i: (i, 0, 0))],
        out_specs=pl.BlockSpec((depth, rows, PAD_CAST_COLS), lambda i: (0, 0, i)),
        compiler_params=_cparams(1), name="pad_cast_bf16",
    )(jnp.transpose(x, (2, 0, 1)))


def _gv_cast_kernel(g_ref, v_ref, o_ref):
    o_ref[0, 0, :, 0:FFN_FC] = g_ref[0].astype(BF16)
    o_ref[0, 0, :, FFN_FC:2 * FFN_FC] = v_ref[0].astype(BF16)


def _gv_cast_bf16(up):
    depth, d, _ = up.shape
    n_chunks = D_FF // FFN_FC
    return pl.pallas_call(
        _gv_cast_kernel, out_shape=jax.ShapeDtypeStruct((depth, n_chunks, d, 2 * FFN_FC), BF16),
        grid=(depth, n_chunks),
        in_specs=[pl.BlockSpec((1, d, FFN_FC), lambda l, c: (l, 0, c)),
                  pl.BlockSpec((1, d, FFN_FC), lambda l, c: (l, 0, n_chunks + c))],
        out_specs=pl.BlockSpec((1, 1, d, 2 * FFN_FC), lambda l, c: (l, c, 0, 0)),
        compiler_params=_cparams(2), name="gv_cast_bf16",
    )(up, up)


def _proj_kernel(*refs, rope, kv_f32, n_alias, cache_slot):
    x_ref, mod_ref, w_ref, qn_ref, kn_ref = refs[:5]
    pos = 5
    if rope:
        ca_ref, sau_ref, sad_ref, cb_ref, sbu_ref, sbd_ref = refs[pos:pos + 6]
        pos += 6
    pos += n_alias
    (qa_o, k3_o, vta_o, qb_o, kb_o, vtb_o, xbc_o, z_o, dt_o) = refs[pos:pos + 9]
    pos += 9
    if kv_f32:
        ka32_o, va32_o, kb32_o, vb32_o = refs[pos:pos + 4]
        pos += 4
    p_scr = refs[pos]

    x = x_ref[0]
    ms = jnp.mean(x * x, axis=-1, keepdims=True)
    shift = mod_ref[0, 0:1, :]
    scale = mod_ref[0, 1:2, :]
    h = (x * lax.rsqrt(ms + EPS)) * (1.0 + scale) + shift
    p_scr[...] = jnp.dot(h.astype(BF16), w_ref[...], preferred_element_type=F32)

    left = _left_mask()

    def rope_a(y):
        if not rope:
            return y
        return _rope(y, ca_ref[...], sau_ref[...], sad_ref[...], HEAD_DIM // 4)

    def rope_b(y):
        if not rope:
            return y
        return _rope(y, cb_ref[...], sbu_ref[...], sbd_ref[...], B_HALF // 4)

    values_t = len(vta_o.shape) == 4
    if values_t:
        n_sub, sub_len = vta_o.shape[0], vta_o.shape[3]
        ones_half = jnp.ones((HEAD_DIM, sub_len), BF16)

    def store_vt(dst, h, vt_half, e):
        for s in range(n_sub):
            dst[s, h, e * HEAD_DIM:(e + 1) * HEAD_DIM, :] = vt_half[:, s * sub_len:(s + 1) * sub_len]
            dst[s, h, (1 - e) * HEAD_DIM:(2 - e) * HEAD_DIM, :] = ones_half

    a_scale = HEAD_DIM ** -0.5 * LOG2E
    for j in range(A_WIDTH // LANES):
        y = p_scr[:, OFF_AQ + j * LANES:OFF_AQ + (j + 1) * LANES]
        y = rope_a(_pair_rmsnorm(y, qn_ref[...]))
        qa_o[0, :, j * LANES:(j + 1) * LANES] = (y * a_scale).astype(BF16)

    k = rope_a(_pair_rmsnorm(p_scr[:, OFF_AK:OFF_AK + LANES], kn_ref[...]))
    v = p_scr[:, OFF_AV:OFF_AV + LANES]
    if kv_f32:
        kv_sub, kv_len = ka32_o.shape[0], ka32_o.shape[2]
        for u in range(kv_sub):
            ka32_o[u, cache_slot] = k[u * kv_len:(u + 1) * kv_len]
            va32_o[u, cache_slot] = v[u * kv_len:(u + 1) * kv_len]
        for ref in (ka32_o, va32_o, kb32_o, vb32_o):
            for other in range(ref.shape[1]):
                if other != cache_slot:
                    ref[:, other] = jnp.zeros((kv_sub, kv_len, ref.shape[3]), F32)
    swapped = pltpu.roll(k, LANES // 2, 1)
    k3_o[0, :, 0:LANES] = jnp.where(left, k, swapped).astype(BF16)
    k3_o[0, :, LANES:2 * LANES] = k.astype(BF16)
    k3_o[0, :, 2 * LANES:3 * LANES] = jnp.where(left, swapped, k).astype(BF16)
    if values_t:
        vt = v.T.astype(BF16)
        for h in range(A_Q_HEADS):
            g = h // A_GROUP
            store_vt(vta_o, h, vt[g * HEAD_DIM:(g + 1) * HEAD_DIM], h % 2)
    else:
        v_sw = pltpu.roll(v, LANES // 2, 1)
        vta_o[0, :, 0:LANES] = jnp.where(left, v, v_sw).astype(BF16)
        vta_o[0, :, LANES:2 * LANES] = v.astype(BF16)
        vta_o[0, :, 2 * LANES:3 * LANES] = jnp.where(left, v_sw, v).astype(BF16)

    b_scale = B_HALF ** -0.5 * LOG2E
    for j in range(B_WIDTH // LANES):
        sl = slice(j * LANES, (j + 1) * LANES)
        q = rope_b(p_scr[:, OFF_BQ + j * LANES:OFF_BQ + (j + 1) * LANES])
        qb_o[0, :, sl] = (q * b_scale).astype(BF16)
        kb = rope_b(p_scr[:, OFF_BK + j * LANES:OFF_BK + (j + 1) * LANES])
        vb = p_scr[:, OFF_BV + j * LANES:OFF_BV + (j + 1) * LANES]
        kb_o[0, :, sl] = kb.astype(BF16)
        if values_t:
            vbt = vb.T.astype(BF16)
            for e in range(2):
                store_vt(vtb_o, 2 * j + e, vbt[e * HEAD_DIM:(e + 1) * HEAD_DIM], e)
        else:
            vtb_o[0, :, sl] = vb.astype(BF16)
        if kv_f32:
            for u in range(kv_sub):
                kb32_o[u, cache_slot, :, sl] = kb[u * kv_len:(u + 1) * kv_len]
                vb32_o[u, cache_slot, :, sl] = vb[u * kv_len:(u + 1) * kv_len]

    xbc_sub, xbc_len = xbc_o.shape[0], xbc_o.shape[2]
    for s in range(C_CONV_CH // LANES):
        col = OFF_CX + s * LANES if s < C_INNER // LANES else OFF_CB + s * LANES - C_INNER
        for u in range(xbc_sub):
            xbc_o[u, s] = p_scr[u * xbc_len:(u + 1) * xbc_len, col:col + LANES]
    z_o[0] = p_scr[:, OFF_CZ:OFF_CZ + C_INNER]
    dt_o[0] = p_scr[:, OFF_DT:OFF_DT + LANES]


def _proj(x, mod, w_in_p, layer, qn, kn, rope_tabs, *, per_batch_mod, kv_f32, seq_len, values_t,
          kv_prev=None):
    depth = w_in_p.shape[0]
    nb, L, d = x.shape
    tm = min(PROJ_TM, L)
    rope = rope_tabs is not None
    mod_map = (lambda b, i: (b, 0, 0)) if per_batch_mod else (lambda b, i: (0, 0, 0))
    row_spec = lambda w: pl.BlockSpec((1, tm, w), lambda b, i: (b, i, 0))
    in_specs = [
        row_spec(d),
        pl.BlockSpec((1, 6, d), mod_map),
        pl.BlockSpec((None, d, PROJ_PAD), lambda b, i: (layer, 0, 0)),
        pl.BlockSpec((1, LANES), lambda b, i: (0, 0)),
        pl.BlockSpec((1, LANES), lambda b, i: (0, 0)),
    ]
    args = [x, mod, w_in_p, qn, kn]
    if rope:
        in_specs += [pl.BlockSpec((tm, LANES), lambda b, i: (i, 0))] * 6
        args += list(rope_tabs)
    widths = [(A_WIDTH, BF16), (A_WIDTH, BF16), (-A_Q_HEADS if values_t else A_WIDTH, BF16),
              (B_WIDTH, BF16), (B_WIDTH, BF16), (-B_HEADS if values_t else B_WIDTH, BF16),
              ("slabs", F32), (C_INNER, F32), (LANES, F32)]
    kv_widths = [LANES, LANES, B_WIDTH, B_WIDTH] if kv_f32 else []
    widths += [("cache", w) for w in kv_widths]
    aliases = {}
    if kv_prev is not None:
        for k_idx, prev in enumerate(kv_prev):
            aliases[len(args)] = len(widths) - len(kv_widths) + k_idx
            in_specs.append(pl.BlockSpec(memory_space=pl.ANY))
            args.append(prev)
    out_shape, out_specs = [], []
    n_seq = nb * L // seq_len
    in_seq = seq_len >= tm
    assert (L == seq_len) if in_seq else (nb == 1 and tm % seq_len == 0)
    n_slab = C_CONV_CH // LANES
    for w, dt in widths:
        if w == "slabs":
            out_shape.append(jax.ShapeDtypeStruct((n_seq, n_slab, seq_len, LANES), dt))
            out_specs.append(pl.BlockSpec((1, n_slab, tm, LANES), lambda b, i: (b, 0, i, 0)) if in_seq else
                             pl.BlockSpec((tm // seq_len, n_slab, seq_len, LANES), lambda b, i: (i, 0, 0, 0)))
        elif w == "cache":
            assert not in_seq
            out_shape.append(jax.ShapeDtypeStruct((n_seq, depth, seq_len, dt), F32))
            out_specs.append(
                pl.BlockSpec((tm // seq_len, depth, seq_len, dt), lambda b, i: (i, 0, 0, 0)) if kv_prev is None
                else pl.BlockSpec((tm // seq_len, 1, seq_len, dt), lambda b, i: (i, layer, 0, 0)))
        elif w > 0:
            out_shape.append(jax.ShapeDtypeStruct((nb, L, w), dt))
            out_specs.append(row_spec(w))
        else:
            out_shape.append(jax.ShapeDtypeStruct((n_seq, -w, LANES, seq_len), dt))
            out_specs.append(pl.BlockSpec((1, -w, LANES, tm), lambda b, i: (b, 0, 0, i)) if in_seq else
                             pl.BlockSpec((tm // seq_len, -w, LANES, seq_len), lambda b, i: (i, 0, 0, 0)))
    return pl.pallas_call(
        functools.partial(_proj_kernel, rope=rope, kv_f32=kv_f32, n_alias=len(aliases),
                          cache_slot=layer if kv_prev is None else 0),
        out_shape=out_shape,
        grid=(nb, L // tm),
        in_specs=in_specs,
        out_specs=out_specs,
        input_output_aliases=aliases,
        scratch_shapes=[pltpu.VMEM((tm, PROJ_PAD), F32)],
        compiler_params=_cparams(2),
        name="proj_rope" if rope else "proj",
    )(*args)


class _ScoreMap:
    def __init__(self, tag, qm, chunks, s_ref):
        self.tag, self.qm, self.chunks, self.s_ref = tag, qm, chunks, s_ref
        self.offsets = [sum(w for _, _, w in chunks[:c]) for c in range(len(chunks))]
        self.m_part = self.m = self.acc = None

    def pass1(self, c):
        k_fn, _, w = self.chunks[c]
        tq = self.qm.shape[0]
        s = lax.dot_general(k_fn(), self.qm, (((1,), (1,)), ((), ())), preferred_element_type=F32)
        self.s_ref[self.offsets[c]:self.offsets[c] + w, :] = s
        part = jnp.max(s.reshape(w // 8, 8, tq), axis=0)
        self.m_part = part if self.m_part is None else jnp.maximum(self.m_part, part)

    def finish_max(self):
        self.m = jnp.max(self.m_part, axis=0, keepdims=True)

    def pass2(self, c):
        _, vt_fn, w = self.chunks[c]
        e = jnp.exp2(self.s_ref[self.offsets[c]:self.offsets[c] + w, :] - self.m).astype(BF16)
        pv = jnp.dot(vt_fn(), e, preferred_element_type=F32)
        self.acc = pv if self.acc is None else self.acc + pv


def _run_score_maps(maps, on_done):
    for mp in maps[:ATTN_AHEAD]:
        for c in range(len(mp.chunks)):
            mp.pass1(c)
        mp.finish_max()
    for i, mp in enumerate(maps):
        nxt = maps[i + ATTN_AHEAD] if i + ATTN_AHEAD < len(maps) else None
        n_next = len(nxt.chunks) if nxt is not None else 0
        for c in range(max(len(mp.chunks), n_next)):
            if c < len(mp.chunks):
                mp.pass2(c)
            if c < n_next:
                nxt.pass1(c)
        if nxt is not None:
            nxt.finish_max()
        on_done(mp)


def _normalised_half(acc, e):
    l = acc[(1 - e) * HEAD_DIM:(1 - e) * HEAD_DIM + 1, :]
    return acc[e * HEAD_DIM:(e + 1) * HEAD_DIM, :] * (1.0 / l)


def _attn_kernel(*refs, has_cache, lam_init, seq_len):
    qa_ref, k3_ref, vta_ref, qb_ref, kb_ref, vtb_ref, lamv_ref, subln_ref = refs[:8]
    pos = 8
    if has_cache:
        cka_ref, cva_ref, ckb_ref, cvb_ref = refs[pos:pos + 4]
        pos += 4
    ao_ref, bo_ref, s_scr = refs[pos:pos + 3]

    left = _left_mask()
    lane = lax.broadcasted_iota(jnp.int32, (1, LANES), 1)
    kc = min(ATTN_KC, seq_len)
    n_kc = seq_len // kc
    maps = []

    def add_map(tag, qm, chunks):
        maps.append(_ScoreMap(tag, qm, chunks, s_scr.at[len(maps) % (ATTN_AHEAD + 1)]))

    def new_chunks(k_ref, vt_ref, j, h):
        sl = slice(j * LANES, (j + 1) * LANES)
        out = []
        for c in range(n_kc):
            rows = slice(c * kc, (c + 1) * kc)
            out.append((functools.partial(lambda r, s: k_ref[0, r, s], rows, sl),
                        functools.partial(lambda r, hh: vt_ref[0, hh, :, r], rows, h), kc))
        return out

    def value_block(vt_half, e):
        ones = jnp.ones_like(vt_half)
        return jnp.concatenate([vt_half, ones] if e == 0 else [ones, vt_half], axis=0).astype(BF16)

    if has_cache:
        ck = cka_ref[0, 0]
        ck_sw = pltpu.roll(ck, LANES // 2, 1)
        ck3 = [jnp.where(left, ck, ck_sw), ck, jnp.where(left, ck_sw, ck)]
        cvt = cva_ref[0, 0].T
        past = ck.shape[0]
    for j in range(A_WIDTH // LANES):
        q = qa_ref[0, :, j * LANES:(j + 1) * LANES]
        for e in range(2):
            h = 2 * j + e
            g = h // A_GROUP
            chunks = new_chunks(k3_ref, vta_ref, j, h)
            if has_cache:
                ckj = ck3[j].astype(BF16)
                cvj = value_block(cvt[g * HEAD_DIM:(g + 1) * HEAD_DIM], e)
                chunks.append((lambda a=ckj: a, lambda a=cvj: a, past))
            qm = jnp.where(left if e == 0 else jnp.logical_not(left), q, jnp.zeros_like(q))
            add_map(("a", j, e, 0), qm, chunks)

    lv = lamv_ref[...]
    lam = (jnp.exp(jnp.sum(lv[0:1] * lv[1:2], axis=-1, keepdims=True))
           - jnp.exp(jnp.sum(lv[2:3] * lv[3:4], axis=-1, keepdims=True)) + lam_init)
    for j in range(B_WIDTH // LANES):
        sl = slice(j * LANES, (j + 1) * LANES)
        q = qb_ref[0, :, sl]
        if has_cache:
            ckj = ckb_ref[0, 0, :, sl].astype(BF16)
            cvt_j = cvb_ref[0, 0, :, sl].T
        for e in range(2):
            h = 2 * j + e
            chunks = new_chunks(kb_ref, vtb_ref, j, h)
            if has_cache:
                cvj = value_block(cvt_j[e * HEAD_DIM:(e + 1) * HEAD_DIM], e)
                chunks.append((lambda a=ckj: a, lambda a=cvj: a, ckj.shape[0]))
            for mi in range(2):
                sel = (lane // B_HALF) == (2 * e + mi)
                add_map(("b", j, e, mi), jnp.where(sel, q, jnp.zeros_like(q)), chunks)

    done = {}

    def on_done(mp):
        kind, j, e, mi = mp.tag
        done[mp.tag] = _normalised_half(mp.acc, e)
        if kind == "a" and e == 1:
            o_t = jnp.concatenate([done[("a", j, 0, 0)], done[("a", j, 1, 0)]], axis=0)
            ao_ref[0, :, j * LANES:(j + 1) * LANES] = o_t.T.astype(BF16)
        if kind == "b" and e == 1 and mi == 1:
            halves = []
            for ee in range(2):
                o = done[("b", j, ee, 0)] - lam * done[("b", j, ee, 1)]
                ms = jnp.mean(o * o, axis=0, keepdims=True)
                halves.append(o * lax.rsqrt(ms + EPS))
            o_t = jnp.concatenate(halves, axis=0)
            bo_ref[0, :, j * LANES:(j + 1) * LANES] = (
                o_t.T * subln_ref[...] * (1.0 - lam_init)).astype(BF16)

    _run_score_maps(maps, on_done)


def _softmax_pv_rows(qm, k, v):
    s = lax.dot_general(qm, k, (((1,), (1,)), ((), ())), preferred_element_type=F32)
    m_part = s[:, 0:LANES]
    for t in range(1, s.shape[1] // LANES):
        m_part = jnp.maximum(m_part, s[:, t * LANES:(t + 1) * LANES])
    e = jnp.exp2(s - jnp.max(m_part, axis=-1, keepdims=True))
    l_part = e[:, 0:LANES]
    for t in range(1, s.shape[1] // LANES):
        l_part = l_part + e[:, t * LANES:(t + 1) * LANES]
    acc = jnp.dot(e.astype(BF16), v, preferred_element_type=F32)
    return acc, jnp.sum(l_part, axis=-1, keepdims=True)


def _attn_rows_kernel(qa_ref, k3_ref, v3_ref, qb_ref, kb_ref, vb_ref, lamv_ref, subln_ref,
                      ao_ref, bo_ref, *, lam_init):
    left = _left_mask()
    lane = lax.broadcasted_iota(jnp.int32, (1, LANES), 1)
    for j in range(A_WIDTH // LANES):
        sl = slice(j * LANES, (j + 1) * LANES)
        q = qa_ref[0, :, sl]
        halves = []
        for e in range(2):
            qm = jnp.where(left if e == 0 else jnp.logical_not(left), q, jnp.zeros_like(q))
            acc, l = _softmax_pv_rows(qm, k3_ref[0, :, sl], v3_ref[0, :, sl])
            halves.append(acc * (1.0 / l))
        ao_ref[0, :, sl] = jnp.where(left, halves[0], halves[1]).astype(BF16)

    lv = lamv_ref[...]
    lam = (jnp.exp(jnp.sum(lv[0:1] * lv[1:2], axis=-1, keepdims=True))
           - jnp.exp(jnp.sum(lv[2:3] * lv[3:4], axis=-1, keepdims=True)) + lam_init)
    for j in range(B_WIDTH // LANES):
        sl = slice(j * LANES, (j + 1) * LANES)
        q = qb_ref[0, :, sl]
        halves = []
        for e in range(2):
            maps = []
            for mi in range(2):
                sel = (lane // B_HALF) == (2 * e + mi)
                acc, l = _softmax_pv_rows(jnp.where(sel, q, jnp.zeros_like(q)), kb_ref[0, :, sl], vb_ref[0, :, sl])
                maps.append(acc * (1.0 / l))
            halves.append(maps[0] - lam * maps[1])
        o = jnp.where(left, halves[0], halves[1])
        bo_ref[0, :, sl] = (_pair_rmsnorm(o, subln_ref[...]) * (1.0 - lam_init)).astype(BF16)


def _attn_rows(qa, k3, v3, qb, kb, vb, lamv, subln, *, lam_init):
    nb, L, _ = qa.shape
    spec = lambda w: pl.BlockSpec((1, L, w), lambda b: (b, 0, 0))
    return pl.pallas_call(
        functools.partial(_attn_rows_kernel, lam_init=lam_init),
        out_shape=[jax.ShapeDtypeStruct((nb, L, A_WIDTH), BF16),
                   jax.ShapeDtypeStruct((nb, L, B_WIDTH), BF16)],
        grid=(nb,),
        in_specs=[spec(A_WIDTH), spec(A_WIDTH), spec(A_WIDTH), spec(B_WIDTH), spec(B_WIDTH), spec(B_WIDTH),
                  pl.BlockSpec((4, B_HALF), lambda b: (0, 0)),
                  pl.BlockSpec((1, LANES), lambda b: (0, 0))],
        out_specs=[spec(A_WIDTH), spec(B_WIDTH)],
        compiler_params=_cparams(1),
        name="attn",
    )(qa, k3, v3, qb, kb, vb, lamv, subln)


def _attn(qa, k3, vta, qb, kb, vtb, lamv, subln, caches, layer, *, lam_init):
    nb, L, _ = qa.shape
    tq = min(ATTN_TQ, L)
    has_cache = caches is not None
    q_spec = lambda w: pl.BlockSpec((1, tq, w), lambda b, i: (b, i, 0))
    kv_spec = lambda w: pl.BlockSpec((1, L, w), lambda b, i: (b, 0, 0))
    vt_spec = lambda n: pl.BlockSpec((1, n, LANES, L), lambda b, i: (b, 0, 0, 0))
    in_specs = [q_spec(A_WIDTH), kv_spec(A_WIDTH), vt_spec(A_Q_HEADS),
                q_spec(B_WIDTH), kv_spec(B_WIDTH), vt_spec(B_HEADS),
                pl.BlockSpec((4, B_HALF), lambda b, i: (0, 0)),
                pl.BlockSpec((1, LANES), lambda b, i: (0, 0))]
    args = [qa, k3, vta, qb, kb, vtb, lamv, subln]
    lk = L
    if has_cache:
        past = caches[0].shape[2]
        lk += past
        for c in caches:
            in_specs.append(pl.BlockSpec((1, 1, past, c.shape[-1]), lambda b, i: (b, layer, 0, 0)))
            args.append(c)
    return pl.pallas_call(
        functools.partial(_attn_kernel, has_cache=has_cache, lam_init=lam_init, seq_len=L),
        out_shape=[jax.ShapeDtypeStruct((nb, L, A_WIDTH), BF16),
                   jax.ShapeDtypeStruct((nb, L, B_WIDTH), BF16)],
        grid=(nb, L // tq),
        in_specs=in_specs,
        out_specs=[q_spec(A_WIDTH), q_spec(B_WIDTH)],
        scratch_shapes=[pltpu.VMEM((ATTN_AHEAD + 1, lk, tq), F32)],
        compiler_params=_cparams(2),
        name="attn_cache" if has_cache else "attn",
    )(*args)


def _prefix_sums(tri, a):
    hi = a.astype(BF16)
    lo = (a - hi.astype(F32)).astype(BF16)
    return (jnp.dot(tri, hi, preferred_element_type=F32)
            + jnp.dot(tri, lo, preferred_element_type=F32))


def _pair_cols(x, c0):
    return jnp.where(_left_mask(), x[:, c0:c0 + 1], x[:, c0 + 1:c0 + 2])


def _ssd_kernel(*refs, has_init, emit_state, seq_len, n_alias, state_slot):
    xbc_ref, z_ref, dt_ref, cw_ref, cb_ref, dtb_ref, alog_ref, dexp_ref, nw_ref = refs[:9]
    pos = 9
    if has_init:
        init_ref = refs[pos]
        pos += 1
    pos += n_alias
    co_ref = refs[pos]
    pos += 1
    if emit_state:
        st_ref = refs[pos]
        pos += 1
    xc_scr, y_scr, st_scr = refs[pos:pos + 3]

    L = seq_len
    rb = min(ROW_BLOCK, L)
    q = SSD_CHUNK
    nc = L // q
    left = _left_mask()

    w0 = cw_ref[0:1, :]
    w1 = cw_ref[1:2, :]
    w2 = cw_ref[2:3, :]
    cb = CONV_ROWS
    half = cb // 2
    hrow = lax.broadcasted_iota(jnp.int32, (half, 1), 0)
    n_u = xbc_ref.shape[0]
    for u, s in [(u, s) for u in range(n_u) for s in range(C_CONV_CH // LANES)]:
        ls = slice(s * LANES, (s + 1) * LANES)
        w0s, w1s, w2s, bs = w0[:, ls], w1[:, ls], w2[:, ls], cb_ref[:, ls]
        for r0 in range(0, L, cb):
            even = xbc_ref[u, s, pl.ds(r0, half, stride=2), :]
            odd = xbc_ref[u, s, pl.ds(r0 + 1, half, stride=2), :]
            if r0 > 0:
                odd_before = xbc_ref[u, s, pl.ds(r0 - 1, half, stride=2), :]
            else:
                odd_before = jnp.where(hrow == 0, 0.0, pltpu.roll(odd, 1, 0))
            if r0 + cb < L:
                even_after = xbc_ref[u, s, pl.ds(r0 + 2, half, stride=2), :]
            else:
                even_after = jnp.where(hrow == half - 1, 0.0, pltpu.roll(even, half - 1, 0))
            y_even = odd_before * w0s + even * w1s + odd * w2s + bs
            y_odd = even * w0s + odd * w1s + even_after * w2s + bs
            xc_scr[u, s, pl.ds(r0, half, stride=2), :] = y_even * _sigmoid(y_even)
            xc_scr[u, s, pl.ds(r0 + 1, half, stride=2), :] = y_odd * _sigmoid(y_odd)
    for u, j in [(u, j) for u in range(n_u) for j in range(C_INNER // LANES)]:
        for r0 in range(0, L, rb):
            y_scr[u, r0:r0 + rb, j * LANES:(j + 1) * LANES] = (
                xc_scr[u, j, r0:r0 + rb, :] * dexp_ref[:, j * LANES:(j + 1) * LANES])

    for u, d in [(u, d) for u in range(n_u) for d in range(2)]:
        if has_init:
            st_scr[u, d] = init_ref[u, 0, d].reshape(C_INNER, C_STATE).T
        else:
            st_scr[u, d] = jnp.zeros((C_STATE, C_INNER), F32)

    a_neg = -jnp.exp(alog_ref[...])
    ti = lax.broadcasted_iota(jnp.int32, (q, q), 0)
    si = lax.broadcasted_iota(jnp.int32, (q, q), 1)
    tri_incl = (si <= ti).astype(BF16)
    src_lane = lax.broadcasted_iota(jnp.int32, (LANES, C_INNER), 0)
    dst_head = lax.broadcasted_iota(jnp.int32, (LANES, C_INNER), 1) // C_HEADDIM
    spread = [(src_lane == d * C_HEADS + dst_head).astype(BF16) for d in range(2)]

    def process(u, d, r0):
        rows = pl.ds(r0, q)
        lo = d * C_HEADS
        causal = (si <= ti) if d == 0 else (si >= ti)
        n_x = C_INNER // LANES
        bm = [xc_scr[u, n_x + g, rows, :] for g in range(C_GROUPS)]
        cm = [xc_scr[u, n_x + C_GROUPS + g, rows, :] for g in range(C_GROUPS)]
        bm_b = [b.astype(BF16) for b in bm]
        cm_b = [c.astype(BF16) for c in cm]
        bmt_b = [b.T.astype(BF16) for b in bm]
        gmat = [lax.dot_general(cm_b[g], bm_b[g], (((1,), (1,)), ((), ())), preferred_element_type=F32)
                for g in range(C_GROUPS)]

        dtr = dt_ref[u, rows, :] + dtb_ref[...]
        dt = jnp.maximum(dtr, 0.0) + jnp.log1p(jnp.exp(-jnp.abs(dtr)))
        a = dt * a_neg
        cs = _prefix_sums(tri_incl, a)
        tot = cs[q - 1:q, :]
        ev = cs if d == 0 else (tot - cs + a)
        e_in = jnp.exp(ev)
        e_out = jnp.exp(tot - ev)
        dec = jnp.exp(tot)
        ev_t = ev.T
        per_head = jnp.concatenate([dt, e_in, e_out], axis=0).astype(BF16)
        wide = jnp.dot(per_head, spread[d], preferred_element_type=F32)
        dt_w, e_in_w, e_out_w = wide[0:q], wide[q:2 * q], wide[2 * q:3 * q]

        st = st_scr.at[u, d]
        for j in range(C_INNER // LANES):
            sl = slice(j * LANES, (j + 1) * LANES)
            h0 = 2 * j
            g0, g1 = h0 // (C_HEADS // C_GROUPS), (h0 + 1) // (C_HEADS // C_GROUPS)
            xdt = xc_scr[u, j, rows, :] * dt_w[:, sl]
            xdt_b = xdt.astype(BF16)
            st_blk = st[:, sl]
            st_b = st_blk.astype(BF16)
            yd = []
            for e in range(2):
                hh = lo + h0 + e
                g = g0 if e == 0 else g1
                diff = ev[:, hh:hh + 1] - ev_t[hh:hh + 1, :]
                lmat = jnp.exp(jnp.where(causal, diff, NEG_BIG))
                yd.append(jnp.dot((gmat[g] * lmat).astype(BF16), xdt_b, preferred_element_type=F32))
            yo0 = jnp.dot(cm_b[g0], st_b, preferred_element_type=F32)
            yo1 = yo0 if g1 == g0 else jnp.dot(cm_b[g1], st_b, preferred_element_type=F32)
            y_blk = (jnp.where(left, yd[0], yd[1])
                     + jnp.where(left, yo0, yo1) * e_in_w[:, sl])
            y_scr[u, rows, sl] = y_scr[u, rows, sl] + y_blk
            xw = (xdt * e_out_w[:, sl]).astype(BF16)
            up0 = jnp.dot(bmt_b[g0], xw, preferred_element_type=F32)
            up1 = up0 if g1 == g0 else jnp.dot(bmt_b[g1], xw, preferred_element_type=F32)
            st[:, sl] = st_blk * _pair_cols(dec, lo + h0) + jnp.where(left, up0, up1)

    if nc <= 2:
        for c in range(nc):
            for u in range(n_u):
                process(u, 0, c * q)
                process(u, 1, (nc - 1 - c) * q)
    else:
        def body(c, carry):
            for u in range(n_u):
                process(u, 0, pl.multiple_of(c * q, q))
                process(u, 1, pl.multiple_of((nc - 1 - c) * q, q))
            return carry
        lax.fori_loop(0, nc, body, 0, unroll=SSD_UNROLL if nc % SSD_UNROLL == 0 else 1)

    for u in range(n_u):
        if emit_state:
            for d in range(2):
                st_ref[u, state_slot, d] = st_scr[u, d].T.reshape(C_HEADS, C_HEADDIM, C_STATE)
            for other in range(st_ref.shape[1]):
                if other != state_slot:
                    st_ref[u, other] = jnp.zeros(st_ref.shape[2:], F32)

        for r0 in range(0, L, rb):
            zz = z_ref[u, r0:r0 + rb, :]
            yg = y_scr[u, r0:r0 + rb, :] * (zz * _sigmoid(zz))
            ms = jnp.mean(yg * yg, axis=-1, keepdims=True)
            co_ref[u, r0:r0 + rb, :] = (yg * lax.rsqrt(ms + EPS) * nw_ref[...]).astype(BF16)


def _ssd(xbc, z, dt, conv_w, conv_b, dtb, alog, dexp, norm_w, state, layer, *, emit_state, depth=1,
         st_prev=None):
    nb, n_slab, L, _ = xbc.shape
    has_init = state is not None
    sb = max(1, min(nb, SSD_STEP_ROWS // L))
    assert nb % sb == 0
    seq_spec = lambda w: pl.BlockSpec((sb, L, w), lambda b: (b, 0, 0))
    par_spec = lambda r, w: pl.BlockSpec((r, w), lambda b: (0, 0))
    in_specs = [pl.BlockSpec((sb, n_slab, L, LANES), lambda b: (b, 0, 0, 0)), seq_spec(C_INNER), seq_spec(LANES),
                par_spec(3, C_CONV_CH), par_spec(1, C_CONV_CH), par_spec(1, LANES), par_spec(1, LANES),
                par_spec(1, C_INNER), par_spec(1, C_INNER)]
    args = [xbc, z, dt, conv_w, conv_b, dtb, alog, dexp, norm_w]
    if has_init:
        in_specs.append(pl.BlockSpec((sb, 1, 2, C_HEADS, C_HEADDIM, C_STATE),
                                     lambda b: (b, layer, 0, 0, 0, 0)))
        args.append(state)
    out_shape = [jax.ShapeDtypeStruct((nb, L, C_INNER), BF16)]
    out_specs = [seq_spec(C_INNER)]
    aliases = {}
    if emit_state:
        out_shape.append(jax.ShapeDtypeStruct((nb, depth, 2, C_HEADS, C_HEADDIM, C_STATE), F32))
        out_specs.append(
            pl.BlockSpec((sb, depth, 2, C_HEADS, C_HEADDIM, C_STATE), lambda b: (b, 0, 0, 0, 0, 0)) if st_prev is None
            else pl.BlockSpec((sb, 1, 2, C_HEADS, C_HEADDIM, C_STATE), lambda b: (b, layer, 0, 0, 0, 0)))
        if st_prev is not None:
            aliases[len(args)] = 1
            in_specs.append(pl.BlockSpec(memory_space=pl.ANY))
            args.append(st_prev)
    return pl.pallas_call(
        functools.partial(_ssd_kernel, has_init=has_init, emit_state=emit_state, seq_len=L,
                          n_alias=len(aliases), state_slot=layer if st_prev is None else 0),
        input_output_aliases=aliases,
        out_shape=out_shape,
        grid=(nb // sb,),
        in_specs=in_specs,
        out_specs=out_specs,
        scratch_shapes=[pltpu.VMEM((sb, n_slab, L, LANES), F32), pltpu.VMEM((sb, L, C_INNER), F32),
                        pltpu.VMEM((sb, 2, C_STATE, C_INNER), F32)],
        compiler_params=_cparams(1),
        name="ssd_init" if has_init else "ssd",
    )(*args)


def _outp_kernel(a_ref, b_ref, c_ref, x_ref, mod_ref, w_ref, x1_o, h2_o):
    mix = jnp.concatenate([a_ref[0], b_ref[0], c_ref[0]], axis=-1)
    o = jnp.dot(mix, w_ref[...], preferred_element_type=F32)
    gate1 = mod_ref[0, 2:3, :]
    shift2 = mod_ref[0, 3:4, :]
    scale2 = mod_ref[0, 4:5, :]
    x1 = x_ref[0] + gate1 * o
    ms = jnp.mean(x1 * x1, axis=-1, keepdims=True)
    x1_o[0] = x1
    h2_o[0] = ((x1 * lax.rsqrt(ms + EPS)) * (1.0 + scale2) + shift2).astype(BF16)


def _outp(a_out, b_out, c_out, x, mod, w_out_b, layer, *, per_batch_mod):
    nb, L, d = x.shape
    tm = min(OUTP_TM, L)
    mod_map = (lambda b, i: (b, 0, 0)) if per_batch_mod else (lambda b, i: (0, 0, 0))
    row_spec = lambda w: pl.BlockSpec((1, tm, w), lambda b, i: (b, i, 0))
    return pl.pallas_call(
        _outp_kernel,
        out_shape=[jax.ShapeDtypeStruct((nb, L, d), F32), jax.ShapeDtypeStruct((nb, L, d), BF16)],
        grid=(nb, L // tm),
        in_specs=[row_spec(A_WIDTH), row_spec(B_WIDTH), row_spec(C_INNER), row_spec(d),
                  pl.BlockSpec((1, 6, d), mod_map),
                  pl.BlockSpec((None, d, d), lambda b, i: (layer, 0, 0))],
        out_specs=[row_spec(d), row_spec(d)],
        compiler_params=_cparams(2),
        name="outp",
    )(a_out, b_out, c_out, x, mod, w_out_b)


HALO = 16


def _ffn_kernel(h_ref, hp_ref, hn_ref, x1_ref, mod_ref, wu_ref, cw_ref, cb_ref, wd_ref, fw_ref,
                o_ref, hext_scr, u_scr, acc_scr, *, seq_len, final_norm):
    tm = h_ref.shape[1]
    i = pl.program_id(1)
    n_chunks = wu_ref.shape[0]
    rb = min(FFN_RB, tm)

    has_prev = ((i * tm) & (seq_len - 1)) != 0
    has_next = ((i * tm + tm) & (seq_len - 1)) != 0
    hext_scr[0:HALO, :] = jnp.where(has_prev, hp_ref[0], jnp.zeros_like(hp_ref[0]))
    hext_scr[HALO:HALO + tm, :] = h_ref[0]
    hext_scr[HALO + tm:HALO + tm + HALO, :] = jnp.where(has_next, hn_ref[0], jnp.zeros_like(hn_ref[0]))
    row8 = lax.broadcasted_iota(jnp.int32, (8, 1), 0)

    n_uslab = 2 * FFN_FC // LANES
    n_oslab = acc_scr.shape[0]
    half = rb // 2

    def up(c, slot):
        u = jnp.dot(hext_scr[...], wu_ref[c], preferred_element_type=F32)
        for s in range(n_uslab):
            u_scr[slot, s] = u[:, s * LANES:(s + 1) * LANES]

    def conv_down(c, slot, first=False):
        cw = cw_ref[c]
        bias = cb_ref[c]
        for r0 in range(0, tm, rb):
            base = HALO + r0
            ys = []
            for s in range(n_uslab):
                ls = slice(s * LANES, (s + 1) * LANES)
                even = u_scr[slot, s, pl.ds(base, half, stride=2), :]
                odd = u_scr[slot, s, pl.ds(base + 1, half, stride=2), :]
                odd_before = u_scr[slot, s, pl.ds(base - 1, half, stride=2), :]
                even_after = u_scr[slot, s, pl.ds(base + 2, half, stride=2), :]
                if r0 > 0 and r0 % seq_len == 0:
                    odd_before = jnp.concatenate(
                        [jnp.where(row8 == 0, 0.0, odd_before[0:8]), odd_before[8:]], axis=0)
                if r0 + rb < tm and (r0 + rb) % seq_len == 0:
                    even_after = jnp.concatenate(
                        [even_after[:half - 8], jnp.where(row8 == 7, 0.0, even_after[half - 8:])], axis=0)
                w0, w1, w2, b = cw[0:1, ls], cw[1:2, ls], cw[2:3, ls], bias[:, ls]
                y_even = odd_before * w0 + even * w1 + odd * w2 + b
                y_odd = even * w0 + odd * w1 + even_after * w2 + b
                ys.append(jnp.concatenate([y_even, y_odd], axis=0))
            n_g = n_uslab // 2
            act = jnp.concatenate([ys[s] * _sigmoid(ys[s]) * ys[n_g + s] for s in range(n_g)], axis=1)
            dn = jnp.dot(act.astype(BF16), wd_ref[c], preferred_element_type=F32)
            for s in range(n_oslab):
                ls = slice(s * LANES, (s + 1) * LANES)
                if first:
                    acc_scr[s, pl.ds(r0, half, stride=2), :] = dn[0:half, ls]
                    acc_scr[s, pl.ds(r0 + 1, half, stride=2), :] = dn[half:rb, ls]
                else:
                    acc_scr[s, pl.ds(r0, half, stride=2), :] += dn[0:half, ls]
                    acc_scr[s, pl.ds(r0 + 1, half, stride=2), :] += dn[half:rb, ls]

    assert n_chunks >= 3
    up(0, 0)
    up(1, 1)
    conv_down(0, 0, first=True)
    up(2, 0)
    conv_down(1, 1)

    def pair(k, carry):
        c = 2 * k
        up(c + 1, 1)
        conv_down(c, 0)
        up(c + 2, 0)
        conv_down(c + 1, 1)
        return carry

    lax.fori_loop(1, (n_chunks - 1) // 2, pair, 0)
    if n_chunks % 2 == 1:
        conv_down(n_chunks - 1, 0)
    else:
        up(n_chunks - 1, 1)
        conv_down(n_chunks - 2, 0)
        conv_down(n_chunks - 1, 1)

    gate2 = mod_ref[0, 5:6, :]
    ffn_out = jnp.concatenate([acc_scr[s] for s in range(n_oslab)], axis=1)
    out = x1_ref[0] + gate2 * ffn_out
    if final_norm:
        ms = jnp.mean(out * out, axis=-1, keepdims=True)
        out = out * lax.rsqrt(ms + EPS) * fw_ref[...]
    o_ref[0] = out


def _ffn(h2, x1, mod, wu_c, cw_c, cb_c, wd_c, layer, final_w, *, per_batch_mod, seq_len, final_norm):
    nb, L, d = x1.shape
    tm = min(FFN_TM, L)
    n_tiles = L // tm
    rb = min(FFN_RB, tm)
    assert seq_len & (seq_len - 1) == 0 and L % seq_len == 0
    assert seq_len % tm == 0 or (tm % seq_len == 0 and seq_len % rb == 0)
    n_chunks = wu_c.shape[1]
    hb = tm // HALO
    n_hblk = L // HALO
    mod_map = (lambda b, i: (b, 0, 0)) if per_batch_mod else (lambda b, i: (0, 0, 0))
    row_spec = lambda w: pl.BlockSpec((1, tm, w), lambda b, i: (b, i, 0))
    const3 = lambda s: pl.BlockSpec(s, lambda b, i: (0, 0, 0))
    return pl.pallas_call(
        functools.partial(_ffn_kernel, seq_len=seq_len, final_norm=final_norm),
        out_shape=jax.ShapeDtypeStruct((nb, L, d), F32),
        grid=(nb, n_tiles),
        in_specs=[row_spec(d),
                  pl.BlockSpec((1, HALO, d), lambda b, i: (b, jnp.maximum(i * hb - 1, 0), 0)),
                  pl.BlockSpec((1, HALO, d), lambda b, i: (b, jnp.minimum((i + 1) * hb, n_hblk - 1), 0)),
                  row_spec(d),
                  pl.BlockSpec((1, 6, d), mod_map),
                  pl.BlockSpec((None, n_chunks, d, 2 * FFN_FC), lambda b, i: (layer, 0, 0, 0)),
                  const3((n_chunks, 3, 2 * FFN_FC)),
                  const3((n_chunks, 1, 2 * FFN_FC)),
                  pl.BlockSpec((None, n_chunks, FFN_FC, d), lambda b, i: (layer, 0, 0, 0)),
                  pl.BlockSpec((1, d), lambda b, i: (0, 0))],
        out_specs=row_spec(d),
        scratch_shapes=[pltpu.VMEM((tm + 2 * HALO, d), BF16),
                        pltpu.VMEM((2, 2 * FFN_FC // LANES, tm + 2 * HALO, LANES), F32),
                        pltpu.VMEM((d // LANES, tm, LANES), F32)],
        compiler_params=_cparams(2),
        name="ffn_final" if final_norm else "ffn",
    )(h2, h2, h2, x1, mod, wu_c, cw_c, cb_c, wd_c, final_w)


def _rope_tables(L, d):
    rows = L // GRID_W
    row = jnp.repeat(jnp.arange(rows), GRID_W).astype(F32)
    col = jnp.tile(jnp.arange(GRID_W), rows).astype(F32)
    quarter = d // 4
    inv = ROPE_THETA ** (-jnp.arange(quarter, dtype=F32) / quarter)
    ang_r = row[:, None] * inv[None, :]
    ang_c = col[:, None] * inv[None, :]
    ang = jnp.concatenate([ang_r, ang_r, ang_c, ang_c], axis=-1)
    cos, sin = jnp.cos(ang), jnp.sin(ang)
    even = ((jnp.arange(d) // quarter) % 2 == 0)[None, :]
    s_up = jnp.where(even, -sin, 0.0)
    s_dn = jnp.where(even, 0.0, sin)
    reps = LANES // d
    return tuple(jnp.tile(t, (1, reps)) for t in (cos, s_up, s_dn))


def _pad_lanes(v, width=LANES):
    v = v.reshape(1, -1).astype(F32)
    return jnp.pad(v, ((0, 0), (0, width - v.shape[1])))


def _layer_params(l, a_q_norm, a_k_norm, b_lambda, b_subln, ssm_conv_w, ssm_conv_b, ssm_A_log,
                  ssm_dt_bias, ssm_D, ssm_norm_w, ffn_conv_w, ffn_conv_b):
    n_chunks = D_FF // FFN_FC

    def gv_chunks(t):
        lead = t.shape[:-1]
        g = t[..., :D_FF].reshape(lead + (n_chunks, FFN_FC))
        v = t[..., D_FF:].reshape(lead + (n_chunks, FFN_FC))
        gv = jnp.concatenate([g, v], axis=-1)
        return jnp.moveaxis(gv, -2, 0)

    return dict(
        qn=jnp.tile(a_q_norm[l], 2).reshape(1, LANES),
        kn=jnp.tile(a_k_norm[l], 2).reshape(1, LANES),
        lamv=b_lambda[l],
        subln=jnp.tile(b_subln[l], 2).reshape(1, LANES),
        conv_w=ssm_conv_w[l],
        conv_b=ssm_conv_b[l].reshape(1, C_CONV_CH),
        dtb=_pad_lanes(ssm_dt_bias[l]),
        alog=_pad_lanes(ssm_A_log[l]),
        dexp=jnp.repeat(ssm_D[l], C_HEADDIM).reshape(1, C_INNER),
        norm_w=ssm_norm_w[l].reshape(1, C_INNER),
        cw=gv_chunks(ffn_conv_w[l]),
        cb=gv_chunks(ffn_conv_b[l].reshape(1, 2 * D_FF)),
    )


def _block(x, mod, p, wts, rope_tabs, caches, state, layer, *, seq_shape, per_batch_mod, is_ctx, lam_init,
           final_w, final_norm, new_caches=None):
    n_seq, seq_len = seq_shape
    depth = wts["w_in"].shape[0]
    outs = _proj(x, mod, wts["w_in"], layer, p["qn"], p["kn"], rope_tabs,
                 per_batch_mod=per_batch_mod, kv_f32=is_ctx, seq_len=seq_len, values_t=caches is not None,
                 kv_prev=None if new_caches is None else new_caches[:4])
    seq = lambda t: t.reshape(n_seq, seq_len, t.shape[-1]) if t.ndim == 3 else t
    qa, k3, va, qb, kb, vb, xbc, z, dt = [seq(t) for t in outs[:9]]
    if caches is None:
        a_out, b_out = _attn_rows(qa, k3, va, qb, kb, vb, p["lamv"], p["subln"], lam_init=lam_init)
    else:
        a_out, b_out = _attn(qa, k3, va, qb, kb, vb, p["lamv"], p["subln"], caches, layer, lam_init=lam_init)
    ssd_out = _ssd(xbc, z, dt, p["conv_w"], p["conv_b"], p["dtb"], p["alog"], p["dexp"], p["norm_w"],
                   state, layer, emit_state=is_ctx, depth=depth,
                   st_prev=None if new_caches is None else new_caches[4])
    c_out = ssd_out[0]
    flat = lambda t: t.reshape(x.shape[0], x.shape[1], t.shape[-1])
    x1, h2 = _outp(flat(a_out), flat(b_out), flat(c_out), x, mod, wts["w_out"], layer,
                   per_batch_mod=per_batch_mod)
    x2 = _ffn(h2, x1, mod, wts["wu"], p["cw"], p["cb"], wts["wd"], layer, final_w,
              per_batch_mod=per_batch_mod, seq_len=seq_len, final_norm=final_norm)
    extras = None
    if is_ctx:
        extras = tuple(outs[9:13]) + (ssd_out[1],)
    return x2, extras


def kernel(x_prompt, x_sample, cache_a_k, cache_a_v, cache_b_k, cache_b_v, state_ssm, c, c_ctx, ada_w, ada_b, w_in, a_q_norm, a_k_norm, b_lambda, b_subln, ssm_conv_w, ssm_conv_b, ssm_A_log, ssm_dt_bias, ssm_D, ssm_norm_w, w_out, ffn_up, ffn_conv_w, ffn_conv_b, ffn_down, final_norm_w):
    depth = w_in.shape[0]
    nbp, lp, d = x_prompt.shape
    nbs, ls, _ = x_sample.shape
    past = cache_a_k.shape[2]

    mod_rows = 8
    assert 1 + nbs <= mod_rows
    cvecs = jnp.concatenate([c_ctx[None, :], c, jnp.zeros((mod_rows - 1 - nbs, d), F32)], axis=0)
    mod = _modulation(cvecs, ada_w, ada_b)

    rope_tabs = _rope_tables(ls, HEAD_DIM) + _rope_tables(ls, B_HALF)
    caches = (cache_a_k.reshape(nbs, depth, past, A_KV_HEADS * HEAD_DIM),
              cache_a_v.reshape(nbs, depth, past, A_KV_HEADS * HEAD_DIM),
              cache_b_k.reshape(nbs, depth, past, B_WIDTH),
              cache_b_v.reshape(nbs, depth, past, B_WIDTH))
    final_w = final_norm_w.reshape(1, d)
    n_chunks = D_FF // FFN_FC
    wts = dict(
        w_in=_pad_cast_bf16(w_in, PROJ_PAD),
        w_out=_cast_bf16(w_out),
        wu=_gv_cast_bf16(ffn_up),
        wd=_cast_bf16(ffn_down).reshape(depth, n_chunks, FFN_FC, d),
    )

    yp = x_prompt.reshape(1, nbp * lp, d)
    ys = x_sample
    new_caches = None
    for l in range(depth):
        lam_init = 0.8 - 0.6 * math.exp(-0.3 * l)
        p = _layer_params(l, a_q_norm, a_k_norm, b_lambda, b_subln, ssm_conv_w, ssm_conv_b,
                          ssm_A_log, ssm_dt_bias, ssm_D, ssm_norm_w, ffn_conv_w, ffn_conv_b)
        last = l == depth - 1
        mod_ctx = mod[l, 0].reshape(1, 6, d)
        mod_lat = mod[l, 1:1 + nbs].reshape(nbs, 6, d)
        yp, new_caches = _block(yp, mod_ctx, p, wts, None, None, None, l, seq_shape=(nbp, lp),
                                per_batch_mod=False, is_ctx=True, lam_init=lam_init,
                                final_w=final_w, final_norm=last, new_caches=new_caches)
        ys, _ = _block(ys, mod_lat, p, wts, rope_tabs, caches, state_ssm, l, seq_shape=(nbs, ls),
                       per_batch_mod=True, is_ctx=False, lam_init=lam_init,
                       final_w=final_w, final_norm=last)

    y_prompt = yp.reshape(nbp, lp, d)
    new_a_k = new_caches[0].reshape(nbp, depth, lp, A_KV_HEADS, HEAD_DIM)
    new_a_v = new_caches[1].reshape(nbp, depth, lp, A_KV_HEADS, HEAD_DIM)
    new_b_k = new_caches[2].reshape(nbp, depth, lp, B_HEADS, 2, B_HALF)
    new_b_v = new_caches[3].reshape(nbp, depth, lp, B_HEADS, 2 * B_HALF)
    new_ssm = new_caches[4]
    return (y_prompt, ys, new_a_k, new_a_v, new_b_k, new_b_v, new_ssm)
```
